```python
import math
import jax
import jax.numpy as jnp
from jax import lax
import numpy as np

D_MODEL = 2048
BATCH = 8
SEQ = 2048
DEPTH = 2

GRID_W = 64
CTX_LEN = 256
N_EVEN = (DEPTH + 1) // 2
N_ODD = DEPTH // 2

DIFF_QK_DIM = 64
DIFF_V_DIM = 2 * DIFF_QK_DIM
FOURIER_WIDTH = D_MODEL // 4
FOURIER_GROUP_DIM = 128
FOURIER_GROUPS = FOURIER_WIDTH // FOURIER_GROUP_DIM
DIFF_WIDTH = D_MODEL - FOURIER_WIDTH
DIFF_HEADS = DIFF_WIDTH // DIFF_V_DIM
EVEN_IN_WIDTH = 3 * DIFF_WIDTH + FOURIER_WIDTH
ROPE_AXIS_DIM = DIFF_QK_DIM // 2
ROPE_BASE = 10000.0
Q_BLOCK = 128

HGRN_EXPAND = 128
HGRN_HEADS = D_MODEL // HGRN_EXPAND
HGRN_HEAD_V = D_MODEL // HGRN_HEADS
FORGET_DIM = HGRN_HEADS * HGRN_EXPAND
HGRN_IN_WIDTH = 3 * FORGET_DIM + 2 * D_MODEL
HGRN_CHUNK = 32

D_FF = 5504
CONV_W = 3

ALPHA = (2.0 * DEPTH) ** 0.25
BETA = (8.0 * DEPTH) ** -0.25
LN_EPS = 1e-6
RMS_EPS = 1e-5
MOD_INIT = 0.5

kernel_name = 'hybrid_diffattn_fnet_hgrn2_dit'


def layer_norm(t, g=None, b=None):
    t32 = t.astype(jnp.float32)
    mu = jnp.mean(t32, axis=-1, keepdims=True)
    var = jnp.mean(jnp.square(t32 - mu), axis=-1, keepdims=True)
    y = (t32 - mu) * lax.rsqrt(var + LN_EPS)
    if g is not None:
        y = y * g.astype(jnp.float32) + b.astype(jnp.float32)
    return y.astype(t.dtype)


def rms_norm(t, w):
    t32 = t.astype(jnp.float32)
    y = t32 * lax.rsqrt(jnp.mean(jnp.square(t32), axis=-1, keepdims=True) + RMS_EPS) * w.astype(jnp.float32)
    return y.astype(t.dtype)


def modulate(t, shift, scale):
    return t * (1 + scale) + shift


def residual_post_norm(h, update, g, b):
    return layer_norm(ALPHA * h + update, g, b)


def to_heads(t, hd):
    b, n, _ = t.shape
    return t.reshape(b, n, -1, hd).transpose(0, 2, 1, 3)


def from_heads(t):
    b, h, n, d = t.shape
    return t.transpose(0, 2, 1, 3).reshape(b, n, h * d)


def rope_tables(ids):
    inv = 1.0 / (ROPE_BASE ** (jnp.arange(0, ROPE_AXIS_DIM, 2, dtype=jnp.float32) / ROPE_AXIS_DIM))
    ang = ids.astype(jnp.float32)[:, None] * inv[None, :]
    return jnp.cos(ang), jnp.sin(ang)


def rotate(t, cos, sin):
    half = t.shape[-1] // 2
    t1, t2 = t[..., :half], t[..., half:]
    cos = cos.astype(t.dtype)
    sin = sin.astype(t.dtype)
    return jnp.concatenate([t1 * cos - t2 * sin, t2 * cos + t1 * sin], axis=-1)


def axial_rope(t, rope):
    cos_r, sin_r, cos_c, sin_c = rope
    return jnp.concatenate([rotate(t[..., :ROPE_AXIS_DIM], cos_r, sin_r),
                            rotate(t[..., ROPE_AXIS_DIM:], cos_c, sin_c)], axis=-1)


def diff_attend(q1, q2, k1, k2, v, lam):
    scale = DIFF_QK_DIM ** -0.5
    s1 = jnp.einsum('bhqd,bhkd->bhqk', q1, k1).astype(jnp.float32) * scale
    s2 = jnp.einsum('bhqd,bhkd->bhqk', q2, k2).astype(jnp.float32) * scale
    p = jax.nn.softmax(s1, axis=-1) - lam * jax.nn.softmax(s2, axis=-1)
    return jnp.einsum('bhqk,bhkd->bhqd', p.astype(v.dtype), v)


def diff_attend_blocked(q1, q2, k1, k2, v, lam):
    b, h, n, d = q1.shape
    nb = n // Q_BLOCK
    blocks = lambda t: t.reshape(b, h, nb, Q_BLOCK, d).transpose(2, 0, 1, 3, 4)
    out = lax.map(lambda qs: diff_attend(qs[0], qs[1], k1, k2, v, lam), (blocks(q1), blocks(q2)))
    return out.transpose(1, 2, 0, 3, 4).reshape(b, h, n, v.shape[-1])


def fourier_mix(u):
    b, n, _ = u.shape
    g = u.reshape(b, n, FOURIER_GROUPS, FOURIER_GROUP_DIM).astype(jnp.float32)
    y = jnp.fft.fft2(g, axes=(1, 3), norm='ortho').real
    return y.reshape(b, n, FOURIER_WIDTH).astype(u.dtype)


def even_mixer(u_lat, u_ctx, rope, w_in, w_out, lam_vecs, subln_w, lam_init, need_ctx):
    splits = [DIFF_WIDTH, 2 * DIFF_WIDTH, 3 * DIFF_WIDTH]
    q_l, k_l, v_l, f_l = jnp.split(u_lat @ w_in, splits, axis=-1)
    q_c, k_c, v_c, f_c = jnp.split(u_ctx @ w_in, splits, axis=-1)
    lv = lam_vecs.astype(jnp.float32)
    lam = jnp.exp(jnp.sum(lv[0] * lv[1])) - jnp.exp(jnp.sum(lv[2] * lv[3])) + lam_init

    def pair(t, use_rope):
        th = to_heads(t, DIFF_V_DIM)
        a, b = th[..., :DIFF_QK_DIM], th[..., DIFF_QK_DIM:]
        if use_rope:
            a, b = axial_rope(a, rope), axial_rope(b, rope)
        return a, b

    q1, q2 = pair(q_l, True)
    k1, k2 = pair(k_l, True)
    kc1, kc2 = pair(k_c, False)
    v = to_heads(v_l, DIFF_V_DIM)
    vc = to_heads(v_c, DIFF_V_DIM)

    def merge(attn, four_in):
        attn = rms_norm(attn, subln_w) * (1.0 - lam_init)
        return jnp.concatenate([from_heads(attn), fourier_mix(four_in)], axis=-1) @ w_out

    keys1 = jnp.concatenate([kc1, k1], axis=2)
    keys2 = jnp.concatenate([kc2, k2], axis=2)
    vals = jnp.concatenate([vc, v], axis=2)
    m_lat = merge(diff_attend_blocked(q1, q2, keys1, keys2, vals, lam), f_l)
    m_ctx = None
    if need_ctx:
        qc1, qc2 = pair(q_c, False)
        m_ctx = merge(diff_attend(qc1, qc2, kc1, kc2, vc, lam), f_c)
    return m_lat, m_ctx


def gla_chunked(q, k, v, logf, state):
    b, n, h, dk = q.shape
    dv = v.shape[-1]
    nc = n // HGRN_CHUNK
    to_chunks = lambda t: t.reshape(b, nc, HGRN_CHUNK, h, t.shape[-1]).transpose(1, 0, 3, 2, 4)
    mask = jnp.tril(jnp.ones((HGRN_CHUNK, HGRN_CHUNK), dtype=bool))

    def step(s, inp):
        qc, kc, vc, gc = inp
        cum = jnp.cumsum(gc, axis=2)
        o_inter = jnp.einsum('bhtk,bhkv->bhtv', qc * jnp.exp(cum), s)
        rel = cum[:, :, :, None, :] - cum[:, :, None, :, :]
        decay = jnp.exp(jnp.where(mask[:, :, None], rel, -jnp.inf))
        scores = jnp.einsum('bhtk,bhsk,bhtsk->bhts', qc, kc, decay)
        o_intra = jnp.einsum('bhts,bhsv->bhtv', scores, vc)
        cum_end = cum[:, :, -1:, :]
        s = jnp.exp(cum_end[:, :, 0, :])[..., None] * s + jnp.einsum('bhsk,bhsv->bhkv', kc * jnp.exp(cum_end - cum), vc)
        return s, o_inter + o_intra

    s_final, o = lax.scan(step, state, (to_chunks(q), to_chunks(k), to_chunks(v), to_chunks(logf)))
    return o.transpose(1, 0, 3, 2, 4).reshape(b, n, h, dv), s_final


def hgrn_mixer(u_lat, u_ctx, w_in, w_out, lb, norm_w, need_ctx):
    splits = [FORGET_DIM, 2 * FORGET_DIM, 3 * FORGET_DIM, 3 * FORGET_DIM + D_MODEL]

    def prep(u):
        b, n, _ = u.shape
        q, f_fwd, f_bwd, i, g = jnp.split(u @ w_in, splits, axis=-1)
        hd = lambda t, d: t.reshape(b, n, HGRN_HEADS, d).astype(jnp.float32)
        q = jax.nn.silu(hd(q, HGRN_EXPAND))
        v = hd(i, HGRN_HEAD_V)
        gates = []
        for d, f_pre in enumerate((f_fwd, f_bwd)):
            f_pre = hd(f_pre, HGRN_EXPAND)
            lb_d = lb[d].reshape(HGRN_HEADS, HGRN_EXPAND)
            logf = jnp.log(lb_d + (1.0 - lb_d) * jax.nn.sigmoid(f_pre))
            k = (1.0 - lb_d) * jax.nn.sigmoid(-f_pre)
            gates.append((k, logf))
        return q, v, gates, g

    qc, vc, gates_c, g_c = prep(u_ctx)
    ql, vl, gates_l, g_l = prep(u_lat)
    zero = jnp.zeros((u_ctx.shape[0], HGRN_HEADS, HGRN_EXPAND, HGRN_HEAD_V), jnp.float32)
    outs_lat, outs_ctx = [], []
    for d in range(2):
        flip = (lambda t: jnp.flip(t, axis=1)) if d == 1 else (lambda t: t)
        (kc, lfc), (kl, lfl) = gates_c[d], gates_l[d]
        oc, s_ctx = gla_chunked(flip(qc), flip(kc), flip(vc), flip(lfc), zero)
        ol, _ = gla_chunked(flip(ql), flip(kl), flip(vl), flip(lfl), s_ctx)
        outs_lat.append(flip(ol))
        outs_ctx.append(flip(oc))

    def readout(o, g, u):
        b, n, _ = u.shape
        gate = jax.nn.silu(g.reshape(b, n, HGRN_HEADS, HGRN_HEAD_V).astype(jnp.float32))
        y = rms_norm(o, norm_w) * gate
        return y.reshape(b, n, D_MODEL).astype(u.dtype) @ w_out

    m_lat = readout(outs_lat[0] + outs_lat[1], g_l, u_lat)
    m_ctx = readout(outs_ctx[0] + outs_ctx[1], g_c, u_ctx) if need_ctx else None
    return m_lat, m_ctx


def conv_ffn(u, w_up, conv_w, conv_b, w_down):
    a, v = jnp.split(u @ w_up, 2, axis=-1)
    n = a.shape[1]
    pad = CONV_W // 2
    a_pad = jnp.pad(a, ((0, 0), (pad, pad), (0, 0)))
    conv = conv_b
    for tap in range(CONV_W):
        conv = conv + a_pad[:, tap:tap + n] * conv_w[tap]
    return (jax.nn.gelu(conv, approximate=False) * v) @ w_down


def setup_inputs(seed: int = 0) -> dict:
    key = jax.random.key(seed)
    ks = jax.random.split(key, 22)
    d = D_MODEL
    nrm = lambda k, shape, scale: jax.random.normal(k, shape, jnp.float32) * scale
    return {
        'x': nrm(ks[0], (BATCH, SEQ, d), 1.0),
        'c': nrm(ks[1], (BATCH, d), 1.0),
        'ctx': nrm(ks[2], (BATCH, CTX_LEN, d), 1.0),
        'c_ctx': nrm(ks[3], (d,), 1.0),
        'mod_w': nrm(ks[4], (DEPTH, d, 6 * d), MOD_INIT * d ** -0.5),
        'mod_b': nrm(ks[5], (DEPTH, 6 * d), 0.02),
        'ln_mix_g': 1.0 + nrm(ks[6], (DEPTH, d), 0.02),
        'ln_mix_b': nrm(ks[7], (DEPTH, d), 0.02),
        'ln_ffn_g': 1.0 + nrm(ks[8], (DEPTH, d), 0.02),
        'ln_ffn_b': nrm(ks[9], (DEPTH, d), 0.02),
        'even_w_in': nrm(ks[10], (N_EVEN, d, EVEN_IN_WIDTH), d ** -0.5),
        'even_w_out': nrm(ks[11], (N_EVEN, DIFF_WIDTH + FOURIER_WIDTH, d), BETA * (DIFF_WIDTH + FOURIER_WIDTH) ** -0.5),
        'diff_lambda': nrm(ks[12], (N_EVEN, 4, DIFF_QK_DIM), 0.1),
        'diff_subln': 1.0 + nrm(ks[13], (N_EVEN, DIFF_V_DIM), 0.02),
        'hgrn_w_in': nrm(ks[14], (N_ODD, d, HGRN_IN_WIDTH), d ** -0.5),
        'hgrn_w_out': nrm(ks[15], (N_ODD, d, d), BETA * d ** -0.5),
        'hgrn_lower_bounds': nrm(ks[16], (2, DEPTH, FORGET_DIM), 1.0),
        'hgrn_norm': 1.0 + nrm(ks[17], (N_ODD, HGRN_HEAD_V), 0.02),
        'ffn_w_up': nrm(ks[18], (DEPTH, d, 2 * D_FF), d ** -0.5),
        'ffn_conv_w': nrm(ks[19], (DEPTH, CONV_W, D_FF), CONV_W ** -0.5),
        'ffn_conv_b': nrm(ks[20], (DEPTH, D_FF), 0.02),
        'ffn_w_down': nrm(ks[21], (DEPTH, D_FF, d), BETA * D_FF ** -0.5),
    }


def reference(x, c, ctx, c_ctx, mod_w, mod_b, ln_mix_g, ln_mix_b, ln_ffn_g, ln_ffn_b,
              even_w_in, even_w_out, diff_lambda, diff_subln,
              hgrn_w_in, hgrn_w_out, hgrn_lower_bounds, hgrn_norm,
              ffn_w_up, ffn_conv_w, ffn_conv_b, ffn_w_down):
    n = x.shape[1]
    rows = n // GRID_W
    row_ids = jnp.repeat(jnp.arange(rows), GRID_W)
    col_ids = jnp.tile(jnp.arange(GRID_W), rows)
    rope = rope_tables(row_ids) + rope_tables(col_ids)

    lb_all = jnp.cumsum(jax.nn.softmax(hgrn_lower_bounds.astype(jnp.float32), axis=1), axis=1)
    lb_all = lb_all - lb_all[:, :1]

    silu_c = jax.nn.silu(c)
    silu_cc = jax.nn.silu(c_ctx)
    h_lat, h_ctx = x, ctx
    for layer in range(DEPTH):
        last = layer == DEPTH - 1
        slot = layer // 2
        mod_l = (silu_c @ mod_w[layer] + mod_b[layer])[:, None, :]
        mod_c = (silu_cc @ mod_w[layer] + mod_b[layer])[None, None, :]
        sh_a, sc_a, gt_a, sh_f, sc_f, gt_f = jnp.split(mod_l, 6, axis=-1)
        csh_a, csc_a, cgt_a, csh_f, csc_f, cgt_f = jnp.split(mod_c, 6, axis=-1)
        u_lat = modulate(layer_norm(h_lat), sh_a, sc_a)
        u_ctx = modulate(layer_norm(h_ctx), csh_a, csc_a)
        if layer % 2 == 0:
            lam_init = 0.8 - 0.6 * math.exp(-0.3 * layer)
            m_lat, m_ctx = even_mixer(u_lat, u_ctx, rope, even_w_in[slot], even_w_out[slot],
                                      diff_lambda[slot], diff_subln[slot], lam_init, not last)
        else:
            m_lat, m_ctx = hgrn_mixer(u_lat, u_ctx, hgrn_w_in[slot], hgrn_w_out[slot],
                                      lb_all[:, layer], hgrn_norm[slot], not last)
        h_lat = residual_post_norm(h_lat, gt_a * m_lat, ln_mix_g[layer], ln_mix_b[layer])
        f_lat = conv_ffn(modulate(layer_norm(h_lat), sh_f, sc_f),
                         ffn_w_up[layer], ffn_conv_w[layer], ffn_conv_b[layer], ffn_w_down[layer])
        h_lat = residual_post_norm(h_lat, gt_f * f_lat, ln_ffn_g[layer], ln_ffn_b[layer])
        if not last:
            h_ctx = residual_post_norm(h_ctx, cgt_a * m_ctx, ln_mix_g[layer], ln_mix_b[layer])
            f_ctx = conv_ffn(modulate(layer_norm(h_ctx), csh_f, csc_f),
                             ffn_w_up[layer], ffn_conv_w[layer], ffn_conv_b[layer], ffn_w_down[layer])
            h_ctx = residual_post_norm(h_ctx, cgt_f * f_ctx, ln_ffn_g[layer], ln_ffn_b[layer])
    return h_lat
```

```python
import functools
import math

import numpy as np
import jax
import jax.numpy as jnp
from jax import lax
from jax.experimental import pallas as pl
from jax.experimental.pallas import tpu as pltpu

F32 = jnp.float32
BF16 = jnp.bfloat16

LANES = 128
SUBLANES = 8
MXU_DIM = 256
VMEM_LIMIT_BYTES = 56 * 1024 * 1024

GRID_W = 64
DIFF_QK_DIM = 64
HEAD_DIM = 128
ROPE_AXIS_DIM = DIFF_QK_DIM // 2
ROPE_BASE = 10000.0
CONV_W = 3
LN_EPS = 1e-6
RMS_EPS = 1e-5
DEPTH = 2
ALPHA = (2.0 * DEPTH) ** 0.25
QK_SCALE = DIFF_QK_DIM ** -0.5

GLA_CHUNK = 128
GLA_LEVELS = (64, 32, 16, 8)
GLA_BAND = 8


def _params(*dims):
    return pltpu.CompilerParams(dimension_semantics=dims, vmem_limit_bytes=VMEM_LIMIT_BYTES)


def _layer_norm_rows(x):
    mu = jnp.mean(x, axis=-1, keepdims=True)
    xc = x - mu
    var = jnp.mean(xc * xc, axis=-1, keepdims=True)
    return xc * lax.rsqrt(var + LN_EPS)


def _silu(x):
    return x * jax.nn.sigmoid(x)


def _mod_kernel(c_ref, w_ref, b_ref, o_ref):
    x = _silu(c_ref[...]).astype(BF16)
    o_ref[...] = jnp.dot(x, w_ref[...].astype(BF16), preferred_element_type=F32) + b_ref[...]


def _modulation(cc, mod_w, mod_b, tn=1024):
    depth, d, n = mod_w.shape
    rows = cc.shape[0]
    return pl.pallas_call(
        _mod_kernel,
        grid=(depth, n // tn),
        in_specs=[
            pl.BlockSpec((rows, d), lambda l, j: (0, 0)),
            pl.BlockSpec((None, d, tn), lambda l, j: (l, 0, j)),
            pl.BlockSpec((None, 1, tn), lambda l, j: (l, 0, j)),
        ],
        out_specs=pl.BlockSpec((None, rows, tn), lambda l, j: (l, 0, j)),
        out_shape=jax.ShapeDtypeStruct((depth, rows, n), F32),
        compiler_params=_params("arbitrary", "arbitrary"),
        name="modulation",
    )(cc, mod_w, mod_b.reshape(depth, 1, n))


def _mod_spec(chunk, d, row_of_tile):
    return pl.BlockSpec((None, 1, d), lambda i, *_: (row_of_tile(i), 0, chunk))


def _even_inproj_kernel(h_ref, sh_ref, sc_ref, w_ref, cos_ref, sin_ref, o_ref, u_ref,
                        *, n_q_tiles, n_qk_tiles, use_rope):
    j = pl.program_id(1)

    @pl.when(j == 0)
    def _():
        u_ref[...] = (_layer_norm_rows(h_ref[...]) * (1.0 + sc_ref[...]) + sh_ref[...]).astype(BF16)

    acc = jnp.dot(u_ref[...], w_ref[...], preferred_element_type=F32)

    @pl.when(j < n_qk_tiles)
    def _():
        y = acc
        if use_rope:
            parts = []
            for hh in range(y.shape[1] // HEAD_DIM):
                yh = y[:, hh * HEAD_DIM:(hh + 1) * HEAD_DIM]
                parts.append(yh * cos_ref[...] + pltpu.roll(yh, HEAD_DIM // 2, axis=1) * sin_ref[...])
            y = jnp.concatenate(parts, axis=1)
        scale = jnp.where(j < n_q_tiles, QK_SCALE, 1.0)
        o_ref[...] = (y * scale).astype(BF16)

    @pl.when(j >= n_qk_tiles)
    def _():
        o_ref[...] = acc.astype(BF16)


def _even_inproj(h, mod3, row_of_tile, w, cos_t, sin_t, *, qk_width, use_rope, tm, tn):
    m, d = h.shape
    n = w.shape[1]
    n_pos_tiles = cos_t.shape[0] // tm if use_rope else 1
    tab_rows = tm if use_rope else cos_t.shape[0]
    kern = functools.partial(_even_inproj_kernel, n_q_tiles=qk_width // tn,
                             n_qk_tiles=2 * qk_width // tn, use_rope=use_rope)
    return pl.pallas_call(
        kern,
        grid=(m // tm, n // tn),
        in_specs=[
            pl.BlockSpec((tm, d), lambda i, j: (i, 0)),
            _mod_spec(0, d, row_of_tile),
            _mod_spec(1, d, row_of_tile),
            pl.BlockSpec((d, tn), lambda i, j: (0, j)),
            pl.BlockSpec((tab_rows, HEAD_DIM), lambda i, j: (i % n_pos_tiles, 0)),
            pl.BlockSpec((tab_rows, HEAD_DIM), lambda i, j: (i % n_pos_tiles, 0)),
        ],
        out_specs=pl.BlockSpec((tm, tn), lambda i, j: (i, j)),
        out_shape=jax.ShapeDtypeStruct((m, n), BF16),
        scratch_shapes=[pltpu.VMEM((tm, d), BF16)],
        compiler_params=_params("arbitrary", "arbitrary"),
        name="even_inproj",
    )(h, mod3, mod3, w, cos_t, sin_t)


def _diff_attn_kernel(*refs, n_seg, lam_init):
    q_ref = refs[0]
    k_refs = refs[1:1 + n_seg]
    v_refs = refs[1 + n_seg:1 + 2 * n_seg]
    lam_ref, subln_ref, o_ref = refs[1 + 2 * n_seg:]

    lv = lam_ref[...]
    lam = (jnp.exp(jnp.sum(lv[0:1] * lv[1:2], axis=1, keepdims=True))
           - jnp.exp(jnp.sum(lv[2:3] * lv[3:4], axis=1, keepdims=True)) + lam_init)

    q = q_ref[...]
    lane = lax.broadcasted_iota(jnp.int32, q.shape, 1)
    first_map = (lane % DIFF_QK_DIM) < ROPE_AXIS_DIM
    zero = jnp.zeros_like(q)
    qs = (jnp.where(first_map, q, zero), jnp.where(first_map, zero, q))

    dn = (((1,), (1,)), ((), ()))
    probs = []
    for qm in qs:
        s = [lax.dot_general(qm, k_ref[...], dn, preferred_element_type=F32) for k_ref in k_refs]
        mx = functools.reduce(jnp.maximum, [jnp.max(x, axis=1, keepdims=True) for x in s])
        e = [jnp.exp(x - mx) for x in s]
        den = functools.reduce(jnp.add, [jnp.sum(x, axis=1, keepdims=True) for x in e])
        probs.append((e, 1.0 / den))
    (e1, r1), (e2, r2) = probs
    r2 = r2 * lam
    acc = None
    for seg in range(n_seg):
        p = (e1[seg] * r1 - e2[seg] * r2).astype(BF16)
        part = jnp.dot(p, v_refs[seg][...], preferred_element_type=F32)
        acc = part if acc is None else acc + part
    ms = jnp.mean(acc * acc, axis=1, keepdims=True)
    y = acc * lax.rsqrt(ms + RMS_EPS) * subln_ref[...] * (1.0 - lam_init)
    o_ref[...] = y.astype(BF16)


def _diff_attention(q_src, kv_srcs, lam_vecs, subln, lam_init, *, n_heads, tq):
    b, nq, _ = q_src.shape
    n_seg = len(kv_srcs)
    k_specs = [pl.BlockSpec((None, s.shape[1], HEAD_DIM), lambda bi, h, t: (bi, 0, n_heads + h))
               for s in kv_srcs]
    v_specs = [pl.BlockSpec((None, s.shape[1], HEAD_DIM), lambda bi, h, t: (bi, 0, 2 * n_heads + h))
               for s in kv_srcs]
    return pl.pallas_call(
        functools.partial(_diff_attn_kernel, n_seg=n_seg, lam_init=lam_init),
        grid=(b, n_heads, nq // tq),
        in_specs=[pl.BlockSpec((None, tq, HEAD_DIM), lambda bi, h, t: (bi, t, h))] + k_specs + v_specs + [
            pl.BlockSpec(lam_vecs.shape, lambda bi, h, t: (0, 0)),
            pl.BlockSpec((1, HEAD_DIM), lambda bi, h, t: (0, 0)),
        ],
        out_specs=pl.BlockSpec((None, tq, HEAD_DIM), lambda bi, h, t: (bi, t, h)),
        out_shape=jax.ShapeDtypeStruct((b, nq, n_heads * HEAD_DIM), BF16),
        compiler_params=_params("arbitrary", "arbitrary", "arbitrary"),
        name="diff_attention",
    )(q_src, *kv_srcs, *kv_srcs, lam_vecs, subln.reshape(1, HEAD_DIM))


def _fourier_kernel(x_ref, dn_ref, cs_ref, o_ref, z_ref):
    n = x_ref.shape[0]

    @pl.when(pl.program_id(1) == 0)
    def _():
        for g in range(x_ref.shape[1] // HEAD_DIM):
            cols = slice(g * HEAD_DIM, (g + 1) * HEAD_DIM)
            zc = jnp.dot(x_ref[:, cols], cs_ref[...], preferred_element_type=F32)
            z_ref[0:n, cols] = zc[:, :HEAD_DIM].astype(BF16)
            z_ref[n:2 * n, cols] = zc[:, HEAD_DIM:].astype(BF16)

    o_ref[...] = jnp.dot(dn_ref[...], z_ref[...], preferred_element_type=F32).astype(BF16)


def _dft_tables(n):
    j = np.arange(n, dtype=np.int64)
    ang = 2.0 * np.pi * ((j[:, None] * j[None, :]) % n).astype(np.float64) / n
    return np.cos(ang) / math.sqrt(n), np.sin(ang) / math.sqrt(n)


def _fourier_mix(src, col_block, width, *, tm):
    b, n, _ = src.shape
    cn, sn = _dft_tables(n)
    cc, sc = _dft_tables(HEAD_DIM)
    dn = jnp.asarray(np.concatenate([cn, -sn], axis=1), dtype=BF16)
    cs = jnp.asarray(np.concatenate([cc, sc], axis=1), dtype=BF16)
    return pl.pallas_call(
        _fourier_kernel,
        grid=(b, n // tm),
        in_specs=[
            pl.BlockSpec((None, n, width), lambda bi, t: (bi, 0, col_block)),
            pl.BlockSpec((tm, 2 * n), lambda bi, t: (t, 0)),
            pl.BlockSpec((HEAD_DIM, 2 * HEAD_DIM), lambda bi, t: (0, 0)),
        ],
        out_specs=pl.BlockSpec((None, tm, width), lambda bi, t: (bi, t, 0)),
        out_shape=jax.ShapeDtypeStruct((b, n, width), BF16),
        scratch_shapes=[pltpu.VMEM((2 * n, width), BF16)],
        compiler_params=_params("arbitrary", "arbitrary"),
        name="fourier_mix",
    )(src, dn, cs)


def _proj_residual_kernel(*refs, n_in, n_k):
    a_refs = refs[:n_in]
    w_refs = refs[n_in:2 * n_in]
    h_ref, gate_ref, g_ref, b_ref, o_ref = refs[2 * n_in:2 * n_in + 5]
    acc_ref = refs[-1] if n_k > 1 else None
    k = pl.program_id(1)

    part = None
    for a_ref, w_ref in zip(a_refs, w_refs):
        p = jnp.dot(a_ref[...], w_ref[...], preferred_element_type=F32)
        part = p if part is None else part + p

    def finish(total):
        x = ALPHA * h_ref[...] + gate_ref[...] * total
        o_ref[...] = _layer_norm_rows(x) * g_ref[...] + b_ref[...]

    if n_k == 1:
        finish(part)
    else:
        @pl.when(k == 0)
        def _():
            acc_ref[...] = part

        @pl.when(jnp.logical_and(k > 0, k < n_k - 1))
        def _():
            acc_ref[...] += part

        @pl.when(k == n_k - 1)
        def _():
            finish(acc_ref[...] + part)


def _proj_residual(a_list, w_list, h, mod3, gate_chunk, row_of_tile, ln_g, ln_b, *, tm, n_k=1):
    m, d = h.shape
    in_specs = []
    for a in a_list:
        in_specs.append(pl.BlockSpec((tm, a.shape[1] // n_k), lambda i, k: (i, k)))
    for w in w_list:
        in_specs.append(pl.BlockSpec((w.shape[0] // n_k, d), lambda i, k: (k, 0)))
    in_specs += [
        pl.BlockSpec((tm, d), lambda i, k: (i, 0)),
        _mod_spec(gate_chunk, d, row_of_tile),
        pl.BlockSpec((1, d), lambda i, k: (0, 0)),
        pl.BlockSpec((1, d), lambda i, k: (0, 0)),
    ]
    scratch = [pltpu.VMEM((tm, d), F32)] if n_k > 1 else []
    return pl.pallas_call(
        functools.partial(_proj_residual_kernel, n_in=len(a_list), n_k=n_k),
        grid=(m // tm, n_k),
        in_specs=in_specs,
        out_specs=pl.BlockSpec((tm, d), lambda i, k: (i, 0)),
        out_shape=jax.ShapeDtypeStruct((m, d), F32),
        scratch_shapes=scratch,
        compiler_params=_params("arbitrary", "arbitrary"),
        name="proj_residual",
    )(*a_list, *w_list, h, mod3, ln_g.reshape(1, d), ln_b.reshape(1, d))


def _ffn_up_kernel(h_ref, hp_ref, hn_ref, sh_ref, sc_ref, wa_ref, wv_ref, cw_ref, cb_ref, o_ref,
                   u_ref, uh_ref, *, tiles_per_seq):
    i = pl.program_id(0)
    j = pl.program_id(1)
    tm = h_ref.shape[0]

    @pl.when(j == 0)
    def _():
        scale = 1.0 + sc_ref[...]
        u_ref[...] = (_layer_norm_rows(h_ref[...]) * scale + sh_ref[...]).astype(BF16)
        uh_ref[0:SUBLANES, :] = (_layer_norm_rows(hp_ref[...]) * scale + sh_ref[...]).astype(BF16)
        uh_ref[SUBLANES:, :] = (_layer_norm_rows(hn_ref[...]) * scale + sh_ref[...]).astype(BF16)

    a = jnp.dot(u_ref[...], wa_ref[...], preferred_element_type=F32)
    v = jnp.dot(u_ref[...], wv_ref[...], preferred_element_type=F32)
    halo = jnp.dot(uh_ref[...], wa_ref[...], preferred_element_type=F32)
    t_in_seq = i % tiles_per_seq
    prev_row = jnp.where(t_in_seq > 0, halo[SUBLANES - 1:SUBLANES], 0.0)
    next_row = jnp.where(t_in_seq < tiles_per_seq - 1, halo[SUBLANES:SUBLANES + 1], 0.0)
    row = lax.broadcasted_iota(jnp.int32, a.shape, 0)
    a_prev = jnp.where(row == 0, prev_row, pltpu.roll(a, 1, axis=0))
    a_next = jnp.where(row == tm - 1, next_row, pltpu.roll(a, tm - 1, axis=0))
    cw = cw_ref[...]
    conv = cb_ref[...] + a_prev * cw[0:1] + a * cw[1:2] + a_next * cw[2:3]
    gelu = 0.5 * conv * (1.0 + lax.erf(conv * math.sqrt(0.5)))
    o_ref[...] = (gelu * v).astype(BF16)


def _ffn_up(h, mod3, row_of_tile, wa, wv, conv_w, conv_b, *, seq_len, tm, tn):
    m, d = h.shape
    n = wa.shape[1]
    tiles_per_seq = seq_len // tm
    blocks_per_tile = tm // SUBLANES
    n_blocks = m // SUBLANES
    kern = functools.partial(_ffn_up_kernel, tiles_per_seq=tiles_per_seq)
    return pl.pallas_call(
        kern,
        grid=(m // tm, n // tn),
        in_specs=[
            pl.BlockSpec((tm, d), lambda i, j: (i, 0)),
            pl.BlockSpec((SUBLANES, d), lambda i, j: (jnp.maximum(i * blocks_per_tile - 1, 0), 0)),
            pl.BlockSpec((SUBLANES, d), lambda i, j: (jnp.minimum((i + 1) * blocks_per_tile, n_blocks - 1), 0)),
            _mod_spec(3, d, row_of_tile),
            _mod_spec(4, d, row_of_tile),
            pl.BlockSpec((d, tn), lambda i, j: (0, j)),
            pl.BlockSpec((d, tn), lambda i, j: (0, j)),
            pl.BlockSpec((CONV_W, tn), lambda i, j: (0, j)),
            pl.BlockSpec((1, tn), lambda i, j: (0, j)),
        ],
        out_specs=pl.BlockSpec((tm, tn), lambda i, j: (i, j)),
        out_shape=jax.ShapeDtypeStruct((m, n), BF16),
        scratch_shapes=[pltpu.VMEM((tm, d), BF16), pltpu.VMEM((2 * SUBLANES, d), BF16)],
        compiler_params=_params("arbitrary", "arbitrary"),
        name="ffn_up",
    )(h, h, h, mod3, mod3, wa, wv, conv_w, conv_b)


def _hgrn_act_kernel(h_ref, sh_ref, sc_ref, w_ref, o_ref, u_ref, *, lin_lo, lin_hi):
    j = pl.program_id(1)

    @pl.when(j == 0)
    def _():
        u_ref[...] = (_layer_norm_rows(h_ref[...]) * (1.0 + sc_ref[...]) + sh_ref[...]).astype(BF16)

    acc = jnp.dot(u_ref[...], w_ref[...], preferred_element_type=F32)
    linear = jnp.logical_and(j >= lin_lo, j < lin_hi)

    @pl.when(linear)
    def _():
        o_ref[...] = acc.astype(BF16)

    @pl.when(jnp.logical_not(linear))
    def _():
        o_ref[...] = _silu(acc).astype(BF16)


def _hgrn_gate_kernel(h_ref, sh_ref, sc_ref, w_ref, lbp_ref, k_ref, lf_ref, u_ref, *, layer):
    j = pl.program_id(1)

    @pl.when(j == 0)
    def _():
        u_ref[...] = (_layer_norm_rows(h_ref[...]) * (1.0 + sc_ref[...]) + sh_ref[...]).astype(BF16)

    x = lbp_ref[...]
    e = jnp.exp(x - jnp.max(x, axis=0, keepdims=True))
    lb = jnp.sum(e[1:layer + 1], axis=0, keepdims=True) / jnp.sum(e, axis=0, keepdims=True)

    f_pre = jnp.dot(u_ref[...], w_ref[...], preferred_element_type=F32)
    k_ref[...] = ((1.0 - lb) * jax.nn.sigmoid(-f_pre)).astype(BF16)
    lf_ref[...] = jnp.log(lb + (1.0 - lb) * jax.nn.sigmoid(f_pre))


def _hgrn_inproj(h, mod3, row_of_tile, w_act, w_gate, lb_params, *, layer, tm, tn):
    m, d = h.shape
    common = [
        pl.BlockSpec((tm, d), lambda i, j: (i, 0)),
        _mod_spec(0, d, row_of_tile),
        _mod_spec(1, d, row_of_tile),
        pl.BlockSpec((d, tn), lambda i, j: (0, j)),
    ]
    tiles = d // tn
    act = pl.pallas_call(
        functools.partial(_hgrn_act_kernel, lin_lo=tiles, lin_hi=2 * tiles),
        grid=(m // tm, w_act.shape[1] // tn),
        in_specs=common,
        out_specs=pl.BlockSpec((tm, tn), lambda i, j: (i, j)),
        out_shape=jax.ShapeDtypeStruct((m, w_act.shape[1]), BF16),
        scratch_shapes=[pltpu.VMEM((tm, d), BF16)],
        compiler_params=_params("arbitrary", "arbitrary"),
        name="hgrn_inproj_act",
    )(h, mod3, mod3, w_act)
    k, logf = pl.pallas_call(
        functools.partial(_hgrn_gate_kernel, layer=layer),
        grid=(m // tm, w_gate.shape[1] // tn),
        in_specs=common + [pl.BlockSpec((None, DEPTH, tn), lambda i, j: (j // tiles, 0, j % tiles))],
        out_specs=[pl.BlockSpec((tm, tn), lambda i, j: (i, j))] * 2,
        out_shape=[jax.ShapeDtypeStruct((m, w_gate.shape[1]), BF16),
                   jax.ShapeDtypeStruct((m, w_gate.shape[1]), F32)],
        scratch_shapes=[pltpu.VMEM((tm, d), BF16)],
        compiler_params=_params("arbitrary", "arbitrary"),
        name="hgrn_inproj_gate",
    )(h, mod3, mod3, w_gate, lb_params)
    return act, k, logf


def _gla_chunk(q, k, lf, v_bf, st, cum_ref, tri, same_block, *, reverse):
    c = GLA_CHUNK
    dn_t = (((1,), (1,)), ((), ()))
    lf_hi = lf.astype(BF16)
    lf_lo = (lf - lf_hi.astype(F32)).astype(BF16)
    cum = (jnp.dot(tri, lf_hi, preferred_element_type=F32)
           + jnp.dot(tri, lf_lo, preferred_element_type=F32))
    cum_ref[...] = cum
    end_row = 0 if reverse else c - 1
    cum_end = cum_ref[end_row:end_row + 1, :]

    o = lax.dot_general((q * jnp.exp(cum)).astype(BF16), st.astype(BF16), dn_t,
                        preferred_element_type=F32)

    scores = None
    zeros = {m: jnp.zeros((m, HEAD_DIM), F32) for m in GLA_LEVELS}
    for m in GLA_LEVELS:
        a_parts, b_parts = [], []
        for blk in range(c // (2 * m)):
            lo, mid, hi = blk * 2 * m, blk * 2 * m + m, (blk + 1) * 2 * m
            if reverse:
                ref_row = cum_ref[mid:mid + 1, :]
                qa = q[lo:mid] * jnp.exp(cum[lo:mid] - ref_row)
                kb = k[mid:hi] * jnp.exp(ref_row - cum[mid:hi])
                a_parts += [qa, zeros[m]]
                b_parts += [zeros[m], kb]
            else:
                ref_row = cum_ref[mid - 1:mid, :]
                qa = q[mid:hi] * jnp.exp(cum[mid:hi] - ref_row)
                kb = k[lo:mid] * jnp.exp(ref_row - cum[lo:mid])
                a_parts += [zeros[m], qa]
                b_parts += [kb, zeros[m]]
        a_m = jnp.concatenate(a_parts, axis=0).astype(BF16)
        b_m = jnp.concatenate(b_parts, axis=0).astype(BF16)
        s_m = lax.dot_general(a_m, b_m, dn_t, preferred_element_type=F32)
        if 2 * m < c:
            s_m = jnp.where(same_block < 2 * m, s_m, 0.0)
        scores = s_m if scores is None else scores + s_m
    o = o + jnp.dot(scores.astype(BF16), v_bf, preferred_element_type=F32)

    v = v_bf.astype(F32)
    row = lax.broadcasted_iota(jnp.int32, (c, HEAD_DIM), 0) % GLA_BAND
    o = o + jnp.sum(q * k, axis=1, keepdims=True) * v
    for dlt in range(1, GLA_BAND):
        shift = (c - dlt) if reverse else dlt
        valid = (row <= GLA_BAND - 1 - dlt) if reverse else (row >= dlt)
        k_d = pltpu.roll(k, shift, axis=0)
        v_d = pltpu.roll(v, shift, axis=0)
        cum_d = pltpu.roll(cum, shift, axis=0)
        decay = jnp.exp(jnp.where(valid, cum - cum_d, -jnp.inf))
        o = o + jnp.sum(q * k_d * decay, axis=1, keepdims=True) * v_d

    k_dec = (k * jnp.exp(cum_end - cum)).astype(BF16)
    st_new = st * jnp.exp(cum_end) + lax.dot_general(
        v_bf, k_dec, (((0,), (0,)), ((), ())), preferred_element_type=F32)
    return o, st_new


def _gla_kernel(*refs, emit_y, emit_state):
    q_ref, kf_ref, kb_ref, lff_ref, lfb_ref, v_ref, gate_ref, s0_ref, nw_ref = refs[:9]
    pos = 9
    y_ref = s_out_ref = None
    if emit_y:
        y_ref = refs[pos]
        pos += 1
    if emit_state:
        s_out_ref = refs[pos]
        pos += 1
    o_ref, cum_ref = refs[pos:pos + 2]

    c = GLA_CHUNK
    n_chunks = q_ref.shape[0] // c
    r_i = lax.broadcasted_iota(jnp.int32, (c, c), 0)
    c_i = lax.broadcasted_iota(jnp.int32, (c, c), 1)
    same_block = jnp.bitwise_xor(r_i, c_i)
    tri_f = jnp.where(c_i <= r_i, 1.0, 0.0).astype(BF16)
    tri_b = jnp.where(c_i >= r_i, 1.0, 0.0).astype(BF16)

    def run(k_ref, lf_ref, s_init, tri, reverse, finalize):
        def body(ci, st):
            chunk = (n_chunks - 1 - ci) if reverse else ci
            rows = pl.ds(pl.multiple_of(chunk * c, c), c)
            o, st = _gla_chunk(q_ref[rows, :].astype(F32), k_ref[rows, :].astype(F32), lf_ref[rows, :],
                               v_ref[rows, :], st, cum_ref, tri, same_block, reverse=reverse)
            if not finalize:
                o_ref[rows, :] = o
            elif emit_y:
                tot = o_ref[rows, :] + o
                ms = jnp.mean(tot * tot, axis=1, keepdims=True)
                y = tot * lax.rsqrt(ms + RMS_EPS) * nw_ref[...] * gate_ref[rows, :].astype(F32)
                y_ref[rows, :] = y.astype(BF16)
            return st
        return lax.fori_loop(0, n_chunks, body, s_init)

    st_f = run(kf_ref, lff_ref, s0_ref[0], tri_f, False, False)
    st_b = run(kb_ref, lfb_ref, s0_ref[1], tri_b, True, True)
    if emit_state:
        s_out_ref[0] = st_f
        s_out_ref[1] = st_b


def _gla(act, k, logf, s0, norm_w, *, n_heads, emit_y, emit_state):
    b, n, _ = act.shape
    hd = HEAD_DIM
    col = lambda off: pl.BlockSpec((None, n, hd), lambda bi, h: (bi, 0, off * n_heads + h))
    st_spec = pl.BlockSpec((None, None, 2, hd, hd), lambda bi, h: (bi, h, 0, 0, 0))
    out_specs, out_shape = [], []
    if emit_y:
        out_specs.append(pl.BlockSpec((None, n, hd), lambda bi, h: (bi, 0, h)))
        out_shape.append(jax.ShapeDtypeStruct((b, n, n_heads * hd), BF16))
    if emit_state:
        out_specs.append(st_spec)
        out_shape.append(jax.ShapeDtypeStruct((b, n_heads, 2, hd, hd), F32))
    outs = pl.pallas_call(
        functools.partial(_gla_kernel, emit_y=emit_y, emit_state=emit_state),
        grid=(b, n_heads),
        in_specs=[col(0), col(0), col(1), col(0), col(1), col(1), col(2), st_spec,
                  pl.BlockSpec((1, hd), lambda bi, h: (0, 0))],
        out_specs=out_specs,
        out_shape=out_shape,
        scratch_shapes=[pltpu.VMEM((n, hd), F32), pltpu.VMEM((GLA_CHUNK, hd), F32)],
        compiler_params=_params("arbitrary", "arbitrary"),
        name="gla_scan",
    )(act, k, k, logf, logf, act, act, s0, norm_w.reshape(1, hd))
    return outs


def _rope_head_perm():
    half = ROPE_AXIS_DIM // 2
    first = [0, 2 * half, 4 * half, 6 * half]
    order = first + [s + half for s in first]
    return np.concatenate([np.arange(s, s + half) for s in order])


def _rope_tables(n):
    half = ROPE_AXIS_DIM // 2
    inv = (1.0 / (ROPE_BASE ** (np.arange(0, ROPE_AXIS_DIM, 2, dtype=np.float32) / ROPE_AXIS_DIM))).astype(np.float32)
    pos = np.arange(n)
    ang_r = (pos // GRID_W).astype(np.float32)[:, None] * inv[None, :]
    ang_c = (pos % GRID_W).astype(np.float32)[:, None] * inv[None, :]
    ang = np.concatenate([ang_r, ang_c] * 4, axis=1)
    sign = np.concatenate([-np.ones(4 * half, np.float32), np.ones(4 * half, np.float32)])
    return jnp.asarray(np.cos(ang), F32), jnp.asarray(np.sin(ang) * sign[None, :], F32)


def kernel(x, c, ctx, c_ctx, mod_w, mod_b, ln_mix_g, ln_mix_b, ln_ffn_g, ln_ffn_b, even_w_in, even_w_out, diff_lambda, diff_subln, hgrn_w_in, hgrn_w_out, hgrn_lower_bounds, hgrn_norm, ffn_w_up, ffn_conv_w, ffn_conv_b, ffn_w_down):
    b, n_lat, d = x.shape
    n_ctx = ctx.shape[1]
    assert mod_w.shape[0] == DEPTH and n_lat % GRID_W == 0
    d_ff = ffn_w_down.shape[1]
    d_ff_pad = -(-d_ff // (2 * MXU_DIM)) * (2 * MXU_DIM)
    diff_width = even_w_out.shape[2] * 3 // 4
    four_width = even_w_in.shape[2] - 3 * diff_width
    n_diff_heads = diff_width // HEAD_DIM
    n_hgrn_heads = d // HEAD_DIM

    cc = jnp.concatenate([c, c_ctx[None, :], jnp.zeros((2 * SUBLANES - b - 1, d), F32)], axis=0)
    mod = _modulation(cc, mod_w, mod_b)
    mod = mod.reshape(DEPTH, cc.shape[0], 1, 6 * d)

    h_lat = x.reshape(b * n_lat, d)
    h_ctx = ctx.reshape(b * n_ctx, d)

    def ffn(h, mod3, row_of_tile, layer, seq_len, tm):
        pad = d_ff_pad - d_ff
        w_up = ffn_w_up[layer]
        wa = jnp.pad(w_up[:, :d_ff], ((0, 0), (0, pad))).astype(BF16)
        wv = jnp.pad(w_up[:, d_ff:], ((0, 0), (0, pad))).astype(BF16)
        cw = jnp.pad(ffn_conv_w[layer], ((0, 0), (0, pad)))
        cb = jnp.pad(ffn_conv_b[layer], (0, pad)).reshape(1, d_ff_pad)
        wd = jnp.pad(ffn_w_down[layer], ((0, pad), (0, 0))).astype(BF16)
        g = _ffn_up(h, mod3, row_of_tile, wa, wv, cw, cb, seq_len=seq_len, tm=tm, tn=512)
        return _proj_residual([g], [wd], h, mod3, 5, row_of_tile, ln_ffn_g[layer], ln_ffn_b[layer],
                              tm=tm, n_k=d_ff_pad // 512)

    layer = 0
    lam_init = 0.8 - 0.6 * math.exp(-0.3 * layer)
    perm = _rope_head_perm()
    qk_cols = (np.arange(2 * n_diff_heads)[:, None] * HEAD_DIM + perm[None, :]).reshape(-1)
    cols = np.concatenate([qk_cols, np.arange(2 * diff_width, even_w_in.shape[2])])
    w_in = even_w_in[0][:, cols].astype(BF16)
    w_out = even_w_out[0].astype(BF16)
    cos_t, sin_t = _rope_tables(n_lat)
    mod3 = mod[layer]

    tm_lat, tm_ctx = min(512, n_lat), min(256, n_ctx)
    lat_row = lambda i: (i * tm_lat) // n_lat
    ctx_row = lambda i: b

    p_lat = _even_inproj(h_lat, mod3, lat_row, w_in, cos_t, sin_t, qk_width=diff_width,
                         use_rope=True, tm=tm_lat, tn=512).reshape(b, n_lat, -1)
    p_ctx = _even_inproj(h_ctx, mod3, ctx_row, w_in, cos_t[:SUBLANES], sin_t[:SUBLANES], qk_width=diff_width,
                         use_rope=False, tm=tm_ctx, tn=512).reshape(b, n_ctx, -1)

    attn_lat = _diff_attention(p_lat, [p_ctx, p_lat], diff_lambda[0], diff_subln[0], lam_init,
                               n_heads=n_diff_heads, tq=min(256, n_lat))
    attn_ctx = _diff_attention(p_ctx, [p_ctx], diff_lambda[0], diff_subln[0], lam_init,
                               n_heads=n_diff_heads, tq=n_ctx)
    four_block = 3 * diff_width // four_width
    four_lat = _fourier_mix(p_lat, four_block, four_width, tm=tm_lat)
    four_ctx = _fourier_mix(p_ctx, four_block, four_width, tm=n_ctx)

    w_list = [w_out[:diff_width], w_out[diff_width:]]
    h_lat = _proj_residual([attn_lat.reshape(b * n_lat, -1), four_lat.reshape(b * n_lat, -1)], w_list,
                           h_lat, mod3, 2, lat_row, ln_mix_g[layer], ln_mix_b[layer], tm=tm_lat)
    h_ctx = _proj_residual([attn_ctx.reshape(b * n_ctx, -1), four_ctx.reshape(b * n_ctx, -1)], w_list,
                           h_ctx, mod3, 2, ctx_row, ln_mix_g[layer], ln_mix_b[layer], tm=tm_ctx)
    h_lat = ffn(h_lat, mod3, lat_row, layer, n_lat, tm_lat)
    h_ctx = ffn(h_ctx, mod3, ctx_row, layer, n_ctx, tm_ctx)

    layer = 1
    mod3 = mod[layer]
    hw = hgrn_w_in[0]
    w_act = jnp.concatenate([hw[:, :d], hw[:, 3 * d:]], axis=1).astype(BF16)
    w_gate = hw[:, d:3 * d].astype(BF16)
    wrap = lambda t, n: t.reshape(b, n, -1)
    act_c, k_c, lf_c = _hgrn_inproj(h_ctx, mod3, ctx_row, w_act, w_gate, hgrn_lower_bounds,
                                    layer=layer, tm=tm_ctx, tn=512)
    act_l, k_l, lf_l = _hgrn_inproj(h_lat, mod3, lat_row, w_act, w_gate, hgrn_lower_bounds,
                                    layer=layer, tm=tm_lat, tn=512)
    zero_state = jnp.zeros((b, n_hgrn_heads, 2, HEAD_DIM, HEAD_DIM), F32)
    (s_ctx,) = _gla(wrap(act_c, n_ctx), wrap(k_c, n_ctx), wrap(lf_c, n_ctx), zero_state, hgrn_norm[0],
                    n_heads=n_hgrn_heads, emit_y=False, emit_state=True)
    (y_lat,) = _gla(wrap(act_l, n_lat), wrap(k_l, n_lat), wrap(lf_l, n_lat), s_ctx, hgrn_norm[0],
                    n_heads=n_hgrn_heads, emit_y=True, emit_state=False)
    h_lat = _proj_residual([y_lat.reshape(b * n_lat, d)], [hgrn_w_out[0].astype(BF16)], h_lat, mod3, 2,
                           lat_row, ln_mix_g[layer], ln_mix_b[layer], tm=tm_lat)
    h_lat = ffn(h_lat, mod3, lat_row, layer, n_lat, tm_lat)
    return h_lat.reshape(b, n_lat, d)
```

```python
import functools
import math

import numpy as np
import jax
import jax.numpy as jnp
from jax import lax
from jax.experimental import pallas as pl
from jax.experimental.pallas import tpu as pltpu

F32 = jnp.float32
BF16 = jnp.bfloat16

LANES = 128
SUBLANES = 8
MXU_DIM = 256
VMEM_LIMIT_BYTES = 56 * 1024 * 1024

GRID_W = 64
DIFF_QK_DIM = 64
HEAD_DIM = 128
ROPE_AXIS_DIM = DIFF_QK_DIM // 2
ROPE_BASE = 10000.0
CONV_W = 3
LN_EPS = 1e-6
RMS_EPS = 1e-5
DEPTH = 2
ALPHA = (2.0 * DEPTH) ** 0.25
QK_SCALE = DIFF_QK_DIM ** -0.5 * math.log2(math.e)
ATTN_SAFE_LOG2 = 60.0

GLA_CHUNK = 128
GLA_LEVELS = (64, 32, 16)
GLA_BLOCK = 16
GLA_SAFE_RANGE = 60.0
GLA_UNROLL = 4


def _params(*dims):
    return pltpu.CompilerParams(dimension_semantics=dims, vmem_limit_bytes=VMEM_LIMIT_BYTES)


def _layer_norm_rows(x):
    mu = jnp.mean(x, axis=-1, keepdims=True)
    xc = x - mu
    var = jnp.mean(xc * xc, axis=-1, keepdims=True)
    return xc * lax.rsqrt(var + LN_EPS)


def _silu(x):
    return x * jax.nn.sigmoid(x)


def _mod_kernel(c_ref, w_ref, b_ref, o_ref):
    x = _silu(c_ref[...]).astype(BF16)
    o_ref[...] = jnp.dot(x, w_ref[...].astype(BF16), preferred_element_type=F32) + b_ref[...]


def _modulation(cc, mod_w, mod_b, tn=1024):
    depth, d, n = mod_w.shape
    rows = cc.shape[0]
    return pl.pallas_call(
        _mod_kernel,
        grid=(depth, n // tn),
        in_specs=[
            pl.BlockSpec((rows, d), lambda l, j: (0, 0)),
            pl.BlockSpec((None, d, tn), lambda l, j: (l, 0, j)),
            pl.BlockSpec((None, 1, tn), lambda l, j: (l, 0, j)),
        ],
        out_specs=pl.BlockSpec((None, rows, tn), lambda l, j: (l, 0, j)),
        out_shape=jax.ShapeDtypeStruct((depth, rows, n), F32),
        compiler_params=_params("arbitrary", "arbitrary"),
        name="modulation",
    )(cc, mod_w, mod_b.reshape(depth, 1, n))


def _mod_spec(chunk, d, row_of_tile):
    return pl.BlockSpec((None, 1, d), lambda i, *_: (row_of_tile(i), 0, chunk))


def _even_inproj_kernel(h_ref, sh_ref, sc_ref, w_ref, cos_ref, sin_ref, o_ref, u_ref,
                        *, n_q_tiles, n_qk_tiles, use_rope):
    j = pl.program_id(1)

    @pl.when(j == 0)
    def _():
        u_ref[...] = (_layer_norm_rows(h_ref[...]) * (1.0 + sc_ref[...]) + sh_ref[...]).astype(BF16)

    acc = jnp.dot(u_ref[...], w_ref[...], preferred_element_type=F32)

    @pl.when(j < n_qk_tiles)
    def _():
        y = acc
        if use_rope:
            parts = []
            for hh in range(y.shape[1] // HEAD_DIM):
                yh = y[:, hh * HEAD_DIM:(hh + 1) * HEAD_DIM]
                parts.append(yh * cos_ref[...] + pltpu.roll(yh, HEAD_DIM // 2, axis=1) * sin_ref[...])
            y = jnp.concatenate(parts, axis=1)
        scale = jnp.where(j < n_q_tiles, QK_SCALE, 1.0)
        o_ref[...] = (y * scale).astype(BF16)

    @pl.when(j >= n_qk_tiles)
    def _():
        o_ref[...] = acc.astype(BF16)


def _even_inproj(h, mod3, row_of_tile, w, cos_t, sin_t, *, qk_width, use_rope, tm, tn):
    m, d = h.shape
    n = w.shape[1]
    n_pos_tiles = cos_t.shape[0] // tm if use_rope else 1
    tab_rows = tm if use_rope else cos_t.shape[0]
    kern = functools.partial(_even_inproj_kernel, n_q_tiles=qk_width // tn,
                             n_qk_tiles=2 * qk_width // tn, use_rope=use_rope)
    return pl.pallas_call(
        kern,
        grid=(m // tm, n // tn),
        in_specs=[
            pl.BlockSpec((tm, d), lambda i, j: (i, 0)),
            _mod_spec(0, d, row_of_tile),
            _mod_spec(1, d, row_of_tile),
            pl.BlockSpec((d, tn), lambda i, j: (0, j)),
            pl.BlockSpec((tab_rows, HEAD_DIM), lambda i, j: (i % n_pos_tiles, 0)),
            pl.BlockSpec((tab_rows, HEAD_DIM), lambda i, j: (i % n_pos_tiles, 0)),
        ],
        out_specs=pl.BlockSpec((tm, tn), lambda i, j: (i, j)),
        out_shape=jax.ShapeDtypeStruct((m, n), BF16),
        scratch_shapes=[pltpu.VMEM((tm, d), BF16)],
        compiler_params=_params("arbitrary", "arbitrary"),
        name="even_inproj",
    )(h, mod3, mod3, w, cos_t, sin_t)


def _diff_attn_kernel(*refs, n_seg, lam_init):
    q_ref = refs[0]
    k_refs = refs[1:1 + n_seg]
    v_refs = refs[1 + n_seg:1 + 2 * n_seg]
    lam_ref, subln_ref, o_ref, knorm_ref = refs[1 + 2 * n_seg:]

    @pl.when(pl.program_id(2) == 0)
    def _():
        worst = None
        for k_ref in k_refs:
            k32 = k_ref[...].astype(F32)
            m = jnp.max(jnp.sum(k32 * k32, axis=1, keepdims=True), axis=0, keepdims=True)
            worst = m if worst is None else jnp.maximum(worst, m)
        knorm_ref[...] = jnp.broadcast_to(worst, knorm_ref.shape)

    lv = lam_ref[...]
    lam = (jnp.exp(jnp.sum(lv[0:1] * lv[1:2], axis=1, keepdims=True))
           - jnp.exp(jnp.sum(lv[2:3] * lv[3:4], axis=1, keepdims=True)) + lam_init)

    q = q_ref[...]
    lane = lax.broadcasted_iota(jnp.int32, q.shape, 1)
    first_map = (lane % DIFF_QK_DIM) < ROPE_AXIS_DIM
    zero = jnp.zeros_like(q)
    qs = (jnp.where(first_map, q, zero), jnp.where(first_map, zero, q))

    dn = (((1,), (1,)), ((), ()))
    q32 = q.astype(F32)
    qnorm = jnp.max(jnp.sum(q32 * q32, axis=1, keepdims=True), axis=0, keepdims=True)
    small_scores = jnp.max(qnorm * knorm_ref[0:1, 0:1]) <= ATTN_SAFE_LOG2 ** 2

    def softmax_times_v(shift_by_max):
        out = []
        for qm in qs:
            s = [lax.dot_general(qm, k_ref[...], dn, preferred_element_type=F32) for k_ref in k_refs]
            if shift_by_max:
                mx = functools.reduce(jnp.maximum, [jnp.max(x, axis=1, keepdims=True) for x in s])
                s = [x - mx for x in s]
            num = den = None
            for seg in range(n_seg):
                e = jnp.exp2(s[seg])
                d = jnp.sum(e, axis=1, keepdims=True)
                part = jnp.dot(e.astype(BF16), v_refs[seg][...], preferred_element_type=F32)
                den = d if den is None else den + d
                num = part if num is None else num + part
            out += [num, den]
        return tuple(out)

    n1, d1, n2, d2 = lax.cond(small_scores, lambda: softmax_times_v(False), lambda: softmax_times_v(True))
    acc = n1 * (1.0 / d1) - n2 * (lam / d2)
    ms = jnp.mean(acc * acc, axis=1, keepdims=True)
    y = acc * lax.rsqrt(ms + RMS_EPS) * subln_ref[...] * (1.0 - lam_init)
    o_ref[...] = y.astype(BF16)


def _diff_attention(q_src, kv_srcs, lam_vecs, subln, lam_init, *, n_heads, tq):
    b, nq, _ = q_src.shape
    n_seg = len(kv_srcs)
    k_specs = [pl.BlockSpec((None, s.shape[1], HEAD_DIM), lambda bi, h, t: (bi, 0, n_heads + h))
               for s in kv_srcs]
    v_specs = [pl.BlockSpec((None, s.shape[1], HEAD_DIM), lambda bi, h, t: (bi, 0, 2 * n_heads + h))
               for s in kv_srcs]
    return pl.pallas_call(
        functools.partial(_diff_attn_kernel, n_seg=n_seg, lam_init=lam_init),
        grid=(b, n_heads, nq // tq),
        in_specs=[pl.BlockSpec((None, tq, HEAD_DIM), lambda bi, h, t: (bi, t, h))] + k_specs + v_specs + [
            pl.BlockSpec(lam_vecs.shape, lambda bi, h, t: (0, 0)),
            pl.BlockSpec((1, HEAD_DIM), lambda bi, h, t: (0, 0)),
        ],
        out_specs=pl.BlockSpec((None, tq, HEAD_DIM), lambda bi, h, t: (bi, t, h)),
        out_shape=jax.ShapeDtypeStruct((b, nq, n_heads * HEAD_DIM), BF16),
        scratch_shapes=[pltpu.VMEM((SUBLANES, LANES), F32)],
        compiler_params=_params("arbitrary", "arbitrary", "arbitrary"),
        name="diff_attention",
    )(q_src, *kv_srcs, *kv_srcs, lam_vecs, subln.reshape(1, HEAD_DIM))


def _fourier_kernel(x_ref, dn_ref, cs_ref, o_ref, z_ref):
    n = x_ref.shape[0]

    @pl.when(pl.program_id(1) == 0)
    def _():
        for g in range(x_ref.shape[1] // HEAD_DIM):
            cols = slice(g * HEAD_DIM, (g + 1) * HEAD_DIM)
            zc = jnp.dot(x_ref[:, cols], cs_ref[...], preferred_element_type=F32)
            z_ref[0:n, cols] = zc[:, :HEAD_DIM].astype(BF16)
            z_ref[n:2 * n, cols] = zc[:, HEAD_DIM:].astype(BF16)

    o_ref[...] = jnp.dot(dn_ref[...], z_ref[...], preferred_element_type=F32).astype(BF16)


def _dft_tables(n):
    j = np.arange(n, dtype=np.int64)
    ang = 2.0 * np.pi * ((j[:, None] * j[None, :]) % n).astype(np.float64) / n
    return np.cos(ang) / math.sqrt(n), np.sin(ang) / math.sqrt(n)


def _fourier_mix(src, col_block, width, *, tm):
    b, n, _ = src.shape
    cn, sn = _dft_tables(n)
    cc, sc = _dft_tables(HEAD_DIM)
    dn = jnp.asarray(np.concatenate([cn, -sn], axis=1), dtype=BF16)
    cs = jnp.asarray(np.concatenate([cc, sc], axis=1), dtype=BF16)
    return pl.pallas_call(
        _fourier_kernel,
        grid=(b, n // tm),
        in_specs=[
            pl.BlockSpec((None, n, width), lambda bi, t: (bi, 0, col_block)),
            pl.BlockSpec((tm, 2 * n), lambda bi, t: (t, 0)),
            pl.BlockSpec((HEAD_DIM, 2 * HEAD_DIM), lambda bi, t: (0, 0)),
        ],
        out_specs=pl.BlockSpec((None, tm, width), lambda bi, t: (bi, t, 0)),
        out_shape=jax.ShapeDtypeStruct((b, n, width), BF16),
        scratch_shapes=[pltpu.VMEM((2 * n, width), BF16)],
        compiler_params=_params("arbitrary", "arbitrary"),
        name="fourier_mix",
    )(src, dn, cs)


def _proj_residual_kernel(*refs, n_in, n_k):
    a_refs = refs[:n_in]
    w_refs = refs[n_in:2 * n_in]
    h_ref, gate_ref, g_ref, b_ref, o_ref = refs[2 * n_in:2 * n_in + 5]
    acc_ref = refs[-1] if n_k > 1 else None
    k = pl.program_id(1)

    part = None
    for a_ref, w_ref in zip(a_refs, w_refs):
        p = jnp.dot(a_ref[...], w_ref[...], preferred_element_type=F32)
        part = p if part is None else part + p

    def finish(total):
        x = ALPHA * h_ref[...] + gate_ref[...] * total
        o_ref[...] = _layer_norm_rows(x) * g_ref[...] + b_ref[...]

    if n_k == 1:
        finish(part)
    else:
        @pl.when(k == 0)
        def _():
            acc_ref[...] = part

        @pl.when(jnp.logical_and(k > 0, k < n_k - 1))
        def _():
            acc_ref[...] += part

        @pl.when(k == n_k - 1)
        def _():
            finish(acc_ref[...] + part)


def _proj_residual(a_list, w_list, h, mod3, gate_chunk, row_of_tile, ln_g, ln_b, *, tm, n_k=1):
    m, d = h.shape
    in_specs = []
    for a in a_list:
        in_specs.append(pl.BlockSpec((tm, a.shape[1] // n_k), lambda i, k: (i, k)))
    for w in w_list:
        in_specs.append(pl.BlockSpec((w.shape[0] // n_k, d), lambda i, k: (k, 0)))
    in_specs += [
        pl.BlockSpec((tm, d), lambda i, k: (i, 0)),
        _mod_spec(gate_chunk, d, row_of_tile),
        pl.BlockSpec((1, d), lambda i, k: (0, 0)),
        pl.BlockSpec((1, d), lambda i, k: (0, 0)),
    ]
    scratch = [pltpu.VMEM((tm, d), F32)] if n_k > 1 else []
    return pl.pallas_call(
        functools.partial(_proj_residual_kernel, n_in=len(a_list), n_k=n_k),
        grid=(m // tm, n_k),
        in_specs=in_specs,
        out_specs=pl.BlockSpec((tm, d), lambda i, k: (i, 0)),
        out_shape=jax.ShapeDtypeStruct((m, d), F32),
        scratch_shapes=scratch,
        compiler_params=_params("arbitrary", "arbitrary"),
        name="proj_residual",
    )(*a_list, *w_list, h, mod3, ln_g.reshape(1, d), ln_b.reshape(1, d))


def _ffn_up_kernel(h_ref, hp_ref, hn_ref, sh_ref, sc_ref, wa_ref, wv_ref, cw_ref, cb_ref, o_ref,
                   u_ref, uh_ref, *, tiles_per_seq):
    i = pl.program_id(0)
    j = pl.program_id(1)
    tm = h_ref.shape[0]

    @pl.when(j == 0)
    def _():
        scale = 1.0 + sc_ref[...]
        u_ref[...] = (_layer_norm_rows(h_ref[...]) * scale + sh_ref[...]).astype(BF16)
        uh_ref[0:SUBLANES, :] = (_layer_norm_rows(hp_ref[...]) * scale + sh_ref[...]).astype(BF16)
        uh_ref[SUBLANES:, :] = (_layer_norm_rows(hn_ref[...]) * scale + sh_ref[...]).astype(BF16)

    a = jnp.dot(u_ref[...], wa_ref[...], preferred_element_type=F32)
    v = jnp.dot(u_ref[...], wv_ref[...], preferred_element_type=F32)
    halo = jnp.dot(uh_ref[...], wa_ref[...], preferred_element_type=F32)
    t_in_seq = i % tiles_per_seq
    prev_row = jnp.where(t_in_seq > 0, halo[SUBLANES - 1:SUBLANES], 0.0)
    next_row = jnp.where(t_in_seq < tiles_per_seq - 1, halo[SUBLANES:SUBLANES + 1], 0.0)
    row = lax.broadcasted_iota(jnp.int32, a.shape, 0)
    a_prev = jnp.where(row == 0, prev_row, pltpu.roll(a, 1, axis=0))
    a_next = jnp.where(row == tm - 1, next_row, pltpu.roll(a, tm - 1, axis=0))
    cw = cw_ref[...]
    conv = cb_ref[...] + a_prev * cw[0:1] + a * cw[1:2] + a_next * cw[2:3]
    gelu = 0.5 * conv * (1.0 + lax.erf(conv * math.sqrt(0.5)))
    o_ref[...] = (gelu * v).astype(BF16)


def _ffn_up(h, mod3, row_of_tile, wa, wv, conv_w, conv_b, *, seq_len, tm, tn):
    m, d = h.shape
    n = wa.shape[1]
    tiles_per_seq = seq_len // tm
    blocks_per_tile = tm // SUBLANES
    n_blocks = m // SUBLANES
    kern = functools.partial(_ffn_up_kernel, tiles_per_seq=tiles_per_seq)
    return pl.pallas_call(
        kern,
        grid=(m // tm, n // tn),
        in_specs=[
            pl.BlockSpec((tm, d), lambda i, j: (i, 0)),
            pl.BlockSpec((SUBLANES, d), lambda i, j: (jnp.maximum(i * blocks_per_tile - 1, 0), 0)),
            pl.BlockSpec((SUBLANES, d), lambda i, j: (jnp.minimum((i + 1) * blocks_per_tile, n_blocks - 1), 0)),
            _mod_spec(3, d, row_of_tile),
            _mod_spec(4, d, row_of_tile),
            pl.BlockSpec((d, tn), lambda i, j: (0, j)),
            pl.BlockSpec((d, tn), lambda i, j: (0, j)),
            pl.BlockSpec((CONV_W, tn), lambda i, j: (0, j)),
            pl.BlockSpec((1, tn), lambda i, j: (0, j)),
        ],
        out_specs=pl.BlockSpec((tm, tn), lambda i, j: (i, j)),
        out_shape=jax.ShapeDtypeStruct((m, n), BF16),
        scratch_shapes=[pltpu.VMEM((tm, d), BF16), pltpu.VMEM((2 * SUBLANES, d), BF16)],
        compiler_params=_params("arbitrary", "arbitrary"),
        name="ffn_up",
    )(h, h, h, mod3, mod3, wa, wv, conv_w, conv_b)


def _hgrn_act_kernel(h_ref, sh_ref, sc_ref, w_ref, o_ref, u_ref, *, lin_lo, lin_hi):
    j = pl.program_id(1)

    @pl.when(j == 0)
    def _():
        u_ref[...] = (_layer_norm_rows(h_ref[...]) * (1.0 + sc_ref[...]) + sh_ref[...]).astype(BF16)

    acc = jnp.dot(u_ref[...], w_ref[...], preferred_element_type=F32)
    linear = jnp.logical_and(j >= lin_lo, j < lin_hi)

    @pl.when(linear)
    def _():
        o_ref[...] = acc.astype(BF16)

    @pl.when(jnp.logical_not(linear))
    def _():
        o_ref[...] = _silu(acc).astype(BF16)


def _hgrn_gate_kernel(h_ref, sh_ref, sc_ref, w_ref, lbp_ref, k_ref, lf_ref, u_ref, *, layer):
    j = pl.program_id(1)

    @pl.when(j == 0)
    def _():
        u_ref[...] = (_layer_norm_rows(h_ref[...]) * (1.0 + sc_ref[...]) + sh_ref[...]).astype(BF16)

    x = lbp_ref[...]
    e = jnp.exp(x - jnp.max(x, axis=0, keepdims=True))
    lb = jnp.sum(e[1:layer + 1], axis=0, keepdims=True) / jnp.sum(e, axis=0, keepdims=True)

    f_pre = jnp.dot(u_ref[...], w_ref[...], preferred_element_type=F32)
    k_ref[...] = ((1.0 - lb) * jax.nn.sigmoid(-f_pre)).astype(BF16)
    lf_ref[...] = jnp.log(lb + (1.0 - lb) * jax.nn.sigmoid(f_pre))


def _hgrn_inproj(h, mod3, row_of_tile, w_act, w_gate, lb_params, *, layer, tm, tn):
    m, d = h.shape
    common = [
        pl.BlockSpec((tm, d), lambda i, j: (i, 0)),
        _mod_spec(0, d, row_of_tile),
        _mod_spec(1, d, row_of_tile),
        pl.BlockSpec((d, tn), lambda i, j: (0, j)),
    ]
    tiles = d // tn
    act = pl.pallas_call(
        functools.partial(_hgrn_act_kernel, lin_lo=tiles, lin_hi=2 * tiles),
        grid=(m // tm, w_act.shape[1] // tn),
        in_specs=common,
        out_specs=pl.BlockSpec((tm, tn), lambda i, j: (i, j)),
        out_shape=jax.ShapeDtypeStruct((m, w_act.shape[1]), BF16),
        scratch_shapes=[pltpu.VMEM((tm, d), BF16)],
        compiler_params=_params("arbitrary", "arbitrary"),
        name="hgrn_inproj_act",
    )(h, mod3, mod3, w_act)
    k, logf = pl.pallas_call(
        functools.partial(_hgrn_gate_kernel, layer=layer),
        grid=(m // tm, w_gate.shape[1] // tn),
        in_specs=common + [pl.BlockSpec((None, DEPTH, tn), lambda i, j: (j // tiles, 0, j % tiles))],
        out_specs=[pl.BlockSpec((tm, tn), lambda i, j: (i, j))] * 2,
        out_shape=[jax.ShapeDtypeStruct((m, w_gate.shape[1]), BF16),
                   jax.ShapeDtypeStruct((m, w_gate.shape[1]), F32)],
        scratch_shapes=[pltpu.VMEM((tm, d), BF16)],
        compiler_params=_params("arbitrary", "arbitrary"),
        name="hgrn_inproj_gate",
    )(h, mod3, mod3, w_gate, lb_params)
    return act, k, logf


def _gla_matrices(reverse):
    c = GLA_CHUNK
    r_i = lax.broadcasted_iota(jnp.int32, (c, c), 0)
    c_i = lax.broadcasted_iota(jnp.int32, (c, c), 1)
    same = jnp.bitwise_xor(r_i, c_i)
    earlier = (c_i >= r_i) if reverse else (c_i <= r_i)
    in_block = same < GLA_BLOCK
    one = lambda m: jnp.where(m, 1.0, 0.0).astype(BF16)
    return dict(tri=one(earlier), tri_blk=one(jnp.logical_and(earlier, in_block)), ones_blk=one(in_block),
                diag_mask=jnp.logical_and(earlier, in_block), same=same)


def _gla_direct_block_terms(q, k, v, cum, bad, *, reverse):
    c = GLA_CHUNK
    row = lax.broadcasted_iota(jnp.int32, (c, HEAD_DIM), 0) % GLA_BLOCK
    qb = jnp.where(bad, q, 0.0)
    acc0 = jnp.sum(qb * k, axis=1, keepdims=True) * v

    def offset(dlt, acc):
        shift = (c - dlt) if reverse else dlt
        valid = (row <= GLA_BLOCK - 1 - dlt) if reverse else (row >= dlt)
        k_d = pltpu.roll(k, shift, axis=0)
        v_d = pltpu.roll(v, shift, axis=0)
        cum_d = pltpu.roll(cum, shift, axis=0)
        decay = jnp.exp(jnp.where(valid, cum - cum_d, -jnp.inf))
        return acc + jnp.sum(qb * k_d * decay, axis=1, keepdims=True) * v_d

    return lax.fori_loop(1, GLA_BLOCK, offset, acc0)


def _gla_chunk(q, k, lf, v_bf, cum_ref, mats, *, reverse, robust, want_o):
    c = GLA_CHUNK
    dn_t = (((1,), (1,)), ((), ()))
    lf_hi = lf.astype(BF16)
    lf_lo = (lf - lf_hi.astype(F32)).astype(BF16)
    two_term = lambda m: (jnp.dot(m, lf_hi, preferred_element_type=F32)
                          + jnp.dot(m, lf_lo, preferred_element_type=F32))
    cum = two_term(mats["tri"])
    w_blk = two_term(mats["tri_blk"])
    cum_ref[...] = cum
    end_row = 0 if reverse else c - 1
    cum_end = cum_ref[end_row:end_row + 1, :]

    qe = (q * jnp.exp(cum)).astype(BF16)
    k_dec = (k * jnp.exp(cum_end - cum)).astype(BF16)
    u = lax.dot_general(v_bf, k_dec, (((0,), (0,)), ((), ())), preferred_element_type=F32)
    dec_end = jnp.exp(cum_end)
    if not want_o:
        return None, qe, u, dec_end

    if robust:
        blk_tot = jnp.dot(mats["ones_blk"], lf_hi, preferred_element_type=F32)
        bad = blk_tot < -GLA_SAFE_RANGE
        a_d = jnp.where(bad, 0.0, q * jnp.exp(w_blk)).astype(BF16)
        b_d = (k * jnp.exp(jnp.minimum(-w_blk, GLA_SAFE_RANGE + 20.0))).astype(BF16)
    else:
        a_d = (q * jnp.exp(w_blk)).astype(BF16)
        b_d = (k * jnp.exp(-w_blk)).astype(BF16)
    scores = jnp.where(mats["diag_mask"], lax.dot_general(a_d, b_d, dn_t, preferred_element_type=F32), 0.0)

    zeros = {m: jnp.zeros((m, HEAD_DIM), F32) for m in GLA_LEVELS}
    for m in GLA_LEVELS:
        a_parts, b_parts = [], []
        for blk in range(c // (2 * m)):
            lo, mid, hi = blk * 2 * m, blk * 2 * m + m, (blk + 1) * 2 * m
            if reverse:
                ref_row = cum_ref[mid:mid + 1, :]
                qa = q[lo:mid] * jnp.exp(cum[lo:mid] - ref_row)
                kb = k[mid:hi] * jnp.exp(ref_row - cum[mid:hi])
                a_parts += [qa, zeros[m]]
                b_parts += [zeros[m], kb]
            else:
                ref_row = cum_ref[mid - 1:mid, :]
                qa = q[mid:hi] * jnp.exp(cum[mid:hi] - ref_row)
                kb = k[lo:mid] * jnp.exp(ref_row - cum[lo:mid])
                a_parts += [zeros[m], qa]
                b_parts += [kb, zeros[m]]
        a_m = jnp.concatenate(a_parts, axis=0).astype(BF16)
        b_m = jnp.concatenate(b_parts, axis=0).astype(BF16)
        s_m = lax.dot_general(a_m, b_m, dn_t, preferred_element_type=F32)
        if 2 * m < c:
            s_m = jnp.where(mats["same"] < 2 * m, s_m, 0.0)
        scores = scores + s_m
    o = jnp.dot(scores.astype(BF16), v_bf, preferred_element_type=F32)
    if robust:
        o = o + _gla_direct_block_terms(q, k, v_bf.astype(F32), cum, bad, reverse=reverse)
    return o, qe, u, dec_end


def _gla_kernel(*refs, emit_y, emit_state):
    q_ref, kf_ref, kb_ref, lff_ref, lfb_ref, v_ref, gate_ref, s0_ref, nw_ref = refs[:9]
    pos = 9
    y_ref = s_out_ref = None
    if emit_y:
        y_ref = refs[pos]
        pos += 1
    if emit_state:
        s_out_ref = refs[pos]
        pos += 1
    of_ref, ob_ref, cumf_ref, cumb_ref, qef_ref, qeb_ref, uf_ref, ub_ref, df_ref, db_ref = refs[pos:pos + 10]

    c = GLA_CHUNK
    n = q_ref.shape[0]
    n_chunks = n // c
    dn_t = (((1,), (1,)), ((), ()))

    def local_pass(robust):
        mats_f, mats_b = _gla_matrices(False), _gla_matrices(True)

        def body(ci, _):
            rows = pl.ds(pl.multiple_of(ci * c, c), c)
            q = q_ref[rows, :].astype(F32)
            v_bf = v_ref[rows, :]
            o_f, qe_f, u_f, d_f = _gla_chunk(q, kf_ref[rows, :].astype(F32), lff_ref[rows, :], v_bf,
                                             cumf_ref.at[rows, :], mats_f, reverse=False, robust=robust,
                                             want_o=emit_y)
            o_b, qe_b, u_b, d_b = _gla_chunk(q, kb_ref[rows, :].astype(F32), lfb_ref[rows, :], v_bf,
                                             cumb_ref.at[rows, :], mats_b, reverse=True, robust=robust,
                                             want_o=emit_y)
            uf_ref[ci] = u_f
            ub_ref[ci] = u_b
            df_ref[ci] = d_f
            db_ref[ci] = d_b
            if emit_y:
                of_ref[rows, :] = o_f
                ob_ref[rows, :] = o_b
                qef_ref[rows, :] = qe_f
                qeb_ref[rows, :] = qe_b
            return 0

        lax.fori_loop(0, n_chunks, body, 0, unroll=1 if robust else min(GLA_UNROLL, n_chunks))

    blocks = lambda ref: jnp.sum(ref[...].reshape(n // GLA_BLOCK, GLA_BLOCK, HEAD_DIM), axis=1)
    worst = jnp.minimum(jnp.min(blocks(lff_ref)), jnp.min(blocks(lfb_ref)))
    lax.cond(worst < -GLA_SAFE_RANGE, lambda: local_pass(True), lambda: local_pass(False))

    def state_pass(ci, carry):
        st_f, st_b = carry
        cb = n_chunks - 1 - ci
        if emit_y:
            rows_f = pl.ds(pl.multiple_of(ci * c, c), c)
            rows_b = pl.ds(pl.multiple_of(cb * c, c), c)
            of_ref[rows_f, :] += lax.dot_general(qef_ref[rows_f, :], st_f.astype(BF16), dn_t,
                                                 preferred_element_type=F32)
            ob_ref[rows_b, :] += lax.dot_general(qeb_ref[rows_b, :], st_b.astype(BF16), dn_t,
                                                 preferred_element_type=F32)
        return st_f * df_ref[ci] + uf_ref[ci], st_b * db_ref[cb] + ub_ref[cb]

    st_f, st_b = lax.fori_loop(0, n_chunks, state_pass, (s0_ref[0], s0_ref[1]))

    if emit_state:
        s_out_ref[0] = st_f
        s_out_ref[1] = st_b
    if emit_y:
        def readout(ci, _):
            rows = pl.ds(pl.multiple_of(ci * c, c), c)
            tot = of_ref[rows, :] + ob_ref[rows, :]
            ms = jnp.mean(tot * tot, axis=1, keepdims=True)
            y = tot * lax.rsqrt(ms + RMS_EPS) * nw_ref[...] * gate_ref[rows, :].astype(F32)
            y_ref[rows, :] = y.astype(BF16)
            return 0
        lax.fori_loop(0, n_chunks, readout, 0)


def _gla(act, k, logf, s0, norm_w, *, n_heads, emit_y, emit_state):
    b, n, _ = act.shape
    hd = HEAD_DIM
    col = lambda off: pl.BlockSpec((None, n, hd), lambda bi, h: (bi, 0, off * n_heads + h))
    st_spec = pl.BlockSpec((None, None, 2, hd, hd), lambda bi, h: (bi, h, 0, 0, 0))
    out_specs, out_shape = [], []
    if emit_y:
        out_specs.append(pl.BlockSpec((None, n, hd), lambda bi, h: (bi, 0, h)))
        out_shape.append(jax.ShapeDtypeStruct((b, n, n_heads * hd), BF16))
    if emit_state:
        out_specs.append(st_spec)
        out_shape.append(jax.ShapeDtypeStruct((b, n_heads, 2, hd, hd), F32))
    outs = pl.pallas_call(
        functools.partial(_gla_kernel, emit_y=emit_y, emit_state=emit_state),
        grid=(b, n_heads),
        in_specs=[col(0), col(0), col(1), col(0), col(1), col(1), col(2), st_spec,
                  pl.BlockSpec((1, hd), lambda bi, h: (0, 0))],
        out_specs=out_specs,
        out_shape=out_shape,
        scratch_shapes=(
            [pltpu.VMEM((n, hd), F32)] * 4
            + [pltpu.VMEM((n, hd), BF16)] * 2
            + [pltpu.VMEM((n // GLA_CHUNK, hd, hd), F32)] * 2
            + [pltpu.VMEM((n // GLA_CHUNK, 1, hd), F32)] * 2),
        compiler_params=_params("arbitrary", "arbitrary"),
        name="gla_scan",
    )(act, k, k, logf, logf, act, act, s0, norm_w.reshape(1, hd))
    return outs


def _rope_head_perm():
    half = ROPE_AXIS_DIM // 2
    first = [0, 2 * half, 4 * half, 6 * half]
    order = first + [s + half for s in first]
    return np.concatenate([np.arange(s, s + half) for s in order])


def _rope_tables(n):
    half = ROPE_AXIS_DIM // 2
    inv = (1.0 / (ROPE_BASE ** (np.arange(0, ROPE_AXIS_DIM, 2, dtype=np.float32) / ROPE_AXIS_DIM))).astype(np.float32)
    pos = np.arange(n)
    ang_r = (pos // GRID_W).astype(np.float32)[:, None] * inv[None, :]
    ang_c = (pos % GRID_W).astype(np.float32)[:, None] * inv[None, :]
    ang = np.concatenate([ang_r, ang_c] * 4, axis=1)
    sign = np.concatenate([-np.ones(4 * half, np.float32), np.ones(4 * half, np.float32)])
    return jnp.asarray(np.cos(ang), F32), jnp.asarray(np.sin(ang) * sign[None, :], F32)


def kernel(x, c, ctx, c_ctx, mod_w, mod_b, ln_mix_g, ln_mix_b, ln_ffn_g, ln_ffn_b, even_w_in, even_w_out, diff_lambda, diff_subln, hgrn_w_in, hgrn_w_out, hgrn_lower_bounds, hgrn_norm, ffn_w_up, ffn_conv_w, ffn_conv_b, ffn_w_down):
    b, n_lat, d = x.shape
    n_ctx = ctx.shape[1]
    assert mod_w.shape[0] == DEPTH and n_lat % GRID_W == 0
    d_ff = ffn_w_down.shape[1]
    d_ff_pad = -(-d_ff // (2 * MXU_DIM)) * (2 * MXU_DIM)
    diff_width = even_w_out.shape[2] * 3 // 4
    four_width = even_w_in.shape[2] - 3 * diff_width
    n_diff_heads = diff_width // HEAD_DIM
    n_hgrn_heads = d // HEAD_DIM

    cc = jnp.concatenate([c, c_ctx[None, :], jnp.zeros((2 * SUBLANES - b - 1, d), F32)], axis=0)
    mod = _modulation(cc, mod_w, mod_b)
    mod = mod.reshape(DEPTH, cc.shape[0], 1, 6 * d)

    h_lat = x.reshape(b * n_lat, d)
    h_ctx = ctx.reshape(b * n_ctx, d)

    def ffn(h, mod3, row_of_tile, layer, seq_len, tm):
        pad = d_ff_pad - d_ff
        w_up = ffn_w_up[layer]
        wa = jnp.pad(w_up[:, :d_ff], ((0, 0), (0, pad))).astype(BF16)
        wv = jnp.pad(w_up[:, d_ff:], ((0, 0), (0, pad))).astype(BF16)
        cw = jnp.pad(ffn_conv_w[layer], ((0, 0), (0, pad)))
        cb = jnp.pad(ffn_conv_b[layer], (0, pad)).reshape(1, d_ff_pad)
        wd = jnp.pad(ffn_w_down[layer], ((0, pad), (0, 0))).astype(BF16)
        g = _ffn_up(h, mod3, row_of_tile, wa, wv, cw, cb, seq_len=seq_len, tm=tm, tn=512)
        return _proj_residual([g], [wd], h, mod3, 5, row_of_tile, ln_ffn_g[layer], ln_ffn_b[layer],
                              tm=tm, n_k=d_ff_pad // 512)

    layer = 0
    lam_init = 0.8 - 0.6 * math.exp(-0.3 * layer)
    perm = _rope_head_perm()
    qk_cols = (np.arange(2 * n_diff_heads)[:, None] * HEAD_DIM + perm[None, :]).reshape(-1)
    cols = np.concatenate([qk_cols, np.arange(2 * diff_width, even_w_in.shape[2])])
    w_in = even_w_in[0][:, cols].astype(BF16)
    w_out = even_w_out[0].astype(BF16)
    cos_t, sin_t = _rope_tables(n_lat)
    mod3 = mod[layer]

    tm_lat, tm_ctx = min(512, n_lat), min(256, n_ctx)
    lat_row = lambda i: (i * tm_lat) // n_lat
    ctx_row = lambda i: b

    p_lat = _even_inproj(h_lat, mod3, lat_row, w_in, cos_t, sin_t, qk_width=diff_width,
                         use_rope=True, tm=tm_lat, tn=512).reshape(b, n_lat, -1)
    p_ctx = _even_inproj(h_ctx, mod3, ctx_row, w_in, cos_t[:SUBLANES], sin_t[:SUBLANES], qk_width=diff_width,
                         use_rope=False, tm=tm_ctx, tn=512).reshape(b, n_ctx, -1)

    attn_lat = _diff_attention(p_lat, [p_ctx, p_lat], diff_lambda[0], diff_subln[0], lam_init,
                               n_heads=n_diff_heads, tq=min(512, n_lat))
    attn_ctx = _diff_attention(p_ctx, [p_ctx], diff_lambda[0], diff_subln[0], lam_init,
                               n_heads=n_diff_heads, tq=n_ctx)
    four_block = 3 * diff_width // four_width
    four_lat = _fourier_mix(p_lat, four_block, four_width, tm=tm_lat)
    four_ctx = _fourier_mix(p_ctx, four_block, four_width, tm=n_ctx)

    w_list = [w_out[:diff_width], w_out[diff_width:]]
    h_lat = _proj_residual([attn_lat.reshape(b * n_lat, -1), four_lat.reshape(b * n_lat, -1)], w_list,
                           h_lat, mod3, 2, lat_row, ln_mix_g[layer], ln_mix_b[layer], tm=tm_lat)
    h_ctx = _proj_residual([attn_ctx.reshape(b * n_ctx, -1), four_ctx.reshape(b * n_ctx, -1)], w_list,
                           h_ctx, mod3, 2, ctx_row, ln_mix_g[layer], ln_mix_b[layer], tm=tm_ctx)
    h_lat = ffn(h_lat, mod3, lat_row, layer, n_lat, tm_lat)
    h_ctx = ffn(h_ctx, mod3, ctx_row, layer, n_ctx, tm_ctx)

    layer = 1
    mod3 = mod[layer]
    hw = hgrn_w_in[0]
    w_act = jnp.concatenate([hw[:, :d], hw[:, 3 * d:]], axis=1).astype(BF16)
    w_gate = hw[:, d:3 * d].astype(BF16)
    wrap = lambda t, n: t.reshape(b, n, -1)
    act_c, k_c, lf_c = _hgrn_inproj(h_ctx, mod3, ctx_row, w_act, w_gate, hgrn_lower_bounds,
                                    layer=layer, tm=tm_ctx, tn=512)
    act_l, k_l, lf_l = _hgrn_inproj(h_lat, mod3, lat_row, w_act, w_gate, hgrn_lower_bounds,
                                    layer=layer, tm=tm_lat, tn=512)
    zero_state = jnp.zeros((b, n_hgrn_heads, 2, HEAD_DIM, HEAD_DIM), F32)
    (s_ctx,) = _gla(wrap(act_c, n_ctx), wrap(k_c, n_ctx), wrap(lf_c, n_ctx), zero_state, hgrn_norm[0],
                    n_heads=n_hgrn_heads, emit_y=False, emit_state=True)
    (y_lat,) = _gla(wrap(act_l, n_lat), wrap(k_l, n_lat), wrap(lf_l, n_lat), s_ctx, hgrn_norm[0],
                    n_heads=n_hgrn_heads, emit_y=True, emit_state=False)
    h_lat = _proj_residual([y_lat.reshape(b * n_lat, d)], [hgrn_w_out[0].astype(BF16)], h_lat, mod3, 2,
                           lat_row, ln_mix_g[layer], ln_mix_b[layer], tm=tm_lat)
    h_lat = ffn(h_lat, mod3, lat_row, layer, n_lat, tm_lat)
    return h_lat.reshape(b, n_lat, d)
```

```python
import functools
import math

import numpy as np
import jax
import jax.numpy as jnp
from jax import lax
from jax.experimental import pallas as pl
from jax.experimental.pallas import tpu as pltpu

F32 = jnp.float32
BF16 = jnp.bfloat16

LANES = 128
SUBLANES = 8
MXU_DIM = 256
VMEM_LIMIT_BYTES = 56 * 1024 * 1024

GRID_W = 64
DIFF_QK_DIM = 64
HEAD_DIM = 128
ROPE_AXIS_DIM = DIFF_QK_DIM // 2
ROPE_BASE = 10000.0
CONV_W = 3
LN_EPS = 1e-6
RMS_EPS = 1e-5
DEPTH = 2
ALPHA = (2.0 * DEPTH) ** 0.25
QK_SCALE = DIFF_QK_DIM ** -0.5 * math.log2(math.e)
ATTN_SAFE_LOG2 = 60.0

GLA_CHUNK = 128
GLA_LEVELS = (64, 32, 16)
GLA_BLOCK = 16
GLA_SAFE_RANGE = 60.0
GLA_UNROLL = 4


def _params(*dims):
    return pltpu.CompilerParams(dimension_semantics=dims, vmem_limit_bytes=VMEM_LIMIT_BYTES)


def _layer_norm_rows(x):
    mu = jnp.mean(x, axis=-1, keepdims=True)
    xc = x - mu
    var = jnp.mean(xc * xc, axis=-1, keepdims=True)
    return xc * lax.rsqrt(var + LN_EPS)


def _silu(x):
    return x * jax.nn.sigmoid(x)


def _mod_kernel(c_ref, w_ref, b_ref, o_ref):
    x = _silu(c_ref[...]).astype(BF16)
    o_ref[...] = jnp.dot(x, w_ref[...].astype(BF16), preferred_element_type=F32) + b_ref[...]


def _modulation(cc, mod_w, mod_b, tn=1024):
    depth, d, n = mod_w.shape
    rows = cc.shape[0]
    return pl.pallas_call(
        _mod_kernel,
        grid=(depth, n // tn),
        in_specs=[
            pl.BlockSpec((rows, d), lambda l, j: (0, 0)),
            pl.BlockSpec((None, d, tn), lambda l, j: (l, 0, j)),
            pl.BlockSpec((None, 1, tn), lambda l, j: (l, 0, j)),
        ],
        out_specs=pl.BlockSpec((None, rows, tn), lambda l, j: (l, 0, j)),
        out_shape=jax.ShapeDtypeStruct((depth, rows, n), F32),
        compiler_params=_params("arbitrary", "arbitrary"),
        name="modulation",
    )(cc, mod_w, mod_b.reshape(depth, 1, n))


def _mod_spec(chunk, d, row_of_tile):
    return pl.BlockSpec((None, 1, d), lambda i, *_: (row_of_tile(i), 0, chunk))


def _even_inproj_kernel(h_ref, sh_ref, sc_ref, w_ref, cos_ref, sin_ref, o_ref, u_ref,
                        *, n_q_tiles, n_qk_tiles, use_rope):
    j = pl.program_id(1)

    @pl.when(j == 0)
    def _():
        u_ref[...] = (_layer_norm_rows(h_ref[...]) * (1.0 + sc_ref[...]) + sh_ref[...]).astype(BF16)

    acc = jnp.dot(u_ref[...], w_ref[...], preferred_element_type=F32)

    @pl.when(j < n_qk_tiles)
    def _():
        y = acc
        if use_rope:
            parts = []
            for hh in range(y.shape[1] // HEAD_DIM):
                yh = y[:, hh * HEAD_DIM:(hh + 1) * HEAD_DIM]
                parts.append(yh * cos_ref[...] + pltpu.roll(yh, HEAD_DIM // 2, axis=1) * sin_ref[...])
            y = jnp.concatenate(parts, axis=1)
        scale = jnp.where(j < n_q_tiles, QK_SCALE, 1.0)
        o_ref[...] = (y * scale).astype(BF16)

    @pl.when(j >= n_qk_tiles)
    def _():
        o_ref[...] = acc.astype(BF16)


def _even_inproj(h, mod3, row_of_tile, w, cos_t, sin_t, *, qk_width, use_rope, tm, tn):
    m, d = h.shape
    n = w.shape[1]
    n_pos_tiles = cos_t.shape[0] // tm if use_rope else 1
    tab_rows = tm if use_rope else cos_t.shape[0]
    kern = functools.partial(_even_inproj_kernel, n_q_tiles=qk_width // tn,
                             n_qk_tiles=2 * qk_width // tn, use_rope=use_rope)
    return pl.pallas_call(
        kern,
        grid=(m // tm, n // tn),
        in_specs=[
            pl.BlockSpec((tm, d), lambda i, j: (i, 0)),
            _mod_spec(0, d, row_of_tile),
            _mod_spec(1, d, row_of_tile),
            pl.BlockSpec((d, tn), lambda i, j: (0, j)),
            pl.BlockSpec((tab_rows, HEAD_DIM), lambda i, j: (i % n_pos_tiles, 0)),
            pl.BlockSpec((tab_rows, HEAD_DIM), lambda i, j: (i % n_pos_tiles, 0)),
        ],
        out_specs=pl.BlockSpec((tm, tn), lambda i, j: (i, j)),
        out_shape=jax.ShapeDtypeStruct((m, n), BF16),
        scratch_shapes=[pltpu.VMEM((tm, d), BF16)],
        compiler_params=_params("arbitrary", "arbitrary"),
        name="even_inproj",
    )(h, mod3, mod3, w, cos_t, sin_t)


def _diff_attn_kernel(*refs, n_seg, lam_init):
    q_ref = refs[0]
    k_refs = refs[1:1 + n_seg]
    v_refs = refs[1 + n_seg:1 + 2 * n_seg]
    lam_ref, subln_ref, o_ref, knorm_ref = refs[1 + 2 * n_seg:]

    @pl.when(pl.program_id(2) == 0)
    def _():
        worst = None
        for k_ref in k_refs:
            k32 = k_ref[...].astype(F32)
            m = jnp.max(jnp.sum(k32 * k32, axis=1, keepdims=True), axis=0, keepdims=True)
            worst = m if worst is None else jnp.maximum(worst, m)
        knorm_ref[...] = jnp.broadcast_to(worst, knorm_ref.shape)

    lv = lam_ref[...]
    lam = (jnp.exp(jnp.sum(lv[0:1] * lv[1:2], axis=1, keepdims=True))
           - jnp.exp(jnp.sum(lv[2:3] * lv[3:4], axis=1, keepdims=True)) + lam_init)

    q = q_ref[...]
    lane = lax.broadcasted_iota(jnp.int32, q.shape, 1)
    first_map = (lane % DIFF_QK_DIM) < ROPE_AXIS_DIM
    zero = jnp.zeros_like(q)
    qs = (jnp.where(first_map, q, zero), jnp.where(first_map, zero, q))

    dn = (((1,), (1,)), ((), ()))
    q32 = q.astype(F32)
    qnorm = jnp.max(jnp.sum(q32 * q32, axis=1, keepdims=True), axis=0, keepdims=True)
    small_scores = jnp.max(qnorm * knorm_ref[0:1, 0:1]) <= ATTN_SAFE_LOG2 ** 2

    def softmax_times_v(shift_by_max):
        out = []
        for qm in qs:
            s = [lax.dot_general(qm, k_ref[...], dn, preferred_element_type=F32) for k_ref in k_refs]
            if shift_by_max:
                mx = functools.reduce(jnp.maximum, [jnp.max(x, axis=1, keepdims=True) for x in s])
                s = [x - mx for x in s]
            num = den = None
            for seg in range(n_seg):
                e = jnp.exp2(s[seg])
                d = jnp.sum(e, axis=1, keepdims=True)
                part = jnp.dot(e.astype(BF16), v_refs[seg][...], preferred_element_type=F32)
                den = d if den is None else den + d
                num = part if num is None else num + part
            out += [num, den]
        return tuple(out)

    n1, d1, n2, d2 = lax.cond(small_scores, lambda: softmax_times_v(False), lambda: softmax_times_v(True))
    acc = n1 * (1.0 / d1) - n2 * (lam / d2)
    ms = jnp.mean(acc * acc, axis=1, keepdims=True)
    y = acc * lax.rsqrt(ms + RMS_EPS) * subln_ref[...] * (1.0 - lam_init)
    o_ref[...] = y.astype(BF16)


def _diff_attention(q_src, kv_srcs, lam_vecs, subln, lam_init, *, n_heads, tq):
    b, nq, _ = q_src.shape
    n_seg = len(kv_srcs)
    k_specs = [pl.BlockSpec((None, s.shape[1], HEAD_DIM), lambda bi, h, t: (bi, 0, n_heads + h))
               for s in kv_srcs]
    v_specs = [pl.BlockSpec((None, s.shape[1], HEAD_DIM), lambda bi, h, t: (bi, 0, 2 * n_heads + h))
               for s in kv_srcs]
    return pl.pallas_call(
        functools.partial(_diff_attn_kernel, n_seg=n_seg, lam_init=lam_init),
        grid=(b, n_heads, nq // tq),
        in_specs=[pl.BlockSpec((None, tq, HEAD_DIM), lambda bi, h, t: (bi, t, h))] + k_specs + v_specs + [
            pl.BlockSpec(lam_vecs.shape, lambda bi, h, t: (0, 0)),
            pl.BlockSpec((1, HEAD_DIM), lambda bi, h, t: (0, 0)),
        ],
        out_specs=pl.BlockSpec((None, tq, HEAD_DIM), lambda bi, h, t: (bi, t, h)),
        out_shape=jax.ShapeDtypeStruct((b, nq, n_heads * HEAD_DIM), BF16),
        scratch_shapes=[pltpu.VMEM((SUBLANES, LANES), F32)],
        compiler_params=_params("arbitrary", "arbitrary", "arbitrary"),
        name="diff_attention",
    )(q_src, *kv_srcs, *kv_srcs, lam_vecs, subln.reshape(1, HEAD_DIM))


def _fourier_kernel(x_ref, dn_ref, cs_ref, o_ref, z_ref):
    n = x_ref.shape[0]

    @pl.when(pl.program_id(1) == 0)
    def _():
        for g in range(x_ref.shape[1] // HEAD_DIM):
            cols = slice(g * HEAD_DIM, (g + 1) * HEAD_DIM)
            zc = jnp.dot(x_ref[:, cols], cs_ref[...], preferred_element_type=F32)
            z_ref[0:n, cols] = zc[:, :HEAD_DIM].astype(BF16)
            z_ref[n:2 * n, cols] = zc[:, HEAD_DIM:].astype(BF16)

    o_ref[...] = jnp.dot(dn_ref[...], z_ref[...], preferred_element_type=F32).astype(BF16)


def _dft_tables(n):
    j = np.arange(n, dtype=np.int64)
    ang = 2.0 * np.pi * ((j[:, None] * j[None, :]) % n).astype(np.float64) / n
    return np.cos(ang) / math.sqrt(n), np.sin(ang) / math.sqrt(n)


def _fourier_mix(src, col_block, width, *, tm):
    b, n, _ = src.shape
    cn, sn = _dft_tables(n)
    cc, sc = _dft_tables(HEAD_DIM)
    dn = jnp.asarray(np.concatenate([cn, -sn], axis=1), dtype=BF16)
    cs = jnp.asarray(np.concatenate([cc, sc], axis=1), dtype=BF16)
    return pl.pallas_call(
        _fourier_kernel,
        grid=(b, n // tm),
        in_specs=[
            pl.BlockSpec((None, n, width), lambda bi, t: (bi, 0, col_block)),
            pl.BlockSpec((tm, 2 * n), lambda bi, t: (t, 0)),
            pl.BlockSpec((HEAD_DIM, 2 * HEAD_DIM), lambda bi, t: (0, 0)),
        ],
        out_specs=pl.BlockSpec((None, tm, width), lambda bi, t: (bi, t, 0)),
        out_shape=jax.ShapeDtypeStruct((b, n, width), BF16),
        scratch_shapes=[pltpu.VMEM((2 * n, width), BF16)],
        compiler_params=_params("arbitrary", "arbitrary"),
        name="fourier_mix",
    )(src, dn, cs)


def _proj_residual_kernel(*refs, n_in, n_k):
    a_refs = refs[:n_in]
    w_refs = refs[n_in:2 * n_in]
    h_ref, gate_ref, g_ref, b_ref, o_ref = refs[2 * n_in:2 * n_in + 5]
    acc_ref = refs[-1] if n_k > 1 else None
    k = pl.program_id(1)

    part = None
    for a_ref, w_ref in zip(a_refs, w_refs):
        p = jnp.dot(a_ref[...], w_ref[...], preferred_element_type=F32)
        part = p if part is None else part + p

    def finish(total):
        x = ALPHA * h_ref[...] + gate_ref[...] * total
        o_ref[...] = _layer_norm_rows(x) * g_ref[...] + b_ref[...]

    if n_k == 1:
        finish(part)
    else:
        @pl.when(k == 0)
        def _():
            acc_ref[...] = part

        @pl.when(jnp.logical_and(k > 0, k < n_k - 1))
        def _():
            acc_ref[...] += part

        @pl.when(k == n_k - 1)
        def _():
            finish(acc_ref[...] + part)


def _proj_residual(a_list, w_list, h, mod3, gate_chunk, row_of_tile, ln_g, ln_b, *, tm, n_k=1):
    m, d = h.shape
    in_specs = []
    for a in a_list:
        in_specs.append(pl.BlockSpec((tm, a.shape[1] // n_k), lambda i, k: (i, k)))
    for w in w_list:
        in_specs.append(pl.BlockSpec((w.shape[0] // n_k, d), lambda i, k: (k, 0)))
    in_specs += [
        pl.BlockSpec((tm, d), lambda i, k: (i, 0)),
        _mod_spec(gate_chunk, d, row_of_tile),
        pl.BlockSpec((1, d), lambda i, k: (0, 0)),
        pl.BlockSpec((1, d), lambda i, k: (0, 0)),
    ]
    scratch = [pltpu.VMEM((tm, d), F32)] if n_k > 1 else []
    return pl.pallas_call(
        functools.partial(_proj_residual_kernel, n_in=len(a_list), n_k=n_k),
        grid=(m // tm, n_k),
        in_specs=in_specs,
        out_specs=pl.BlockSpec((tm, d), lambda i, k: (i, 0)),
        out_shape=jax.ShapeDtypeStruct((m, d), F32),
        scratch_shapes=scratch,
        compiler_params=_params("arbitrary", "arbitrary"),
        name="proj_residual",
    )(*a_list, *w_list, h, mod3, ln_g.reshape(1, d), ln_b.reshape(1, d))


def _ffn_up_kernel(h_ref, hp_ref, hn_ref, sh_ref, sc_ref, wa_ref, wv_ref, cw_ref, cb_ref, o_ref,
                   u_ref, uh_ref, *, tiles_per_seq):
    i = pl.program_id(0)
    j = pl.program_id(1)
    tm = h_ref.shape[0]

    @pl.when(j == 0)
    def _():
        scale = 1.0 + sc_ref[...]
        u_ref[...] = (_layer_norm_rows(h_ref[...]) * scale + sh_ref[...]).astype(BF16)
        uh_ref[0:SUBLANES, :] = (_layer_norm_rows(hp_ref[...]) * scale + sh_ref[...]).astype(BF16)
        uh_ref[SUBLANES:, :] = (_layer_norm_rows(hn_ref[...]) * scale + sh_ref[...]).astype(BF16)

    t_in_seq = i % tiles_per_seq
    row = lax.broadcasted_iota(jnp.int32, (tm, MXU_DIM), 0)
    for c0 in range(0, o_ref.shape[1], MXU_DIM):
        cols = slice(c0, c0 + MXU_DIM)
        a = jnp.dot(u_ref[...], wa_ref[:, cols], preferred_element_type=F32)
        v = jnp.dot(u_ref[...], wv_ref[:, cols], preferred_element_type=F32)
        halo = jnp.dot(uh_ref[...], wa_ref[:, cols], preferred_element_type=F32)
        prev_row = jnp.where(t_in_seq > 0, halo[SUBLANES - 1:SUBLANES], 0.0)
        next_row = jnp.where(t_in_seq < tiles_per_seq - 1, halo[SUBLANES:SUBLANES + 1], 0.0)
        a_prev = jnp.where(row == 0, prev_row, pltpu.roll(a, 1, axis=0))
        a_next = jnp.where(row == tm - 1, next_row, pltpu.roll(a, tm - 1, axis=0))
        cw = cw_ref[:, cols]
        conv = cb_ref[:, cols] + a_prev * cw[0:1] + a * cw[1:2] + a_next * cw[2:3]
        gelu = 0.5 * conv * (1.0 + lax.erf(conv * math.sqrt(0.5)))
        o_ref[:, cols] = (gelu * v).astype(BF16)


def _ffn_up(h, mod3, row_of_tile, wa, wv, conv_w, conv_b, *, seq_len, tm, tn):
    m, d = h.shape
    n = wa.shape[1]
    tiles_per_seq = seq_len // tm
    blocks_per_tile = tm // SUBLANES
    n_blocks = m // SUBLANES
    kern = functools.partial(_ffn_up_kernel, tiles_per_seq=tiles_per_seq)
    return pl.pallas_call(
        kern,
        grid=(m // tm, n // tn),
        in_specs=[
            pl.BlockSpec((tm, d), lambda i, j: (i, 0)),
            pl.BlockSpec((SUBLANES, d), lambda i, j: (jnp.maximum(i * blocks_per_tile - 1, 0), 0)),
            pl.BlockSpec((SUBLANES, d), lambda i, j: (jnp.minimum((i + 1) * blocks_per_tile, n_blocks - 1), 0)),
            _mod_spec(3, d, row_of_tile),
            _mod_spec(4, d, row_of_tile),
            pl.BlockSpec((d, tn), lambda i, j: (0, j)),
            pl.BlockSpec((d, tn), lambda i, j: (0, j)),
            pl.BlockSpec((CONV_W, tn), lambda i, j: (0, j)),
            pl.BlockSpec((1, tn), lambda i, j: (0, j)),
        ],
        out_specs=pl.BlockSpec((tm, tn), lambda i, j: (i, j)),
        out_shape=jax.ShapeDtypeStruct((m, n), BF16),
        scratch_shapes=[pltpu.VMEM((tm, d), BF16), pltpu.VMEM((2 * SUBLANES, d), BF16)],
        compiler_params=_params("arbitrary", "arbitrary"),
        name="ffn_up",
    )(h, h, h, mod3, mod3, wa, wv, conv_w, conv_b)


def _hgrn_act_kernel(h_ref, sh_ref, sc_ref, w_ref, o_ref, u_ref, *, lin_lo, lin_hi):
    j = pl.program_id(1)

    @pl.when(j == 0)
    def _():
        u_ref[...] = (_layer_norm_rows(h_ref[...]) * (1.0 + sc_ref[...]) + sh_ref[...]).astype(BF16)

    acc = jnp.dot(u_ref[...], w_ref[...], preferred_element_type=F32)
    linear = jnp.logical_and(j >= lin_lo, j < lin_hi)

    @pl.when(linear)
    def _():
        o_ref[...] = acc.astype(BF16)

    @pl.when(jnp.logical_not(linear))
    def _():
        o_ref[...] = _silu(acc).astype(BF16)


def _hgrn_gate_kernel(h_ref, sh_ref, sc_ref, w_ref, lbp_ref, k_ref, lf_ref, u_ref, *, layer):
    j = pl.program_id(1)

    @pl.when(j == 0)
    def _():
        u_ref[...] = (_layer_norm_rows(h_ref[...]) * (1.0 + sc_ref[...]) + sh_ref[...]).astype(BF16)

    x = lbp_ref[...]
    e = jnp.exp(x - jnp.max(x, axis=0, keepdims=True))
    lb = jnp.sum(e[1:layer + 1], axis=0, keepdims=True) / jnp.sum(e, axis=0, keepdims=True)

    f_pre = jnp.dot(u_ref[...], w_ref[...], preferred_element_type=F32)
    k_ref[...] = ((1.0 - lb) * jax.nn.sigmoid(-f_pre)).astype(BF16)
    lf_ref[...] = jnp.log(lb + (1.0 - lb) * jax.nn.sigmoid(f_pre))


def _hgrn_inproj(h, mod3, row_of_tile, w_act, w_gate, lb_params, *, layer, tm, tn):
    m, d = h.shape
    common = [
        pl.BlockSpec((tm, d), lambda i, j: (i, 0)),
        _mod_spec(0, d, row_of_tile),
        _mod_spec(1, d, row_of_tile),
        pl.BlockSpec((d, tn), lambda i, j: (0, j)),
    ]
    tiles = d // tn
    act = pl.pallas_call(
        functools.partial(_hgrn_act_kernel, lin_lo=tiles, lin_hi=2 * tiles),
        grid=(m // tm, w_act.shape[1] // tn),
        in_specs=common,
        out_specs=pl.BlockSpec((tm, tn), lambda i, j: (i, j)),
        out_shape=jax.ShapeDtypeStruct((m, w_act.shape[1]), BF16),
        scratch_shapes=[pltpu.VMEM((tm, d), BF16)],
        compiler_params=_params("arbitrary", "arbitrary"),
        name="hgrn_inproj_act",
    )(h, mod3, mod3, w_act)
    k, logf = pl.pallas_call(
        functools.partial(_hgrn_gate_kernel, layer=layer),
        grid=(m // tm, w_gate.shape[1] // tn),
        in_specs=common + [pl.BlockSpec((None, DEPTH, tn), lambda i, j: (j // tiles, 0, j % tiles))],
        out_specs=[pl.BlockSpec((tm, tn), lambda i, j: (i, j))] * 2,
        out_shape=[jax.ShapeDtypeStruct((m, w_gate.shape[1]), BF16),
                   jax.ShapeDtypeStruct((m, w_gate.shape[1]), F32)],
        scratch_shapes=[pltpu.VMEM((tm, d), BF16)],
        compiler_params=_params("arbitrary", "arbitrary"),
        name="hgrn_inproj_gate",
    )(h, mod3, mod3, w_gate, lb_params)
    return act, k, logf


def _gla_matrices(reverse):
    c = GLA_CHUNK
    r_i = lax.broadcasted_iota(jnp.int32, (c, c), 0)
    c_i = lax.broadcasted_iota(jnp.int32, (c, c), 1)
    same = jnp.bitwise_xor(r_i, c_i)
    earlier = (c_i >= r_i) if reverse else (c_i <= r_i)
    in_block = same < GLA_BLOCK
    one = lambda m: jnp.where(m, 1.0, 0.0).astype(BF16)
    return dict(tri=one(earlier), tri_blk=one(jnp.logical_and(earlier, in_block)), ones_blk=one(in_block),
                diag_mask=jnp.logical_and(earlier, in_block), same=same)


def _gla_direct_block_terms(q, k, v, cum, bad, *, reverse):
    c = GLA_CHUNK
    row = lax.broadcasted_iota(jnp.int32, (c, HEAD_DIM), 0) % GLA_BLOCK
    qb = jnp.where(bad, q, 0.0)
    acc0 = jnp.sum(qb * k, axis=1, keepdims=True) * v

    def offset(dlt, acc):
        shift = (c - dlt) if reverse else dlt
        valid = (row <= GLA_BLOCK - 1 - dlt) if reverse else (row >= dlt)
        k_d = pltpu.roll(k, shift, axis=0)
        v_d = pltpu.roll(v, shift, axis=0)
        cum_d = pltpu.roll(cum, shift, axis=0)
        decay = jnp.exp(jnp.where(valid, cum - cum_d, -jnp.inf))
        return acc + jnp.sum(qb * k_d * decay, axis=1, keepdims=True) * v_d

    return lax.fori_loop(1, GLA_BLOCK, offset, acc0)


def _gla_prefix_sums(lf, mats):
    lf_hi = lf.astype(BF16)
    lf_lo = (lf - lf_hi.astype(F32)).astype(BF16)
    two_term = lambda m: (jnp.dot(m, lf_hi, preferred_element_type=F32)
                          + jnp.dot(m, lf_lo, preferred_element_type=F32))
    return two_term(mats["tri"]), two_term(mats["tri_blk"])


def _gla_chunk(q, k, lf, v_bf, cum_ref, w_blk, mats, *, reverse, robust, want_o):
    c = GLA_CHUNK
    dn_t = (((1,), (1,)), ((), ()))
    cum = cum_ref[...]
    end_row = 0 if reverse else c - 1
    cum_end = cum_ref[end_row:end_row + 1, :]

    qe = (q * jnp.exp(cum)).astype(BF16)
    k_dec = (k * jnp.exp(cum_end - cum)).astype(BF16)
    u = lax.dot_general(v_bf, k_dec, (((0,), (0,)), ((), ())), preferred_element_type=F32)
    dec_end = jnp.exp(cum_end)
    if not want_o:
        return None, None, qe, u, dec_end

    if robust:
        blk_tot = jnp.dot(mats["ones_blk"], lf.astype(BF16), preferred_element_type=F32)
        bad = blk_tot < -GLA_SAFE_RANGE
        a_d = jnp.where(bad, 0.0, q * jnp.exp(w_blk)).astype(BF16)
        b_d = (k * jnp.exp(jnp.minimum(-w_blk, GLA_SAFE_RANGE + 20.0))).astype(BF16)
    else:
        a_d = (q * jnp.exp(w_blk)).astype(BF16)
        b_d = (k * jnp.exp(-w_blk)).astype(BF16)
    scores = jnp.where(mats["diag_mask"], lax.dot_general(a_d, b_d, dn_t, preferred_element_type=F32), 0.0)

    zeros = {m: jnp.zeros((m, HEAD_DIM), F32) for m in GLA_LEVELS}
    for m in GLA_LEVELS:
        a_parts, b_parts = [], []
        for blk in range(c // (2 * m)):
            lo, mid, hi = blk * 2 * m, blk * 2 * m + m, (blk + 1) * 2 * m
            if reverse:
                ref_row = cum_ref[mid:mid + 1, :]
                qa = q[lo:mid] * jnp.exp(cum[lo:mid] - ref_row)
                kb = k[mid:hi] * jnp.exp(ref_row - cum[mid:hi])
                a_parts += [qa, zeros[m]]
                b_parts += [zeros[m], kb]
            else:
                ref_row = cum_ref[mid - 1:mid, :]
                qa = q[mid:hi] * jnp.exp(cum[mid:hi] - ref_row)
                kb = k[lo:mid] * jnp.exp(ref_row - cum[lo:mid])
                a_parts += [zeros[m], qa]
                b_parts += [kb, zeros[m]]
        a_m = jnp.concatenate(a_parts, axis=0).astype(BF16)
        b_m = jnp.concatenate(b_parts, axis=0).astype(BF16)
        s_m = lax.dot_general(a_m, b_m, dn_t, preferred_element_type=F32)
        if 2 * m < c:
            s_m = jnp.where(mats["same"] < 2 * m, s_m, 0.0)
        scores = scores + s_m
    if robust:
        direct = _gla_direct_block_terms(q, k, v_bf.astype(F32), cum, bad, reverse=reverse)
    else:
        direct = jnp.zeros((c, HEAD_DIM), F32)
    return direct, scores.astype(BF16), qe, u, dec_end


def _gla_kernel(*refs, emit_y, emit_state):
    q_ref, kf_ref, kb_ref, lff_ref, lfb_ref, v_ref, gate_ref, s0_ref, nw_ref = refs[:9]
    pos = 9
    y_ref = s_out_ref = None
    if emit_y:
        y_ref = refs[pos]
        pos += 1
    if emit_state:
        s_out_ref = refs[pos]
        pos += 1
    (of_ref, ob_ref, cumf_ref, cumb_ref, wf_ref, wb_ref, qef_ref, qeb_ref, scf_ref, scb_ref,
     uf_ref, ub_ref, df_ref, db_ref) = refs[pos:pos + 14]

    c = GLA_CHUNK
    n = q_ref.shape[0]
    n_chunks = n // c
    unroll = min(GLA_UNROLL, n_chunks)
    dn_t = (((1,), (1,)), ((), ()))
    chunk_rows = lambda ci: pl.ds(pl.multiple_of(ci * c, c), c)

    def prefix_pass():
        mats_f, mats_b = _gla_matrices(False), _gla_matrices(True)

        def body(ci, _):
            rows = chunk_rows(ci)
            cumf_ref[rows, :], wf_ref[rows, :] = _gla_prefix_sums(lff_ref[rows, :], mats_f)
            cumb_ref[rows, :], wb_ref[rows, :] = _gla_prefix_sums(lfb_ref[rows, :], mats_b)
            return 0

        lax.fori_loop(0, n_chunks, body, 0, unroll=unroll)

    prefix_pass()

    def local_pass(robust):
        mats_f, mats_b = _gla_matrices(False), _gla_matrices(True)

        def body(ci, _):
            rows = chunk_rows(ci)
            q = q_ref[rows, :].astype(F32)
            v_bf = v_ref[rows, :]
            x_f, sc_f, qe_f, u_f, d_f = _gla_chunk(
                q, kf_ref[rows, :].astype(F32), lff_ref[rows, :], v_bf, cumf_ref.at[rows, :], wf_ref[rows, :],
                mats_f, reverse=False, robust=robust, want_o=emit_y)
            x_b, sc_b, qe_b, u_b, d_b = _gla_chunk(
                q, kb_ref[rows, :].astype(F32), lfb_ref[rows, :], v_bf, cumb_ref.at[rows, :], wb_ref[rows, :],
                mats_b, reverse=True, robust=robust, want_o=emit_y)
            uf_ref[ci] = u_f
            ub_ref[ci] = u_b
            df_ref[ci] = d_f
            db_ref[ci] = d_b
            if emit_y:
                of_ref[rows, :] = x_f
                ob_ref[rows, :] = x_b
                scf_ref[rows, :] = sc_f
                scb_ref[rows, :] = sc_b
                qef_ref[rows, :] = qe_f
                qeb_ref[rows, :] = qe_b
            return 0

        lax.fori_loop(0, n_chunks, body, 0, unroll=1 if robust else unroll)

    worst = jnp.minimum(jnp.min(wf_ref[...]), jnp.min(wb_ref[...]))
    lax.cond(worst < -GLA_SAFE_RANGE, lambda: local_pass(True), lambda: local_pass(False))

    def state_pass(ci, carry):
        st_f, st_b = carry
        cb = n_chunks - 1 - ci
        if emit_y:
            rows_f, rows_b = chunk_rows(ci), chunk_rows(cb)
            of_ref[rows_f, :] += (
                jnp.dot(scf_ref[rows_f, :], v_ref[rows_f, :], preferred_element_type=F32)
                + lax.dot_general(qef_ref[rows_f, :], st_f.astype(BF16), dn_t, preferred_element_type=F32))
            ob_ref[rows_b, :] += (
                jnp.dot(scb_ref[rows_b, :], v_ref[rows_b, :], preferred_element_type=F32)
                + lax.dot_general(qeb_ref[rows_b, :], st_b.astype(BF16), dn_t, preferred_element_type=F32))
        return st_f * df_ref[ci] + uf_ref[ci], st_b * db_ref[cb] + ub_ref[cb]

    st_f, st_b = lax.fori_loop(0, n_chunks, state_pass, (s0_ref[0], s0_ref[1]), unroll=unroll)

    if emit_state:
        s_out_ref[0] = st_f
        s_out_ref[1] = st_b
    if emit_y:
        def readout(ci, _):
            rows = pl.ds(pl.multiple_of(ci * c, c), c)
            tot = of_ref[rows, :] + ob_ref[rows, :]
            ms = jnp.mean(tot * tot, axis=1, keepdims=True)
            y = tot * lax.rsqrt(ms + RMS_EPS) * nw_ref[...] * gate_ref[rows, :].astype(F32)
            y_ref[rows, :] = y.astype(BF16)
            return 0
        lax.fori_loop(0, n_chunks, readout, 0, unroll=unroll)


def _gla(act, k, logf, s0, norm_w, *, n_heads, emit_y, emit_state):
    b, n, _ = act.shape
    hd = HEAD_DIM
    col = lambda off: pl.BlockSpec((None, n, hd), lambda bi, h: (bi, 0, off * n_heads + h))
    st_spec = pl.BlockSpec((None, None, 2, hd, hd), lambda bi, h: (bi, h, 0, 0, 0))
    out_specs, out_shape = [], []
    if emit_y:
        out_specs.append(pl.BlockSpec((None, n, hd), lambda bi, h: (bi, 0, h)))
        out_shape.append(jax.ShapeDtypeStruct((b, n, n_heads * hd), BF16))
    if emit_state:
        out_specs.append(st_spec)
        out_shape.append(jax.ShapeDtypeStruct((b, n_heads, 2, hd, hd), F32))
    outs = pl.pallas_call(
        functools.partial(_gla_kernel, emit_y=emit_y, emit_state=emit_state),
        grid=(b, n_heads),
        in_specs=[col(0), col(0), col(1), col(0), col(1), col(1), col(2), st_spec,
                  pl.BlockSpec((1, hd), lambda bi, h: (0, 0))],
        out_specs=out_specs,
        out_shape=out_shape,
        scratch_shapes=(
            [pltpu.VMEM((n, hd), F32)] * 6
            + [pltpu.VMEM((n, hd), BF16)] * 4
            + [pltpu.VMEM((n // GLA_CHUNK, hd, hd), F32)] * 2
            + [pltpu.VMEM((n // GLA_CHUNK, 1, hd), F32)] * 2),
        compiler_params=_params("arbitrary", "arbitrary"),
        name="gla_scan",
    )(act, k, k, logf, logf, act, act, s0, norm_w.reshape(1, hd))
    return outs


def _rope_head_perm():
    half = ROPE_AXIS_DIM // 2
    first = [0, 2 * half, 4 * half, 6 * half]
    order = first + [s + half for s in first]
    return np.concatenate([np.arange(s, s + half) for s in order])


def _rope_tables(n):
    half = ROPE_AXIS_DIM // 2
    inv = (1.0 / (ROPE_BASE ** (np.arange(0, ROPE_AXIS_DIM, 2, dtype=np.float32) / ROPE_AXIS_DIM))).astype(np.float32)
    pos = np.arange(n)
    ang_r = (pos // GRID_W).astype(np.float32)[:, None] * inv[None, :]
    ang_c = (pos % GRID_W).astype(np.float32)[:, None] * inv[None, :]
    ang = np.concatenate([ang_r, ang_c] * 4, axis=1)
    sign = np.concatenate([-np.ones(4 * half, np.float32), np.ones(4 * half, np.float32)])
    return jnp.asarray(np.cos(ang), F32), jnp.asarray(np.sin(ang) * sign[None, :], F32)


def kernel(x, c, ctx, c_ctx, mod_w, mod_b, ln_mix_g, ln_mix_b, ln_ffn_g, ln_ffn_b, even_w_in, even_w_out, diff_lambda, diff_subln, hgrn_w_in, hgrn_w_out, hgrn_lower_bounds, hgrn_norm, ffn_w_up, ffn_conv_w, ffn_conv_b, ffn_w_down):
    b, n_lat, d = x.shape
    n_ctx = ctx.shape[1]
    assert mod_w.shape[0] == DEPTH and n_lat % GRID_W == 0
    d_ff = ffn_w_down.shape[1]
    d_ff_pad = -(-d_ff // (2 * MXU_DIM)) * (2 * MXU_DIM)
    diff_width = even_w_out.shape[2] * 3 // 4
    four_width = even_w_in.shape[2] - 3 * diff_width
    n_diff_heads = diff_width // HEAD_DIM
    n_hgrn_heads = d // HEAD_DIM

    cc = jnp.concatenate([c, c_ctx[None, :], jnp.zeros((2 * SUBLANES - b - 1, d), F32)], axis=0)
    mod = _modulation(cc, mod_w, mod_b)
    mod = mod.reshape(DEPTH, cc.shape[0], 1, 6 * d)

    h_lat = x.reshape(b * n_lat, d)
    h_ctx = ctx.reshape(b * n_ctx, d)

    tm_wide_lat, tm_lat, tm_ctx = min(1024, n_lat), min(512, n_lat), min(256, n_ctx)
    lat_row = lambda tm: (lambda i: (i * tm) // n_lat)
    ctx_row = lambda tm: (lambda i: b)
    down_k_tiles = 4 if d_ff_pad % (4 * LANES) == 0 else d_ff_pad // 512

    def ffn(h, mod3, row_fn, layer, seq_len, tm_up, tm_down):
        pad = d_ff_pad - d_ff
        w_up = ffn_w_up[layer]
        wa = jnp.pad(w_up[:, :d_ff], ((0, 0), (0, pad))).astype(BF16)
        wv = jnp.pad(w_up[:, d_ff:], ((0, 0), (0, pad))).astype(BF16)
        cw = jnp.pad(ffn_conv_w[layer], ((0, 0), (0, pad)))
        cb = jnp.pad(ffn_conv_b[layer], (0, pad)).reshape(1, d_ff_pad)
        wd = jnp.pad(ffn_w_down[layer], ((0, pad), (0, 0))).astype(BF16)
        g = _ffn_up(h, mod3, row_fn(tm_up), wa, wv, cw, cb, seq_len=seq_len, tm=tm_up, tn=512)
        return _proj_residual([g], [wd], h, mod3, 5, row_fn(tm_down), ln_ffn_g[layer], ln_ffn_b[layer],
                              tm=tm_down, n_k=down_k_tiles)

    layer = 0
    lam_init = 0.8 - 0.6 * math.exp(-0.3 * layer)
    perm = _rope_head_perm()
    qk_cols = (np.arange(2 * n_diff_heads)[:, None] * HEAD_DIM + perm[None, :]).reshape(-1)
    cols = np.concatenate([qk_cols, np.arange(2 * diff_width, even_w_in.shape[2])])
    w_in = even_w_in[0][:, cols].astype(BF16)
    w_out = even_w_out[0].astype(BF16)
    cos_t, sin_t = _rope_tables(n_lat)
    mod3 = mod[layer]

    p_lat = _even_inproj(h_lat, mod3, lat_row(tm_wide_lat), w_in, cos_t, sin_t, qk_width=diff_width,
                         use_rope=True, tm=tm_wide_lat, tn=512).reshape(b, n_lat, -1)
    p_ctx = _even_inproj(h_ctx, mod3, ctx_row(tm_ctx), w_in, cos_t[:SUBLANES], sin_t[:SUBLANES],
                         qk_width=diff_width, use_rope=False, tm=tm_ctx, tn=512).reshape(b, n_ctx, -1)

    attn_lat = _diff_attention(p_lat, [p_ctx, p_lat], diff_lambda[0], diff_subln[0], lam_init,
                               n_heads=n_diff_heads, tq=min(512, n_lat))
    attn_ctx = _diff_attention(p_ctx, [p_ctx], diff_lambda[0], diff_subln[0], lam_init,
                               n_heads=n_diff_heads, tq=n_ctx)
    four_block = 3 * diff_width // four_width
    four_lat = _fourier_mix(p_lat, four_block, four_width, tm=tm_lat)
    four_ctx = _fourier_mix(p_ctx, four_block, four_width, tm=n_ctx)

    w_list = [w_out[:diff_width], w_out[diff_width:]]
    h_lat = _proj_residual([attn_lat.reshape(b * n_lat, -1), four_lat.reshape(b * n_lat, -1)], w_list,
                           h_lat, mod3, 2, lat_row(tm_lat), ln_mix_g[layer], ln_mix_b[layer], tm=tm_lat)
    h_ctx = _proj_residual([attn_ctx.reshape(b * n_ctx, -1), four_ctx.reshape(b * n_ctx, -1)], w_list,
                           h_ctx, mod3, 2, ctx_row(tm_ctx), ln_mix_g[layer], ln_mix_b[layer], tm=tm_ctx)
    h_lat = ffn(h_lat, mod3, lat_row, layer, n_lat, tm_wide_lat, tm_lat)
    h_ctx = ffn(h_ctx, mod3, ctx_row, layer, n_ctx, tm_ctx, tm_ctx)

    layer = 1
    mod3 = mod[layer]
    hw = hgrn_w_in[0]
    w_act = jnp.concatenate([hw[:, :d], hw[:, 3 * d:]], axis=1).astype(BF16)
    w_gate = hw[:, d:3 * d].astype(BF16)
    wrap = lambda t, n: t.reshape(b, n, -1)
    act_c, k_c, lf_c = _hgrn_inproj(h_ctx, mod3, ctx_row(tm_ctx), w_act, w_gate, hgrn_lower_bounds,
                                    layer=layer, tm=tm_ctx, tn=512)
    act_l, k_l, lf_l = _hgrn_inproj(h_lat, mod3, lat_row(tm_wide_lat), w_act, w_gate, hgrn_lower_bounds,
                                    layer=layer, tm=tm_wide_lat, tn=512)
    zero_state = jnp.zeros((b, n_hgrn_heads, 2, HEAD_DIM, HEAD_DIM), F32)
    (s_ctx,) = _gla(wrap(act_c, n_ctx), wrap(k_c, n_ctx), wrap(lf_c, n_ctx), zero_state, hgrn_norm[0],
                    n_heads=n_hgrn_heads, emit_y=False, emit_state=True)
    (y_lat,) = _gla(wrap(act_l, n_lat), wrap(k_l, n_lat), wrap(lf_l, n_lat), s_ctx, hgrn_norm[0],
                    n_heads=n_hgrn_heads, emit_y=True, emit_state=False)
    h_lat = _proj_residual([y_lat.reshape(b * n_lat, d)], [hgrn_w_out[0].astype(BF16)], h_lat, mod3, 2,
                           lat_row(tm_lat), ln_mix_g[layer], ln_mix_b[layer], tm=tm_lat)
    h_lat = ffn(h_lat, mod3, lat_row, layer, n_lat, tm_wide_lat, tm_lat)
    return h_lat.reshape(b, n_lat, d)
```

```python
import functools
import math

import numpy as np
import jax
import jax.numpy as jnp
from jax import lax
from jax.experimental import pallas as pl
from jax.experimental.pallas import tpu as pltpu

F32 = jnp.float32
BF16 = jnp.bfloat16

LANES = 128
SUBLANES = 8
MXU_DIM = 256
VMEM_LIMIT_BYTES = 56 * 1024 * 1024

GRID_W = 64
DIFF_QK_DIM = 64
HEAD_DIM = 128
ROPE_AXIS_DIM = DIFF_QK_DIM // 2
ROPE_BASE = 10000.0
CONV_W = 3
LN_EPS = 1e-6
RMS_EPS = 1e-5
DEPTH = 2
ALPHA = (2.0 * DEPTH) ** 0.25
QK_SCALE = DIFF_QK_DIM ** -0.5 * math.log2(math.e)
ATTN_SAFE_LOG2 = 60.0

GLA_CHUNK = 128
GLA_LEVELS = (64, 32, 16)
GLA_BLOCK = 16
LOG2_E = math.log2(math.e)
GLA_SAFE_RANGE = 80.0
GLA_UNROLL = 4


def _params(*dims):
    return pltpu.CompilerParams(dimension_semantics=dims, vmem_limit_bytes=VMEM_LIMIT_BYTES)


def _layer_norm_rows(x):
    mu = jnp.mean(x, axis=-1, keepdims=True)
    xc = x - mu
    var = jnp.mean(xc * xc, axis=-1, keepdims=True)
    return xc * lax.rsqrt(var + LN_EPS)


def _silu(x):
    return x * jax.nn.sigmoid(x)


def _mod_kernel(c_ref, w_ref, b_ref, o_ref):
    x = _silu(c_ref[...]).astype(BF16)
    o_ref[...] = jnp.dot(x, w_ref[...].astype(BF16), preferred_element_type=F32) + b_ref[...]


def _modulation(cc, mod_w, mod_b, tn=1024):
    depth, d, n = mod_w.shape
    rows = cc.shape[0]
    return pl.pallas_call(
        _mod_kernel,
        grid=(depth, n // tn),
        in_specs=[
            pl.BlockSpec((rows, d), lambda l, j: (0, 0)),
            pl.BlockSpec((None, d, tn), lambda l, j: (l, 0, j)),
            pl.BlockSpec((None, 1, tn), lambda l, j: (l, 0, j)),
        ],
        out_specs=pl.BlockSpec((None, rows, tn), lambda l, j: (l, 0, j)),
        out_shape=jax.ShapeDtypeStruct((depth, rows, n), F32),
        compiler_params=_params("arbitrary", "arbitrary"),
        name="modulation",
    )(cc, mod_w, mod_b.reshape(depth, 1, n))


def _mod_spec(chunk, d, row_of_tile):
    return pl.BlockSpec((None, 1, d), lambda i, *_: (row_of_tile(i), 0, chunk))


def _even_inproj_kernel(h_ref, sh_ref, sc_ref, w_ref, cscale_ref, cos_ref, sin_ref, o_ref, u_ref,
                        *, n_rope_tiles, use_rope):
    j = pl.program_id(1)

    @pl.when(j == 0)
    def _():
        u_ref[...] = (_layer_norm_rows(h_ref[...]) * (1.0 + sc_ref[...]) + sh_ref[...]).astype(BF16)

    def tile(rope):
        for c0 in range(0, o_ref.shape[1], MXU_DIM):
            cols = slice(c0, c0 + MXU_DIM)
            y = jnp.dot(u_ref[...], w_ref[:, cols], preferred_element_type=F32)
            if rope:
                parts = []
                for hh in range(MXU_DIM // HEAD_DIM):
                    yh = y[:, hh * HEAD_DIM:(hh + 1) * HEAD_DIM]
                    parts.append(yh * cos_ref[...] + pltpu.roll(yh, HEAD_DIM // 2, axis=1) * sin_ref[...])
                y = jnp.concatenate(parts, axis=1)
            o_ref[:, cols] = (y * cscale_ref[:, cols]).astype(BF16)

    if use_rope:
        pl.when(j < n_rope_tiles)(lambda: tile(True))
        pl.when(j >= n_rope_tiles)(lambda: tile(False))
    else:
        tile(False)


def _even_inproj(h, mod3, row_of_tile, w, col_scale, cos_t, sin_t, *, qk_width, use_rope, tm, tn):
    m, d = h.shape
    n = w.shape[1]
    n_pos_tiles = cos_t.shape[0] // tm if use_rope else 1
    tab_rows = tm if use_rope else cos_t.shape[0]
    kern = functools.partial(_even_inproj_kernel, n_rope_tiles=2 * qk_width // tn, use_rope=use_rope)
    return pl.pallas_call(
        kern,
        grid=(m // tm, n // tn),
        in_specs=[
            pl.BlockSpec((tm, d), lambda i, j: (i, 0)),
            _mod_spec(0, d, row_of_tile),
            _mod_spec(1, d, row_of_tile),
            pl.BlockSpec((d, tn), lambda i, j: (0, j)),
            pl.BlockSpec((1, tn), lambda i, j: (0, j)),
            pl.BlockSpec((tab_rows, HEAD_DIM), lambda i, j: (i % n_pos_tiles, 0)),
            pl.BlockSpec((tab_rows, HEAD_DIM), lambda i, j: (i % n_pos_tiles, 0)),
        ],
        out_specs=pl.BlockSpec((tm, tn), lambda i, j: (i, j)),
        out_shape=jax.ShapeDtypeStruct((m, n), BF16),
        scratch_shapes=[pltpu.VMEM((tm, d), BF16)],
        compiler_params=_params("arbitrary", "arbitrary"),
        name="even_inproj",
    )(h, mod3, mod3, w, col_scale, cos_t, sin_t)


def _diff_attn_kernel(*refs, n_seg, lam_init):
    q_ref = refs[0]
    k_refs = refs[1:1 + n_seg]
    v_refs = refs[1 + n_seg:1 + 2 * n_seg]
    lam_ref, subln_ref, o_ref, knorm_ref = refs[1 + 2 * n_seg:]

    @pl.when(pl.program_id(2) == 0)
    def _():
        worst = None
        for k_ref in k_refs:
            k32 = k_ref[...].astype(F32)
            m = jnp.max(jnp.sum(k32 * k32, axis=1, keepdims=True), axis=0, keepdims=True)
            worst = m if worst is None else jnp.maximum(worst, m)
        knorm_ref[...] = jnp.broadcast_to(worst, knorm_ref.shape)

    lv = lam_ref[...]
    lam = (jnp.exp(jnp.sum(lv[0:1] * lv[1:2], axis=1, keepdims=True))
           - jnp.exp(jnp.sum(lv[2:3] * lv[3:4], axis=1, keepdims=True)) + lam_init)

    q = q_ref[...]
    lane = lax.broadcasted_iota(jnp.int32, q.shape, 1)
    first_map = (lane % DIFF_QK_DIM) < ROPE_AXIS_DIM
    zero = jnp.zeros_like(q)
    qs = (jnp.where(first_map, q, zero), jnp.where(first_map, zero, q))

    dn = (((1,), (1,)), ((), ()))
    q32 = q.astype(F32)
    qnorm = jnp.max(jnp.sum(q32 * q32, axis=1, keepdims=True), axis=0, keepdims=True)
    small_scores = jnp.max(qnorm * knorm_ref[0:1, 0:1]) <= ATTN_SAFE_LOG2 ** 2

    def softmax_times_v(shift_by_max):
        out = []
        for qm in qs:
            s = [lax.dot_general(qm, k_ref[...], dn, preferred_element_type=F32) for k_ref in k_refs]
            if shift_by_max:
                mx = functools.reduce(jnp.maximum, [jnp.max(x, axis=1, keepdims=True) for x in s])
                s = [x - mx for x in s]
            num = den = None
            for seg in range(n_seg):
                e = jnp.exp2(s[seg])
                d = jnp.sum(e, axis=1, keepdims=True)
                part = jnp.dot(e.astype(BF16), v_refs[seg][...], preferred_element_type=F32)
                den = d if den is None else den + d
                num = part if num is None else num + part
            out += [num, den]
        return tuple(out)

    n1, d1, n2, d2 = lax.cond(small_scores, lambda: softmax_times_v(False), lambda: softmax_times_v(True))
    acc = n1 * (1.0 / d1) - n2 * (lam / d2)
    ms = jnp.mean(acc * acc, axis=1, keepdims=True)
    y = acc * lax.rsqrt(ms + RMS_EPS) * subln_ref[...] * (1.0 - lam_init)
    o_ref[...] = y.astype(BF16)


def _diff_attention(q_src, kv_srcs, lam_vecs, subln, lam_init, *, n_heads, tq):
    b, nq, _ = q_src.shape
    n_seg = len(kv_srcs)
    k_specs = [pl.BlockSpec((None, s.shape[1], HEAD_DIM), lambda bi, h, t: (bi, 0, n_heads + h))
               for s in kv_srcs]
    v_specs = [pl.BlockSpec((None, s.shape[1], HEAD_DIM), lambda bi, h, t: (bi, 0, 2 * n_heads + h))
               for s in kv_srcs]
    return pl.pallas_call(
        functools.partial(_diff_attn_kernel, n_seg=n_seg, lam_init=lam_init),
        grid=(b, n_heads, nq // tq),
        in_specs=[pl.BlockSpec((None, tq, HEAD_DIM), lambda bi, h, t: (bi, t, h))] + k_specs + v_specs + [
            pl.BlockSpec(lam_vecs.shape, lambda bi, h, t: (0, 0)),
            pl.BlockSpec((1, HEAD_DIM), lambda bi, h, t: (0, 0)),
        ],
        out_specs=pl.BlockSpec((None, tq, HEAD_DIM), lambda bi, h, t: (bi, t, h)),
        out_shape=jax.ShapeDtypeStruct((b, nq, n_heads * HEAD_DIM), BF16),
        scratch_shapes=[pltpu.VMEM((SUBLANES, LANES), F32)],
        compiler_params=_params("arbitrary", "arbitrary", "arbitrary"),
        name="diff_attention",
    )(q_src, *kv_srcs, *kv_srcs, lam_vecs, subln.reshape(1, HEAD_DIM))


def _fourier_kernel(x_ref, dn_ref, cs_ref, o_ref, z_ref):
    n = x_ref.shape[0]

    @pl.when(pl.program_id(1) == 0)
    def _():
        for g in range(x_ref.shape[1] // HEAD_DIM):
            cols = slice(g * HEAD_DIM, (g + 1) * HEAD_DIM)
            zc = jnp.dot(x_ref[:, cols], cs_ref[...], preferred_element_type=F32)
            z_ref[0:n, cols] = zc[:, :HEAD_DIM].astype(BF16)
            z_ref[n:2 * n, cols] = zc[:, HEAD_DIM:].astype(BF16)

    o_ref[...] = jnp.dot(dn_ref[...], z_ref[...], preferred_element_type=F32).astype(BF16)


def _dft_tables(n):
    j = np.arange(n, dtype=np.int64)
    ang = 2.0 * np.pi * ((j[:, None] * j[None, :]) % n).astype(np.float64) / n
    return np.cos(ang) / math.sqrt(n), np.sin(ang) / math.sqrt(n)


def _fourier_mix(src, col_block, width, *, tm):
    b, n, _ = src.shape
    cn, sn = _dft_tables(n)
    cc, sc = _dft_tables(HEAD_DIM)
    dn = jnp.asarray(np.concatenate([cn, -sn], axis=1), dtype=BF16)
    cs = jnp.asarray(np.concatenate([cc, sc], axis=1), dtype=BF16)
    return pl.pallas_call(
        _fourier_kernel,
        grid=(b, n // tm),
        in_specs=[
            pl.BlockSpec((None, n, width), lambda bi, t: (bi, 0, col_block)),
            pl.BlockSpec((tm, 2 * n), lambda bi, t: (t, 0)),
            pl.BlockSpec((HEAD_DIM, 2 * HEAD_DIM), lambda bi, t: (0, 0)),
        ],
        out_specs=pl.BlockSpec((None, tm, width), lambda bi, t: (bi, t, 0)),
        out_shape=jax.ShapeDtypeStruct((b, n, width), BF16),
        scratch_shapes=[pltpu.VMEM((2 * n, width), BF16)],
        compiler_params=_params("arbitrary", "arbitrary"),
        name="fourier_mix",
    )(src, dn, cs)


def _proj_residual_kernel(*refs, n_in, n_k):
    a_refs = refs[:n_in]
    w_refs = refs[n_in:2 * n_in]
    h_ref, gate_ref, g_ref, b_ref, o_ref = refs[2 * n_in:2 * n_in + 5]
    acc_ref = refs[-1] if n_k > 1 else None
    k = pl.program_id(1)

    part = None
    for a_ref, w_ref in zip(a_refs, w_refs):
        p = jnp.dot(a_ref[...], w_ref[...], preferred_element_type=F32)
        part = p if part is None else part + p

    def finish(total):
        x = ALPHA * h_ref[...] + gate_ref[...] * total
        o_ref[...] = _layer_norm_rows(x) * g_ref[...] + b_ref[...]

    if n_k == 1:
        finish(part)
    else:
        @pl.when(k == 0)
        def _():
            acc_ref[...] = part

        @pl.when(jnp.logical_and(k > 0, k < n_k - 1))
        def _():
            acc_ref[...] += part

        @pl.when(k == n_k - 1)
        def _():
            finish(acc_ref[...] + part)


def _proj_residual(a_list, w_list, h, mod3, gate_chunk, row_of_tile, ln_g, ln_b, *, tm, n_k=1):
    m, d = h.shape
    in_specs = []
    for a in a_list:
        in_specs.append(pl.BlockSpec((tm, a.shape[1] // n_k), lambda i, k: (i, k)))
    for w in w_list:
        in_specs.append(pl.BlockSpec((w.shape[0] // n_k, d), lambda i, k: (k, 0)))
    in_specs += [
        pl.BlockSpec((tm, d), lambda i, k: (i, 0)),
        _mod_spec(gate_chunk, d, row_of_tile),
        pl.BlockSpec((1, d), lambda i, k: (0, 0)),
        pl.BlockSpec((1, d), lambda i, k: (0, 0)),
    ]
    scratch = [pltpu.VMEM((tm, d), F32)] if n_k > 1 else []
    return pl.pallas_call(
        functools.partial(_proj_residual_kernel, n_in=len(a_list), n_k=n_k),
        grid=(m // tm, n_k),
        in_specs=in_specs,
        out_specs=pl.BlockSpec((tm, d), lambda i, k: (i, 0)),
        out_shape=jax.ShapeDtypeStruct((m, d), F32),
        scratch_shapes=scratch,
        compiler_params=_params("arbitrary", "arbitrary"),
        name="proj_residual",
    )(*a_list, *w_list, h, mod3, ln_g.reshape(1, d), ln_b.reshape(1, d))


def _ffn_up_kernel(h_ref, hp_ref, hn_ref, sh_ref, sc_ref, wa_ref, wv_ref, cw_ref, cb_ref, o_ref,
                   u_ref, uh_ref, *, tiles_per_seq):
    i = pl.program_id(0)
    j = pl.program_id(1)
    tm = h_ref.shape[0]

    @pl.when(j == 0)
    def _():
        scale = 1.0 + sc_ref[...]
        u_ref[...] = (_layer_norm_rows(h_ref[...]) * scale + sh_ref[...]).astype(BF16)
        uh_ref[0:SUBLANES, :] = (_layer_norm_rows(hp_ref[...]) * scale + sh_ref[...]).astype(BF16)
        uh_ref[SUBLANES:, :] = (_layer_norm_rows(hn_ref[...]) * scale + sh_ref[...]).astype(BF16)

    t_in_seq = i % tiles_per_seq
    row = lax.broadcasted_iota(jnp.int32, (tm, MXU_DIM), 0)
    for c0 in range(0, o_ref.shape[1], MXU_DIM):
        cols = slice(c0, c0 + MXU_DIM)
        a = jnp.dot(u_ref[...], wa_ref[:, cols], preferred_element_type=F32)
        v = jnp.dot(u_ref[...], wv_ref[:, cols], preferred_element_type=F32)
        halo = jnp.dot(uh_ref[...], wa_ref[:, cols], preferred_element_type=F32)
        prev_row = jnp.where(t_in_seq > 0, halo[SUBLANES - 1:SUBLANES], 0.0)
        next_row = jnp.where(t_in_seq < tiles_per_seq - 1, halo[SUBLANES:SUBLANES + 1], 0.0)
        a_prev = jnp.where(row == 0, prev_row, pltpu.roll(a, 1, axis=0))
        a_next = jnp.where(row == tm - 1, next_row, pltpu.roll(a, tm - 1, axis=0))
        cw = cw_ref[:, cols]
        conv = cb_ref[:, cols] + a_prev * cw[0:1] + a * cw[1:2] + a_next * cw[2:3]
        gelu = 0.5 * conv * (1.0 + lax.erf(conv * math.sqrt(0.5)))
        o_ref[:, cols] = (gelu * v).astype(BF16)


def _ffn_up(h, mod3, row_of_tile, wa, wv, conv_w, conv_b, *, seq_len, tm, tn):
    m, d = h.shape
    n = wa.shape[1]
    tiles_per_seq = seq_len // tm
    blocks_per_tile = tm // SUBLANES
    n_blocks = m // SUBLANES
    kern = functools.partial(_ffn_up_kernel, tiles_per_seq=tiles_per_seq)
    return pl.pallas_call(
        kern,
        grid=(m // tm, n // tn),
        in_specs=[
            pl.BlockSpec((tm, d), lambda i, j: (i, 0)),
            pl.BlockSpec((SUBLANES, d), lambda i, j: (jnp.maximum(i * blocks_per_tile - 1, 0), 0)),
            pl.BlockSpec((SUBLANES, d), lambda i, j: (jnp.minimum((i + 1) * blocks_per_tile, n_blocks - 1), 0)),
            _mod_spec(3, d, row_of_tile),
            _mod_spec(4, d, row_of_tile),
            pl.BlockSpec((d, tn), lambda i, j: (0, j)),
            pl.BlockSpec((d, tn), lambda i, j: (0, j)),
            pl.BlockSpec((CONV_W, tn), lambda i, j: (0, j)),
            pl.BlockSpec((1, tn), lambda i, j: (0, j)),
        ],
        out_specs=pl.BlockSpec((tm, tn), lambda i, j: (i, j)),
        out_shape=jax.ShapeDtypeStruct((m, n), BF16),
        scratch_shapes=[pltpu.VMEM((tm, d), BF16), pltpu.VMEM((2 * SUBLANES, d), BF16)],
        compiler_params=_params("arbitrary", "arbitrary"),
        name="ffn_up",
    )(h, h, h, mod3, mod3, wa, wv, conv_w, conv_b)


def _hgrn_act_kernel(h_ref, sh_ref, sc_ref, w_ref, o_ref, u_ref, *, lin_lo, lin_hi):
    j = pl.program_id(1)

    @pl.when(j == 0)
    def _():
        u_ref[...] = (_layer_norm_rows(h_ref[...]) * (1.0 + sc_ref[...]) + sh_ref[...]).astype(BF16)

    def tile(activation):
        for c0 in range(0, o_ref.shape[1], MXU_DIM):
            cols = slice(c0, c0 + MXU_DIM)
            acc = jnp.dot(u_ref[...], w_ref[:, cols], preferred_element_type=F32)
            o_ref[:, cols] = activation(acc).astype(BF16)

    linear = jnp.logical_and(j >= lin_lo, j < lin_hi)
    pl.when(linear)(lambda: tile(lambda x: x))
    pl.when(jnp.logical_not(linear))(lambda: tile(_silu))


def _hgrn_gate_kernel(h_ref, sh_ref, sc_ref, w_ref, lbp_ref, k_ref, lf_ref, u_ref, *, layer):
    j = pl.program_id(1)

    @pl.when(j == 0)
    def _():
        u_ref[...] = (_layer_norm_rows(h_ref[...]) * (1.0 + sc_ref[...]) + sh_ref[...]).astype(BF16)

    for c0 in range(0, k_ref.shape[1], MXU_DIM):
        cols = slice(c0, c0 + MXU_DIM)
        x = lbp_ref[:, cols]
        e = jnp.exp(x - jnp.max(x, axis=0, keepdims=True))
        lb = jnp.sum(e[1:layer + 1], axis=0, keepdims=True) / jnp.sum(e, axis=0, keepdims=True)

        f_pre = jnp.dot(u_ref[...], w_ref[:, cols], preferred_element_type=F32)
        gate = jax.nn.sigmoid(f_pre)
        k_ref[:, cols] = ((1.0 - lb) * (1.0 - gate)).astype(BF16)
        lf_ref[:, cols] = jnp.log(lb + (1.0 - lb) * gate)


def _hgrn_inproj(h, mod3, row_of_tile, w_act, w_gate, lb_params, *, layer, tm, tn):
    m, d = h.shape
    common = [
        pl.BlockSpec((tm, d), lambda i, j: (i, 0)),
        _mod_spec(0, d, row_of_tile),
        _mod_spec(1, d, row_of_tile),
        pl.BlockSpec((d, tn), lambda i, j: (0, j)),
    ]
    tiles = d // tn
    act = pl.pallas_call(
        functools.partial(_hgrn_act_kernel, lin_lo=tiles, lin_hi=2 * tiles),
        grid=(m // tm, w_act.shape[1] // tn),
        in_specs=common,
        out_specs=pl.BlockSpec((tm, tn), lambda i, j: (i, j)),
        out_shape=jax.ShapeDtypeStruct((m, w_act.shape[1]), BF16),
        scratch_shapes=[pltpu.VMEM((tm, d), BF16)],
        compiler_params=_params("arbitrary", "arbitrary"),
        name="hgrn_inproj_act",
    )(h, mod3, mod3, w_act)
    k, logf = pl.pallas_call(
        functools.partial(_hgrn_gate_kernel, layer=layer),
        grid=(m // tm, w_gate.shape[1] // tn),
        in_specs=common + [pl.BlockSpec((None, DEPTH, tn), lambda i, j: (j // tiles, 0, j % tiles))],
        out_specs=[pl.BlockSpec((tm, tn), lambda i, j: (i, j))] * 2,
        out_shape=[jax.ShapeDtypeStruct((m, w_gate.shape[1]), BF16),
                   jax.ShapeDtypeStruct((m, w_gate.shape[1]), F32)],
        scratch_shapes=[pltpu.VMEM((tm, d), BF16)],
        compiler_params=_params("arbitrary", "arbitrary"),
        name="hgrn_inproj_gate",
    )(h, mod3, mod3, w_gate, lb_params)
    return act, k, logf


def _gla_matrices(reverse):
    c = GLA_CHUNK
    r_i = lax.broadcasted_iota(jnp.int32, (c, c), 0)
    c_i = lax.broadcasted_iota(jnp.int32, (c, c), 1)
    same = jnp.bitwise_xor(r_i, c_i)
    earlier = (c_i >= r_i) if reverse else (c_i <= r_i)
    in_block = same < GLA_BLOCK
    one = lambda m: jnp.where(m, 1.0, 0.0).astype(BF16)
    return dict(tri=one(earlier), tri_blk=one(jnp.logical_and(earlier, in_block)), ones_blk=one(in_block),
                diag_mask=jnp.logical_and(earlier, in_block), same=same)


def _gla_direct_block_terms(q, k, v, cum, bad, *, reverse):
    c = GLA_CHUNK
    row = lax.broadcasted_iota(jnp.int32, (c, HEAD_DIM), 0) % GLA_BLOCK
    qb = jnp.where(bad, q, 0.0)
    acc0 = jnp.sum(qb * k, axis=1, keepdims=True) * v

    def offset(dlt, acc):
        shift = (c - dlt) if reverse else dlt
        valid = (row <= GLA_BLOCK - 1 - dlt) if reverse else (row >= dlt)
        k_d = pltpu.roll(k, shift, axis=0)
        v_d = pltpu.roll(v, shift, axis=0)
        cum_d = pltpu.roll(cum, shift, axis=0)
        decay = jnp.exp2(jnp.where(valid, cum - cum_d, -jnp.inf))
        return acc + jnp.sum(qb * k_d * decay, axis=1, keepdims=True) * v_d

    return lax.fori_loop(1, GLA_BLOCK, offset, acc0)


def _gla_prefix_sums(lf, mats):
    lf = lf * LOG2_E
    lf_hi = lf.astype(BF16)
    lf_lo = (lf - lf_hi.astype(F32)).astype(BF16)
    two_term = lambda m: (jnp.dot(m, lf_hi, preferred_element_type=F32)
                          + jnp.dot(m, lf_lo, preferred_element_type=F32))
    return two_term(mats["tri"]), two_term(mats["tri_blk"])


def _gla_chunk(q, k, lf, v_bf, cum_ref, w_blk, mats, *, reverse, robust, want_o):
    c = GLA_CHUNK
    dn_t = (((1,), (1,)), ((), ()))
    cum = cum_ref[...]
    end_row = 0 if reverse else c - 1
    cum_end = cum_ref[end_row:end_row + 1, :]

    qe = (q * jnp.exp2(cum)).astype(BF16)
    k_dec = (k * jnp.exp2(cum_end - cum)).astype(BF16)
    u = lax.dot_general(v_bf, k_dec, (((0,), (0,)), ((), ())), preferred_element_type=F32)
    dec_end = jnp.exp2(cum_end)
    if not want_o:
        return None, None, qe, u, dec_end

    if robust:
        blk_tot = jnp.dot(mats["ones_blk"], (lf * LOG2_E).astype(BF16), preferred_element_type=F32)
        bad = blk_tot < -GLA_SAFE_RANGE
        a_d = jnp.where(bad, 0.0, q * jnp.exp2(w_blk)).astype(BF16)
        b_d = (k * jnp.exp2(jnp.minimum(-w_blk, GLA_SAFE_RANGE + 20.0))).astype(BF16)
    else:
        a_d = (q * jnp.exp2(w_blk)).astype(BF16)
        b_d = (k * jnp.exp2(-w_blk)).astype(BF16)
    scores = jnp.where(mats["diag_mask"], lax.dot_general(a_d, b_d, dn_t, preferred_element_type=F32), 0.0)

    zeros = {m: jnp.zeros((m, HEAD_DIM), F32) for m in GLA_LEVELS}
    for m in GLA_LEVELS:
        a_parts, b_parts = [], []
        for blk in range(c // (2 * m)):
            lo, mid, hi = blk * 2 * m, blk * 2 * m + m, (blk + 1) * 2 * m
            if reverse:
                ref_row = cum_ref[mid:mid + 1, :]
                qa = q[lo:mid] * jnp.exp2(cum[lo:mid] - ref_row)
                kb = k[mid:hi] * jnp.exp2(ref_row - cum[mid:hi])
                a_parts += [qa, zeros[m]]
                b_parts += [zeros[m], kb]
            else:
                ref_row = cum_ref[mid - 1:mid, :]
                qa = q[mid:hi] * jnp.exp2(cum[mid:hi] - ref_row)
                kb = k[lo:mid] * jnp.exp2(ref_row - cum[lo:mid])
                a_parts += [zeros[m], qa]
                b_parts += [kb, zeros[m]]
        a_m = jnp.concatenate(a_parts, axis=0).astype(BF16)
        b_m = jnp.concatenate(b_parts, axis=0).astype(BF16)
        s_m = lax.dot_general(a_m, b_m, dn_t, preferred_element_type=F32)
        if 2 * m < c:
            s_m = jnp.where(mats["same"] < 2 * m, s_m, 0.0)
        scores = scores + s_m
    if robust:
        direct = _gla_direct_block_terms(q, k, v_bf.astype(F32), cum, bad, reverse=reverse)
    else:
        direct = jnp.zeros((c, HEAD_DIM), F32)
    return direct, scores.astype(BF16), qe, u, dec_end


def _gla_kernel(*refs, emit_y, emit_state):
    q_ref, kf_ref, kb_ref, lff_ref, lfb_ref, v_ref, gate_ref, s0_ref, nw_ref = refs[:9]
    pos = 9
    y_ref = s_out_ref = None
    if emit_y:
        y_ref = refs[pos]
        pos += 1
    if emit_state:
        s_out_ref = refs[pos]
        pos += 1
    (of_ref, ob_ref, cumf_ref, cumb_ref, wf_ref, wb_ref, qef_ref, qeb_ref, scf_ref, scb_ref,
     uf_ref, ub_ref, df_ref, db_ref) = refs[pos:pos + 14]

    c = GLA_CHUNK
    n = q_ref.shape[0]
    n_chunks = n // c
    unroll = min(GLA_UNROLL, n_chunks)
    dn_t = (((1,), (1,)), ((), ()))
    chunk_rows = lambda ci: pl.ds(pl.multiple_of(ci * c, c), c)
    mats_f, mats_b = _gla_matrices(False), _gla_matrices(True)

    def prefix_pass():
        def body(ci, _):
            rows = chunk_rows(ci)
            cumf_ref[rows, :], wf_ref[rows, :] = _gla_prefix_sums(lff_ref[rows, :], mats_f)
            cumb_ref[rows, :], wb_ref[rows, :] = _gla_prefix_sums(lfb_ref[rows, :], mats_b)
            return 0

        lax.fori_loop(0, n_chunks, body, 0, unroll=unroll)

    prefix_pass()

    def local_pass(robust):
        def body(ci, _):
            rows = chunk_rows(ci)
            q = q_ref[rows, :].astype(F32)
            v_bf = v_ref[rows, :]
            x_f, sc_f, qe_f, u_f, d_f = _gla_chunk(
                q, kf_ref[rows, :].astype(F32), lff_ref[rows, :], v_bf, cumf_ref.at[rows, :], wf_ref[rows, :],
                mats_f, reverse=False, robust=robust, want_o=emit_y)
            x_b, sc_b, qe_b, u_b, d_b = _gla_chunk(
                q, kb_ref[rows, :].astype(F32), lfb_ref[rows, :], v_bf, cumb_ref.at[rows, :], wb_ref[rows, :],
                mats_b, reverse=True, robust=robust, want_o=emit_y)
            uf_ref[ci] = u_f
            ub_ref[ci] = u_b
            df_ref[ci] = d_f
            db_ref[ci] = d_b
            if emit_y:
                of_ref[rows, :] = x_f
                ob_ref[rows, :] = x_b
                scf_ref[rows, :] = sc_f
                scb_ref[rows, :] = sc_b
                qef_ref[rows, :] = qe_f
                qeb_ref[rows, :] = qe_b
            return 0

        lax.fori_loop(0, n_chunks, body, 0, unroll=1 if robust else unroll)

    worst = jnp.minimum(jnp.min(wf_ref[...]), jnp.min(wb_ref[...]))
    lax.cond(worst < -GLA_SAFE_RANGE, lambda: local_pass(True), lambda: local_pass(False))

    def state_pass(ci, carry):
        st_f, st_b = carry
        cb = n_chunks - 1 - ci
        if emit_y:
            rows_f, rows_b = chunk_rows(ci), chunk_rows(cb)
            of_ref[rows_f, :] += (
                jnp.dot(scf_ref[rows_f, :], v_ref[rows_f, :], preferred_element_type=F32)
                + lax.dot_general(qef_ref[rows_f, :], st_f.astype(BF16), dn_t, preferred_element_type=F32))
            ob_ref[rows_b, :] += (
                jnp.dot(scb_ref[rows_b, :], v_ref[rows_b, :], preferred_element_type=F32)
                + lax.dot_general(qeb_ref[rows_b, :], st_b.astype(BF16), dn_t, preferred_element_type=F32))
        return st_f * df_ref[ci] + uf_ref[ci], st_b * db_ref[cb] + ub_ref[cb]

    st_f, st_b = lax.fori_loop(0, n_chunks, state_pass, (s0_ref[0], s0_ref[1]), unroll=unroll)

    if emit_state:
        s_out_ref[0] = st_f
        s_out_ref[1] = st_b
    if emit_y:
        def readout(ci, _):
            rows = pl.ds(pl.multiple_of(ci * c, c), c)
            tot = of_ref[rows, :] + ob_ref[rows, :]
            ms = jnp.mean(tot * tot, axis=1, keepdims=True)
            y = tot * lax.rsqrt(ms + RMS_EPS) * nw_ref[...] * gate_ref[rows, :].astype(F32)
            y_ref[rows, :] = y.astype(BF16)
            return 0
        lax.fori_loop(0, n_chunks, readout, 0, unroll=unroll)


def _gla(act, k, logf, s0, norm_w, *, n_heads, emit_y, emit_state):
    b, n, _ = act.shape
    hd = HEAD_DIM
    col = lambda off: pl.BlockSpec((None, n, hd), lambda bi, h: (bi, 0, off * n_heads + h))
    st_spec = pl.BlockSpec((None, None, 2, hd, hd), lambda bi, h: (bi, h, 0, 0, 0))
    out_specs, out_shape = [], []
    if emit_y:
        out_specs.append(pl.BlockSpec((None, n, hd), lambda bi, h: (bi, 0, h)))
        out_shape.append(jax.ShapeDtypeStruct((b, n, n_heads * hd), BF16))
    if emit_state:
        out_specs.append(st_spec)
        out_shape.append(jax.ShapeDtypeStruct((b, n_heads, 2, hd, hd), F32))
    outs = pl.pallas_call(
        functools.partial(_gla_kernel, emit_y=emit_y, emit_state=emit_state),
        grid=(b, n_heads),
        in_specs=[col(0), col(0), col(1), col(0), col(1), col(1), col(2), st_spec,
                  pl.BlockSpec((1, hd), lambda bi, h: (0, 0))],
        out_specs=out_specs,
        out_shape=out_shape,
        scratch_shapes=(
            [pltpu.VMEM((n, hd), F32)] * 6
            + [pltpu.VMEM((n, hd), BF16)] * 4
            + [pltpu.VMEM((n // GLA_CHUNK, hd, hd), F32)] * 2
            + [pltpu.VMEM((n // GLA_CHUNK, 1, hd), F32)] * 2),
        compiler_params=_params("arbitrary", "arbitrary"),
        name="gla_scan",
    )(act, k, k, logf, logf, act, act, s0, norm_w.reshape(1, hd))
    return outs


def _rope_head_perm():
    half = ROPE_AXIS_DIM // 2
    first = [0, 2 * half, 4 * half, 6 * half]
    order = first + [s + half for s in first]
    return np.concatenate([np.arange(s, s + half) for s in order])


def _rope_tables(n):
    half = ROPE_AXIS_DIM // 2
    inv = (1.0 / (ROPE_BASE ** (np.arange(0, ROPE_AXIS_DIM, 2, dtype=np.float32) / ROPE_AXIS_DIM))).astype(np.float32)
    pos = np.arange(n)
    ang_r = (pos // GRID_W).astype(np.float32)[:, None] * inv[None, :]
    ang_c = (pos % GRID_W).astype(np.float32)[:, None] * inv[None, :]
    ang = np.concatenate([ang_r, ang_c] * 4, axis=1)
    sign = np.concatenate([-np.ones(4 * half, np.float32), np.ones(4 * half, np.float32)])
    return jnp.asarray(np.cos(ang), F32), jnp.asarray(np.sin(ang) * sign[None, :], F32)


def kernel(x, c, ctx, c_ctx, mod_w, mod_b, ln_mix_g, ln_mix_b, ln_ffn_g, ln_ffn_b, even_w_in, even_w_out, diff_lambda, diff_subln, hgrn_w_in, hgrn_w_out, hgrn_lower_bounds, hgrn_norm, ffn_w_up, ffn_conv_w, ffn_conv_b, ffn_w_down):
    b, n_lat, d = x.shape
    n_ctx = ctx.shape[1]
    assert mod_w.shape[0] == DEPTH and n_lat % GRID_W == 0
    d_ff = ffn_w_down.shape[1]
    d_ff_pad = -(-d_ff // (2 * MXU_DIM)) * (2 * MXU_DIM)
    diff_width = even_w_out.shape[2] * 3 // 4
    four_width = even_w_in.shape[2] - 3 * diff_width
    n_diff_heads = diff_width // HEAD_DIM
    n_hgrn_heads = d // HEAD_DIM

    cc = jnp.concatenate([c, c_ctx[None, :], jnp.zeros((2 * SUBLANES - b - 1, d), F32)], axis=0)
    mod = _modulation(cc, mod_w, mod_b)
    mod = mod.reshape(DEPTH, cc.shape[0], 1, 6 * d)

    h_lat = x.reshape(b * n_lat, d)
    h_ctx = ctx.reshape(b * n_ctx, d)

    tm_wide_lat, tm_lat, tm_ctx = min(1024, n_lat), min(512, n_lat), min(256, n_ctx)
    lat_row = lambda tm: (lambda i: (i * tm) // n_lat)
    ctx_row = lambda tm: (lambda i: b)
    down_k_tiles = 4 if d_ff_pad % (4 * LANES) == 0 else d_ff_pad // 512

    def ffn(h, mod3, row_fn, layer, seq_len, tm_up, tm_down):
        pad = d_ff_pad - d_ff
        w_up = ffn_w_up[layer]
        wa = jnp.pad(w_up[:, :d_ff], ((0, 0), (0, pad))).astype(BF16)
        wv = jnp.pad(w_up[:, d_ff:], ((0, 0), (0, pad))).astype(BF16)
        cw = jnp.pad(ffn_conv_w[layer], ((0, 0), (0, pad)))
        cb = jnp.pad(ffn_conv_b[layer], (0, pad)).reshape(1, d_ff_pad)
        wd = jnp.pad(ffn_w_down[layer], ((0, pad), (0, 0))).astype(BF16)
        g = _ffn_up(h, mod3, row_fn(tm_up), wa, wv, cw, cb, seq_len=seq_len, tm=tm_up, tn=512)
        return _proj_residual([g], [wd], h, mod3, 5, row_fn(tm_down), ln_ffn_g[layer], ln_ffn_b[layer],
                              tm=tm_down, n_k=down_k_tiles)

    layer = 0
    lam_init = 0.8 - 0.6 * math.exp(-0.3 * layer)
    perm = _rope_head_perm()
    qk_cols = (np.arange(2 * n_diff_heads)[:, None] * HEAD_DIM + perm[None, :]).reshape(-1)
    cols = np.concatenate([qk_cols, np.arange(2 * diff_width, even_w_in.shape[2])])
    w_in = even_w_in[0][:, cols].astype(BF16)
    w_out = even_w_out[0].astype(BF16)
    cos_t, sin_t = _rope_tables(n_lat)
    mod3 = mod[layer]

    col_scale = jnp.asarray(np.where(np.arange(w_in.shape[1]) < diff_width, QK_SCALE, 1.0)[None, :], F32)
    p_lat = _even_inproj(h_lat, mod3, lat_row(tm_wide_lat), w_in, col_scale, cos_t, sin_t, qk_width=diff_width,
                         use_rope=True, tm=tm_wide_lat, tn=1024).reshape(b, n_lat, -1)
    p_ctx = _even_inproj(h_ctx, mod3, ctx_row(tm_ctx), w_in, col_scale, cos_t[:SUBLANES], sin_t[:SUBLANES],
                         qk_width=diff_width, use_rope=False, tm=tm_ctx, tn=1024).reshape(b, n_ctx, -1)

    attn_lat = _diff_attention(p_lat, [p_ctx, p_lat], diff_lambda[0], diff_subln[0], lam_init,
                               n_heads=n_diff_heads, tq=min(512, n_lat))
    attn_ctx = _diff_attention(p_ctx, [p_ctx], diff_lambda[0], diff_subln[0], lam_init,
                               n_heads=n_diff_heads, tq=n_ctx)
    four_block = 3 * diff_width // four_width
    four_lat = _fourier_mix(p_lat, four_block, four_width, tm=tm_lat)
    four_ctx = _fourier_mix(p_ctx, four_block, four_width, tm=n_ctx)

    w_list = [w_out[:diff_width], w_out[diff_width:]]
    h_lat = _proj_residual([attn_lat.reshape(b * n_lat, -1), four_lat.reshape(b * n_lat, -1)], w_list,
                           h_lat, mod3, 2, lat_row(tm_lat), ln_mix_g[layer], ln_mix_b[layer], tm=tm_lat)
    h_ctx = _proj_residual([attn_ctx.reshape(b * n_ctx, -1), four_ctx.reshape(b * n_ctx, -1)], w_list,
                           h_ctx, mod3, 2, ctx_row(tm_ctx), ln_mix_g[layer], ln_mix_b[layer], tm=tm_ctx)
    h_lat = ffn(h_lat, mod3, lat_row, layer, n_lat, tm_wide_lat, tm_lat)
    h_ctx = ffn(h_ctx, mod3, ctx_row, layer, n_ctx, tm_ctx, tm_ctx)

    layer = 1
    mod3 = mod[layer]
    hw = hgrn_w_in[0]
    w_act = jnp.concatenate([hw[:, :d], hw[:, 3 * d:]], axis=1).astype(BF16)
    w_gate = hw[:, d:3 * d].astype(BF16)
    wrap = lambda t, n: t.reshape(b, n, -1)
    act_c, k_c, lf_c = _hgrn_inproj(h_ctx, mod3, ctx_row(tm_ctx), w_act, w_gate, hgrn_lower_bounds,
                                    layer=layer, tm=tm_ctx, tn=1024)
    act_l, k_l, lf_l = _hgrn_inproj(h_lat, mod3, lat_row(tm_wide_lat), w_act, w_gate, hgrn_lower_bounds,
                                    layer=layer, tm=tm_wide_lat, tn=1024)
    zero_state = jnp.zeros((b, n_hgrn_heads, 2, HEAD_DIM, HEAD_DIM), F32)
    (s_ctx,) = _gla(wrap(act_c, n_ctx), wrap(k_c, n_ctx), wrap(lf_c, n_ctx), zero_state, hgrn_norm[0],
                    n_heads=n_hgrn_heads, emit_y=False, emit_state=True)
    (y_lat,) = _gla(wrap(act_l, n_lat), wrap(k_l, n_lat), wrap(lf_l, n_lat), s_ctx, hgrn_norm[0],
                    n_heads=n_hgrn_heads, emit_y=True, emit_state=False)
    h_lat = _proj_residual([y_lat.reshape(b * n_lat, d)], [hgrn_w_out[0].astype(BF16)], h_lat, mod3, 2,
                           lat_row(tm_lat), ln_mix_g[layer], ln_mix_b[layer], tm=tm_lat)
    h_lat = ffn(h_lat, mod3, lat_row, layer, n_lat, tm_wide_lat, tm_lat)
    return h_lat.reshape(b, n_lat, d)
```

```python
import functools
import math

import numpy as np
import jax
import jax.numpy as jnp
from jax import lax
from jax.experimental import pallas as pl
from jax.experimental.pallas import tpu as pltpu

F32 = jnp.float32
BF16 = jnp.bfloat16

LANES = 128
SUBLANES = 8
MXU_DIM = 256
VMEM_LIMIT_BYTES = 56 * 1024 * 1024

GRID_W = 64
DIFF_QK_DIM = 64
HEAD_DIM = 128
ROPE_AXIS_DIM = DIFF_QK_DIM // 2
ROPE_BASE = 10000.0
CONV_W = 3
LN_EPS = 1e-6
RMS_EPS = 1e-5
DEPTH = 2
ALPHA = (2.0 * DEPTH) ** 0.25
QK_SCALE = DIFF_QK_DIM ** -0.5 * math.log2(math.e)
ATTN_SAFE_LOG2 = 60.0

GLA_CHUNK = 128
GLA_LEVELS = (64, 32, 16)
GLA_BLOCK = 16
LOG2_E = math.log2(math.e)
GLA_SAFE_RANGE = 80.0
GLA_UNROLL = 4


def _params(*dims):
    return pltpu.CompilerParams(dimension_semantics=dims, vmem_limit_bytes=VMEM_LIMIT_BYTES)


def _layer_norm_rows(x):
    mu = jnp.mean(x, axis=-1, keepdims=True)
    xc = x - mu
    var = jnp.mean(xc * xc, axis=-1, keepdims=True)
    return xc * lax.rsqrt(var + LN_EPS)


def _silu(x):
    return x * jax.nn.sigmoid(x)


def _mod_kernel(c_ref, w_ref, b_ref, o_ref):
    x = _silu(c_ref[...]).astype(BF16)
    o_ref[...] = jnp.dot(x, w_ref[...].astype(BF16), preferred_element_type=F32) + b_ref[...]


def _modulation(cc, mod_w, mod_b, tn=1024):
    depth, d, n = mod_w.shape
    rows = cc.shape[0]
    return pl.pallas_call(
        _mod_kernel,
        grid=(depth, n // tn),
        in_specs=[
            pl.BlockSpec((rows, d), lambda l, j: (0, 0)),
            pl.BlockSpec((None, d, tn), lambda l, j: (l, 0, j)),
            pl.BlockSpec((None, 1, tn), lambda l, j: (l, 0, j)),
        ],
        out_specs=pl.BlockSpec((None, rows, tn), lambda l, j: (l, 0, j)),
        out_shape=jax.ShapeDtypeStruct((depth, rows, n), F32),
        compiler_params=_params("arbitrary", "arbitrary"),
        name="modulation",
    )(cc, mod_w, mod_b.reshape(depth, 1, n))


def _mod_spec(chunk, d, row_of_tile):
    return pl.BlockSpec((None, 1, d), lambda i, *_: (row_of_tile(i), 0, chunk))


def _even_inproj_kernel(h_ref, sh_ref, sc_ref, w_ref, cscale_ref, cos_ref, sin_ref, o_ref, u_ref,
                        *, n_rope_tiles, use_rope):
    j = pl.program_id(1)

    @pl.when(j == 0)
    def _():
        u_ref[...] = (_layer_norm_rows(h_ref[...]) * (1.0 + sc_ref[...]) + sh_ref[...]).astype(BF16)

    def tile(rope):
        for c0 in range(0, o_ref.shape[1], MXU_DIM):
            cols = slice(c0, c0 + MXU_DIM)
            y = jnp.dot(u_ref[...], w_ref[:, cols], preferred_element_type=F32)
            if rope:
                parts = []
                for hh in range(MXU_DIM // HEAD_DIM):
                    yh = y[:, hh * HEAD_DIM:(hh + 1) * HEAD_DIM]
                    parts.append(yh * cos_ref[...] + pltpu.roll(yh, HEAD_DIM // 2, axis=1) * sin_ref[...])
                y = jnp.concatenate(parts, axis=1)
            o_ref[:, cols] = (y * cscale_ref[:, cols]).astype(BF16)

    if use_rope:
        pl.when(j < n_rope_tiles)(lambda: tile(True))
        pl.when(j >= n_rope_tiles)(lambda: tile(False))
    else:
        tile(False)


def _even_inproj(h, mod3, row_of_tile, w, col_scale, cos_t, sin_t, *, qk_width, use_rope, tm, tn):
    m, d = h.shape
    n = w.shape[1]
    n_pos_tiles = cos_t.shape[0] // tm if use_rope else 1
    tab_rows = tm if use_rope else cos_t.shape[0]
    kern = functools.partial(_even_inproj_kernel, n_rope_tiles=2 * qk_width // tn, use_rope=use_rope)
    return pl.pallas_call(
        kern,
        grid=(m // tm, n // tn),
        in_specs=[
            pl.BlockSpec((tm, d), lambda i, j: (i, 0)),
            _mod_spec(0, d, row_of_tile),
            _mod_spec(1, d, row_of_tile),
            pl.BlockSpec((d, tn), lambda i, j: (0, j)),
            pl.BlockSpec((1, tn), lambda i, j: (0, j)),
            pl.BlockSpec((tab_rows, HEAD_DIM), lambda i, j: (i % n_pos_tiles, 0)),
            pl.BlockSpec((tab_rows, HEAD_DIM), lambda i, j: (i % n_pos_tiles, 0)),
        ],
        out_specs=pl.BlockSpec((tm, tn), lambda i, j: (i, j)),
        out_shape=jax.ShapeDtypeStruct((m, n), BF16),
        scratch_shapes=[pltpu.VMEM((tm, d), BF16)],
        compiler_params=_params("arbitrary", "arbitrary"),
        name="even_inproj",
    )(h, mod3, mod3, w, col_scale, cos_t, sin_t)


def _diff_attn_kernel(*refs, n_seg, lam_init):
    q_ref = refs[0]
    k_refs = refs[1:1 + n_seg]
    v_refs = refs[1 + n_seg:1 + 2 * n_seg]
    lam_ref, subln_ref, o_ref, knorm_ref = refs[1 + 2 * n_seg:]

    @pl.when(pl.program_id(2) == 0)
    def _():
        worst = None
        for k_ref in k_refs:
            k32 = k_ref[...].astype(F32)
            m = jnp.max(jnp.sum(k32 * k32, axis=1, keepdims=True), axis=0, keepdims=True)
            worst = m if worst is None else jnp.maximum(worst, m)
        knorm_ref[...] = jnp.broadcast_to(worst, knorm_ref.shape)

    lv = lam_ref[...]
    lam = (jnp.exp(jnp.sum(lv[0:1] * lv[1:2], axis=1, keepdims=True))
           - jnp.exp(jnp.sum(lv[2:3] * lv[3:4], axis=1, keepdims=True)) + lam_init)

    q = q_ref[...]
    lane = lax.broadcasted_iota(jnp.int32, q.shape, 1)
    first_map = (lane % DIFF_QK_DIM) < ROPE_AXIS_DIM
    zero = jnp.zeros_like(q)
    qs = (jnp.where(first_map, q, zero), jnp.where(first_map, zero, q))

    dn = (((1,), (1,)), ((), ()))
    q32 = q.astype(F32)
    qnorm = jnp.max(jnp.sum(q32 * q32, axis=1, keepdims=True), axis=0, keepdims=True)
    small_scores = jnp.max(qnorm * knorm_ref[0:1, 0:1]) <= ATTN_SAFE_LOG2 ** 2

    def softmax_times_v(shift_by_max):
        out = []
        for qm in qs:
            s = [lax.dot_general(qm, k_ref[...], dn, preferred_element_type=F32) for k_ref in k_refs]
            if shift_by_max:
                mx = functools.reduce(jnp.maximum, [jnp.max(x, axis=1, keepdims=True) for x in s])
                s = [x - mx for x in s]
            num = den = None
            for seg in range(n_seg):
                e = jnp.exp2(s[seg])
                d = jnp.sum(e, axis=1, keepdims=True)
                part = jnp.dot(e.astype(BF16), v_refs[seg][...], preferred_element_type=F32)
                den = d if den is None else den + d
                num = part if num is None else num + part
            out += [num, den]
        return tuple(out)

    n1, d1, n2, d2 = lax.cond(small_scores, lambda: softmax_times_v(False), lambda: softmax_times_v(True))
    acc = n1 * (1.0 / d1) - n2 * (lam / d2)
    ms = jnp.mean(acc * acc, axis=1, keepdims=True)
    y = acc * lax.rsqrt(ms + RMS_EPS) * subln_ref[...] * (1.0 - lam_init)
    o_ref[...] = y.astype(BF16)


def _diff_attention(q_src, kv_srcs, lam_vecs, subln, lam_init, *, n_heads, tq):
    b, nq, _ = q_src.shape
    n_seg = len(kv_srcs)
    k_specs = [pl.BlockSpec((None, s.shape[1], HEAD_DIM), lambda bi, h, t: (bi, 0, n_heads + h))
               for s in kv_srcs]
    v_specs = [pl.BlockSpec((None, s.shape[1], HEAD_DIM), lambda bi, h, t: (bi, 0, 2 * n_heads + h))
               for s in kv_srcs]
    return pl.pallas_call(
        functools.partial(_diff_attn_kernel, n_seg=n_seg, lam_init=lam_init),
        grid=(b, n_heads, nq // tq),
        in_specs=[pl.BlockSpec((None, tq, HEAD_DIM), lambda bi, h, t: (bi, t, h))] + k_specs + v_specs + [
            pl.BlockSpec(lam_vecs.shape, lambda bi, h, t: (0, 0)),
            pl.BlockSpec((1, HEAD_DIM), lambda bi, h, t: (0, 0)),
        ],
        out_specs=pl.BlockSpec((None, tq, HEAD_DIM), lambda bi, h, t: (bi, t, h)),
        out_shape=jax.ShapeDtypeStruct((b, nq, n_heads * HEAD_DIM), BF16),
        scratch_shapes=[pltpu.VMEM((SUBLANES, LANES), F32)],
        compiler_params=_params("arbitrary", "arbitrary", "arbitrary"),
        name="diff_attention",
    )(q_src, *kv_srcs, *kv_srcs, lam_vecs, subln.reshape(1, HEAD_DIM))


def _fourier_kernel(x_ref, dn_ref, cs_ref, o_ref, z_ref):
    n = x_ref.shape[0]

    @pl.when(pl.program_id(1) == 0)
    def _():
        for g in range(x_ref.shape[1] // HEAD_DIM):
            cols = slice(g * HEAD_DIM, (g + 1) * HEAD_DIM)
            zc = jnp.dot(x_ref[:, cols], cs_ref[...], preferred_element_type=F32)
            z_ref[0:n, cols] = zc[:, :HEAD_DIM].astype(BF16)
            z_ref[n:2 * n, cols] = zc[:, HEAD_DIM:].astype(BF16)

    o_ref[...] = jnp.dot(dn_ref[...], z_ref[...], preferred_element_type=F32).astype(BF16)


def _dft_tables(n):
    j = np.arange(n, dtype=np.int64)
    ang = 2.0 * np.pi * ((j[:, None] * j[None, :]) % n).astype(np.float64) / n
    return np.cos(ang) / math.sqrt(n), np.sin(ang) / math.sqrt(n)


def _fourier_mix(src, col_block, width, *, tm):
    b, n, _ = src.shape
    cn, sn = _dft_tables(n)
    cc, sc = _dft_tables(HEAD_DIM)
    dn = jnp.asarray(np.concatenate([cn, -sn], axis=1), dtype=BF16)
    cs = jnp.asarray(np.concatenate([cc, sc], axis=1), dtype=BF16)
    return pl.pallas_call(
        _fourier_kernel,
        grid=(b, n // tm),
        in_specs=[
            pl.BlockSpec((None, n, width), lambda bi, t: (bi, 0, col_block)),
            pl.BlockSpec((tm, 2 * n), lambda bi, t: (t, 0)),
            pl.BlockSpec((HEAD_DIM, 2 * HEAD_DIM), lambda bi, t: (0, 0)),
        ],
        out_specs=pl.BlockSpec((None, tm, width), lambda bi, t: (bi, t, 0)),
        out_shape=jax.ShapeDtypeStruct((b, n, width), BF16),
        scratch_shapes=[pltpu.VMEM((2 * n, width), BF16)],
        compiler_params=_params("arbitrary", "arbitrary"),
        name="fourier_mix",
    )(src, dn, cs)


def _proj_residual_kernel(*refs, n_in, n_k):
    a_refs = refs[:n_in]
    w_refs = refs[n_in:2 * n_in]
    h_ref, gate_ref, g_ref, b_ref, o_ref = refs[2 * n_in:2 * n_in + 5]
    k = pl.program_id(1)
    tm, d = o_ref.shape
    col_block, row_block = 2 * MXU_DIM, MXU_DIM

    def accumulate(first):
        for c0 in range(0, d, col_block):
            cols = slice(c0, c0 + col_block)
            part = None
            for a_ref, w_ref in zip(a_refs, w_refs):
                p = jnp.dot(a_ref[...], w_ref[:, cols], preferred_element_type=F32)
                part = p if part is None else part + p
            if first:
                o_ref[:, cols] = part
            else:
                o_ref[:, cols] += part

    def finish():
        for r0 in range(0, tm, row_block):
            rows = slice(r0, r0 + row_block)
            x = ALPHA * h_ref[rows, :] + gate_ref[...] * o_ref[rows, :]
            o_ref[rows, :] = _layer_norm_rows(x) * g_ref[...] + b_ref[...]

    if n_k == 1:
        accumulate(True)
        finish()
    else:
        pl.when(k == 0)(lambda: accumulate(True))
        pl.when(k > 0)(lambda: accumulate(False))
        pl.when(k == n_k - 1)(finish)


def _proj_residual(a_list, w_list, h, mod3, gate_chunk, row_of_tile, ln_g, ln_b, *, tm, n_k=1):
    m, d = h.shape
    in_specs = []
    for a in a_list:
        in_specs.append(pl.BlockSpec((tm, a.shape[1] // n_k), lambda i, k: (i, k)))
    for a, (w, first_row) in zip(a_list, w_list):
        rows = a.shape[1] // n_k
        in_specs.append(pl.BlockSpec((rows, d), lambda i, k, blk=first_row // rows: (blk + k, 0)))
    in_specs += [
        pl.BlockSpec((tm, d), lambda i, k: (i, 0)),
        _mod_spec(gate_chunk, d, row_of_tile),
        pl.BlockSpec((1, d), lambda i, k: (0, 0)),
        pl.BlockSpec((1, d), lambda i, k: (0, 0)),
    ]
    return pl.pallas_call(
        functools.partial(_proj_residual_kernel, n_in=len(a_list), n_k=n_k),
        grid=(m // tm, n_k),
        in_specs=in_specs,
        out_specs=pl.BlockSpec((tm, d), lambda i, k: (i, 0)),
        out_shape=jax.ShapeDtypeStruct((m, d), F32),
        compiler_params=_params("arbitrary", "arbitrary"),
        name="proj_residual",
    )(*a_list, *[w for w, _ in w_list], h, mod3, ln_g.reshape(1, d), ln_b.reshape(1, d))


def _ffn_up_kernel(h_ref, hp_ref, hn_ref, sh_ref, sc_ref, wa_ref, wv_ref, cw_ref, cb_ref, o_ref,
                   u_ref, uh_ref, *, tiles_per_seq):
    i = pl.program_id(0)
    j = pl.program_id(1)
    tm = h_ref.shape[0]

    @pl.when(j == 0)
    def _():
        scale = 1.0 + sc_ref[...]
        u_ref[...] = (_layer_norm_rows(h_ref[...]) * scale + sh_ref[...]).astype(BF16)
        uh_ref[0:SUBLANES, :] = (_layer_norm_rows(hp_ref[...]) * scale + sh_ref[...]).astype(BF16)
        uh_ref[SUBLANES:, :] = (_layer_norm_rows(hn_ref[...]) * scale + sh_ref[...]).astype(BF16)

    t_in_seq = i % tiles_per_seq
    row = lax.broadcasted_iota(jnp.int32, (tm, MXU_DIM), 0)
    for c0 in range(0, o_ref.shape[1], MXU_DIM):
        cols = slice(c0, c0 + MXU_DIM)
        a = jnp.dot(u_ref[...], wa_ref[:, cols], preferred_element_type=F32)
        v = jnp.dot(u_ref[...], wv_ref[:, cols], preferred_element_type=F32)
        halo = jnp.dot(uh_ref[...], wa_ref[:, cols], preferred_element_type=F32)
        prev_row = jnp.where(t_in_seq > 0, halo[SUBLANES - 1:SUBLANES], 0.0)
        next_row = jnp.where(t_in_seq < tiles_per_seq - 1, halo[SUBLANES:SUBLANES + 1], 0.0)
        a_prev = jnp.where(row == 0, prev_row, pltpu.roll(a, 1, axis=0))
        a_next = jnp.where(row == tm - 1, next_row, pltpu.roll(a, tm - 1, axis=0))
        cw = cw_ref[:, cols]
        conv = cb_ref[:, cols] + a_prev * cw[0:1] + a * cw[1:2] + a_next * cw[2:3]
        gelu = 0.5 * conv * (1.0 + lax.erf(conv * math.sqrt(0.5)))
        o_ref[:, cols] = (gelu * v).astype(BF16)


def _ffn_up(h, mod3, row_of_tile, wa, wv, conv_w, conv_b, *, seq_len, tm, tn):
    m, d = h.shape
    n = wa.shape[1]
    tiles_per_seq = seq_len // tm
    blocks_per_tile = tm // SUBLANES
    n_blocks = m // SUBLANES
    kern = functools.partial(_ffn_up_kernel, tiles_per_seq=tiles_per_seq)
    return pl.pallas_call(
        kern,
        grid=(m // tm, n // tn),
        in_specs=[
            pl.BlockSpec((tm, d), lambda i, j: (i, 0)),
            pl.BlockSpec((SUBLANES, d), lambda i, j: (jnp.maximum(i * blocks_per_tile - 1, 0), 0)),
            pl.BlockSpec((SUBLANES, d), lambda i, j: (jnp.minimum((i + 1) * blocks_per_tile, n_blocks - 1), 0)),
            _mod_spec(3, d, row_of_tile),
            _mod_spec(4, d, row_of_tile),
            pl.BlockSpec((d, tn), lambda i, j: (0, j)),
            pl.BlockSpec((d, tn), lambda i, j: (0, j)),
            pl.BlockSpec((CONV_W, tn), lambda i, j: (0, j)),
            pl.BlockSpec((1, tn), lambda i, j: (0, j)),
        ],
        out_specs=pl.BlockSpec((tm, tn), lambda i, j: (i, j)),
        out_shape=jax.ShapeDtypeStruct((m, n), BF16),
        scratch_shapes=[pltpu.VMEM((tm, d), BF16), pltpu.VMEM((2 * SUBLANES, d), BF16)],
        compiler_params=_params("arbitrary", "arbitrary"),
        name="ffn_up",
    )(h, h, h, mod3, mod3, wa, wv, conv_w, conv_b)


def _hgrn_act_kernel(h_ref, sh_ref, sc_ref, w_ref, o_ref, u_ref, *, lin_lo, lin_hi):
    j = pl.program_id(1)

    @pl.when(j == 0)
    def _():
        u_ref[...] = (_layer_norm_rows(h_ref[...]) * (1.0 + sc_ref[...]) + sh_ref[...]).astype(BF16)

    def tile(activation):
        for c0 in range(0, o_ref.shape[1], MXU_DIM):
            cols = slice(c0, c0 + MXU_DIM)
            acc = jnp.dot(u_ref[...], w_ref[:, cols], preferred_element_type=F32)
            o_ref[:, cols] = activation(acc).astype(BF16)

    linear = jnp.logical_and(j >= lin_lo, j < lin_hi)
    pl.when(linear)(lambda: tile(lambda x: x))
    pl.when(jnp.logical_not(linear))(lambda: tile(_silu))


def _hgrn_gate_kernel(h_ref, sh_ref, sc_ref, w_ref, lbp_ref, k_ref, lf_ref, u_ref, *, layer):
    j = pl.program_id(1)

    @pl.when(j == 0)
    def _():
        u_ref[...] = (_layer_norm_rows(h_ref[...]) * (1.0 + sc_ref[...]) + sh_ref[...]).astype(BF16)

    for c0 in range(0, k_ref.shape[1], MXU_DIM):
        cols = slice(c0, c0 + MXU_DIM)
        x = lbp_ref[:, cols]
        e = jnp.exp(x - jnp.max(x, axis=0, keepdims=True))
        lb = jnp.sum(e[1:layer + 1], axis=0, keepdims=True) / jnp.sum(e, axis=0, keepdims=True)

        f_pre = jnp.dot(u_ref[...], w_ref[:, cols], preferred_element_type=F32)
        gate = jax.nn.sigmoid(f_pre)
        k_ref[:, cols] = ((1.0 - lb) * (1.0 - gate)).astype(BF16)
        lf_ref[:, cols] = jnp.log(lb + (1.0 - lb) * gate)


def _hgrn_inproj(h, mod3, row_of_tile, w, lb_params, *, layer, tm, tn):
    m, d = h.shape
    tiles = d // tn
    common = [
        pl.BlockSpec((tm, d), lambda i, j: (i, 0)),
        _mod_spec(0, d, row_of_tile),
        _mod_spec(1, d, row_of_tile),
    ]
    act_w = pl.BlockSpec((d, tn), lambda i, j: (0, jnp.where(j < tiles, j, j + 2 * tiles)))
    gate_w = pl.BlockSpec((d, tn), lambda i, j: (0, j + tiles))
    act = pl.pallas_call(
        functools.partial(_hgrn_act_kernel, lin_lo=tiles, lin_hi=2 * tiles),
        grid=(m // tm, 3 * tiles),
        in_specs=common + [act_w],
        out_specs=pl.BlockSpec((tm, tn), lambda i, j: (i, j)),
        out_shape=jax.ShapeDtypeStruct((m, 3 * d), BF16),
        scratch_shapes=[pltpu.VMEM((tm, d), BF16)],
        compiler_params=_params("arbitrary", "arbitrary"),
        name="hgrn_inproj_act",
    )(h, mod3, mod3, w)
    k, logf = pl.pallas_call(
        functools.partial(_hgrn_gate_kernel, layer=layer),
        grid=(m // tm, 2 * tiles),
        in_specs=common + [gate_w, pl.BlockSpec((None, DEPTH, tn), lambda i, j: (j // tiles, 0, j % tiles))],
        out_specs=[pl.BlockSpec((tm, tn), lambda i, j: (i, j))] * 2,
        out_shape=[jax.ShapeDtypeStruct((m, 2 * d), BF16), jax.ShapeDtypeStruct((m, 2 * d), F32)],
        scratch_shapes=[pltpu.VMEM((tm, d), BF16)],
        compiler_params=_params("arbitrary", "arbitrary"),
        name="hgrn_inproj_gate",
    )(h, mod3, mod3, w, lb_params)
    return act, k, logf


def _gla_matrices(reverse):
    c = GLA_CHUNK
    r_i = lax.broadcasted_iota(jnp.int32, (c, c), 0)
    c_i = lax.broadcasted_iota(jnp.int32, (c, c), 1)
    same = jnp.bitwise_xor(r_i, c_i)
    earlier = (c_i >= r_i) if reverse else (c_i <= r_i)
    in_block = same < GLA_BLOCK
    one = lambda m: jnp.where(m, 1.0, 0.0).astype(BF16)
    return dict(tri=one(earlier), tri_blk=one(jnp.logical_and(earlier, in_block)), ones_blk=one(in_block),
                diag_mask=jnp.logical_and(earlier, in_block), same=same)


def _gla_direct_block_terms(q, k, v, cum, bad, *, reverse):
    c = GLA_CHUNK
    row = lax.broadcasted_iota(jnp.int32, (c, HEAD_DIM), 0) % GLA_BLOCK
    qb = jnp.where(bad, q, 0.0)
    acc0 = jnp.sum(qb * k, axis=1, keepdims=True) * v

    def offset(dlt, acc):
        shift = (c - dlt) if reverse else dlt
        valid = (row <= GLA_BLOCK - 1 - dlt) if reverse else (row >= dlt)
        k_d = pltpu.roll(k, shift, axis=0)
        v_d = pltpu.roll(v, shift, axis=0)
        cum_d = pltpu.roll(cum, shift, axis=0)
        decay = jnp.exp2(jnp.where(valid, cum - cum_d, -jnp.inf))
        return acc + jnp.sum(qb * k_d * decay, axis=1, keepdims=True) * v_d

    return lax.fori_loop(1, GLA_BLOCK, offset, acc0)


def _gla_prefix_sums(lf, mats):
    lf = lf * LOG2_E
    lf_hi = lf.astype(BF16)
    lf_lo = (lf - lf_hi.astype(F32)).astype(BF16)
    two_term = lambda m: (jnp.dot(m, lf_hi, preferred_element_type=F32)
                          + jnp.dot(m, lf_lo, preferred_element_type=F32))
    return two_term(mats["tri"]), two_term(mats["tri_blk"])


def _gla_chunk(q, k, lf, v_bf, cum_ref, w_blk, mats, *, reverse, robust, want_o):
    c = GLA_CHUNK
    dn_t = (((1,), (1,)), ((), ()))
    cum = cum_ref[...]
    end_row = 0 if reverse else c - 1
    cum_end = cum_ref[end_row:end_row + 1, :]

    qe = (q * jnp.exp2(cum)).astype(BF16)
    k_dec = (k * jnp.exp2(cum_end - cum)).astype(BF16)
    u = lax.dot_general(v_bf, k_dec, (((0,), (0,)), ((), ())), preferred_element_type=F32)
    dec_end = jnp.exp2(cum_end)
    if not want_o:
        return None, None, qe, u, dec_end

    if robust:
        blk_tot = jnp.dot(mats["ones_blk"], (lf * LOG2_E).astype(BF16), preferred_element_type=F32)
        bad = blk_tot < -GLA_SAFE_RANGE
        a_d = jnp.where(bad, 0.0, q * jnp.exp2(w_blk)).astype(BF16)
        b_d = (k * jnp.exp2(jnp.minimum(-w_blk, GLA_SAFE_RANGE + 20.0))).astype(BF16)
    else:
        a_d = (q * jnp.exp2(w_blk)).astype(BF16)
        b_d = (k * jnp.exp2(-w_blk)).astype(BF16)
    scores = jnp.where(mats["diag_mask"], lax.dot_general(a_d, b_d, dn_t, preferred_element_type=F32), 0.0)

    zeros = {m: jnp.zeros((m, HEAD_DIM), F32) for m in GLA_LEVELS}
    for m in GLA_LEVELS:
        a_parts, b_parts = [], []
        for blk in range(c // (2 * m)):
            lo, mid, hi = blk * 2 * m, blk * 2 * m + m, (blk + 1) * 2 * m
            if reverse:
                ref_row = cum_ref[mid:mid + 1, :]
                qa = q[lo:mid] * jnp.exp2(cum[lo:mid] - ref_row)
                kb = k[mid:hi] * jnp.exp2(ref_row - cum[mid:hi])
                a_parts += [qa, zeros[m]]
                b_parts += [zeros[m], kb]
            else:
                ref_row = cum_ref[mid - 1:mid, :]
                qa = q[mid:hi] * jnp.exp2(cum[mid:hi] - ref_row)
                kb = k[lo:mid] * jnp.exp2(ref_row - cum[lo:mid])
                a_parts += [zeros[m], qa]
                b_parts += [kb, zeros[m]]
        a_m = jnp.concatenate(a_parts, axis=0).astype(BF16)
        b_m = jnp.concatenate(b_parts, axis=0).astype(BF16)
        s_m = lax.dot_general(a_m, b_m, dn_t, preferred_element_type=F32)
        if 2 * m < c:
            s_m = jnp.where(mats["same"] < 2 * m, s_m, 0.0)
        scores = scores + s_m
    if robust:
        direct = _gla_direct_block_terms(q, k, v_bf.astype(F32), cum, bad, reverse=reverse)
    else:
        direct = jnp.zeros((c, HEAD_DIM), F32)
    return direct, scores.astype(BF16), qe, u, dec_end


def _gla_kernel(*refs, emit_y, emit_state):
    q_ref, kf_ref, kb_ref, lff_ref, lfb_ref, v_ref, gate_ref, s0_ref, nw_ref = refs[:9]
    pos = 9
    y_ref = s_out_ref = None
    if emit_y:
        y_ref = refs[pos]
        pos += 1
    if emit_state:
        s_out_ref = refs[pos]
        pos += 1
    (of_ref, ob_ref, cumf_ref, cumb_ref, wf_ref, wb_ref, qef_ref, qeb_ref, scf_ref, scb_ref,
     uf_ref, ub_ref, df_ref, db_ref) = refs[pos:pos + 14]

    c = GLA_CHUNK
    n = q_ref.shape[0]
    n_chunks = n // c
    unroll = min(GLA_UNROLL, n_chunks)
    dn_t = (((1,), (1,)), ((), ()))
    chunk_rows = lambda ci: pl.ds(pl.multiple_of(ci * c, c), c)
    mats_f, mats_b = _gla_matrices(False), _gla_matrices(True)

    def prefix_pass():
        def body(ci, _):
            rows = chunk_rows(ci)
            cumf_ref[rows, :], wf_ref[rows, :] = _gla_prefix_sums(lff_ref[rows, :], mats_f)
            cumb_ref[rows, :], wb_ref[rows, :] = _gla_prefix_sums(lfb_ref[rows, :], mats_b)
            return 0

        lax.fori_loop(0, n_chunks, body, 0, unroll=unroll)

    prefix_pass()

    def local_pass(robust):
        def body(ci, _):
            rows = chunk_rows(ci)
            q = q_ref[rows, :].astype(F32)
            v_bf = v_ref[rows, :]
            x_f, sc_f, qe_f, u_f, d_f = _gla_chunk(
                q, kf_ref[rows, :].astype(F32), lff_ref[rows, :], v_bf, cumf_ref.at[rows, :], wf_ref[rows, :],
                mats_f, reverse=False, robust=robust, want_o=emit_y)
            x_b, sc_b, qe_b, u_b, d_b = _gla_chunk(
                q, kb_ref[rows, :].astype(F32), lfb_ref[rows, :], v_bf, cumb_ref.at[rows, :], wb_ref[rows, :],
                mats_b, reverse=True, robust=robust, want_o=emit_y)
            uf_ref[ci] = u_f
            ub_ref[ci] = u_b
            df_ref[ci] = d_f
            db_ref[ci] = d_b
            if emit_y:
                of_ref[rows, :] = x_f
                ob_ref[rows, :] = x_b
                scf_ref[rows, :] = sc_f
                scb_ref[rows, :] = sc_b
                qef_ref[rows, :] = qe_f
                qeb_ref[rows, :] = qe_b
            return 0

        lax.fori_loop(0, n_chunks, body, 0, unroll=1 if robust else unroll)

    worst = jnp.minimum(jnp.min(wf_ref[...]), jnp.min(wb_ref[...]))
    lax.cond(worst < -GLA_SAFE_RANGE, lambda: local_pass(True), lambda: local_pass(False))

    def state_pass(ci, carry):
        st_f, st_b = carry
        cb = n_chunks - 1 - ci
        if emit_y:
            rows_f, rows_b = chunk_rows(ci), chunk_rows(cb)
            of_ref[rows_f, :] += (
                jnp.dot(scf_ref[rows_f, :], v_ref[rows_f, :], preferred_element_type=F32)
                + lax.dot_general(qef_ref[rows_f, :], st_f.astype(BF16), dn_t, preferred_element_type=F32))
            ob_ref[rows_b, :] += (
                jnp.dot(scb_ref[rows_b, :], v_ref[rows_b, :], preferred_element_type=F32)
                + lax.dot_general(qeb_ref[rows_b, :], st_b.astype(BF16), dn_t, preferred_element_type=F32))
        return st_f * df_ref[ci] + uf_ref[ci], st_b * db_ref[cb] + ub_ref[cb]

    st_f, st_b = lax.fori_loop(0, n_chunks, state_pass, (s0_ref[0], s0_ref[1]), unroll=unroll)

    if emit_state:
        s_out_ref[0] = st_f
        s_out_ref[1] = st_b
    if emit_y:
        def readout(ci, _):
            rows = pl.ds(pl.multiple_of(ci * c, c), c)
            tot = of_ref[rows, :] + ob_ref[rows, :]
            ms = jnp.mean(tot * tot, axis=1, keepdims=True)
            y = tot * lax.rsqrt(ms + RMS_EPS) * nw_ref[...] * gate_ref[rows, :].astype(F32)
            y_ref[rows, :] = y.astype(BF16)
            return 0
        lax.fori_loop(0, n_chunks, readout, 0, unroll=unroll)


def _gla(act, k, logf, s0, norm_w, *, n_heads, emit_y, emit_state):
    b, n, _ = act.shape
    hd = HEAD_DIM
    col = lambda off: pl.BlockSpec((None, n, hd), lambda bi, h: (bi, 0, off * n_heads + h))
    st_spec = pl.BlockSpec((None, None, 2, hd, hd), lambda bi, h: (bi, h, 0, 0, 0))
    out_specs, out_shape = [], []
    if emit_y:
        out_specs.append(pl.BlockSpec((None, n, hd), lambda bi, h: (bi, 0, h)))
        out_shape.append(jax.ShapeDtypeStruct((b, n, n_heads * hd), BF16))
    if emit_state:
        out_specs.append(st_spec)
        out_shape.append(jax.ShapeDtypeStruct((b, n_heads, 2, hd, hd), F32))
    outs = pl.pallas_call(
        functools.partial(_gla_kernel, emit_y=emit_y, emit_state=emit_state),
        grid=(b, n_heads),
        in_specs=[col(0), col(0), col(1), col(0), col(1), col(1), col(2), st_spec,
                  pl.BlockSpec((1, hd), lambda bi, h: (0, 0))],
        out_specs=out_specs,
        out_shape=out_shape,
        scratch_shapes=(
            [pltpu.VMEM((n, hd), F32)] * 6
            + [pltpu.VMEM((n, hd), BF16)] * 4
            + [pltpu.VMEM((n // GLA_CHUNK, hd, hd), F32)] * 2
            + [pltpu.VMEM((n // GLA_CHUNK, 1, hd), F32)] * 2),
        compiler_params=_params("arbitrary", "arbitrary"),
        name="gla_scan",
    )(act, k, k, logf, logf, act, act, s0, norm_w.reshape(1, hd))
    return outs


def _rope_head_perm():
    half = ROPE_AXIS_DIM // 2
    first = [0, 2 * half, 4 * half, 6 * half]
    order = first + [s + half for s in first]
    return np.concatenate([np.arange(s, s + half) for s in order])


def _rope_tables(n):
    half = ROPE_AXIS_DIM // 2
    inv = (1.0 / (ROPE_BASE ** (np.arange(0, ROPE_AXIS_DIM, 2, dtype=np.float32) / ROPE_AXIS_DIM))).astype(np.float32)
    pos = np.arange(n)
    ang_r = (pos // GRID_W).astype(np.float32)[:, None] * inv[None, :]
    ang_c = (pos % GRID_W).astype(np.float32)[:, None] * inv[None, :]
    ang = np.concatenate([ang_r, ang_c] * 4, axis=1)
    sign = np.concatenate([-np.ones(4 * half, np.float32), np.ones(4 * half, np.float32)])
    return jnp.asarray(np.cos(ang), F32), jnp.asarray(np.sin(ang) * sign[None, :], F32)


def kernel(x, c, ctx, c_ctx, mod_w, mod_b, ln_mix_g, ln_mix_b, ln_ffn_g, ln_ffn_b, even_w_in, even_w_out, diff_lambda, diff_subln, hgrn_w_in, hgrn_w_out, hgrn_lower_bounds, hgrn_norm, ffn_w_up, ffn_conv_w, ffn_conv_b, ffn_w_down):
    b, n_lat, d = x.shape
    n_ctx = ctx.shape[1]
    assert mod_w.shape[0] == DEPTH and n_lat % GRID_W == 0
    d_ff = ffn_w_down.shape[1]
    d_ff_pad = -(-d_ff // (2 * MXU_DIM)) * (2 * MXU_DIM)
    diff_width = even_w_out.shape[2] * 3 // 4
    four_width = even_w_in.shape[2] - 3 * diff_width
    n_diff_heads = diff_width // HEAD_DIM
    n_hgrn_heads = d // HEAD_DIM

    cc = jnp.concatenate([c, c_ctx[None, :], jnp.zeros((2 * SUBLANES - b - 1, d), F32)], axis=0)
    mod = _modulation(cc, mod_w, mod_b)
    mod = mod.reshape(DEPTH, cc.shape[0], 1, 6 * d)

    h_lat = x.reshape(b * n_lat, d)
    h_ctx = ctx.reshape(b * n_ctx, d)

    tm_wide_lat, tm_lat, tm_ctx = min(1024, n_lat), min(512, n_lat), min(256, n_ctx)
    lat_row = lambda tm: (lambda i: (i * tm) // n_lat)
    ctx_row = lambda tm: (lambda i: b)
    down_k_tiles = 4 if d_ff_pad % (4 * LANES) == 0 else d_ff_pad // 512

    def ffn(h, mod3, row_fn, layer, seq_len, tm_up, tm_down):
        pad = d_ff_pad - d_ff
        w_up = ffn_w_up[layer]
        wa = jnp.pad(w_up[:, :d_ff], ((0, 0), (0, pad))).astype(BF16)
        wv = jnp.pad(w_up[:, d_ff:], ((0, 0), (0, pad))).astype(BF16)
        cw = jnp.pad(ffn_conv_w[layer], ((0, 0), (0, pad)))
        cb = jnp.pad(ffn_conv_b[layer], (0, pad)).reshape(1, d_ff_pad)
        wd = jnp.pad(ffn_w_down[layer], ((0, pad), (0, 0))).astype(BF16)
        g = _ffn_up(h, mod3, row_fn(tm_up), wa, wv, cw, cb, seq_len=seq_len, tm=tm_up, tn=512)
        return _proj_residual([g], [(wd, 0)], h, mod3, 5, row_fn(tm_down), ln_ffn_g[layer], ln_ffn_b[layer],
                              tm=tm_down, n_k=down_k_tiles)

    layer = 0
    lam_init = 0.8 - 0.6 * math.exp(-0.3 * layer)
    perm = _rope_head_perm()
    qk_cols = (np.arange(2 * n_diff_heads)[:, None] * HEAD_DIM + perm[None, :]).reshape(-1)
    cols = np.concatenate([qk_cols, np.arange(2 * diff_width, even_w_in.shape[2])])
    w_in = even_w_in[0][:, cols].astype(BF16)
    w_out = even_w_out[0].astype(BF16)
    cos_t, sin_t = _rope_tables(n_lat)
    mod3 = mod[layer]

    col_scale = jnp.asarray(np.where(np.arange(w_in.shape[1]) < diff_width, QK_SCALE, 1.0)[None, :], F32)
    p_lat = _even_inproj(h_lat, mod3, lat_row(tm_wide_lat), w_in, col_scale, cos_t, sin_t, qk_width=diff_width,
                         use_rope=True, tm=tm_wide_lat, tn=1024).reshape(b, n_lat, -1)
    p_ctx = _even_inproj(h_ctx, mod3, ctx_row(tm_ctx), w_in, col_scale, cos_t[:SUBLANES], sin_t[:SUBLANES],
                         qk_width=diff_width, use_rope=False, tm=tm_ctx, tn=1024).reshape(b, n_ctx, -1)

    attn_lat = _diff_attention(p_lat, [p_ctx, p_lat], diff_lambda[0], diff_subln[0], lam_init,
                               n_heads=n_diff_heads, tq=min(512, n_lat))
    attn_ctx = _diff_attention(p_ctx, [p_ctx], diff_lambda[0], diff_subln[0], lam_init,
                               n_heads=n_diff_heads, tq=n_ctx)
    four_block = 3 * diff_width // four_width
    four_lat = _fourier_mix(p_lat, four_block, four_width, tm=tm_lat)
    four_ctx = _fourier_mix(p_ctx, four_block, four_width, tm=n_ctx)

    w_list = [(w_out, 0), (w_out, diff_width)]
    h_lat = _proj_residual([attn_lat.reshape(b * n_lat, -1), four_lat.reshape(b * n_lat, -1)], w_list,
                           h_lat, mod3, 2, lat_row(tm_lat), ln_mix_g[layer], ln_mix_b[layer], tm=tm_lat)
    h_ctx = _proj_residual([attn_ctx.reshape(b * n_ctx, -1), four_ctx.reshape(b * n_ctx, -1)], w_list,
                           h_ctx, mod3, 2, ctx_row(tm_ctx), ln_mix_g[layer], ln_mix_b[layer], tm=tm_ctx)
    h_lat = ffn(h_lat, mod3, lat_row, layer, n_lat, tm_wide_lat, tm_wide_lat)
    h_ctx = ffn(h_ctx, mod3, ctx_row, layer, n_ctx, tm_ctx, tm_ctx)

    layer = 1
    mod3 = mod[layer]
    hw = hgrn_w_in[0].astype(BF16)
    wrap = lambda t, n: t.reshape(b, n, -1)
    act_c, k_c, lf_c = _hgrn_inproj(h_ctx, mod3, ctx_row(tm_ctx), hw, hgrn_lower_bounds,
                                    layer=layer, tm=tm_ctx, tn=1024)
    act_l, k_l, lf_l = _hgrn_inproj(h_lat, mod3, lat_row(tm_wide_lat), hw, hgrn_lower_bounds,
                                    layer=layer, tm=tm_wide_lat, tn=1024)
    zero_state = jnp.zeros((b, n_hgrn_heads, 2, HEAD_DIM, HEAD_DIM), F32)
    (s_ctx,) = _gla(wrap(act_c, n_ctx), wrap(k_c, n_ctx), wrap(lf_c, n_ctx), zero_state, hgrn_norm[0],
                    n_heads=n_hgrn_heads, emit_y=False, emit_state=True)
    (y_lat,) = _gla(wrap(act_l, n_lat), wrap(k_l, n_lat), wrap(lf_l, n_lat), s_ctx, hgrn_norm[0],
                    n_heads=n_hgrn_heads, emit_y=True, emit_state=False)
    h_lat = _proj_residual([y_lat.reshape(b * n_lat, d)], [(hgrn_w_out[0].astype(BF16), 0)], h_lat, mod3, 2,
                           lat_row(tm_lat), ln_mix_g[layer], ln_mix_b[layer], tm=tm_lat)
    h_lat = ffn(h_lat, mod3, lat_row, layer, n_lat, tm_wide_lat, tm_wide_lat)
    return h_lat.reshape(b, n_lat, d)
```

```python
import functools
import math

import numpy as np
import jax
import jax.numpy as jnp
from jax import lax
from jax.experimental import pallas as pl
from jax.experimental.pallas import tpu as pltpu

F32 = jnp.float32
BF16 = jnp.bfloat16

LANES = 128
SUBLANES = 8
MXU_DIM = 256
VMEM_LIMIT_BYTES = 56 * 1024 * 1024

GRID_W = 64
DIFF_QK_DIM = 64
HEAD_DIM = 128
ROPE_AXIS_DIM = DIFF_QK_DIM // 2
ROPE_BASE = 10000.0
CONV_W = 3
LN_EPS = 1e-6
RMS_EPS = 1e-5
DEPTH = 2
ALPHA = (2.0 * DEPTH) ** 0.25
QK_SCALE = DIFF_QK_DIM ** -0.5 * math.log2(math.e)
ATTN_SAFE_LOG2 = 60.0

GLA_CHUNK = 128
GLA_LEVELS = (64, 32, 16)
GLA_BLOCK = 16
LOG2_E = math.log2(math.e)
GLA_SAFE_RANGE = 80.0
GLA_UNROLL = 4


def _params(*dims):
    return pltpu.CompilerParams(dimension_semantics=dims, vmem_limit_bytes=VMEM_LIMIT_BYTES)


def _layer_norm_rows(x):
    mu = jnp.mean(x, axis=-1, keepdims=True)
    xc = x - mu
    var = jnp.mean(xc * xc, axis=-1, keepdims=True)
    return xc * lax.rsqrt(var + LN_EPS)


def _silu(x):
    return x * jax.nn.sigmoid(x)


def _mod_kernel(c_ref, w_ref, b_ref, o_ref):
    x = _silu(c_ref[...]).astype(BF16)
    o_ref[...] = jnp.dot(x, w_ref[...].astype(BF16), preferred_element_type=F32) + b_ref[...]


def _modulation(cc, mod_w, mod_b, tn=1024):
    depth, d, n = mod_w.shape
    rows = cc.shape[0]
    return pl.pallas_call(
        _mod_kernel,
        grid=(depth, n // tn),
        in_specs=[
            pl.BlockSpec((rows, d), lambda l, j: (0, 0)),
            pl.BlockSpec((None, d, tn), lambda l, j: (l, 0, j)),
            pl.BlockSpec((None, 1, tn), lambda l, j: (l, 0, j)),
        ],
        out_specs=pl.BlockSpec((None, rows, tn), lambda l, j: (l, 0, j)),
        out_shape=jax.ShapeDtypeStruct((depth, rows, n), F32),
        compiler_params=_params("arbitrary", "arbitrary"),
        name="modulation",
    )(cc, mod_w, mod_b.reshape(depth, 1, n))


def _mod_spec(chunk, d, row_of_tile):
    return pl.BlockSpec((None, 1, d), lambda i, *_: (row_of_tile(i), 0, chunk))


def _even_inproj_kernel(h_ref, sh_ref, sc_ref, w_ref, cscale_ref, cos_ref, sin_ref, o_ref, u_ref,
                        *, n_rope_tiles, use_rope):
    j = pl.program_id(1)

    @pl.when(j == 0)
    def _():
        u_ref[...] = (_layer_norm_rows(h_ref[...]) * (1.0 + sc_ref[...]) + sh_ref[...]).astype(BF16)

    def tile(rope):
        for c0 in range(0, o_ref.shape[1], MXU_DIM):
            cols = slice(c0, c0 + MXU_DIM)
            y = jnp.dot(u_ref[...], w_ref[:, cols], preferred_element_type=F32)
            if rope:
                parts = []
                for hh in range(MXU_DIM // HEAD_DIM):
                    yh = y[:, hh * HEAD_DIM:(hh + 1) * HEAD_DIM]
                    parts.append(yh * cos_ref[...] + pltpu.roll(yh, HEAD_DIM // 2, axis=1) * sin_ref[...])
                y = jnp.concatenate(parts, axis=1)
            o_ref[:, cols] = (y * cscale_ref[:, cols]).astype(BF16)

    if use_rope:
        pl.when(j < n_rope_tiles)(lambda: tile(True))
        pl.when(j >= n_rope_tiles)(lambda: tile(False))
    else:
        tile(False)


def _even_inproj(h, mod3, row_of_tile, w, col_scale, cos_t, sin_t, *, qk_width, use_rope, tm, tn):
    m, d = h.shape
    n = w.shape[1]
    n_pos_tiles = cos_t.shape[0] // tm if use_rope else 1
    tab_rows = tm if use_rope else cos_t.shape[0]
    kern = functools.partial(_even_inproj_kernel, n_rope_tiles=2 * qk_width // tn, use_rope=use_rope)
    return pl.pallas_call(
        kern,
        grid=(m // tm, n // tn),
        in_specs=[
            pl.BlockSpec((tm, d), lambda i, j: (i, 0)),
            _mod_spec(0, d, row_of_tile),
            _mod_spec(1, d, row_of_tile),
            pl.BlockSpec((d, tn), lambda i, j: (0, j)),
            pl.BlockSpec((1, tn), lambda i, j: (0, j)),
            pl.BlockSpec((tab_rows, HEAD_DIM), lambda i, j: (i % n_pos_tiles, 0)),
            pl.BlockSpec((tab_rows, HEAD_DIM), lambda i, j: (i % n_pos_tiles, 0)),
        ],
        out_specs=pl.BlockSpec((tm, tn), lambda i, j: (i, j)),
        out_shape=jax.ShapeDtypeStruct((m, n), BF16),
        scratch_shapes=[pltpu.VMEM((tm, d), BF16)],
        compiler_params=_params("arbitrary", "arbitrary"),
        name="even_inproj",
    )(h, mod3, mod3, w, col_scale, cos_t, sin_t)


def _diff_attn_kernel(*refs, n_seg, lam_init, tq):
    q_ref = refs[0]
    k_refs = refs[1:1 + n_seg]
    v_refs = refs[1 + n_seg:1 + 2 * n_seg]
    lam_ref, subln_ref, o_ref = refs[1 + 2 * n_seg:]

    lv = lam_ref[...]
    lam = (jnp.exp(jnp.sum(lv[0:1] * lv[1:2], axis=1, keepdims=True))
           - jnp.exp(jnp.sum(lv[2:3] * lv[3:4], axis=1, keepdims=True)) + lam_init)

    def max_sq_norm(ref):
        x = ref[...].astype(F32)
        return jnp.max(jnp.sum(x * x, axis=1, keepdims=True), axis=0, keepdims=True)

    knorm = functools.reduce(jnp.maximum, [max_sq_norm(k_ref) for k_ref in k_refs])
    small_scores = jnp.max(max_sq_norm(q_ref) * knorm) <= ATTN_SAFE_LOG2 ** 2

    dn = (((1,), (1,)), ((), ()))
    lane = lax.broadcasted_iota(jnp.int32, (tq, HEAD_DIM), 1)
    first_map = (lane % DIFF_QK_DIM) < ROPE_AXIS_DIM

    def query_tile(t, shift_by_max):
        rows = pl.ds(pl.multiple_of(t * tq, tq), tq)
        q = q_ref[rows, :]
        zero = jnp.zeros_like(q)
        maps = []
        for qm in (jnp.where(first_map, q, zero), jnp.where(first_map, zero, q)):
            s = [lax.dot_general(qm, k_ref[...], dn, preferred_element_type=F32) for k_ref in k_refs]
            if shift_by_max:
                mx = functools.reduce(jnp.maximum, [jnp.max(x, axis=1, keepdims=True) for x in s])
                s = [x - mx for x in s]
            num = den = None
            for seg in range(n_seg):
                e = jnp.exp2(s[seg])
                d = jnp.sum(e, axis=1, keepdims=True)
                part = jnp.dot(e.astype(BF16), v_refs[seg][...], preferred_element_type=F32)
                den = d if den is None else den + d
                num = part if num is None else num + part
            maps.append((num, den))
        (n1, d1), (n2, d2) = maps
        acc = n1 * (1.0 / d1) - n2 * (lam / d2)
        ms = jnp.mean(acc * acc, axis=1, keepdims=True)
        y = acc * lax.rsqrt(ms + RMS_EPS) * subln_ref[...] * (1.0 - lam_init)
        o_ref[rows, :] = y.astype(BF16)

    def all_tiles(shift_by_max):
        def body(t, carry):
            query_tile(t, shift_by_max)
            return carry
        lax.fori_loop(0, q_ref.shape[0] // tq, body, 0)

    lax.cond(small_scores, lambda: all_tiles(False), lambda: all_tiles(True))


def _diff_attention(q_src, kv_srcs, lam_vecs, subln, lam_init, *, n_heads, tq):
    b, nq, _ = q_src.shape
    n_seg = len(kv_srcs)
    k_specs = [pl.BlockSpec((None, s.shape[1], HEAD_DIM), lambda bi, h: (bi, 0, n_heads + h))
               for s in kv_srcs]
    v_specs = [pl.BlockSpec((None, s.shape[1], HEAD_DIM), lambda bi, h: (bi, 0, 2 * n_heads + h))
               for s in kv_srcs]
    return pl.pallas_call(
        functools.partial(_diff_attn_kernel, n_seg=n_seg, lam_init=lam_init, tq=tq),
        grid=(b, n_heads),
        in_specs=[pl.BlockSpec((None, nq, HEAD_DIM), lambda bi, h: (bi, 0, h))] + k_specs + v_specs + [
            pl.BlockSpec(lam_vecs.shape, lambda bi, h: (0, 0)),
            pl.BlockSpec((1, HEAD_DIM), lambda bi, h: (0, 0)),
        ],
        out_specs=pl.BlockSpec((None, nq, HEAD_DIM), lambda bi, h: (bi, 0, h)),
        out_shape=jax.ShapeDtypeStruct((b, nq, n_heads * HEAD_DIM), BF16),
        compiler_params=_params("arbitrary", "arbitrary"),
        name="diff_attention",
    )(q_src, *kv_srcs, *kv_srcs, lam_vecs, subln.reshape(1, HEAD_DIM))


def _fourier_kernel(x_ref, dn_ref, cs_ref, o_ref, z_ref):
    n = x_ref.shape[0]

    @pl.when(pl.program_id(1) == 0)
    def _():
        for g in range(x_ref.shape[1] // HEAD_DIM):
            cols = slice(g * HEAD_DIM, (g + 1) * HEAD_DIM)
            zc = jnp.dot(x_ref[:, cols], cs_ref[...], preferred_element_type=F32)
            z_ref[0:n, cols] = zc[:, :HEAD_DIM].astype(BF16)
            z_ref[n:2 * n, cols] = zc[:, HEAD_DIM:].astype(BF16)

    o_ref[...] = jnp.dot(dn_ref[...], z_ref[...], preferred_element_type=F32).astype(BF16)


def _dft_tables(n):
    j = np.arange(n, dtype=np.int64)
    ang = 2.0 * np.pi * ((j[:, None] * j[None, :]) % n).astype(np.float64) / n
    return np.cos(ang) / math.sqrt(n), np.sin(ang) / math.sqrt(n)


def _fourier_mix(src, col_block, width, *, tm):
    b, n, _ = src.shape
    cn, sn = _dft_tables(n)
    cc, sc = _dft_tables(HEAD_DIM)
    dn = jnp.asarray(np.concatenate([cn, -sn], axis=1), dtype=BF16)
    cs = jnp.asarray(np.concatenate([cc, sc], axis=1), dtype=BF16)
    return pl.pallas_call(
        _fourier_kernel,
        grid=(b, n // tm),
        in_specs=[
            pl.BlockSpec((None, n, width), lambda bi, t: (bi, 0, col_block)),
            pl.BlockSpec((tm, 2 * n), lambda bi, t: (t, 0)),
            pl.BlockSpec((HEAD_DIM, 2 * HEAD_DIM), lambda bi, t: (0, 0)),
        ],
        out_specs=pl.BlockSpec((None, tm, width), lambda bi, t: (bi, t, 0)),
        out_shape=jax.ShapeDtypeStruct((b, n, width), BF16),
        scratch_shapes=[pltpu.VMEM((2 * n, width), BF16)],
        compiler_params=_params("arbitrary", "arbitrary"),
        name="fourier_mix",
    )(src, dn, cs)


def _proj_residual_kernel(*refs, n_in, n_k):
    a_refs = refs[:n_in]
    w_refs = refs[n_in:2 * n_in]
    h_ref, gate_ref, g_ref, b_ref, o_ref = refs[2 * n_in:2 * n_in + 5]
    k = pl.program_id(1)
    tm, d = o_ref.shape
    col_block, row_block = 2 * MXU_DIM, MXU_DIM

    def accumulate(first):
        for c0 in range(0, d, col_block):
            cols = slice(c0, c0 + col_block)
            part = None
            for a_ref, w_ref in zip(a_refs, w_refs):
                p = jnp.dot(a_ref[...], w_ref[:, cols], preferred_element_type=F32)
                part = p if part is None else part + p
            if first:
                o_ref[:, cols] = part
            else:
                o_ref[:, cols] += part

    def finish():
        for r0 in range(0, tm, row_block):
            rows = slice(r0, r0 + row_block)
            x = ALPHA * h_ref[rows, :] + gate_ref[...] * o_ref[rows, :]
            o_ref[rows, :] = _layer_norm_rows(x) * g_ref[...] + b_ref[...]

    if n_k == 1:
        accumulate(True)
        finish()
    else:
        pl.when(k == 0)(lambda: accumulate(True))
        pl.when(k > 0)(lambda: accumulate(False))
        pl.when(k == n_k - 1)(finish)


def _proj_residual(a_list, w_list, h, mod3, gate_chunk, row_of_tile, ln_g, ln_b, *, tm, n_k=1):
    m, d = h.shape
    in_specs = []
    for a in a_list:
        in_specs.append(pl.BlockSpec((tm, a.shape[1] // n_k), lambda i, k: (i, k)))
    for a, (w, first_row) in zip(a_list, w_list):
        rows = a.shape[1] // n_k
        in_specs.append(pl.BlockSpec((rows, d), lambda i, k, blk=first_row // rows: (blk + k, 0)))
    in_specs += [
        pl.BlockSpec((tm, d), lambda i, k: (i, 0)),
        _mod_spec(gate_chunk, d, row_of_tile),
        pl.BlockSpec((1, d), lambda i, k: (0, 0)),
        pl.BlockSpec((1, d), lambda i, k: (0, 0)),
    ]
    return pl.pallas_call(
        functools.partial(_proj_residual_kernel, n_in=len(a_list), n_k=n_k),
        grid=(m // tm, n_k),
        in_specs=in_specs,
        out_specs=pl.BlockSpec((tm, d), lambda i, k: (i, 0)),
        out_shape=jax.ShapeDtypeStruct((m, d), F32),
        compiler_params=_params("arbitrary", "arbitrary"),
        name="proj_residual",
    )(*a_list, *[w for w, _ in w_list], h, mod3, ln_g.reshape(1, d), ln_b.reshape(1, d))


def _ffn_up_kernel(h_ref, hp_ref, hn_ref, sh_ref, sc_ref, wa_ref, wv_ref, cw_ref, cb_ref, o_ref,
                   u_ref, uh_ref, *, tiles_per_seq):
    i = pl.program_id(0)
    j = pl.program_id(1)
    tm = h_ref.shape[0]

    @pl.when(j == 0)
    def _():
        scale = 1.0 + sc_ref[...]
        u_ref[...] = (_layer_norm_rows(h_ref[...]) * scale + sh_ref[...]).astype(BF16)
        uh_ref[0:SUBLANES, :] = (_layer_norm_rows(hp_ref[...]) * scale + sh_ref[...]).astype(BF16)
        uh_ref[SUBLANES:, :] = (_layer_norm_rows(hn_ref[...]) * scale + sh_ref[...]).astype(BF16)

    t_in_seq = i % tiles_per_seq
    row = lax.broadcasted_iota(jnp.int32, (tm, MXU_DIM), 0)
    for c0 in range(0, o_ref.shape[1], MXU_DIM):
        cols = slice(c0, c0 + MXU_DIM)
        a = jnp.dot(u_ref[...], wa_ref[:, cols], preferred_element_type=F32)
        v = jnp.dot(u_ref[...], wv_ref[:, cols], preferred_element_type=F32)
        halo = jnp.dot(uh_ref[...], wa_ref[:, cols], preferred_element_type=F32)
        prev_row = jnp.where(t_in_seq > 0, halo[SUBLANES - 1:SUBLANES], 0.0)
        next_row = jnp.where(t_in_seq < tiles_per_seq - 1, halo[SUBLANES:SUBLANES + 1], 0.0)
        a_prev = jnp.where(row == 0, prev_row, pltpu.roll(a, 1, axis=0))
        a_next = jnp.where(row == tm - 1, next_row, pltpu.roll(a, tm - 1, axis=0))
        cw = cw_ref[:, cols]
        conv = cb_ref[:, cols] + a_prev * cw[0:1] + a * cw[1:2] + a_next * cw[2:3]
        gelu = 0.5 * conv * (1.0 + lax.erf(conv * math.sqrt(0.5)))
        o_ref[:, cols] = (gelu * v).astype(BF16)


def _ffn_up(h, mod3, row_of_tile, wa, wv, conv_w, conv_b, *, seq_len, tm, tn):
    m, d = h.shape
    n = wa.shape[1]
    tiles_per_seq = seq_len // tm
    blocks_per_tile = tm // SUBLANES
    n_blocks = m // SUBLANES
    kern = functools.partial(_ffn_up_kernel, tiles_per_seq=tiles_per_seq)
    return pl.pallas_call(
        kern,
        grid=(m // tm, n // tn),
        in_specs=[
            pl.BlockSpec((tm, d), lambda i, j: (i, 0)),
            pl.BlockSpec((SUBLANES, d), lambda i, j: (jnp.maximum(i * blocks_per_tile - 1, 0), 0)),
            pl.BlockSpec((SUBLANES, d), lambda i, j: (jnp.minimum((i + 1) * blocks_per_tile, n_blocks - 1), 0)),
            _mod_spec(3, d, row_of_tile),
            _mod_spec(4, d, row_of_tile),
            pl.BlockSpec((d, tn), lambda i, j: (0, j)),
            pl.BlockSpec((d, tn), lambda i, j: (0, j)),
            pl.BlockSpec((CONV_W, tn), lambda i, j: (0, j)),
            pl.BlockSpec((1, tn), lambda i, j: (0, j)),
        ],
        out_specs=pl.BlockSpec((tm, tn), lambda i, j: (i, j)),
        out_shape=jax.ShapeDtypeStruct((m, n), BF16),
        scratch_shapes=[pltpu.VMEM((tm, d), BF16), pltpu.VMEM((2 * SUBLANES, d), BF16)],
        compiler_params=_params("arbitrary", "arbitrary"),
        name="ffn_up",
    )(h, h, h, mod3, mod3, wa, wv, conv_w, conv_b)


def _hgrn_act_kernel(h_ref, sh_ref, sc_ref, w_ref, o_ref, u_ref, *, lin_lo, lin_hi):
    j = pl.program_id(1)

    @pl.when(j == 0)
    def _():
        u_ref[...] = (_layer_norm_rows(h_ref[...]) * (1.0 + sc_ref[...]) + sh_ref[...]).astype(BF16)

    def tile(activation):
        for c0 in range(0, o_ref.shape[1], MXU_DIM):
            cols = slice(c0, c0 + MXU_DIM)
            acc = jnp.dot(u_ref[...], w_ref[:, cols], preferred_element_type=F32)
            o_ref[:, cols] = activation(acc).astype(BF16)

    linear = jnp.logical_and(j >= lin_lo, j < lin_hi)
    pl.when(linear)(lambda: tile(lambda x: x))
    pl.when(jnp.logical_not(linear))(lambda: tile(_silu))


def _hgrn_gate_kernel(h_ref, sh_ref, sc_ref, w_ref, lbp_ref, k_ref, lf_ref, u_ref, *, layer):
    j = pl.program_id(1)

    @pl.when(j == 0)
    def _():
        u_ref[...] = (_layer_norm_rows(h_ref[...]) * (1.0 + sc_ref[...]) + sh_ref[...]).astype(BF16)

    for c0 in range(0, k_ref.shape[1], MXU_DIM):
        cols = slice(c0, c0 + MXU_DIM)
        x = lbp_ref[:, cols]
        e = jnp.exp(x - jnp.max(x, axis=0, keepdims=True))
        lb = jnp.sum(e[1:layer + 1], axis=0, keepdims=True) / jnp.sum(e, axis=0, keepdims=True)

        f_pre = jnp.dot(u_ref[...], w_ref[:, cols], preferred_element_type=F32)
        gate = jax.nn.sigmoid(f_pre)
        k_ref[:, cols] = ((1.0 - lb) * (1.0 - gate)).astype(BF16)
        lf_ref[:, cols] = jnp.log(lb + (1.0 - lb) * gate)


def _hgrn_inproj(h, mod3, row_of_tile, w, lb_params, *, layer, tm, tn):
    m, d = h.shape
    tiles = d // tn
    common = [
        pl.BlockSpec((tm, d), lambda i, j: (i, 0)),
        _mod_spec(0, d, row_of_tile),
        _mod_spec(1, d, row_of_tile),
    ]
    act_w = pl.BlockSpec((d, tn), lambda i, j: (0, jnp.where(j < tiles, j, j + 2 * tiles)))
    gate_w = pl.BlockSpec((d, tn), lambda i, j: (0, j + tiles))
    act = pl.pallas_call(
        functools.partial(_hgrn_act_kernel, lin_lo=tiles, lin_hi=2 * tiles),
        grid=(m // tm, 3 * tiles),
        in_specs=common + [act_w],
        out_specs=pl.BlockSpec((tm, tn), lambda i, j: (i, j)),
        out_shape=jax.ShapeDtypeStruct((m, 3 * d), BF16),
        scratch_shapes=[pltpu.VMEM((tm, d), BF16)],
        compiler_params=_params("arbitrary", "arbitrary"),
        name="hgrn_inproj_act",
    )(h, mod3, mod3, w)
    k, logf = pl.pallas_call(
        functools.partial(_hgrn_gate_kernel, layer=layer),
        grid=(m // tm, 2 * tiles),
        in_specs=common + [gate_w, pl.BlockSpec((None, DEPTH, tn), lambda i, j: (j // tiles, 0, j % tiles))],
        out_specs=[pl.BlockSpec((tm, tn), lambda i, j: (i, j))] * 2,
        out_shape=[jax.ShapeDtypeStruct((m, 2 * d), BF16), jax.ShapeDtypeStruct((m, 2 * d), F32)],
        scratch_shapes=[pltpu.VMEM((tm, d), BF16)],
        compiler_params=_params("arbitrary", "arbitrary"),
        name="hgrn_inproj_gate",
    )(h, mod3, mod3, w, lb_params)
    return act, k, logf


def _gla_matrices(reverse):
    c = GLA_CHUNK
    r_i = lax.broadcasted_iota(jnp.int32, (c, c), 0)
    c_i = lax.broadcasted_iota(jnp.int32, (c, c), 1)
    same = jnp.bitwise_xor(r_i, c_i)
    earlier = (c_i >= r_i) if reverse else (c_i <= r_i)
    in_block = same < GLA_BLOCK
    one = lambda m: jnp.where(m, 1.0, 0.0).astype(BF16)
    return dict(tri=one(earlier), tri_blk=one(jnp.logical_and(earlier, in_block)), ones_blk=one(in_block),
                diag_mask=jnp.logical_and(earlier, in_block), same=same)


def _gla_direct_block_terms(q, k, v, cum, bad, *, reverse):
    c = GLA_CHUNK
    row = lax.broadcasted_iota(jnp.int32, (c, HEAD_DIM), 0) % GLA_BLOCK
    qb = jnp.where(bad, q, 0.0)
    acc0 = jnp.sum(qb * k, axis=1, keepdims=True) * v

    def offset(dlt, acc):
        shift = (c - dlt) if reverse else dlt
        valid = (row <= GLA_BLOCK - 1 - dlt) if reverse else (row >= dlt)
        k_d = pltpu.roll(k, shift, axis=0)
        v_d = pltpu.roll(v, shift, axis=0)
        cum_d = pltpu.roll(cum, shift, axis=0)
        decay = jnp.exp2(jnp.where(valid, cum - cum_d, -jnp.inf))
        return acc + jnp.sum(qb * k_d * decay, axis=1, keepdims=True) * v_d

    return lax.fori_loop(1, GLA_BLOCK, offset, acc0)


def _gla_prefix_sums(lf, mats):
    lf = lf * LOG2_E
    lf_hi = lf.astype(BF16)
    lf_lo = (lf - lf_hi.astype(F32)).astype(BF16)
    two_term = lambda m: (jnp.dot(m, lf_hi, preferred_element_type=F32)
                          + jnp.dot(m, lf_lo, preferred_element_type=F32))
    return two_term(mats["tri"]), two_term(mats["tri_blk"])


def _gla_chunk(q, k, lf, v_bf, cum_ref, w_blk, mats, *, reverse, robust, want_o):
    c = GLA_CHUNK
    dn_t = (((1,), (1,)), ((), ()))
    cum = cum_ref[...]
    end_row = 0 if reverse else c - 1
    cum_end = cum_ref[end_row:end_row + 1, :]

    decayed = lambda x, log2_decay: (x * jnp.exp2(log2_decay)).astype(BF16)
    qe = decayed(q, cum)
    k_dec = decayed(k, cum_end - cum)
    u = lax.dot_general(v_bf, k_dec, (((0,), (0,)), ((), ())), preferred_element_type=F32)
    dec_end = jnp.exp2(cum_end)
    if not want_o:
        return None, None, qe, u, dec_end

    if robust:
        blk_tot = jnp.dot(mats["ones_blk"], (lf * LOG2_E).astype(BF16), preferred_element_type=F32)
        bad = blk_tot < -GLA_SAFE_RANGE
        a_d = jnp.where(bad, 0.0, q * jnp.exp2(w_blk)).astype(BF16)
        b_d = decayed(k, jnp.minimum(-w_blk, GLA_SAFE_RANGE + 20.0))
    else:
        a_d = decayed(q, w_blk)
        b_d = decayed(k, -w_blk)
    scores = jnp.where(mats["diag_mask"], lax.dot_general(a_d, b_d, dn_t, preferred_element_type=F32), 0.0)

    zeros = {m: jnp.zeros((m, HEAD_DIM), BF16) for m in GLA_LEVELS}
    for m in GLA_LEVELS:
        a_parts, b_parts = [], []
        for blk in range(c // (2 * m)):
            lo, mid, hi = blk * 2 * m, blk * 2 * m + m, (blk + 1) * 2 * m
            if reverse:
                ref_row = cum_ref[mid:mid + 1, :]
                qa = decayed(q[lo:mid], cum[lo:mid] - ref_row)
                kb = decayed(k[mid:hi], ref_row - cum[mid:hi])
                a_parts += [qa, zeros[m]]
                b_parts += [zeros[m], kb]
            else:
                ref_row = cum_ref[mid - 1:mid, :]
                qa = decayed(q[mid:hi], cum[mid:hi] - ref_row)
                kb = decayed(k[lo:mid], ref_row - cum[lo:mid])
                a_parts += [zeros[m], qa]
                b_parts += [kb, zeros[m]]
        a_m = jnp.concatenate(a_parts, axis=0)
        b_m = jnp.concatenate(b_parts, axis=0)
        s_m = lax.dot_general(a_m, b_m, dn_t, preferred_element_type=F32)
        if 2 * m < c:
            s_m = jnp.where(mats["same"] < 2 * m, s_m, 0.0)
        scores = scores + s_m
    if robust:
        direct = _gla_direct_block_terms(q, k, v_bf.astype(F32), cum, bad, reverse=reverse)
    else:
        direct = jnp.zeros((c, HEAD_DIM), F32)
    return direct, scores.astype(BF16), qe, u, dec_end


def _gla_kernel(*refs, emit_y, emit_state):
    q_ref, kf_ref, kb_ref, lff_ref, lfb_ref, v_ref, gate_ref, s0_ref, nw_ref = refs[:9]
    pos = 9
    y_ref = s_out_ref = None
    if emit_y:
        y_ref = refs[pos]
        pos += 1
    if emit_state:
        s_out_ref = refs[pos]
        pos += 1
    (of_ref, ob_ref, cumf_ref, cumb_ref, wf_ref, wb_ref, qef_ref, qeb_ref, scf_ref, scb_ref,
     uf_ref, ub_ref, df_ref, db_ref) = refs[pos:pos + 14]

    c = GLA_CHUNK
    n = q_ref.shape[0]
    n_chunks = n // c
    unroll = min(GLA_UNROLL, n_chunks)
    dn_t = (((1,), (1,)), ((), ()))
    chunk_rows = lambda ci: pl.ds(pl.multiple_of(ci * c, c), c)
    mats_f, mats_b = _gla_matrices(False), _gla_matrices(True)

    def prefix_pass():
        def body(ci, _):
            rows = chunk_rows(ci)
            cumf_ref[rows, :], wf_ref[rows, :] = _gla_prefix_sums(lff_ref[rows, :], mats_f)
            cumb_ref[rows, :], wb_ref[rows, :] = _gla_prefix_sums(lfb_ref[rows, :], mats_b)
            return 0

        lax.fori_loop(0, n_chunks, body, 0, unroll=unroll)

    prefix_pass()

    def local_pass(robust):
        def body(ci, _):
            rows = chunk_rows(ci)
            q = q_ref[rows, :].astype(F32)
            v_bf = v_ref[rows, :]
            x_f, sc_f, qe_f, u_f, d_f = _gla_chunk(
                q, kf_ref[rows, :].astype(F32), lff_ref[rows, :], v_bf, cumf_ref.at[rows, :], wf_ref[rows, :],
                mats_f, reverse=False, robust=robust, want_o=emit_y)
            x_b, sc_b, qe_b, u_b, d_b = _gla_chunk(
                q, kb_ref[rows, :].astype(F32), lfb_ref[rows, :], v_bf, cumb_ref.at[rows, :], wb_ref[rows, :],
                mats_b, reverse=True, robust=robust, want_o=emit_y)
            uf_ref[ci] = u_f
            ub_ref[ci] = u_b
            df_ref[ci] = d_f
            db_ref[ci] = d_b
            if emit_y:
                of_ref[rows, :] = x_f
                ob_ref[rows, :] = x_b
                scf_ref[rows, :] = sc_f
                scb_ref[rows, :] = sc_b
                qef_ref[rows, :] = qe_f
                qeb_ref[rows, :] = qe_b
            return 0

        lax.fori_loop(0, n_chunks, body, 0, unroll=1 if robust else unroll)

    worst = jnp.minimum(jnp.min(wf_ref[...]), jnp.min(wb_ref[...]))
    lax.cond(worst < -GLA_SAFE_RANGE, lambda: local_pass(True), lambda: local_pass(False))

    def state_pass(ci, carry):
        st_f, st_b = carry
        cb = n_chunks - 1 - ci
        if emit_y:
            rows_f, rows_b = chunk_rows(ci), chunk_rows(cb)
            of_ref[rows_f, :] += (
                jnp.dot(scf_ref[rows_f, :], v_ref[rows_f, :], preferred_element_type=F32)
                + lax.dot_general(qef_ref[rows_f, :], st_f.astype(BF16), dn_t, preferred_element_type=F32))
            ob_ref[rows_b, :] += (
                jnp.dot(scb_ref[rows_b, :], v_ref[rows_b, :], preferred_element_type=F32)
                + lax.dot_general(qeb_ref[rows_b, :], st_b.astype(BF16), dn_t, preferred_element_type=F32))
        return st_f * df_ref[ci] + uf_ref[ci], st_b * db_ref[cb] + ub_ref[cb]

    st_f, st_b = lax.fori_loop(0, n_chunks, state_pass, (s0_ref[0], s0_ref[1]), unroll=unroll)

    if emit_state:
        s_out_ref[0] = st_f
        s_out_ref[1] = st_b
    if emit_y:
        def readout(ci, _):
            rows = pl.ds(pl.multiple_of(ci * c, c), c)
            tot = of_ref[rows, :] + ob_ref[rows, :]
            ms = jnp.mean(tot * tot, axis=1, keepdims=True)
            y = tot * lax.rsqrt(ms + RMS_EPS) * nw_ref[...] * gate_ref[rows, :].astype(F32)
            y_ref[rows, :] = y.astype(BF16)
            return 0
        lax.fori_loop(0, n_chunks, readout, 0, unroll=unroll)


def _gla(act, k, logf, s0, norm_w, *, n_heads, emit_y, emit_state):
    b, n, _ = act.shape
    hd = HEAD_DIM
    col = lambda off: pl.BlockSpec((None, n, hd), lambda bi, h: (bi, 0, off * n_heads + h))
    st_spec = pl.BlockSpec((None, None, 2, hd, hd), lambda bi, h: (bi, h, 0, 0, 0))
    out_specs, out_shape = [], []
    if emit_y:
        out_specs.append(pl.BlockSpec((None, n, hd), lambda bi, h: (bi, 0, h)))
        out_shape.append(jax.ShapeDtypeStruct((b, n, n_heads * hd), BF16))
    if emit_state:
        out_specs.append(st_spec)
        out_shape.append(jax.ShapeDtypeStruct((b, n_heads, 2, hd, hd), F32))
    outs = pl.pallas_call(
        functools.partial(_gla_kernel, emit_y=emit_y, emit_state=emit_state),
        grid=(b, n_heads),
        in_specs=[col(0), col(0), col(1), col(0), col(1), col(1), col(2), st_spec,
                  pl.BlockSpec((1, hd), lambda bi, h: (0, 0))],
        out_specs=out_specs,
        out_shape=out_shape,
        scratch_shapes=(
            [pltpu.VMEM((n, hd), F32)] * 6
            + [pltpu.VMEM((n, hd), BF16)] * 4
            + [pltpu.VMEM((n // GLA_CHUNK, hd, hd), F32)] * 2
            + [pltpu.VMEM((n // GLA_CHUNK, 1, hd), F32)] * 2),
        compiler_params=_params("arbitrary", "arbitrary"),
        name="gla_scan",
    )(act, k, k, logf, logf, act, act, s0, norm_w.reshape(1, hd))
    return outs


def _rope_head_perm():
    half = ROPE_AXIS_DIM // 2
    first = [0, 2 * half, 4 * half, 6 * half]
    order = first + [s + half for s in first]
    return np.concatenate([np.arange(s, s + half) for s in order])


def _rope_tables(n):
    half = ROPE_AXIS_DIM // 2
    inv = (1.0 / (ROPE_BASE ** (np.arange(0, ROPE_AXIS_DIM, 2, dtype=np.float32) / ROPE_AXIS_DIM))).astype(np.float32)
    pos = np.arange(n)
    ang_r = (pos // GRID_W).astype(np.float32)[:, None] * inv[None, :]
    ang_c = (pos % GRID_W).astype(np.float32)[:, None] * inv[None, :]
    ang = np.concatenate([ang_r, ang_c] * 4, axis=1)
    sign = np.concatenate([-np.ones(4 * half, np.float32), np.ones(4 * half, np.float32)])
    return jnp.asarray(np.cos(ang), F32), jnp.asarray(np.sin(ang) * sign[None, :], F32)


def kernel(x, c, ctx, c_ctx, mod_w, mod_b, ln_mix_g, ln_mix_b, ln_ffn_g, ln_ffn_b, even_w_in, even_w_out, diff_lambda, diff_subln, hgrn_w_in, hgrn_w_out, hgrn_lower_bounds, hgrn_norm, ffn_w_up, ffn_conv_w, ffn_conv_b, ffn_w_down):
    b, n_lat, d = x.shape
    n_ctx = ctx.shape[1]
    assert mod_w.shape[0] == DEPTH and n_lat % GRID_W == 0
    d_ff = ffn_w_down.shape[1]
    d_ff_pad = -(-d_ff // (2 * MXU_DIM)) * (2 * MXU_DIM)
    diff_width = even_w_out.shape[2] * 3 // 4
    four_width = even_w_in.shape[2] - 3 * diff_width
    n_diff_heads = diff_width // HEAD_DIM
    n_hgrn_heads = d // HEAD_DIM

    cc = jnp.concatenate([c, c_ctx[None, :], jnp.zeros((2 * SUBLANES - b - 1, d), F32)], axis=0)
    mod = _modulation(cc, mod_w, mod_b)
    mod = mod.reshape(DEPTH, cc.shape[0], 1, 6 * d)

    h_lat = x.reshape(b * n_lat, d)
    h_ctx = ctx.reshape(b * n_ctx, d)

    tm_wide_lat, tm_lat, tm_ctx = min(1024, n_lat), min(512, n_lat), min(256, n_ctx)
    lat_row = lambda tm: (lambda i: (i * tm) // n_lat)
    ctx_row = lambda tm: (lambda i: b)
    down_k_tiles = 4 if d_ff_pad % (4 * LANES) == 0 else d_ff_pad // 512

    def ffn(h, mod3, row_fn, layer, seq_len, tm_up, tm_down):
        pad = d_ff_pad - d_ff
        w_up = ffn_w_up[layer]
        wa = jnp.pad(w_up[:, :d_ff], ((0, 0), (0, pad))).astype(BF16)
        wv = jnp.pad(w_up[:, d_ff:], ((0, 0), (0, pad))).astype(BF16)
        cw = jnp.pad(ffn_conv_w[layer], ((0, 0), (0, pad)))
        cb = jnp.pad(ffn_conv_b[layer], (0, pad)).reshape(1, d_ff_pad)
        wd = jnp.pad(ffn_w_down[layer], ((0, pad), (0, 0))).astype(BF16)
        g = _ffn_up(h, mod3, row_fn(tm_up), wa, wv, cw, cb, seq_len=seq_len, tm=tm_up, tn=512)
        return _proj_residual([g], [(wd, 0)], h, mod3, 5, row_fn(tm_down), ln_ffn_g[layer], ln_ffn_b[layer],
                              tm=tm_down, n_k=down_k_tiles)

    layer = 0
    lam_init = 0.8 - 0.6 * math.exp(-0.3 * layer)
    perm = _rope_head_perm()
    qk_cols = (np.arange(2 * n_diff_heads)[:, None] * HEAD_DIM + perm[None, :]).reshape(-1)
    cols = np.concatenate([qk_cols, np.arange(2 * diff_width, even_w_in.shape[2])])
    w_in = even_w_in[0][:, cols].astype(BF16)
    w_out = even_w_out[0].astype(BF16)
    cos_t, sin_t = _rope_tables(n_lat)
    mod3 = mod[layer]

    col_scale = jnp.asarray(np.where(np.arange(w_in.shape[1]) < diff_width, QK_SCALE, 1.0)[None, :], F32)
    p_lat = _even_inproj(h_lat, mod3, lat_row(tm_wide_lat), w_in, col_scale, cos_t, sin_t, qk_width=diff_width,
                         use_rope=True, tm=tm_wide_lat, tn=1024).reshape(b, n_lat, -1)
    p_ctx = _even_inproj(h_ctx, mod3, ctx_row(tm_ctx), w_in, col_scale, cos_t[:SUBLANES], sin_t[:SUBLANES],
                         qk_width=diff_width, use_rope=False, tm=tm_ctx, tn=1024).reshape(b, n_ctx, -1)

    attn_lat = _diff_attention(p_lat, [p_ctx, p_lat], diff_lambda[0], diff_subln[0], lam_init,
                               n_heads=n_diff_heads, tq=min(512, n_lat))
    attn_ctx = _diff_attention(p_ctx, [p_ctx], diff_lambda[0], diff_subln[0], lam_init,
                               n_heads=n_diff_heads, tq=n_ctx)
    four_block = 3 * diff_width // four_width
    four_lat = _fourier_mix(p_lat, four_block, four_width, tm=tm_lat)
    four_ctx = _fourier_mix(p_ctx, four_block, four_width, tm=n_ctx)

    w_list = [(w_out, 0), (w_out, diff_width)]
    h_lat = _proj_residual([attn_lat.reshape(b * n_lat, -1), four_lat.reshape(b * n_lat, -1)], w_list,
                           h_lat, mod3, 2, lat_row(tm_lat), ln_mix_g[layer], ln_mix_b[layer], tm=tm_lat)
    h_ctx = _proj_residual([attn_ctx.reshape(b * n_ctx, -1), four_ctx.reshape(b * n_ctx, -1)], w_list,
                           h_ctx, mod3, 2, ctx_row(tm_ctx), ln_mix_g[layer], ln_mix_b[layer], tm=tm_ctx)
    h_lat = ffn(h_lat, mod3, lat_row, layer, n_lat, tm_wide_lat, tm_wide_lat)
    h_ctx = ffn(h_ctx, mod3, ctx_row, layer, n_ctx, tm_ctx, tm_ctx)

    layer = 1
    mod3 = mod[layer]
    hw = hgrn_w_in[0].astype(BF16)
    wrap = lambda t, n: t.reshape(b, n, -1)
    act_c, k_c, lf_c = _hgrn_inproj(h_ctx, mod3, ctx_row(tm_ctx), hw, hgrn_lower_bounds,
                                    layer=layer, tm=tm_ctx, tn=1024)
    act_l, k_l, lf_l = _hgrn_inproj(h_lat, mod3, lat_row(tm_wide_lat), hw, hgrn_lower_bounds,
                                    layer=layer, tm=tm_wide_lat, tn=1024)
    zero_state = jnp.zeros((b, n_hgrn_heads, 2, HEAD_DIM, HEAD_DIM), F32)
    (s_ctx,) = _gla(wrap(act_c, n_ctx), wrap(k_c, n_ctx), wrap(lf_c, n_ctx), zero_state, hgrn_norm[0],
                    n_heads=n_hgrn_heads, emit_y=False, emit_state=True)
    (y_lat,) = _gla(wrap(act_l, n_lat), wrap(k_l, n_lat), wrap(lf_l, n_lat), s_ctx, hgrn_norm[0],
                    n_heads=n_hgrn_heads, emit_y=True, emit_state=False)
    h_lat = _proj_residual([y_lat.reshape(b * n_lat, d)], [(hgrn_w_out[0].astype(BF16), 0)], h_lat, mod3, 2,
                           lat_row(tm_lat), ln_mix_g[layer], ln_mix_b[layer], tm=tm_lat)
    h_lat = ffn(h_lat, mod3, lat_row, layer, n_lat, tm_wide_lat, tm_wide_lat)
    return h_lat.reshape(b, n_lat, d)
```

```python
import functools
import math

import numpy as np
import jax
import jax.numpy as jnp
from jax import lax
from jax.experimental import pallas as pl
from jax.experimental.pallas import tpu as pltpu

F32 = jnp.float32
BF16 = jnp.bfloat16

LANES = 128
SUBLANES = 8
MXU_DIM = 256
VMEM_LIMIT_BYTES = 56 * 1024 * 1024

GRID_W = 64
DIFF_QK_DIM = 64
HEAD_DIM = 128
ROPE_AXIS_DIM = DIFF_QK_DIM // 2
ROPE_BASE = 10000.0
CONV_W = 3
LN_EPS = 1e-6
RMS_EPS = 1e-5
DEPTH = 2
ALPHA = (2.0 * DEPTH) ** 0.25
QK_SCALE = DIFF_QK_DIM ** -0.5 * math.log2(math.e)
ATTN_SAFE_LOG2 = 60.0

GLA_CHUNK = 128
GLA_LEVELS = (64, 32, 16)
GLA_BLOCK = 16
LOG2_E = math.log2(math.e)
GLA_SAFE_RANGE = 80.0
GLA_UNROLL = 4


def _params(*dims):
    return pltpu.CompilerParams(dimension_semantics=dims, vmem_limit_bytes=VMEM_LIMIT_BYTES)


def _layer_norm_rows(x):
    mu = jnp.mean(x, axis=-1, keepdims=True)
    xc = x - mu
    var = jnp.mean(xc * xc, axis=-1, keepdims=True)
    return xc * lax.rsqrt(var + LN_EPS)


def _silu(x):
    return x * jax.nn.sigmoid(x)


def _mod_kernel(c_ref, w_ref, b_ref, o_ref):
    x = _silu(c_ref[...]).astype(BF16)
    o_ref[...] = jnp.dot(x, w_ref[...].astype(BF16), preferred_element_type=F32) + b_ref[...]


def _modulation(cc, mod_w, mod_b, tn=1024):
    depth, d, n = mod_w.shape
    rows = cc.shape[0]
    return pl.pallas_call(
        _mod_kernel,
        grid=(depth, n // tn),
        in_specs=[
            pl.BlockSpec((rows, d), lambda l, j: (0, 0)),
            pl.BlockSpec((None, d, tn), lambda l, j: (l, 0, j)),
            pl.BlockSpec((None, 1, tn), lambda l, j: (l, 0, j)),
        ],
        out_specs=pl.BlockSpec((None, rows, tn), lambda l, j: (l, 0, j)),
        out_shape=jax.ShapeDtypeStruct((depth, rows, n), F32),
        compiler_params=_params("arbitrary", "arbitrary"),
        name="modulation",
    )(cc, mod_w, mod_b.reshape(depth, 1, n))


def _mod_spec(chunk, d, row_of_tile):
    return pl.BlockSpec((None, 1, d), lambda i, *_: (row_of_tile(i), 0, chunk))


def _even_inproj_kernel(h_ref, sh_ref, sc_ref, w_ref, cscale_ref, cos_ref, sin_ref, o_ref, u_ref,
                        *, n_rope_tiles, use_rope):
    j = pl.program_id(1)

    @pl.when(j == 0)
    def _():
        u_ref[...] = (_layer_norm_rows(h_ref[...]) * (1.0 + sc_ref[...]) + sh_ref[...]).astype(BF16)

    def tile(rope):
        for c0 in range(0, o_ref.shape[1], MXU_DIM):
            cols = slice(c0, c0 + MXU_DIM)
            y = jnp.dot(u_ref[...], w_ref[:, cols], preferred_element_type=F32)
            if rope:
                parts = []
                for hh in range(MXU_DIM // HEAD_DIM):
                    yh = y[:, hh * HEAD_DIM:(hh + 1) * HEAD_DIM]
                    parts.append(yh * cos_ref[...] + pltpu.roll(yh, HEAD_DIM // 2, axis=1) * sin_ref[...])
                y = jnp.concatenate(parts, axis=1)
            o_ref[:, cols] = (y * cscale_ref[:, cols]).astype(BF16)

    if use_rope:
        pl.when(j < n_rope_tiles)(lambda: tile(True))
        pl.when(j >= n_rope_tiles)(lambda: tile(False))
    else:
        tile(False)


def _even_inproj(h, mod3, row_of_tile, w, col_scale, cos_t, sin_t, *, qk_width, use_rope, tm, tn):
    m, d = h.shape
    n = w.shape[1]
    n_pos_tiles = cos_t.shape[0] // tm if use_rope else 1
    tab_rows = tm if use_rope else cos_t.shape[0]
    kern = functools.partial(_even_inproj_kernel, n_rope_tiles=2 * qk_width // tn, use_rope=use_rope)
    return pl.pallas_call(
        kern,
        grid=(m // tm, n // tn),
        in_specs=[
            pl.BlockSpec((tm, d), lambda i, j: (i, 0)),
            _mod_spec(0, d, row_of_tile),
            _mod_spec(1, d, row_of_tile),
            pl.BlockSpec((d, tn), lambda i, j: (0, j)),
            pl.BlockSpec((1, tn), lambda i, j: (0, j)),
            pl.BlockSpec((tab_rows, HEAD_DIM), lambda i, j: (i % n_pos_tiles, 0)),
            pl.BlockSpec((tab_rows, HEAD_DIM), lambda i, j: (i % n_pos_tiles, 0)),
        ],
        out_specs=pl.BlockSpec((tm, tn), lambda i, j: (i, j)),
        out_shape=jax.ShapeDtypeStruct((m, n), BF16),
        scratch_shapes=[pltpu.VMEM((tm, d), BF16)],
        compiler_params=_params("arbitrary", "arbitrary"),
        name="even_inproj",
    )(h, mod3, mod3, w, col_scale, cos_t, sin_t)


def _diff_attn_kernel(*refs, n_seg, lam_init, tq):
    q_ref = refs[0]
    k_refs = refs[1:1 + n_seg]
    v_refs = refs[1 + n_seg:1 + 2 * n_seg]
    lam_ref, subln_ref, o_ref = refs[1 + 2 * n_seg:]

    lv = lam_ref[...]
    lam = (jnp.exp(jnp.sum(lv[0:1] * lv[1:2], axis=1, keepdims=True))
           - jnp.exp(jnp.sum(lv[2:3] * lv[3:4], axis=1, keepdims=True)) + lam_init)

    ones = jnp.ones((HEAD_DIM, HEAD_DIM), BF16)

    def max_sq_norm(ref):
        x = ref[...]
        row_sums = jnp.dot(x * x, ones, preferred_element_type=F32)
        return jnp.max(row_sums)

    knorm = functools.reduce(jnp.maximum, [max_sq_norm(k_ref) for k_ref in k_refs])
    small_scores = max_sq_norm(q_ref) * knorm <= 0.98 * ATTN_SAFE_LOG2 ** 2

    dn = (((1,), (1,)), ((), ()))
    lane = lax.broadcasted_iota(jnp.int32, (tq, HEAD_DIM), 1)
    first_map = (lane % DIFF_QK_DIM) < ROPE_AXIS_DIM

    def query_tile(t, shift_by_max):
        rows = pl.ds(pl.multiple_of(t * tq, tq), tq)
        q = q_ref[rows, :]
        zero = jnp.zeros_like(q)
        maps = []
        for qm in (jnp.where(first_map, q, zero), jnp.where(first_map, zero, q)):
            s = [lax.dot_general(qm, k_ref[...], dn, preferred_element_type=F32) for k_ref in k_refs]
            if shift_by_max:
                mx = functools.reduce(jnp.maximum, [jnp.max(x, axis=1, keepdims=True) for x in s])
                s = [x - mx for x in s]
            num = den = None
            for seg in range(n_seg):
                e = jnp.exp2(s[seg])
                d = jnp.sum(e, axis=1, keepdims=True)
                part = jnp.dot(e.astype(BF16), v_refs[seg][...], preferred_element_type=F32)
                den = d if den is None else den + d
                num = part if num is None else num + part
            maps.append((num, den))
        (n1, d1), (n2, d2) = maps
        acc = n1 * (1.0 / d1) - n2 * (lam / d2)
        ms = jnp.mean(acc * acc, axis=1, keepdims=True)
        y = acc * lax.rsqrt(ms + RMS_EPS) * subln_ref[...] * (1.0 - lam_init)
        o_ref[rows, :] = y.astype(BF16)

    def all_tiles(shift_by_max):
        def body(t, carry):
            query_tile(t, shift_by_max)
            return carry
        lax.fori_loop(0, q_ref.shape[0] // tq, body, 0)

    lax.cond(small_scores, lambda: all_tiles(False), lambda: all_tiles(True))


def _diff_attention(q_src, kv_srcs, lam_vecs, subln, lam_init, *, n_heads, tq):
    b, nq, _ = q_src.shape
    n_seg = len(kv_srcs)
    k_specs = [pl.BlockSpec((None, s.shape[1], HEAD_DIM), lambda bi, h: (bi, 0, n_heads + h))
               for s in kv_srcs]
    v_specs = [pl.BlockSpec((None, s.shape[1], HEAD_DIM), lambda bi, h: (bi, 0, 2 * n_heads + h))
               for s in kv_srcs]
    return pl.pallas_call(
        functools.partial(_diff_attn_kernel, n_seg=n_seg, lam_init=lam_init, tq=tq),
        grid=(b, n_heads),
        in_specs=[pl.BlockSpec((None, nq, HEAD_DIM), lambda bi, h: (bi, 0, h))] + k_specs + v_specs + [
            pl.BlockSpec(lam_vecs.shape, lambda bi, h: (0, 0)),
            pl.BlockSpec((1, HEAD_DIM), lambda bi, h: (0, 0)),
        ],
        out_specs=pl.BlockSpec((None, nq, HEAD_DIM), lambda bi, h: (bi, 0, h)),
        out_shape=jax.ShapeDtypeStruct((b, nq, n_heads * HEAD_DIM), BF16),
        compiler_params=_params("arbitrary", "arbitrary"),
        name="diff_attention",
    )(q_src, *kv_srcs, *kv_srcs, lam_vecs, subln.reshape(1, HEAD_DIM))


def _fourier_kernel(x_ref, dn_ref, cs_ref, o_ref, z_ref):
    n = x_ref.shape[0]

    @pl.when(pl.program_id(1) == 0)
    def _():
        for g in range(x_ref.shape[1] // HEAD_DIM):
            cols = slice(g * HEAD_DIM, (g + 1) * HEAD_DIM)
            zc = jnp.dot(x_ref[:, cols], cs_ref[...], preferred_element_type=F32)
            z_ref[0:n, cols] = zc[:, :HEAD_DIM].astype(BF16)
            z_ref[n:2 * n, cols] = zc[:, HEAD_DIM:].astype(BF16)

    o_ref[...] = jnp.dot(dn_ref[...], z_ref[...], preferred_element_type=F32).astype(BF16)


def _dft_tables(n):
    j = np.arange(n, dtype=np.int64)
    ang = 2.0 * np.pi * ((j[:, None] * j[None, :]) % n).astype(np.float64) / n
    return np.cos(ang) / math.sqrt(n), np.sin(ang) / math.sqrt(n)


def _fourier_mix(src, col_block, width, *, tm):
    b, n, _ = src.shape
    cn, sn = _dft_tables(n)
    cc, sc = _dft_tables(HEAD_DIM)
    dn = jnp.asarray(np.concatenate([cn, -sn], axis=1), dtype=BF16)
    cs = jnp.asarray(np.concatenate([cc, sc], axis=1), dtype=BF16)
    return pl.pallas_call(
        _fourier_kernel,
        grid=(b, n // tm),
        in_specs=[
            pl.BlockSpec((None, n, width), lambda bi, t: (bi, 0, col_block)),
            pl.BlockSpec((tm, 2 * n), lambda bi, t: (t, 0)),
            pl.BlockSpec((HEAD_DIM, 2 * HEAD_DIM), lambda bi, t: (0, 0)),
        ],
        out_specs=pl.BlockSpec((None, tm, width), lambda bi, t: (bi, t, 0)),
        out_shape=jax.ShapeDtypeStruct((b, n, width), BF16),
        scratch_shapes=[pltpu.VMEM((2 * n, width), BF16)],
        compiler_params=_params("arbitrary", "arbitrary"),
        name="fourier_mix",
    )(src, dn, cs)


def _proj_residual_kernel(*refs, n_in, n_k):
    a_refs = refs[:n_in]
    w_refs = refs[n_in:2 * n_in]
    h_ref, gate_ref, g_ref, b_ref, o_ref = refs[2 * n_in:2 * n_in + 5]
    k = pl.program_id(1)
    tm, d = o_ref.shape
    col_block, row_block = 2 * MXU_DIM, MXU_DIM

    def accumulate(first):
        for c0 in range(0, d, col_block):
            cols = slice(c0, c0 + col_block)
            part = None
            for a_ref, w_ref in zip(a_refs, w_refs):
                p = jnp.dot(a_ref[...], w_ref[:, cols], preferred_element_type=F32)
                part = p if part is None else part + p
            if first:
                o_ref[:, cols] = part
            else:
                o_ref[:, cols] += part

    def finish():
        for r0 in range(0, tm, row_block):
            rows = slice(r0, r0 + row_block)
            x = ALPHA * h_ref[rows, :] + gate_ref[...] * o_ref[rows, :]
            o_ref[rows, :] = _layer_norm_rows(x) * g_ref[...] + b_ref[...]

    if n_k == 1:
        accumulate(True)
        finish()
    else:
        pl.when(k == 0)(lambda: accumulate(True))
        pl.when(k > 0)(lambda: accumulate(False))
        pl.when(k == n_k - 1)(finish)


def _proj_residual(a_list, w_list, h, mod3, gate_chunk, row_of_tile, ln_g, ln_b, *, tm, n_k=1):
    m, d = h.shape
    in_specs = []
    for a in a_list:
        in_specs.append(pl.BlockSpec((tm, a.shape[1] // n_k), lambda i, k: (i, k)))
    for a, (w, first_row) in zip(a_list, w_list):
        rows = a.shape[1] // n_k
        in_specs.append(pl.BlockSpec((rows, d), lambda i, k, blk=first_row // rows: (blk + k, 0)))
    in_specs += [
        pl.BlockSpec((tm, d), lambda i, k: (i, 0)),
        _mod_spec(gate_chunk, d, row_of_tile),
        pl.BlockSpec((1, d), lambda i, k: (0, 0)),
        pl.BlockSpec((1, d), lambda i, k: (0, 0)),
    ]
    return pl.pallas_call(
        functools.partial(_proj_residual_kernel, n_in=len(a_list), n_k=n_k),
        grid=(m // tm, n_k),
        in_specs=in_specs,
        out_specs=pl.BlockSpec((tm, d), lambda i, k: (i, 0)),
        out_shape=jax.ShapeDtypeStruct((m, d), F32),
        compiler_params=_params("arbitrary", "arbitrary"),
        name="proj_residual",
    )(*a_list, *[w for w, _ in w_list], h, mod3, ln_g.reshape(1, d), ln_b.reshape(1, d))


def _ffn_up_kernel(h_ref, hp_ref, hn_ref, sh_ref, sc_ref, wa_ref, wv_ref, cw_ref, cb_ref, o_ref,
                   u_ref, uh_ref, *, seq_len):
    i = pl.program_id(0)
    j = pl.program_id(1)
    tm = h_ref.shape[0]
    whole_sequences = tm >= seq_len
    tiles_per_seq = max(seq_len // tm, 1)

    @pl.when(j == 0)
    def _():
        scale = 1.0 + sc_ref[...]
        u_ref[...] = (_layer_norm_rows(h_ref[...]) * scale + sh_ref[...]).astype(BF16)
        if not whole_sequences:
            uh_ref[0:SUBLANES, :] = (_layer_norm_rows(hp_ref[...]) * scale + sh_ref[...]).astype(BF16)
            uh_ref[SUBLANES:, :] = (_layer_norm_rows(hn_ref[...]) * scale + sh_ref[...]).astype(BF16)

    t_in_seq = i % tiles_per_seq
    row = lax.broadcasted_iota(jnp.int32, (tm, MXU_DIM), 0)
    row_in_seq = row % seq_len
    for c0 in range(0, o_ref.shape[1], MXU_DIM):
        cols = slice(c0, c0 + MXU_DIM)
        a = jnp.dot(u_ref[...], wa_ref[:, cols], preferred_element_type=F32)
        v = jnp.dot(u_ref[...], wv_ref[:, cols], preferred_element_type=F32)
        if whole_sequences:
            a_prev = jnp.where(row_in_seq == 0, 0.0, pltpu.roll(a, 1, axis=0))
            a_next = jnp.where(row_in_seq == seq_len - 1, 0.0, pltpu.roll(a, tm - 1, axis=0))
        else:
            halo = jnp.dot(uh_ref[...], wa_ref[:, cols], preferred_element_type=F32)
            prev_row = jnp.where(t_in_seq > 0, halo[SUBLANES - 1:SUBLANES], 0.0)
            next_row = jnp.where(t_in_seq < tiles_per_seq - 1, halo[SUBLANES:SUBLANES + 1], 0.0)
            a_prev = jnp.where(row == 0, prev_row, pltpu.roll(a, 1, axis=0))
            a_next = jnp.where(row == tm - 1, next_row, pltpu.roll(a, tm - 1, axis=0))
        cw = cw_ref[:, cols]
        conv = cb_ref[:, cols] + a_prev * cw[0:1] + a * cw[1:2] + a_next * cw[2:3]
        gelu = 0.5 * conv * (1.0 + lax.erf(conv * math.sqrt(0.5)))
        o_ref[:, cols] = (gelu * v).astype(BF16)


def _ffn_up(h, mod3, row_of_tile, wa, wv, conv_w, conv_b, *, seq_len, tm, tn):
    m, d = h.shape
    n = wa.shape[1]
    assert tm % seq_len == 0 or seq_len % tm == 0
    blocks_per_tile = tm // SUBLANES
    n_blocks = m // SUBLANES
    kern = functools.partial(_ffn_up_kernel, seq_len=seq_len)
    return pl.pallas_call(
        kern,
        grid=(m // tm, n // tn),
        in_specs=[
            pl.BlockSpec((tm, d), lambda i, j: (i, 0)),
            pl.BlockSpec((SUBLANES, d), lambda i, j: (jnp.maximum(i * blocks_per_tile - 1, 0), 0)),
            pl.BlockSpec((SUBLANES, d), lambda i, j: (jnp.minimum((i + 1) * blocks_per_tile, n_blocks - 1), 0)),
            _mod_spec(3, d, row_of_tile),
            _mod_spec(4, d, row_of_tile),
            pl.BlockSpec((d, tn), lambda i, j: (0, j)),
            pl.BlockSpec((d, tn), lambda i, j: (0, j)),
            pl.BlockSpec((CONV_W, tn), lambda i, j: (0, j)),
            pl.BlockSpec((1, tn), lambda i, j: (0, j)),
        ],
        out_specs=pl.BlockSpec((tm, tn), lambda i, j: (i, j)),
        out_shape=jax.ShapeDtypeStruct((m, n), BF16),
        scratch_shapes=[pltpu.VMEM((tm, d), BF16), pltpu.VMEM((2 * SUBLANES, d), BF16)],
        compiler_params=_params("arbitrary", "arbitrary"),
        name="ffn_up",
    )(h, h, h, mod3, mod3, wa, wv, conv_w, conv_b)


def _hgrn_act_kernel(h_ref, sh_ref, sc_ref, w_ref, o_ref, u_ref, *, lin_lo, lin_hi):
    j = pl.program_id(1)

    @pl.when(j == 0)
    def _():
        u_ref[...] = (_layer_norm_rows(h_ref[...]) * (1.0 + sc_ref[...]) + sh_ref[...]).astype(BF16)

    def tile(activation):
        for c0 in range(0, o_ref.shape[1], MXU_DIM):
            cols = slice(c0, c0 + MXU_DIM)
            acc = jnp.dot(u_ref[...], w_ref[:, cols], preferred_element_type=F32)
            o_ref[:, cols] = activation(acc).astype(BF16)

    linear = jnp.logical_and(j >= lin_lo, j < lin_hi)
    pl.when(linear)(lambda: tile(lambda x: x))
    pl.when(jnp.logical_not(linear))(lambda: tile(_silu))


def _hgrn_gate_kernel(h_ref, sh_ref, sc_ref, w_ref, lbp_ref, k_ref, lf_ref, u_ref, *, layer):
    j = pl.program_id(1)

    @pl.when(j == 0)
    def _():
        u_ref[...] = (_layer_norm_rows(h_ref[...]) * (1.0 + sc_ref[...]) + sh_ref[...]).astype(BF16)

    for c0 in range(0, k_ref.shape[1], MXU_DIM):
        cols = slice(c0, c0 + MXU_DIM)
        x = lbp_ref[:, cols]
        e = jnp.exp(x - jnp.max(x, axis=0, keepdims=True))
        lb = jnp.sum(e[1:layer + 1], axis=0, keepdims=True) / jnp.sum(e, axis=0, keepdims=True)

        f_pre = jnp.dot(u_ref[...], w_ref[:, cols], preferred_element_type=F32)
        gate = jax.nn.sigmoid(f_pre)
        k_ref[:, cols] = ((1.0 - lb) * (1.0 - gate)).astype(BF16)
        lf_ref[:, cols] = jnp.log(lb + (1.0 - lb) * gate)


def _hgrn_inproj(h, mod3, row_of_tile, w, lb_params, *, layer, tm, tn):
    m, d = h.shape
    tiles = d // tn
    common = [
        pl.BlockSpec((tm, d), lambda i, j: (i, 0)),
        _mod_spec(0, d, row_of_tile),
        _mod_spec(1, d, row_of_tile),
    ]
    act_w = pl.BlockSpec((d, tn), lambda i, j: (0, jnp.where(j < tiles, j, j + 2 * tiles)))
    gate_w = pl.BlockSpec((d, tn), lambda i, j: (0, j + tiles))
    act = pl.pallas_call(
        functools.partial(_hgrn_act_kernel, lin_lo=tiles, lin_hi=2 * tiles),
        grid=(m // tm, 3 * tiles),
        in_specs=common + [act_w],
        out_specs=pl.BlockSpec((tm, tn), lambda i, j: (i, j)),
        out_shape=jax.ShapeDtypeStruct((m, 3 * d), BF16),
        scratch_shapes=[pltpu.VMEM((tm, d), BF16)],
        compiler_params=_params("arbitrary", "arbitrary"),
        name="hgrn_inproj_act",
    )(h, mod3, mod3, w)
    k, logf = pl.pallas_call(
        functools.partial(_hgrn_gate_kernel, layer=layer),
        grid=(m // tm, 2 * tiles),
        in_specs=common + [gate_w, pl.BlockSpec((None, DEPTH, tn), lambda i, j: (j // tiles, 0, j % tiles))],
        out_specs=[pl.BlockSpec((tm, tn), lambda i, j: (i, j))] * 2,
        out_shape=[jax.ShapeDtypeStruct((m, 2 * d), BF16), jax.ShapeDtypeStruct((m, 2 * d), F32)],
        scratch_shapes=[pltpu.VMEM((tm, d), BF16)],
        compiler_params=_params("arbitrary", "arbitrary"),
        name="hgrn_inproj_gate",
    )(h, mod3, mod3, w, lb_params)
    return act, k, logf


def _gla_matrices(reverse):
    c = GLA_CHUNK
    r_i = lax.broadcasted_iota(jnp.int32, (c, c), 0)
    c_i = lax.broadcasted_iota(jnp.int32, (c, c), 1)
    same = jnp.bitwise_xor(r_i, c_i)
    earlier = (c_i >= r_i) if reverse else (c_i <= r_i)
    in_block = same < GLA_BLOCK
    one = lambda m: jnp.where(m, 1.0, 0.0).astype(BF16)
    return dict(tri=one(earlier), tri_blk=one(jnp.logical_and(earlier, in_block)), ones_blk=one(in_block),
                diag_mask=jnp.logical_and(earlier, in_block), same=same)


def _gla_direct_block_terms(q, k, v, cum, bad, *, reverse):
    c = GLA_CHUNK
    row = lax.broadcasted_iota(jnp.int32, (c, HEAD_DIM), 0) % GLA_BLOCK
    qb = jnp.where(bad, q, 0.0)
    acc0 = jnp.sum(qb * k, axis=1, keepdims=True) * v

    def offset(dlt, acc):
        shift = (c - dlt) if reverse else dlt
        valid = (row <= GLA_BLOCK - 1 - dlt) if reverse else (row >= dlt)
        k_d = pltpu.roll(k, shift, axis=0)
        v_d = pltpu.roll(v, shift, axis=0)
        cum_d = pltpu.roll(cum, shift, axis=0)
        decay = jnp.exp2(jnp.where(valid, cum - cum_d, -jnp.inf))
        return acc + jnp.sum(qb * k_d * decay, axis=1, keepdims=True) * v_d

    return lax.fori_loop(1, GLA_BLOCK, offset, acc0)


def _gla_prefix_sums(lf, mats):
    lf = lf * LOG2_E
    lf_hi = lf.astype(BF16)
    lf_lo = (lf - lf_hi.astype(F32)).astype(BF16)
    two_term = lambda m: (jnp.dot(m, lf_hi, preferred_element_type=F32)
                          + jnp.dot(m, lf_lo, preferred_element_type=F32))
    return two_term(mats["tri"]), two_term(mats["tri_blk"])


def _gla_chunk(q, k, lf, v_bf, cum_ref, w_blk, mats, *, reverse, robust, want_o):
    c = GLA_CHUNK
    dn_t = (((1,), (1,)), ((), ()))
    cum = cum_ref[...]
    end_row = 0 if reverse else c - 1
    cum_end = cum_ref[end_row:end_row + 1, :]

    decayed = lambda x, log2_decay: (x * jnp.exp2(log2_decay)).astype(BF16)
    qe = decayed(q, cum)
    k_dec = decayed(k, cum_end - cum)
    u = lax.dot_general(v_bf, k_dec, (((0,), (0,)), ((), ())), preferred_element_type=F32)
    dec_end = jnp.exp2(cum_end)
    if not want_o:
        return None, None, qe, u, dec_end

    if robust:
        blk_tot = jnp.dot(mats["ones_blk"], (lf * LOG2_E).astype(BF16), preferred_element_type=F32)
        bad = blk_tot < -GLA_SAFE_RANGE
        a_d = jnp.where(bad, 0.0, q * jnp.exp2(w_blk)).astype(BF16)
        b_d = decayed(k, jnp.minimum(-w_blk, GLA_SAFE_RANGE + 20.0))
    else:
        a_d = decayed(q, w_blk)
        b_d = decayed(k, -w_blk)
    scores = jnp.where(mats["diag_mask"], lax.dot_general(a_d, b_d, dn_t, preferred_element_type=F32), 0.0)

    zeros = {m: jnp.zeros((m, HEAD_DIM), BF16) for m in GLA_LEVELS}
    for m in GLA_LEVELS:
        a_parts, b_parts = [], []
        for blk in range(c // (2 * m)):
            lo, mid, hi = blk * 2 * m, blk * 2 * m + m, (blk + 1) * 2 * m
            if reverse:
                ref_row = cum_ref[mid:mid + 1, :]
                qa = decayed(q[lo:mid], cum[lo:mid] - ref_row)
                kb = decayed(k[mid:hi], ref_row - cum[mid:hi])
                a_parts += [qa, zeros[m]]
                b_parts += [zeros[m], kb]
            else:
                ref_row = cum_ref[mid - 1:mid, :]
                qa = decayed(q[mid:hi], cum[mid:hi] - ref_row)
                kb = decayed(k[lo:mid], ref_row - cum[lo:mid])
                a_parts += [zeros[m], qa]
                b_parts += [kb, zeros[m]]
        a_m = jnp.concatenate(a_parts, axis=0)
        b_m = jnp.concatenate(b_parts, axis=0)
        s_m = lax.dot_general(a_m, b_m, dn_t, preferred_element_type=F32)
        if 2 * m < c:
            s_m = jnp.where(mats["same"] < 2 * m, s_m, 0.0)
        scores = scores + s_m
    if robust:
        direct = _gla_direct_block_terms(q, k, v_bf.astype(F32), cum, bad, reverse=reverse)
    else:
        direct = jnp.zeros((c, HEAD_DIM), F32)
    return direct, scores.astype(BF16), qe, u, dec_end


def _gla_kernel(*refs, emit_y, emit_state):
    q_ref, kf_ref, kb_ref, lff_ref, lfb_ref, v_ref, gate_ref, s0_ref, nw_ref = refs[:9]
    pos = 9
    y_ref = s_out_ref = None
    if emit_y:
        y_ref = refs[pos]
        pos += 1
    if emit_state:
        s_out_ref = refs[pos]
        pos += 1
    (of_ref, ob_ref, cumf_ref, cumb_ref, wf_ref, wb_ref, qef_ref, qeb_ref, scf_ref, scb_ref,
     uf_ref, ub_ref, df_ref, db_ref) = refs[pos:pos + 14]

    c = GLA_CHUNK
    n = q_ref.shape[0]
    n_chunks = n // c
    unroll = min(GLA_UNROLL, n_chunks)
    dn_t = (((1,), (1,)), ((), ()))
    chunk_rows = lambda ci: pl.ds(pl.multiple_of(ci * c, c), c)
    mats_f, mats_b = _gla_matrices(False), _gla_matrices(True)

    def prefix_pass():
        def body(ci, _):
            rows = chunk_rows(ci)
            cumf_ref[rows, :], wf_ref[rows, :] = _gla_prefix_sums(lff_ref[rows, :], mats_f)
            cumb_ref[rows, :], wb_ref[rows, :] = _gla_prefix_sums(lfb_ref[rows, :], mats_b)
            return 0

        lax.fori_loop(0, n_chunks, body, 0, unroll=unroll)

    prefix_pass()

    def local_pass(robust):
        def body(ci, _):
            rows = chunk_rows(ci)
            q = q_ref[rows, :].astype(F32)
            v_bf = v_ref[rows, :]
            x_f, sc_f, qe_f, u_f, d_f = _gla_chunk(
                q, kf_ref[rows, :].astype(F32), lff_ref[rows, :], v_bf, cumf_ref.at[rows, :], wf_ref[rows, :],
                mats_f, reverse=False, robust=robust, want_o=emit_y)
            x_b, sc_b, qe_b, u_b, d_b = _gla_chunk(
                q, kb_ref[rows, :].astype(F32), lfb_ref[rows, :], v_bf, cumb_ref.at[rows, :], wb_ref[rows, :],
                mats_b, reverse=True, robust=robust, want_o=emit_y)
            uf_ref[ci] = u_f
            ub_ref[ci] = u_b
            df_ref[ci] = d_f
            db_ref[ci] = d_b
            if emit_y:
                of_ref[rows, :] = x_f
                ob_ref[rows, :] = x_b
                scf_ref[rows, :] = sc_f
                scb_ref[rows, :] = sc_b
                qef_ref[rows, :] = qe_f
                qeb_ref[rows, :] = qe_b
            return 0

        lax.fori_loop(0, n_chunks, body, 0, unroll=1 if robust else unroll)

    worst = jnp.minimum(jnp.min(wf_ref[...]), jnp.min(wb_ref[...]))
    lax.cond(worst < -GLA_SAFE_RANGE, lambda: local_pass(True), lambda: local_pass(False))

    def state_pass(ci, carry):
        st_f, st_b = carry
        cb = n_chunks - 1 - ci
        if emit_y:
            rows_f, rows_b = chunk_rows(ci), chunk_rows(cb)
            of_ref[rows_f, :] += (
                jnp.dot(scf_ref[rows_f, :], v_ref[rows_f, :], preferred_element_type=F32)
                + lax.dot_general(qef_ref[rows_f, :], st_f.astype(BF16), dn_t, preferred_element_type=F32))
            ob_ref[rows_b, :] += (
                jnp.dot(scb_ref[rows_b, :], v_ref[rows_b, :], preferred_element_type=F32)
                + lax.dot_general(qeb_ref[rows_b, :], st_b.astype(BF16), dn_t, preferred_element_type=F32))
        return st_f * df_ref[ci] + uf_ref[ci], st_b * db_ref[cb] + ub_ref[cb]

    st_f, st_b = lax.fori_loop(0, n_chunks, state_pass, (s0_ref[0], s0_ref[1]), unroll=unroll)

    if emit_state:
        s_out_ref[0] = st_f
        s_out_ref[1] = st_b
    if emit_y:
        def readout(ci, _):
            rows = pl.ds(pl.multiple_of(ci * c, c), c)
            tot = of_ref[rows, :] + ob_ref[rows, :]
            ms = jnp.mean(tot * tot, axis=1, keepdims=True)
            y = tot * lax.rsqrt(ms + RMS_EPS) * nw_ref[...] * gate_ref[rows, :].astype(F32)
            y_ref[rows, :] = y.astype(BF16)
            return 0
        lax.fori_loop(0, n_chunks, readout, 0, unroll=unroll)


def _gla(act, k, logf, s0, norm_w, *, n_heads, emit_y, emit_state):
    b, n, _ = act.shape
    hd = HEAD_DIM
    col = lambda off: pl.BlockSpec((None, n, hd), lambda bi, h: (bi, 0, off * n_heads + h))
    st_spec = pl.BlockSpec((None, None, 2, hd, hd), lambda bi, h: (bi, h, 0, 0, 0))
    out_specs, out_shape = [], []
    if emit_y:
        out_specs.append(pl.BlockSpec((None, n, hd), lambda bi, h: (bi, 0, h)))
        out_shape.append(jax.ShapeDtypeStruct((b, n, n_heads * hd), BF16))
    if emit_state:
        out_specs.append(st_spec)
        out_shape.append(jax.ShapeDtypeStruct((b, n_heads, 2, hd, hd), F32))
    outs = pl.pallas_call(
        functools.partial(_gla_kernel, emit_y=emit_y, emit_state=emit_state),
        grid=(b, n_heads),
        in_specs=[col(0), col(0), col(1), col(0), col(1), col(1), col(2), st_spec,
                  pl.BlockSpec((1, hd), lambda bi, h: (0, 0))],
        out_specs=out_specs,
        out_shape=out_shape,
        scratch_shapes=(
            [pltpu.VMEM((n, hd), F32)] * 6
            + [pltpu.VMEM((n, hd), BF16)] * 4
            + [pltpu.VMEM((n // GLA_CHUNK, hd, hd), F32)] * 2
            + [pltpu.VMEM((n // GLA_CHUNK, 1, hd), F32)] * 2),
        compiler_params=_params("arbitrary", "arbitrary"),
        name="gla_scan",
    )(act, k, k, logf, logf, act, act, s0, norm_w.reshape(1, hd))
    return outs


def _rope_head_perm():
    half = ROPE_AXIS_DIM // 2
    first = [0, 2 * half, 4 * half, 6 * half]
    order = first + [s + half for s in first]
    return np.concatenate([np.arange(s, s + half) for s in order])


def _rope_tables(n):
    half = ROPE_AXIS_DIM // 2
    inv = (1.0 / (ROPE_BASE ** (np.arange(0, ROPE_AXIS_DIM, 2, dtype=np.float32) / ROPE_AXIS_DIM))).astype(np.float32)
    pos = np.arange(n)
    ang_r = (pos // GRID_W).astype(np.float32)[:, None] * inv[None, :]
    ang_c = (pos % GRID_W).astype(np.float32)[:, None] * inv[None, :]
    ang = np.concatenate([ang_r, ang_c] * 4, axis=1)
    sign = np.concatenate([-np.ones(4 * half, np.float32), np.ones(4 * half, np.float32)])
    return jnp.asarray(np.cos(ang), F32), jnp.asarray(np.sin(ang) * sign[None, :], F32)


def kernel(x, c, ctx, c_ctx, mod_w, mod_b, ln_mix_g, ln_mix_b, ln_ffn_g, ln_ffn_b, even_w_in, even_w_out, diff_lambda, diff_subln, hgrn_w_in, hgrn_w_out, hgrn_lower_bounds, hgrn_norm, ffn_w_up, ffn_conv_w, ffn_conv_b, ffn_w_down):
    b, n_lat, d = x.shape
    n_ctx = ctx.shape[1]
    assert mod_w.shape[0] == DEPTH and n_lat % GRID_W == 0
    d_ff = ffn_w_down.shape[1]
    d_ff_pad = -(-d_ff // (2 * MXU_DIM)) * (2 * MXU_DIM)
    diff_width = even_w_out.shape[2] * 3 // 4
    four_width = even_w_in.shape[2] - 3 * diff_width
    n_diff_heads = diff_width // HEAD_DIM
    n_hgrn_heads = d // HEAD_DIM

    cc = jnp.concatenate([c, c_ctx[None, :], jnp.zeros((2 * SUBLANES - b - 1, d), F32)], axis=0)
    mod = _modulation(cc, mod_w, mod_b)
    mod = mod.reshape(DEPTH, cc.shape[0], 1, 6 * d)

    h_lat = x.reshape(b * n_lat, d)
    h_ctx = ctx.reshape(b * n_ctx, d)

    tm_wide_lat, tm_lat = min(1024, n_lat), min(512, n_lat)
    ctx_tile = lambda cap: n_ctx * max(1, min(cap // n_ctx, b)) if cap >= n_ctx else cap
    tm_wide_ctx, tm_ctx = ctx_tile(1024), ctx_tile(512)
    assert (b * n_ctx) % tm_wide_ctx == 0 and (b * n_ctx) % tm_ctx == 0
    lat_row = lambda tm: (lambda i: (i * tm) // n_lat)
    ctx_row = lambda tm: (lambda i: b)
    down_k_tiles = 4 if d_ff_pad % (4 * LANES) == 0 else d_ff_pad // 512

    def ffn(h, mod3, row_fn, layer, seq_len, tm_up, tm_down):
        pad = d_ff_pad - d_ff
        w_up = ffn_w_up[layer]
        wa = jnp.pad(w_up[:, :d_ff], ((0, 0), (0, pad))).astype(BF16)
        wv = jnp.pad(w_up[:, d_ff:], ((0, 0), (0, pad))).astype(BF16)
        cw = jnp.pad(ffn_conv_w[layer], ((0, 0), (0, pad)))
        cb = jnp.pad(ffn_conv_b[layer], (0, pad)).reshape(1, d_ff_pad)
        wd = jnp.pad(ffn_w_down[layer], ((0, pad), (0, 0))).astype(BF16)
        g = _ffn_up(h, mod3, row_fn(tm_up), wa, wv, cw, cb, seq_len=seq_len, tm=tm_up, tn=512)
        return _proj_residual([g], [(wd, 0)], h, mod3, 5, row_fn(tm_down), ln_ffn_g[layer], ln_ffn_b[layer],
                              tm=tm_down, n_k=down_k_tiles)

    layer = 0
    lam_init = 0.8 - 0.6 * math.exp(-0.3 * layer)
    perm = _rope_head_perm()
    qk_cols = (np.arange(2 * n_diff_heads)[:, None] * HEAD_DIM + perm[None, :]).reshape(-1)
    cols = np.concatenate([qk_cols, np.arange(2 * diff_width, even_w_in.shape[2])])
    w_in = even_w_in[0][:, cols].astype(BF16)
    w_out = even_w_out[0].astype(BF16)
    cos_t, sin_t = _rope_tables(n_lat)
    mod3 = mod[layer]

    col_scale = jnp.asarray(np.where(np.arange(w_in.shape[1]) < diff_width, QK_SCALE, 1.0)[None, :], F32)
    p_lat = _even_inproj(h_lat, mod3, lat_row(tm_wide_lat), w_in, col_scale, cos_t, sin_t, qk_width=diff_width,
                         use_rope=True, tm=tm_wide_lat, tn=1024).reshape(b, n_lat, -1)
    p_ctx = _even_inproj(h_ctx, mod3, ctx_row(tm_wide_ctx), w_in, col_scale, cos_t[:SUBLANES], sin_t[:SUBLANES],
                         qk_width=diff_width, use_rope=False, tm=tm_wide_ctx, tn=1024).reshape(b, n_ctx, -1)

    attn_lat = _diff_attention(p_lat, [p_ctx, p_lat], diff_lambda[0], diff_subln[0], lam_init,
                               n_heads=n_diff_heads, tq=min(512, n_lat))
    attn_ctx = _diff_attention(p_ctx, [p_ctx], diff_lambda[0], diff_subln[0], lam_init,
                               n_heads=n_diff_heads, tq=n_ctx)
    four_block = 3 * diff_width // four_width
    four_lat = _fourier_mix(p_lat, four_block, four_width, tm=tm_lat)
    four_ctx = _fourier_mix(p_ctx, four_block, four_width, tm=n_ctx)

    w_list = [(w_out, 0), (w_out, diff_width)]
    h_lat = _proj_residual([attn_lat.reshape(b * n_lat, -1), four_lat.reshape(b * n_lat, -1)], w_list,
                           h_lat, mod3, 2, lat_row(tm_lat), ln_mix_g[layer], ln_mix_b[layer], tm=tm_lat)
    h_ctx = _proj_residual([attn_ctx.reshape(b * n_ctx, -1), four_ctx.reshape(b * n_ctx, -1)], w_list,
                           h_ctx, mod3, 2, ctx_row(tm_ctx), ln_mix_g[layer], ln_mix_b[layer], tm=tm_ctx)
    h_lat = ffn(h_lat, mod3, lat_row, layer, n_lat, tm_wide_lat, tm_wide_lat)
    h_ctx = ffn(h_ctx, mod3, ctx_row, layer, n_ctx, tm_wide_ctx, tm_wide_ctx)

    layer = 1
    mod3 = mod[layer]
    hw = hgrn_w_in[0].astype(BF16)
    wrap = lambda t, n: t.reshape(b, n, -1)
    act_c, k_c, lf_c = _hgrn_inproj(h_ctx, mod3, ctx_row(tm_wide_ctx), hw, hgrn_lower_bounds,
                                    layer=layer, tm=tm_wide_ctx, tn=1024)
    act_l, k_l, lf_l = _hgrn_inproj(h_lat, mod3, lat_row(tm_wide_lat), hw, hgrn_lower_bounds,
                                    layer=layer, tm=tm_wide_lat, tn=1024)
    zero_state = jnp.zeros((b, n_hgrn_heads, 2, HEAD_DIM, HEAD_DIM), F32)
    (s_ctx,) = _gla(wrap(act_c, n_ctx), wrap(k_c, n_ctx), wrap(lf_c, n_ctx), zero_state, hgrn_norm[0],
                    n_heads=n_hgrn_heads, emit_y=False, emit_state=True)
    (y_lat,) = _gla(wrap(act_l, n_lat), wrap(k_l, n_lat), wrap(lf_l, n_lat), s_ctx, hgrn_norm[0],
                    n_heads=n_hgrn_heads, emit_y=True, emit_state=False)
    h_lat = _proj_residual([y_lat.reshape(b * n_lat, d)], [(hgrn_w_out[0].astype(BF16), 0)], h_lat, mod3, 2,
                           lat_row(tm_lat), ln_mix_g[layer], ln_mix_b[layer], tm=tm_lat)
    h_lat = ffn(h_lat, mod3, lat_row, layer, n_lat, tm_wide_lat, tm_wide_lat)
    return h_lat.reshape(b, n_lat, d)
```

```python
import functools
import math

import numpy as np
import jax
import jax.numpy as jnp
from jax import lax
from jax.experimental import pallas as pl
from jax.experimental.pallas import tpu as pltpu

F32 = jnp.float32
BF16 = jnp.bfloat16

LANES = 128
SUBLANES = 8
MXU_DIM = 256
VMEM_LIMIT_BYTES = 56 * 1024 * 1024

GRID_W = 64
DIFF_QK_DIM = 64
HEAD_DIM = 128
ROPE_AXIS_DIM = DIFF_QK_DIM // 2
ROPE_BASE = 10000.0
CONV_W = 3
LN_EPS = 1e-6
RMS_EPS = 1e-5
DEPTH = 2
ALPHA = (2.0 * DEPTH) ** 0.25
QK_SCALE = DIFF_QK_DIM ** -0.5 * math.log2(math.e)
ATTN_SAFE_LOG2 = 60.0

GLA_CHUNK = 128
GLA_LEVELS = (64, 32, 16)
GLA_BLOCK = 16
LOG2_E = math.log2(math.e)
GLA_SAFE_RANGE = 80.0
GLA_UNROLL = 16


def _params(*dims):
    return pltpu.CompilerParams(dimension_semantics=dims, vmem_limit_bytes=VMEM_LIMIT_BYTES)


def _layer_norm_rows(x):
    mu = jnp.mean(x, axis=-1, keepdims=True)
    xc = x - mu
    var = jnp.mean(xc * xc, axis=-1, keepdims=True)
    return xc * lax.rsqrt(var + LN_EPS)


def _silu(x):
    return x * jax.nn.sigmoid(x)


def _mod_kernel(c_ref, w_ref, b_ref, o_ref):
    x = _silu(c_ref[...]).astype(BF16)
    o_ref[...] = jnp.dot(x, w_ref[...].astype(BF16), preferred_element_type=F32) + b_ref[...]


def _modulation(cc, mod_w, mod_b, tn=1024):
    depth, d, n = mod_w.shape
    rows = cc.shape[0]
    return pl.pallas_call(
        _mod_kernel,
        grid=(depth, n // tn),
        in_specs=[
            pl.BlockSpec((rows, d), lambda l, j: (0, 0)),
            pl.BlockSpec((None, d, tn), lambda l, j: (l, 0, j)),
            pl.BlockSpec((None, 1, tn), lambda l, j: (l, 0, j)),
        ],
        out_specs=pl.BlockSpec((None, rows, tn), lambda l, j: (l, 0, j)),
        out_shape=jax.ShapeDtypeStruct((depth, rows, n), F32),
        compiler_params=_params("arbitrary", "arbitrary"),
        name="modulation",
    )(cc, mod_w, mod_b.reshape(depth, 1, n))


def _mod_spec(chunk, d, row_of_tile):
    return pl.BlockSpec((None, 1, d), lambda i, *_: (row_of_tile(i), 0, chunk))


def _even_inproj_kernel(h_ref, sh_ref, sc_ref, w_ref, cscale_ref, cos_ref, sin_ref, o_ref, u_ref,
                        *, n_rope_tiles, use_rope):
    j = pl.program_id(1)

    @pl.when(j == 0)
    def _():
        u_ref[...] = (_layer_norm_rows(h_ref[...]) * (1.0 + sc_ref[...]) + sh_ref[...]).astype(BF16)

    def tile(rope):
        for c0 in range(0, o_ref.shape[1], MXU_DIM):
            cols = slice(c0, c0 + MXU_DIM)
            y = jnp.dot(u_ref[...], w_ref[:, cols], preferred_element_type=F32)
            if rope:
                parts = []
                for hh in range(MXU_DIM // HEAD_DIM):
                    yh = y[:, hh * HEAD_DIM:(hh + 1) * HEAD_DIM]
                    parts.append(yh * cos_ref[...] + pltpu.roll(yh, HEAD_DIM // 2, axis=1) * sin_ref[...])
                y = jnp.concatenate(parts, axis=1)
            o_ref[:, cols] = (y * cscale_ref[:, cols]).astype(BF16)

    if use_rope:
        pl.when(j < n_rope_tiles)(lambda: tile(True))
        pl.when(j >= n_rope_tiles)(lambda: tile(False))
    else:
        tile(False)


def _even_inproj(h, mod3, row_of_tile, w, col_scale, cos_t, sin_t, *, qk_width, use_rope, tm, tn):
    m, d = h.shape
    n = w.shape[1]
    n_pos_tiles = cos_t.shape[0] // tm if use_rope else 1
    tab_rows = tm if use_rope else cos_t.shape[0]
    kern = functools.partial(_even_inproj_kernel, n_rope_tiles=2 * qk_width // tn, use_rope=use_rope)
    return pl.pallas_call(
        kern,
        grid=(m // tm, n // tn),
        in_specs=[
            pl.BlockSpec((tm, d), lambda i, j: (i, 0)),
            _mod_spec(0, d, row_of_tile),
            _mod_spec(1, d, row_of_tile),
            pl.BlockSpec((d, tn), lambda i, j: (0, j)),
            pl.BlockSpec((1, tn), lambda i, j: (0, j)),
            pl.BlockSpec((tab_rows, HEAD_DIM), lambda i, j: (i % n_pos_tiles, 0)),
            pl.BlockSpec((tab_rows, HEAD_DIM), lambda i, j: (i % n_pos_tiles, 0)),
        ],
        out_specs=pl.BlockSpec((tm, tn), lambda i, j: (i, j)),
        out_shape=jax.ShapeDtypeStruct((m, n), BF16),
        scratch_shapes=[pltpu.VMEM((tm, d), BF16)],
        compiler_params=_params("arbitrary", "arbitrary"),
        name="even_inproj",
    )(h, mod3, mod3, w, col_scale, cos_t, sin_t)


def _diff_attn_kernel(*refs, n_seg, lam_init, tq):
    q_ref = refs[0]
    k_refs = refs[1:1 + n_seg]
    v_refs = refs[1 + n_seg:1 + 2 * n_seg]
    lam_ref, subln_ref, o_ref = refs[1 + 2 * n_seg:]

    lv = lam_ref[...]
    lam = (jnp.exp(jnp.sum(lv[0:1] * lv[1:2], axis=1, keepdims=True))
           - jnp.exp(jnp.sum(lv[2:3] * lv[3:4], axis=1, keepdims=True)) + lam_init)

    ones = jnp.ones((HEAD_DIM, HEAD_DIM), BF16)

    def max_sq_norm(ref):
        x = ref[...]
        row_sums = jnp.dot(x * x, ones, preferred_element_type=F32)
        return jnp.max(row_sums)

    knorm = functools.reduce(jnp.maximum, [max_sq_norm(k_ref) for k_ref in k_refs])
    small_scores = max_sq_norm(q_ref) * knorm <= 0.98 * ATTN_SAFE_LOG2 ** 2

    dn = (((1,), (1,)), ((), ()))
    lane = lax.broadcasted_iota(jnp.int32, (tq, HEAD_DIM), 1)
    first_map = (lane % DIFF_QK_DIM) < ROPE_AXIS_DIM

    def query_tile(t, shift_by_max):
        rows = pl.ds(pl.multiple_of(t * tq, tq), tq)
        q = q_ref[rows, :]
        zero = jnp.zeros_like(q)
        maps = []
        for qm in (jnp.where(first_map, q, zero), jnp.where(first_map, zero, q)):
            s = [lax.dot_general(qm, k_ref[...], dn, preferred_element_type=F32) for k_ref in k_refs]
            if shift_by_max:
                mx = functools.reduce(jnp.maximum, [jnp.max(x, axis=1, keepdims=True) for x in s])
                s = [x - mx for x in s]
            num = den = None
            for seg in range(n_seg):
                e = jnp.exp2(s[seg])
                d = jnp.sum(e, axis=1, keepdims=True)
                part = jnp.dot(e.astype(BF16), v_refs[seg][...], preferred_element_type=F32)
                den = d if den is None else den + d
                num = part if num is None else num + part
            maps.append((num, den))
        (n1, d1), (n2, d2) = maps
        acc = n1 * (1.0 / d1) - n2 * (lam / d2)
        ms = jnp.mean(acc * acc, axis=1, keepdims=True)
        y = acc * lax.rsqrt(ms + RMS_EPS) * subln_ref[...] * (1.0 - lam_init)
        o_ref[rows, :] = y.astype(BF16)

    def all_tiles(shift_by_max):
        def body(t, carry):
            query_tile(t, shift_by_max)
            return carry
        n_tiles = q_ref.shape[0] // tq
        lax.fori_loop(0, n_tiles, body, 0, unroll=1 if shift_by_max else min(4, n_tiles))

    lax.cond(small_scores, lambda: all_tiles(False), lambda: all_tiles(True))


def _diff_attention(q_src, kv_srcs, lam_vecs, subln, lam_init, *, n_heads, tq):
    b, nq, _ = q_src.shape
    n_seg = len(kv_srcs)
    k_specs = [pl.BlockSpec((None, s.shape[1], HEAD_DIM), lambda bi, h: (bi, 0, n_heads + h))
               for s in kv_srcs]
    v_specs = [pl.BlockSpec((None, s.shape[1], HEAD_DIM), lambda bi, h: (bi, 0, 2 * n_heads + h))
               for s in kv_srcs]
    return pl.pallas_call(
        functools.partial(_diff_attn_kernel, n_seg=n_seg, lam_init=lam_init, tq=tq),
        grid=(b, n_heads),
        in_specs=[pl.BlockSpec((None, nq, HEAD_DIM), lambda bi, h: (bi, 0, h))] + k_specs + v_specs + [
            pl.BlockSpec(lam_vecs.shape, lambda bi, h: (0, 0)),
            pl.BlockSpec((1, HEAD_DIM), lambda bi, h: (0, 0)),
        ],
        out_specs=pl.BlockSpec((None, nq, HEAD_DIM), lambda bi, h: (bi, 0, h)),
        out_shape=jax.ShapeDtypeStruct((b, nq, n_heads * HEAD_DIM), BF16),
        compiler_params=_params("arbitrary", "arbitrary"),
        name="diff_attention",
    )(q_src, *kv_srcs, *kv_srcs, lam_vecs, subln.reshape(1, HEAD_DIM))


def _fourier_kernel(x_ref, dn_ref, cs_ref, o_ref, z_ref):
    n = x_ref.shape[0]

    @pl.when(pl.program_id(1) == 0)
    def _():
        for g in range(x_ref.shape[1] // HEAD_DIM):
            cols = slice(g * HEAD_DIM, (g + 1) * HEAD_DIM)
            zc = jnp.dot(x_ref[:, cols], cs_ref[...], preferred_element_type=F32)
            z_ref[0:n, cols] = zc[:, :HEAD_DIM].astype(BF16)
            z_ref[n:2 * n, cols] = zc[:, HEAD_DIM:].astype(BF16)

    o_ref[...] = jnp.dot(dn_ref[...], z_ref[...], preferred_element_type=F32).astype(BF16)


def _dft_tables(n):
    j = np.arange(n, dtype=np.int64)
    ang = 2.0 * np.pi * ((j[:, None] * j[None, :]) % n).astype(np.float64) / n
    return np.cos(ang) / math.sqrt(n), np.sin(ang) / math.sqrt(n)


def _fourier_mix(src, col_block, width, *, tm):
    b, n, _ = src.shape
    cn, sn = _dft_tables(n)
    cc, sc = _dft_tables(HEAD_DIM)
    dn = jnp.asarray(np.concatenate([cn, -sn], axis=1), dtype=BF16)
    cs = jnp.asarray(np.concatenate([cc, sc], axis=1), dtype=BF16)
    return pl.pallas_call(
        _fourier_kernel,
        grid=(b, n // tm),
        in_specs=[
            pl.BlockSpec((None, n, width), lambda bi, t: (bi, 0, col_block)),
            pl.BlockSpec((tm, 2 * n), lambda bi, t: (t, 0)),
            pl.BlockSpec((HEAD_DIM, 2 * HEAD_DIM), lambda bi, t: (0, 0)),
        ],
        out_specs=pl.BlockSpec((None, tm, width), lambda bi, t: (bi, t, 0)),
        out_shape=jax.ShapeDtypeStruct((b, n, width), BF16),
        scratch_shapes=[pltpu.VMEM((2 * n, width), BF16)],
        compiler_params=_params("arbitrary", "arbitrary"),
        name="fourier_mix",
    )(src, dn, cs)


def _proj_residual_kernel(*refs, n_in, n_k):
    a_refs = refs[:n_in]
    w_refs = refs[n_in:2 * n_in]
    h_ref, gate_ref, g_ref, b_ref, o_ref = refs[2 * n_in:2 * n_in + 5]
    k = pl.program_id(1)
    tm, d = o_ref.shape
    col_block, row_block = 2 * MXU_DIM, MXU_DIM

    def accumulate(first):
        for c0 in range(0, d, col_block):
            cols = slice(c0, c0 + col_block)
            part = None
            for a_ref, w_ref in zip(a_refs, w_refs):
                p = jnp.dot(a_ref[...], w_ref[:, cols], preferred_element_type=F32)
                part = p if part is None else part + p
            if first:
                o_ref[:, cols] = part
            else:
                o_ref[:, cols] += part

    def finish():
        for r0 in range(0, tm, row_block):
            rows = slice(r0, r0 + row_block)
            x = ALPHA * h_ref[rows, :] + gate_ref[...] * o_ref[rows, :]
            o_ref[rows, :] = _layer_norm_rows(x) * g_ref[...] + b_ref[...]

    if n_k == 1:
        accumulate(True)
        finish()
    else:
        pl.when(k == 0)(lambda: accumulate(True))
        pl.when(k > 0)(lambda: accumulate(False))
        pl.when(k == n_k - 1)(finish)


def _proj_residual(a_list, w_list, h, mod3, gate_chunk, row_of_tile, ln_g, ln_b, *, tm, n_k=1):
    m, d = h.shape
    in_specs = []
    for a in a_list:
        in_specs.append(pl.BlockSpec((tm, a.shape[1] // n_k), lambda i, k: (i, k)))
    for a, (w, first_row) in zip(a_list, w_list):
        rows = a.shape[1] // n_k
        in_specs.append(pl.BlockSpec((rows, d), lambda i, k, blk=first_row // rows: (blk + k, 0)))
    in_specs += [
        pl.BlockSpec((tm, d), lambda i, k: (i, 0)),
        _mod_spec(gate_chunk, d, row_of_tile),
        pl.BlockSpec((1, d), lambda i, k: (0, 0)),
        pl.BlockSpec((1, d), lambda i, k: (0, 0)),
    ]
    return pl.pallas_call(
        functools.partial(_proj_residual_kernel, n_in=len(a_list), n_k=n_k),
        grid=(m // tm, n_k),
        in_specs=in_specs,
        out_specs=pl.BlockSpec((tm, d), lambda i, k: (i, 0)),
        out_shape=jax.ShapeDtypeStruct((m, d), F32),
        compiler_params=_params("arbitrary", "arbitrary"),
        name="proj_residual",
    )(*a_list, *[w for w, _ in w_list], h, mod3, ln_g.reshape(1, d), ln_b.reshape(1, d))


def _ffn_up_kernel(h_ref, hp_ref, hn_ref, sh_ref, sc_ref, wa_ref, wv_ref, cw_ref, cb_ref, o_ref,
                   u_ref, uh_ref, *, seq_len):
    i = pl.program_id(0)
    j = pl.program_id(1)
    tm = h_ref.shape[0]
    whole_sequences = tm >= seq_len
    tiles_per_seq = max(seq_len // tm, 1)

    @pl.when(j == 0)
    def _():
        scale = 1.0 + sc_ref[...]
        u_ref[...] = (_layer_norm_rows(h_ref[...]) * scale + sh_ref[...]).astype(BF16)
        if not whole_sequences:
            uh_ref[0:SUBLANES, :] = (_layer_norm_rows(hp_ref[...]) * scale + sh_ref[...]).astype(BF16)
            uh_ref[SUBLANES:, :] = (_layer_norm_rows(hn_ref[...]) * scale + sh_ref[...]).astype(BF16)

    t_in_seq = i % tiles_per_seq
    row = lax.broadcasted_iota(jnp.int32, (tm, MXU_DIM), 0)
    row_in_seq = row % seq_len
    for c0 in range(0, o_ref.shape[1], MXU_DIM):
        cols = slice(c0, c0 + MXU_DIM)
        a = jnp.dot(u_ref[...], wa_ref[:, cols], preferred_element_type=F32)
        v = jnp.dot(u_ref[...], wv_ref[:, cols], preferred_element_type=F32)
        if whole_sequences:
            a_prev = jnp.where(row_in_seq == 0, 0.0, pltpu.roll(a, 1, axis=0))
            a_next = jnp.where(row_in_seq == seq_len - 1, 0.0, pltpu.roll(a, tm - 1, axis=0))
        else:
            halo = jnp.dot(uh_ref[...], wa_ref[:, cols], preferred_element_type=F32)
            prev_row = jnp.where(t_in_seq > 0, halo[SUBLANES - 1:SUBLANES], 0.0)
            next_row = jnp.where(t_in_seq < tiles_per_seq - 1, halo[SUBLANES:SUBLANES + 1], 0.0)
            a_prev = jnp.where(row == 0, prev_row, pltpu.roll(a, 1, axis=0))
            a_next = jnp.where(row == tm - 1, next_row, pltpu.roll(a, tm - 1, axis=0))
        cw = cw_ref[:, cols]
        conv = cb_ref[:, cols] + a_prev * cw[0:1] + a * cw[1:2] + a_next * cw[2:3]
        gelu = 0.5 * conv * (1.0 + lax.erf(conv * math.sqrt(0.5)))
        o_ref[:, cols] = (gelu * v).astype(BF16)


def _ffn_up(h, mod3, row_of_tile, wa, wv, conv_w, conv_b, *, seq_len, tm, tn):
    m, d = h.shape
    n = wa.shape[1]
    assert tm % seq_len == 0 or seq_len % tm == 0
    blocks_per_tile = tm // SUBLANES
    n_blocks = m // SUBLANES
    kern = functools.partial(_ffn_up_kernel, seq_len=seq_len)
    return pl.pallas_call(
        kern,
        grid=(m // tm, n // tn),
        in_specs=[
            pl.BlockSpec((tm, d), lambda i, j: (i, 0)),
            pl.BlockSpec((SUBLANES, d), lambda i, j: (jnp.maximum(i * blocks_per_tile - 1, 0), 0)),
            pl.BlockSpec((SUBLANES, d), lambda i, j: (jnp.minimum((i + 1) * blocks_per_tile, n_blocks - 1), 0)),
            _mod_spec(3, d, row_of_tile),
            _mod_spec(4, d, row_of_tile),
            pl.BlockSpec((d, tn), lambda i, j: (0, j)),
            pl.BlockSpec((d, tn), lambda i, j: (0, j)),
            pl.BlockSpec((CONV_W, tn), lambda i, j: (0, j)),
            pl.BlockSpec((1, tn), lambda i, j: (0, j)),
        ],
        out_specs=pl.BlockSpec((tm, tn), lambda i, j: (i, j)),
        out_shape=jax.ShapeDtypeStruct((m, n), BF16),
        scratch_shapes=[pltpu.VMEM((tm, d), BF16), pltpu.VMEM((2 * SUBLANES, d), BF16)],
        compiler_params=_params("arbitrary", "arbitrary"),
        name="ffn_up",
    )(h, h, h, mod3, mod3, wa, wv, conv_w, conv_b)


def _hgrn_act_kernel(h_ref, sh_ref, sc_ref, w_ref, o_ref, u_ref, *, lin_lo, lin_hi):
    j = pl.program_id(1)

    @pl.when(j == 0)
    def _():
        u_ref[...] = (_layer_norm_rows(h_ref[...]) * (1.0 + sc_ref[...]) + sh_ref[...]).astype(BF16)

    def tile(activation):
        for c0 in range(0, o_ref.shape[1], MXU_DIM):
            cols = slice(c0, c0 + MXU_DIM)
            acc = jnp.dot(u_ref[...], w_ref[:, cols], preferred_element_type=F32)
            o_ref[:, cols] = activation(acc).astype(BF16)

    linear = jnp.logical_and(j >= lin_lo, j < lin_hi)
    pl.when(linear)(lambda: tile(lambda x: x))
    pl.when(jnp.logical_not(linear))(lambda: tile(_silu))


def _hgrn_gate_kernel(h_ref, sh_ref, sc_ref, w_ref, lbp_ref, k_ref, lf_ref, u_ref, *, layer):
    j = pl.program_id(1)

    @pl.when(j == 0)
    def _():
        u_ref[...] = (_layer_norm_rows(h_ref[...]) * (1.0 + sc_ref[...]) + sh_ref[...]).astype(BF16)

    for c0 in range(0, k_ref.shape[1], MXU_DIM):
        cols = slice(c0, c0 + MXU_DIM)
        x = lbp_ref[:, cols]
        e = jnp.exp(x - jnp.max(x, axis=0, keepdims=True))
        lb = jnp.sum(e[1:layer + 1], axis=0, keepdims=True) / jnp.sum(e, axis=0, keepdims=True)

        f_pre = jnp.dot(u_ref[...], w_ref[:, cols], preferred_element_type=F32)
        gate = jax.nn.sigmoid(f_pre)
        k_ref[:, cols] = ((1.0 - lb) * (1.0 - gate)).astype(BF16)
        lf_ref[:, cols] = jnp.log(lb + (1.0 - lb) * gate)


def _hgrn_inproj(h, mod3, row_of_tile, w, lb_params, *, layer, tm, tn):
    m, d = h.shape
    tiles = d // tn
    common = [
        pl.BlockSpec((tm, d), lambda i, j: (i, 0)),
        _mod_spec(0, d, row_of_tile),
        _mod_spec(1, d, row_of_tile),
    ]
    act_w = pl.BlockSpec((d, tn), lambda i, j: (0, jnp.where(j < tiles, j, j + 2 * tiles)))
    gate_w = pl.BlockSpec((d, tn), lambda i, j: (0, j + tiles))
    act = pl.pallas_call(
        functools.partial(_hgrn_act_kernel, lin_lo=tiles, lin_hi=2 * tiles),
        grid=(m // tm, 3 * tiles),
        in_specs=common + [act_w],
        out_specs=pl.BlockSpec((tm, tn), lambda i, j: (i, j)),
        out_shape=jax.ShapeDtypeStruct((m, 3 * d), BF16),
        scratch_shapes=[pltpu.VMEM((tm, d), BF16)],
        compiler_params=_params("arbitrary", "arbitrary"),
        name="hgrn_inproj_act",
    )(h, mod3, mod3, w)
    k, logf = pl.pallas_call(
        functools.partial(_hgrn_gate_kernel, layer=layer),
        grid=(m // tm, 2 * tiles),
        in_specs=common + [gate_w, pl.BlockSpec((None, DEPTH, tn), lambda i, j: (j // tiles, 0, j % tiles))],
        out_specs=[pl.BlockSpec((tm, tn), lambda i, j: (i, j))] * 2,
        out_shape=[jax.ShapeDtypeStruct((m, 2 * d), BF16), jax.ShapeDtypeStruct((m, 2 * d), F32)],
        scratch_shapes=[pltpu.VMEM((tm, d), BF16)],
        compiler_params=_params("arbitrary", "arbitrary"),
        name="hgrn_inproj_gate",
    )(h, mod3, mod3, w, lb_params)
    return act, k, logf


def _gla_matrices(reverse):
    c = GLA_CHUNK
    r_i = lax.broadcasted_iota(jnp.int32, (c, c), 0)
    c_i = lax.broadcasted_iota(jnp.int32, (c, c), 1)
    same = jnp.bitwise_xor(r_i, c_i)
    earlier = (c_i >= r_i) if reverse else (c_i <= r_i)
    in_block = same < GLA_BLOCK
    one = lambda m: jnp.where(m, 1.0, 0.0).astype(BF16)
    return dict(tri=one(earlier), tri_blk=one(jnp.logical_and(earlier, in_block)), ones_blk=one(in_block),
                diag_mask=jnp.logical_and(earlier, in_block), same=same)


def _gla_direct_block_terms(q, k, v, cum, bad, *, reverse):
    c = GLA_CHUNK
    row = lax.broadcasted_iota(jnp.int32, (c, HEAD_DIM), 0) % GLA_BLOCK
    qb = jnp.where(bad, q, 0.0)
    acc0 = jnp.sum(qb * k, axis=1, keepdims=True) * v

    def offset(dlt, acc):
        shift = (c - dlt) if reverse else dlt
        valid = (row <= GLA_BLOCK - 1 - dlt) if reverse else (row >= dlt)
        k_d = pltpu.roll(k, shift, axis=0)
        v_d = pltpu.roll(v, shift, axis=0)
        cum_d = pltpu.roll(cum, shift, axis=0)
        decay = jnp.exp2(jnp.where(valid, cum - cum_d, -jnp.inf))
        return acc + jnp.sum(qb * k_d * decay, axis=1, keepdims=True) * v_d

    return lax.fori_loop(1, GLA_BLOCK, offset, acc0)


def _gla_prefix_sums(lf, mats):
    lf = lf * LOG2_E
    lf_hi = lf.astype(BF16)
    lf_lo = (lf - lf_hi.astype(F32)).astype(BF16)
    two_term = lambda m: (jnp.dot(m, lf_hi, preferred_element_type=F32)
                          + jnp.dot(m, lf_lo, preferred_element_type=F32))
    return two_term(mats["tri"]), two_term(mats["tri_blk"])


def _gla_chunk(q, k, lf, v_bf, cum_ref, w_blk, mats, *, reverse, robust, want_o):
    c = GLA_CHUNK
    dn_t = (((1,), (1,)), ((), ()))
    cum = cum_ref[...]
    end_row = 0 if reverse else c - 1
    cum_end = cum_ref[end_row:end_row + 1, :]

    decayed = lambda x, log2_decay: (x * jnp.exp2(log2_decay)).astype(BF16)
    qe = decayed(q, cum)
    k_dec = decayed(k, cum_end - cum)
    u = lax.dot_general(v_bf, k_dec, (((0,), (0,)), ((), ())), preferred_element_type=F32)
    dec_end = jnp.exp2(cum_end)
    if not want_o:
        return None, None, qe, u, dec_end

    if robust:
        blk_tot = jnp.dot(mats["ones_blk"], (lf * LOG2_E).astype(BF16), preferred_element_type=F32)
        bad = blk_tot < -GLA_SAFE_RANGE
        a_d = jnp.where(bad, 0.0, q * jnp.exp2(w_blk)).astype(BF16)
        b_d = decayed(k, jnp.minimum(-w_blk, GLA_SAFE_RANGE + 20.0))
    else:
        a_d = decayed(q, w_blk)
        b_d = decayed(k, -w_blk)
    scores = jnp.where(mats["diag_mask"], lax.dot_general(a_d, b_d, dn_t, preferred_element_type=F32), 0.0)

    zeros = {m: jnp.zeros((m, HEAD_DIM), BF16) for m in GLA_LEVELS}
    for m in GLA_LEVELS:
        a_parts, b_parts = [], []
        for blk in range(c // (2 * m)):
            lo, mid, hi = blk * 2 * m, blk * 2 * m + m, (blk + 1) * 2 * m
            if reverse:
                ref_row = cum_ref[mid:mid + 1, :]
                qa = decayed(q[lo:mid], cum[lo:mid] - ref_row)
                kb = decayed(k[mid:hi], ref_row - cum[mid:hi])
                a_parts += [qa, zeros[m]]
                b_parts += [zeros[m], kb]
            else:
                ref_row = cum_ref[mid - 1:mid, :]
                qa = decayed(q[mid:hi], cum[mid:hi] - ref_row)
                kb = decayed(k[lo:mid], ref_row - cum[lo:mid])
                a_parts += [zeros[m], qa]
                b_parts += [kb, zeros[m]]
        a_m = jnp.concatenate(a_parts, axis=0)
        b_m = jnp.concatenate(b_parts, axis=0)
        s_m = lax.dot_general(a_m, b_m, dn_t, preferred_element_type=F32)
        if 2 * m < c:
            s_m = jnp.where(mats["same"] < 2 * m, s_m, 0.0)
        scores = scores + s_m
    if robust:
        direct = _gla_direct_block_terms(q, k, v_bf.astype(F32), cum, bad, reverse=reverse)
    else:
        direct = jnp.zeros((c, HEAD_DIM), F32)
    return direct, scores.astype(BF16), qe, u, dec_end


def _gla_kernel(*refs, emit_y, emit_state):
    q_ref, kf_ref, kb_ref, lff_ref, lfb_ref, v_ref, gate_ref, s0_ref, nw_ref = refs[:9]
    pos = 9
    y_ref = s_out_ref = None
    if emit_y:
        y_ref = refs[pos]
        pos += 1
    if emit_state:
        s_out_ref = refs[pos]
        pos += 1
    (of_ref, ob_ref, cumf_ref, cumb_ref, wf_ref, wb_ref, qef_ref, qeb_ref, scf_ref, scb_ref,
     uf_ref, ub_ref, df_ref, db_ref) = refs[pos:pos + 14]

    c = GLA_CHUNK
    n = q_ref.shape[0]
    n_chunks = n // c
    unroll = min(GLA_UNROLL, n_chunks)
    dn_t = (((1,), (1,)), ((), ()))
    chunk_rows = lambda ci: pl.ds(pl.multiple_of(ci * c, c), c)
    mats_f, mats_b = _gla_matrices(False), _gla_matrices(True)

    def prefix_pass():
        def body(ci, _):
            rows = chunk_rows(ci)
            cumf_ref[rows, :], wf_ref[rows, :] = _gla_prefix_sums(lff_ref[rows, :], mats_f)
            cumb_ref[rows, :], wb_ref[rows, :] = _gla_prefix_sums(lfb_ref[rows, :], mats_b)
            return 0

        lax.fori_loop(0, n_chunks, body, 0, unroll=unroll)

    prefix_pass()

    def local_pass(robust):
        def body(ci, _):
            rows = chunk_rows(ci)
            q = q_ref[rows, :].astype(F32)
            v_bf = v_ref[rows, :]
            x_f, sc_f, qe_f, u_f, d_f = _gla_chunk(
                q, kf_ref[rows, :].astype(F32), lff_ref[rows, :], v_bf, cumf_ref.at[rows, :], wf_ref[rows, :],
                mats_f, reverse=False, robust=robust, want_o=emit_y)
            x_b, sc_b, qe_b, u_b, d_b = _gla_chunk(
                q, kb_ref[rows, :].astype(F32), lfb_ref[rows, :], v_bf, cumb_ref.at[rows, :], wb_ref[rows, :],
                mats_b, reverse=True, robust=robust, want_o=emit_y)
            uf_ref[ci] = u_f
            ub_ref[ci] = u_b
            df_ref[ci] = d_f
            db_ref[ci] = d_b
            if emit_y:
                of_ref[rows, :] = x_f
                ob_ref[rows, :] = x_b
                scf_ref[rows, :] = sc_f
                scb_ref[rows, :] = sc_b
                qef_ref[rows, :] = qe_f
                qeb_ref[rows, :] = qe_b
            return 0

        lax.fori_loop(0, n_chunks, body, 0, unroll=1 if robust else unroll)

    worst = jnp.minimum(jnp.min(wf_ref[...]), jnp.min(wb_ref[...]))
    lax.cond(worst < -GLA_SAFE_RANGE, lambda: local_pass(True), lambda: local_pass(False))

    def state_pass(ci, carry):
        st_f, st_b = carry
        cb = n_chunks - 1 - ci
        if emit_y:
            rows_f, rows_b = chunk_rows(ci), chunk_rows(cb)
            of_ref[rows_f, :] += (
                jnp.dot(scf_ref[rows_f, :], v_ref[rows_f, :], preferred_element_type=F32)
                + lax.dot_general(qef_ref[rows_f, :], st_f.astype(BF16), dn_t, preferred_element_type=F32))
            ob_ref[rows_b, :] += (
                jnp.dot(scb_ref[rows_b, :], v_ref[rows_b, :], preferred_element_type=F32)
                + lax.dot_general(qeb_ref[rows_b, :], st_b.astype(BF16), dn_t, preferred_element_type=F32))
        return st_f * df_ref[ci] + uf_ref[ci], st_b * db_ref[cb] + ub_ref[cb]

    st_f, st_b = lax.fori_loop(0, n_chunks, state_pass, (s0_ref[0], s0_ref[1]), unroll=unroll)

    if emit_state:
        s_out_ref[0] = st_f
        s_out_ref[1] = st_b
    if emit_y:
        def readout(ci, _):
            rows = pl.ds(pl.multiple_of(ci * c, c), c)
            tot = of_ref[rows, :] + ob_ref[rows, :]
            ms = jnp.mean(tot * tot, axis=1, keepdims=True)
            y = tot * lax.rsqrt(ms + RMS_EPS) * nw_ref[...] * gate_ref[rows, :].astype(F32)
            y_ref[rows, :] = y.astype(BF16)
            return 0
        lax.fori_loop(0, n_chunks, readout, 0, unroll=unroll)


def _gla(act, k, logf, s0, norm_w, *, n_heads, emit_y, emit_state):
    b, n, _ = act.shape
    hd = HEAD_DIM
    col = lambda off: pl.BlockSpec((None, n, hd), lambda bi, h: (bi, 0, off * n_heads + h))
    st_spec = pl.BlockSpec((None, None, 2, hd, hd), lambda bi, h: (bi, h, 0, 0, 0))
    out_specs, out_shape = [], []
    if emit_y:
        out_specs.append(pl.BlockSpec((None, n, hd), lambda bi, h: (bi, 0, h)))
        out_shape.append(jax.ShapeDtypeStruct((b, n, n_heads * hd), BF16))
    if emit_state:
        out_specs.append(st_spec)
        out_shape.append(jax.ShapeDtypeStruct((b, n_heads, 2, hd, hd), F32))
    outs = pl.pallas_call(
        functools.partial(_gla_kernel, emit_y=emit_y, emit_state=emit_state),
        grid=(b, n_heads),
        in_specs=[col(0), col(0), col(1), col(0), col(1), col(1), col(2), st_spec,
                  pl.BlockSpec((1, hd), lambda bi, h: (0, 0))],
        out_specs=out_specs,
        out_shape=out_shape,
        scratch_shapes=(
            [pltpu.VMEM((n, hd), F32)] * 6
            + [pltpu.VMEM((n, hd), BF16)] * 4
            + [pltpu.VMEM((n // GLA_CHUNK, hd, hd), F32)] * 2
            + [pltpu.VMEM((n // GLA_CHUNK, 1, hd), F32)] * 2),
        compiler_params=_params("arbitrary", "arbitrary"),
        name="gla_scan",
    )(act, k, k, logf, logf, act, act, s0, norm_w.reshape(1, hd))
    return outs


def _rope_head_perm():
    half = ROPE_AXIS_DIM // 2
    first = [0, 2 * half, 4 * half, 6 * half]
    order = first + [s + half for s in first]
    return np.concatenate([np.arange(s, s + half) for s in order])


def _rope_tables(n):
    half = ROPE_AXIS_DIM // 2
    inv = (1.0 / (ROPE_BASE ** (np.arange(0, ROPE_AXIS_DIM, 2, dtype=np.float32) / ROPE_AXIS_DIM))).astype(np.float32)
    pos = np.arange(n)
    ang_r = (pos // GRID_W).astype(np.float32)[:, None] * inv[None, :]
    ang_c = (pos % GRID_W).astype(np.float32)[:, None] * inv[None, :]
    ang = np.concatenate([ang_r, ang_c] * 4, axis=1)
    sign = np.concatenate([-np.ones(4 * half, np.float32), np.ones(4 * half, np.float32)])
    return jnp.asarray(np.cos(ang), F32), jnp.asarray(np.sin(ang) * sign[None, :], F32)


def kernel(x, c, ctx, c_ctx, mod_w, mod_b, ln_mix_g, ln_mix_b, ln_ffn_g, ln_ffn_b, even_w_in, even_w_out, diff_lambda, diff_subln, hgrn_w_in, hgrn_w_out, hgrn_lower_bounds, hgrn_norm, ffn_w_up, ffn_conv_w, ffn_conv_b, ffn_w_down):
    b, n_lat, d = x.shape
    n_ctx = ctx.shape[1]
    assert mod_w.shape[0] == DEPTH and n_lat % GRID_W == 0
    d_ff = ffn_w_down.shape[1]
    d_ff_pad = -(-d_ff // (2 * MXU_DIM)) * (2 * MXU_DIM)
    diff_width = even_w_out.shape[2] * 3 // 4
    four_width = even_w_in.shape[2] - 3 * diff_width
    n_diff_heads = diff_width // HEAD_DIM
    n_hgrn_heads = d // HEAD_DIM

    cc = jnp.concatenate([c, c_ctx[None, :], jnp.zeros((2 * SUBLANES - b - 1, d), F32)], axis=0)
    mod = _modulation(cc, mod_w, mod_b)
    mod = mod.reshape(DEPTH, cc.shape[0], 1, 6 * d)

    h_lat = x.reshape(b * n_lat, d)
    h_ctx = ctx.reshape(b * n_ctx, d)

    tm_wide_lat, tm_lat = min(1024, n_lat), min(512, n_lat)
    ctx_tile = lambda cap: n_ctx * max(1, min(cap // n_ctx, b)) if cap >= n_ctx else cap
    tm_wide_ctx, tm_ctx = ctx_tile(1024), ctx_tile(512)
    assert (b * n_ctx) % tm_wide_ctx == 0 and (b * n_ctx) % tm_ctx == 0
    lat_row = lambda tm: (lambda i: (i * tm) // n_lat)
    ctx_row = lambda tm: (lambda i: b)
    down_k_tiles = 4 if d_ff_pad % (4 * LANES) == 0 else d_ff_pad // 512

    def ffn(h, mod3, row_fn, layer, seq_len, tm_up, tm_down):
        pad = d_ff_pad - d_ff
        w_up = ffn_w_up[layer]
        wa = jnp.pad(w_up[:, :d_ff], ((0, 0), (0, pad))).astype(BF16)
        wv = jnp.pad(w_up[:, d_ff:], ((0, 0), (0, pad))).astype(BF16)
        cw = jnp.pad(ffn_conv_w[layer], ((0, 0), (0, pad)))
        cb = jnp.pad(ffn_conv_b[layer], (0, pad)).reshape(1, d_ff_pad)
        wd = jnp.pad(ffn_w_down[layer], ((0, pad), (0, 0))).astype(BF16)
        g = _ffn_up(h, mod3, row_fn(tm_up), wa, wv, cw, cb, seq_len=seq_len, tm=tm_up, tn=512)
        return _proj_residual([g], [(wd, 0)], h, mod3, 5, row_fn(tm_down), ln_ffn_g[layer], ln_ffn_b[layer],
                              tm=tm_down, n_k=down_k_tiles)

    layer = 0
    lam_init = 0.8 - 0.6 * math.exp(-0.3 * layer)
    perm = _rope_head_perm()
    qk_cols = (np.arange(2 * n_diff_heads)[:, None] * HEAD_DIM + perm[None, :]).reshape(-1)
    cols = np.concatenate([qk_cols, np.arange(2 * diff_width, even_w_in.shape[2])])
    w_in = even_w_in[0][:, cols].astype(BF16)
    w_out = even_w_out[0].astype(BF16)
    cos_t, sin_t = _rope_tables(n_lat)
    mod3 = mod[layer]

    col_scale = jnp.asarray(np.where(np.arange(w_in.shape[1]) < diff_width, QK_SCALE, 1.0)[None, :], F32)
    p_lat = _even_inproj(h_lat, mod3, lat_row(tm_wide_lat), w_in, col_scale, cos_t, sin_t, qk_width=diff_width,
                         use_rope=True, tm=tm_wide_lat, tn=1024).reshape(b, n_lat, -1)
    p_ctx = _even_inproj(h_ctx, mod3, ctx_row(tm_wide_ctx), w_in, col_scale, cos_t[:SUBLANES], sin_t[:SUBLANES],
                         qk_width=diff_width, use_rope=False, tm=tm_wide_ctx, tn=1024).reshape(b, n_ctx, -1)

    attn_lat = _diff_attention(p_lat, [p_ctx, p_lat], diff_lambda[0], diff_subln[0], lam_init,
                               n_heads=n_diff_heads, tq=min(512, n_lat))
    attn_ctx = _diff_attention(p_ctx, [p_ctx], diff_lambda[0], diff_subln[0], lam_init,
                               n_heads=n_diff_heads, tq=n_ctx)
    four_block = 3 * diff_width // four_width
    four_lat = _fourier_mix(p_lat, four_block, four_width, tm=tm_lat)
    four_ctx = _fourier_mix(p_ctx, four_block, four_width, tm=n_ctx)

    w_list = [(w_out, 0), (w_out, diff_width)]
    h_lat = _proj_residual([attn_lat.reshape(b * n_lat, -1), four_lat.reshape(b * n_lat, -1)], w_list,
                           h_lat, mod3, 2, lat_row(tm_lat), ln_mix_g[layer], ln_mix_b[layer], tm=tm_lat)
    h_ctx = _proj_residual([attn_ctx.reshape(b * n_ctx, -1), four_ctx.reshape(b * n_ctx, -1)], w_list,
                           h_ctx, mod3, 2, ctx_row(tm_ctx), ln_mix_g[layer], ln_mix_b[layer], tm=tm_ctx)
    h_lat = ffn(h_lat, mod3, lat_row, layer, n_lat, tm_wide_lat, tm_wide_lat)
    h_ctx = ffn(h_ctx, mod3, ctx_row, layer, n_ctx, tm_wide_ctx, tm_wide_ctx)

    layer = 1
    mod3 = mod[layer]
    hw = hgrn_w_in[0].astype(BF16)
    wrap = lambda t, n: t.reshape(b, n, -1)
    act_c, k_c, lf_c = _hgrn_inproj(h_ctx, mod3, ctx_row(tm_wide_ctx), hw, hgrn_lower_bounds,
                                    layer=layer, tm=tm_wide_ctx, tn=1024)
    act_l, k_l, lf_l = _hgrn_inproj(h_lat, mod3, lat_row(tm_wide_lat), hw, hgrn_lower_bounds,
                                    layer=layer, tm=tm_wide_lat, tn=1024)
    zero_state = jnp.zeros((b, n_hgrn_heads, 2, HEAD_DIM, HEAD_DIM), F32)
    (s_ctx,) = _gla(wrap(act_c, n_ctx), wrap(k_c, n_ctx), wrap(lf_c, n_ctx), zero_state, hgrn_norm[0],
                    n_heads=n_hgrn_heads, emit_y=False, emit_state=True)
    (y_lat,) = _gla(wrap(act_l, n_lat), wrap(k_l, n_lat), wrap(lf_l, n_lat), s_ctx, hgrn_norm[0],
                    n_heads=n_hgrn_heads, emit_y=True, emit_state=False)
    h_lat = _proj_residual([y_lat.reshape(b * n_lat, d)], [(hgrn_w_out[0].astype(BF16), 0)], h_lat, mod3, 2,
                           lat_row(tm_lat), ln_mix_g[layer], ln_mix_b[layer], tm=tm_lat)
    h_lat = ffn(h_lat, mod3, lat_row, layer, n_lat, tm_wide_lat, tm_wide_lat)
    return h_lat.reshape(b, n_lat, d)
```

```python
import functools
import math

import numpy as np
import jax
import jax.numpy as jnp
from jax import lax
from jax.experimental import pallas as pl
from jax.experimental.pallas import tpu as pltpu

F32 = jnp.float32
BF16 = jnp.bfloat16

LANES = 128
SUBLANES = 8
MXU_DIM = 256
VMEM_LIMIT_BYTES = 56 * 1024 * 1024

GRID_W = 64
DIFF_QK_DIM = 64
HEAD_DIM = 128
ROPE_AXIS_DIM = DIFF_QK_DIM // 2
ROPE_BASE = 10000.0
CONV_W = 3
LN_EPS = 1e-6
RMS_EPS = 1e-5
DEPTH = 2
ALPHA = (2.0 * DEPTH) ** 0.25
QK_SCALE = DIFF_QK_DIM ** -0.5 * math.log2(math.e)
ATTN_SAFE_LOG2 = 60.0

GLA_CHUNK = 128
GLA_LEVELS = (64, 32, 16)
GLA_BLOCK = 16
LOG2_E = math.log2(math.e)
GLA_SAFE_RANGE = 80.0
GLA_UNROLL = 16


def _params(*dims):
    return pltpu.CompilerParams(dimension_semantics=dims, vmem_limit_bytes=VMEM_LIMIT_BYTES)


def _layer_norm_rows(x):
    mu = jnp.mean(x, axis=-1, keepdims=True)
    xc = x - mu
    var = jnp.mean(xc * xc, axis=-1, keepdims=True)
    return xc * lax.rsqrt(var + LN_EPS)


def _silu(x):
    return x * jax.nn.sigmoid(x)


def _mod_kernel(c_ref, w_ref, b_ref, o_ref):
    x = _silu(c_ref[...]).astype(BF16)
    o_ref[...] = jnp.dot(x, w_ref[...].astype(BF16), preferred_element_type=F32) + b_ref[...]


def _modulation(cc, mod_w, mod_b, tn=1024):
    depth, d, n = mod_w.shape
    rows = cc.shape[0]
    return pl.pallas_call(
        _mod_kernel,
        grid=(depth, n // tn),
        in_specs=[
            pl.BlockSpec((rows, d), lambda l, j: (0, 0)),
            pl.BlockSpec((None, d, tn), lambda l, j: (l, 0, j)),
            pl.BlockSpec((None, 1, tn), lambda l, j: (l, 0, j)),
        ],
        out_specs=pl.BlockSpec((None, rows, tn), lambda l, j: (l, 0, j)),
        out_shape=jax.ShapeDtypeStruct((depth, rows, n), F32),
        compiler_params=_params("arbitrary", "arbitrary"),
        name="modulation",
    )(cc, mod_w, mod_b.reshape(depth, 1, n))


def _mod_spec(chunk, d, row_of_tile):
    return pl.BlockSpec((None, 1, d), lambda i, *_: (row_of_tile(i), 0, chunk))


def _even_inproj_kernel(h_ref, sh_ref, sc_ref, w_ref, cscale_ref, cos_ref, sin_ref, o_ref, u_ref,
                        *, n_rope_tiles, use_rope):
    j = pl.program_id(1)

    @pl.when(j == 0)
    def _():
        u_ref[...] = (_layer_norm_rows(h_ref[...]) * (1.0 + sc_ref[...]) + sh_ref[...]).astype(BF16)

    def tile(rope):
        for c0 in range(0, o_ref.shape[1], MXU_DIM):
            cols = slice(c0, c0 + MXU_DIM)
            y = jnp.dot(u_ref[...], w_ref[:, cols], preferred_element_type=F32)
            if rope:
                parts = []
                for hh in range(MXU_DIM // HEAD_DIM):
                    yh = y[:, hh * HEAD_DIM:(hh + 1) * HEAD_DIM]
                    parts.append(yh * cos_ref[...] + pltpu.roll(yh, HEAD_DIM // 2, axis=1) * sin_ref[...])
                y = jnp.concatenate(parts, axis=1)
            o_ref[:, cols] = (y * cscale_ref[:, cols]).astype(BF16)

    if use_rope:
        pl.when(j < n_rope_tiles)(lambda: tile(True))
        pl.when(j >= n_rope_tiles)(lambda: tile(False))
    else:
        tile(False)


def _even_inproj(h, mod3, row_of_tile, w, col_scale, cos_t, sin_t, *, qk_width, use_rope, tm, tn):
    m, d = h.shape
    n = w.shape[1]
    n_pos_tiles = cos_t.shape[0] // tm if use_rope else 1
    tab_rows = tm if use_rope else cos_t.shape[0]
    kern = functools.partial(_even_inproj_kernel, n_rope_tiles=2 * qk_width // tn, use_rope=use_rope)
    return pl.pallas_call(
        kern,
        grid=(m // tm, n // tn),
        in_specs=[
            pl.BlockSpec((tm, d), lambda i, j: (i, 0)),
            _mod_spec(0, d, row_of_tile),
            _mod_spec(1, d, row_of_tile),
            pl.BlockSpec((d, tn), lambda i, j: (0, j)),
            pl.BlockSpec((1, tn), lambda i, j: (0, j)),
            pl.BlockSpec((tab_rows, HEAD_DIM), lambda i, j: (i % n_pos_tiles, 0)),
            pl.BlockSpec((tab_rows, HEAD_DIM), lambda i, j: (i % n_pos_tiles, 0)),
        ],
        out_specs=pl.BlockSpec((tm, tn), lambda i, j: (i, j)),
        out_shape=jax.ShapeDtypeStruct((m, n), BF16),
        scratch_shapes=[pltpu.VMEM((tm, d), BF16)],
        compiler_params=_params("arbitrary", "arbitrary"),
        name="even_inproj",
    )(h, mod3, mod3, w, col_scale, cos_t, sin_t)


def _diff_attn_kernel(*refs, n_seg, lam_init, tq):
    q_ref = refs[0]
    k_refs = refs[1:1 + n_seg]
    v_refs = refs[1 + n_seg:1 + 2 * n_seg]
    lam_ref, subln_ref, o_ref = refs[1 + 2 * n_seg:]

    lv = lam_ref[...]
    lam = (jnp.exp(jnp.sum(lv[0:1] * lv[1:2], axis=1, keepdims=True))
           - jnp.exp(jnp.sum(lv[2:3] * lv[3:4], axis=1, keepdims=True)) + lam_init)

    ones = jnp.ones((HEAD_DIM, HEAD_DIM), BF16)

    def max_sq_norm(ref):
        x = ref[...]
        row_sums = jnp.dot(x * x, ones, preferred_element_type=F32)
        return jnp.max(row_sums)

    knorm = functools.reduce(jnp.maximum, [max_sq_norm(k_ref) for k_ref in k_refs])
    small_scores = max_sq_norm(q_ref) * knorm <= 0.98 * ATTN_SAFE_LOG2 ** 2

    dn = (((1,), (1,)), ((), ()))
    lane = lax.broadcasted_iota(jnp.int32, (tq, HEAD_DIM), 1)
    first_map = (lane % DIFF_QK_DIM) < ROPE_AXIS_DIM

    def query_tile(t, shift_by_max):
        rows = pl.ds(pl.multiple_of(t * tq, tq), tq)
        q = q_ref[rows, :]
        zero = jnp.zeros_like(q)
        maps = []
        for qm in (jnp.where(first_map, q, zero), jnp.where(first_map, zero, q)):
            s = [lax.dot_general(qm, k_ref[...], dn, preferred_element_type=F32) for k_ref in k_refs]
            if shift_by_max:
                mx = functools.reduce(jnp.maximum, [jnp.max(x, axis=1, keepdims=True) for x in s])
                s = [x - mx for x in s]
            num = den = None
            for seg in range(n_seg):
                e = jnp.exp2(s[seg])
                d = jnp.sum(e, axis=1, keepdims=True)
                part = jnp.dot(e.astype(BF16), v_refs[seg][...], preferred_element_type=F32)
                den = d if den is None else den + d
                num = part if num is None else num + part
            maps.append((num, den))
        (n1, d1), (n2, d2) = maps
        acc = n1 * (1.0 / d1) - n2 * (lam / d2)
        ms = jnp.mean(acc * acc, axis=1, keepdims=True)
        y = acc * lax.rsqrt(ms + RMS_EPS) * subln_ref[...] * (1.0 - lam_init)
        o_ref[rows, :] = y.astype(BF16)

    def all_tiles(shift_by_max):
        def body(t, carry):
            query_tile(t, shift_by_max)
            return carry
        n_tiles = q_ref.shape[0] // tq
        lax.fori_loop(0, n_tiles, body, 0, unroll=1 if shift_by_max else min(4, n_tiles))

    lax.cond(small_scores, lambda: all_tiles(False), lambda: all_tiles(True))


def _diff_attention(q_src, kv_srcs, lam_vecs, subln, lam_init, *, n_heads, tq):
    b, nq, _ = q_src.shape
    n_seg = len(kv_srcs)
    k_specs = [pl.BlockSpec((None, s.shape[1], HEAD_DIM), lambda bi, h: (bi, 0, n_heads + h))
               for s in kv_srcs]
    v_specs = [pl.BlockSpec((None, s.shape[1], HEAD_DIM), lambda bi, h: (bi, 0, 2 * n_heads + h))
               for s in kv_srcs]
    return pl.pallas_call(
        functools.partial(_diff_attn_kernel, n_seg=n_seg, lam_init=lam_init, tq=tq),
        grid=(b, n_heads),
        in_specs=[pl.BlockSpec((None, nq, HEAD_DIM), lambda bi, h: (bi, 0, h))] + k_specs + v_specs + [
            pl.BlockSpec(lam_vecs.shape, lambda bi, h: (0, 0)),
            pl.BlockSpec((1, HEAD_DIM), lambda bi, h: (0, 0)),
        ],
        out_specs=pl.BlockSpec((None, nq, HEAD_DIM), lambda bi, h: (bi, 0, h)),
        out_shape=jax.ShapeDtypeStruct((b, nq, n_heads * HEAD_DIM), BF16),
        compiler_params=_params("arbitrary", "arbitrary"),
        name="diff_attention",
    )(q_src, *kv_srcs, *kv_srcs, lam_vecs, subln.reshape(1, HEAD_DIM))


def _fourier_kernel(x_ref, dn_ref, cs_ref, o_ref, z_ref):
    n = x_ref.shape[0]

    @pl.when(pl.program_id(1) == 0)
    def _():
        for g in range(x_ref.shape[1] // HEAD_DIM):
            cols = slice(g * HEAD_DIM, (g + 1) * HEAD_DIM)
            zc = jnp.dot(x_ref[:, cols], cs_ref[...], preferred_element_type=F32)
            z_ref[0:n, cols] = zc[:, :HEAD_DIM].astype(BF16)
            z_ref[n:2 * n, cols] = zc[:, HEAD_DIM:].astype(BF16)

    o_ref[...] = jnp.dot(dn_ref[...], z_ref[...], preferred_element_type=F32).astype(BF16)


def _dft_tables(n):
    j = np.arange(n, dtype=np.int64)
    ang = 2.0 * np.pi * ((j[:, None] * j[None, :]) % n).astype(np.float64) / n
    return np.cos(ang) / math.sqrt(n), np.sin(ang) / math.sqrt(n)


def _fourier_mix(src, col_block, width, *, tm):
    b, n, _ = src.shape
    cn, sn = _dft_tables(n)
    cc, sc = _dft_tables(HEAD_DIM)
    dn = jnp.asarray(np.concatenate([cn, -sn], axis=1), dtype=BF16)
    cs = jnp.asarray(np.concatenate([cc, sc], axis=1), dtype=BF16)
    return pl.pallas_call(
        _fourier_kernel,
        grid=(b, n // tm),
        in_specs=[
            pl.BlockSpec((None, n, width), lambda bi, t: (bi, 0, col_block)),
            pl.BlockSpec((tm, 2 * n), lambda bi, t: (t, 0)),
            pl.BlockSpec((HEAD_DIM, 2 * HEAD_DIM), lambda bi, t: (0, 0)),
        ],
        out_specs=pl.BlockSpec((None, tm, width), lambda bi, t: (bi, t, 0)),
        out_shape=jax.ShapeDtypeStruct((b, n, width), BF16),
        scratch_shapes=[pltpu.VMEM((2 * n, width), BF16)],
        compiler_params=_params("arbitrary", "arbitrary"),
        name="fourier_mix",
    )(src, dn, cs)


def _proj_residual_kernel(*refs, n_in, n_k):
    a_refs = refs[:n_in]
    w_refs = refs[n_in:2 * n_in]
    h_ref, gate_ref, g_ref, b_ref, o_ref = refs[2 * n_in:2 * n_in + 5]
    k = pl.program_id(1)
    tm, d = o_ref.shape
    col_block, row_block = 2 * MXU_DIM, MXU_DIM

    def accumulate(first):
        for c0 in range(0, d, col_block):
            cols = slice(c0, c0 + col_block)
            part = None
            for a_ref, w_ref in zip(a_refs, w_refs):
                p = jnp.dot(a_ref[...], w_ref[:, cols], preferred_element_type=F32)
                part = p if part is None else part + p
            if first:
                o_ref[:, cols] = part
            else:
                o_ref[:, cols] += part

    def finish():
        for r0 in range(0, tm, row_block):
            rows = slice(r0, r0 + row_block)
            x = ALPHA * h_ref[rows, :] + gate_ref[...] * o_ref[rows, :]
            o_ref[rows, :] = _layer_norm_rows(x) * g_ref[...] + b_ref[...]

    if n_k == 1:
        accumulate(True)
        finish()
    else:
        pl.when(k == 0)(lambda: accumulate(True))
        pl.when(k > 0)(lambda: accumulate(False))
        pl.when(k == n_k - 1)(finish)


def _proj_residual(a_list, w_list, h, mod3, gate_chunk, row_of_tile, ln_g, ln_b, *, tm, n_k=1):
    m, d = h.shape
    in_specs = []
    for a in a_list:
        in_specs.append(pl.BlockSpec((tm, a.shape[1] // n_k), lambda i, k: (i, k)))
    for a, (w, first_row) in zip(a_list, w_list):
        rows = a.shape[1] // n_k
        in_specs.append(pl.BlockSpec((rows, d), lambda i, k, blk=first_row // rows: (blk + k, 0)))
    in_specs += [
        pl.BlockSpec((tm, d), lambda i, k: (i, 0)),
        _mod_spec(gate_chunk, d, row_of_tile),
        pl.BlockSpec((1, d), lambda i, k: (0, 0)),
        pl.BlockSpec((1, d), lambda i, k: (0, 0)),
    ]
    return pl.pallas_call(
        functools.partial(_proj_residual_kernel, n_in=len(a_list), n_k=n_k),
        grid=(m // tm, n_k),
        in_specs=in_specs,
        out_specs=pl.BlockSpec((tm, d), lambda i, k: (i, 0)),
        out_shape=jax.ShapeDtypeStruct((m, d), F32),
        compiler_params=_params("arbitrary", "arbitrary"),
        name="proj_residual",
    )(*a_list, *[w for w, _ in w_list], h, mod3, ln_g.reshape(1, d), ln_b.reshape(1, d))


def _ffn_up_kernel(h_ref, hp_ref, hn_ref, sh_ref, sc_ref, wa_ref, wv_ref, cw_ref, cb_ref, o_ref,
                   u_ref, uh_ref, *, seq_len):
    i = pl.program_id(0)
    j = pl.program_id(1)
    tm = h_ref.shape[0]
    whole_sequences = tm >= seq_len
    tiles_per_seq = max(seq_len // tm, 1)

    @pl.when(j == 0)
    def _():
        scale = 1.0 + sc_ref[...]
        u_ref[...] = (_layer_norm_rows(h_ref[...]) * scale + sh_ref[...]).astype(BF16)
        if not whole_sequences:
            uh_ref[0:SUBLANES, :] = (_layer_norm_rows(hp_ref[...]) * scale + sh_ref[...]).astype(BF16)
            uh_ref[SUBLANES:, :] = (_layer_norm_rows(hn_ref[...]) * scale + sh_ref[...]).astype(BF16)

    t_in_seq = i % tiles_per_seq
    row = lax.broadcasted_iota(jnp.int32, (tm, MXU_DIM), 0)
    row_in_seq = row % seq_len
    for c0 in range(0, o_ref.shape[1], MXU_DIM):
        cols = slice(c0, c0 + MXU_DIM)
        a = jnp.dot(u_ref[...], wa_ref[:, cols], preferred_element_type=F32)
        v = jnp.dot(u_ref[...], wv_ref[:, cols], preferred_element_type=F32)
        if whole_sequences:
            a_prev = jnp.where(row_in_seq == 0, 0.0, pltpu.roll(a, 1, axis=0))
            a_next = jnp.where(row_in_seq == seq_len - 1, 0.0, pltpu.roll(a, tm - 1, axis=0))
        else:
            halo = jnp.dot(uh_ref[...], wa_ref[:, cols], preferred_element_type=F32)
            prev_row = jnp.where(t_in_seq > 0, halo[SUBLANES - 1:SUBLANES], 0.0)
            next_row = jnp.where(t_in_seq < tiles_per_seq - 1, halo[SUBLANES:SUBLANES + 1], 0.0)
            a_prev = jnp.where(row == 0, prev_row, pltpu.roll(a, 1, axis=0))
            a_next = jnp.where(row == tm - 1, next_row, pltpu.roll(a, tm - 1, axis=0))
        cw = cw_ref[:, cols]
        conv = cb_ref[:, cols] + a_prev * cw[0:1] + a * cw[1:2] + a_next * cw[2:3]
        gelu = 0.5 * conv * (1.0 + lax.erf(conv * math.sqrt(0.5)))
        o_ref[:, cols] = (gelu * v).astype(BF16)


def _ffn_up(h, mod3, row_of_tile, wa, wv, conv_w, conv_b, *, seq_len, tm, tn):
    m, d = h.shape
    n = wa.shape[1]
    assert tm % seq_len == 0 or seq_len % tm == 0
    blocks_per_tile = tm // SUBLANES
    n_blocks = m // SUBLANES
    kern = functools.partial(_ffn_up_kernel, seq_len=seq_len)
    return pl.pallas_call(
        kern,
        grid=(m // tm, n // tn),
        in_specs=[
            pl.BlockSpec((tm, d), lambda i, j: (i, 0)),
            pl.BlockSpec((SUBLANES, d), lambda i, j: (jnp.maximum(i * blocks_per_tile - 1, 0), 0)),
            pl.BlockSpec((SUBLANES, d), lambda i, j: (jnp.minimum((i + 1) * blocks_per_tile, n_blocks - 1), 0)),
            _mod_spec(3, d, row_of_tile),
            _mod_spec(4, d, row_of_tile),
            pl.BlockSpec((d, tn), lambda i, j: (0, j)),
            pl.BlockSpec((d, tn), lambda i, j: (0, j)),
            pl.BlockSpec((CONV_W, tn), lambda i, j: (0, j)),
            pl.BlockSpec((1, tn), lambda i, j: (0, j)),
        ],
        out_specs=pl.BlockSpec((tm, tn), lambda i, j: (i, j)),
        out_shape=jax.ShapeDtypeStruct((m, n), BF16),
        scratch_shapes=[pltpu.VMEM((tm, d), BF16), pltpu.VMEM((2 * SUBLANES, d), BF16)],
        compiler_params=_params("arbitrary", "arbitrary"),
        name="ffn_up",
    )(h, h, h, mod3, mod3, wa, wv, conv_w, conv_b)


def _hgrn_act_kernel(h_ref, sh_ref, sc_ref, w_ref, o_ref, u_ref, *, lin_lo, lin_hi):
    j = pl.program_id(1)

    @pl.when(j == 0)
    def _():
        u_ref[...] = (_layer_norm_rows(h_ref[...]) * (1.0 + sc_ref[...]) + sh_ref[...]).astype(BF16)

    def tile(activation):
        for c0 in range(0, o_ref.shape[1], MXU_DIM):
            cols = slice(c0, c0 + MXU_DIM)
            acc = jnp.dot(u_ref[...], w_ref[:, cols].astype(BF16), preferred_element_type=F32)
            o_ref[:, cols] = activation(acc).astype(BF16)

    linear = jnp.logical_and(j >= lin_lo, j < lin_hi)
    pl.when(linear)(lambda: tile(lambda x: x))
    pl.when(jnp.logical_not(linear))(lambda: tile(_silu))


def _hgrn_gate_kernel(h_ref, sh_ref, sc_ref, w_ref, lbp_ref, k_ref, lf_ref, u_ref, *, layer):
    j = pl.program_id(1)

    @pl.when(j == 0)
    def _():
        u_ref[...] = (_layer_norm_rows(h_ref[...]) * (1.0 + sc_ref[...]) + sh_ref[...]).astype(BF16)

    for c0 in range(0, k_ref.shape[1], MXU_DIM):
        cols = slice(c0, c0 + MXU_DIM)
        x = lbp_ref[:, cols]
        e = jnp.exp(x - jnp.max(x, axis=0, keepdims=True))
        lb = jnp.sum(e[1:layer + 1], axis=0, keepdims=True) / jnp.sum(e, axis=0, keepdims=True)

        f_pre = jnp.dot(u_ref[...], w_ref[:, cols].astype(BF16), preferred_element_type=F32)
        gate = jax.nn.sigmoid(f_pre)
        k_ref[:, cols] = ((1.0 - lb) * (1.0 - gate)).astype(BF16)
        lf_ref[:, cols] = jnp.log(lb + (1.0 - lb) * gate)


def _hgrn_inproj(h, mod3, row_of_tile, w, lb_params, *, layer, tm, tn):
    m, d = h.shape
    tiles = d // tn
    common = [
        pl.BlockSpec((tm, d), lambda i, j: (i, 0)),
        _mod_spec(0, d, row_of_tile),
        _mod_spec(1, d, row_of_tile),
    ]
    act_w = pl.BlockSpec((d, tn), lambda i, j: (0, jnp.where(j < tiles, j, j + 2 * tiles)))
    gate_w = pl.BlockSpec((d, tn), lambda i, j: (0, j + tiles))
    act = pl.pallas_call(
        functools.partial(_hgrn_act_kernel, lin_lo=tiles, lin_hi=2 * tiles),
        grid=(m // tm, 3 * tiles),
        in_specs=common + [act_w],
        out_specs=pl.BlockSpec((tm, tn), lambda i, j: (i, j)),
        out_shape=jax.ShapeDtypeStruct((m, 3 * d), BF16),
        scratch_shapes=[pltpu.VMEM((tm, d), BF16)],
        compiler_params=_params("arbitrary", "arbitrary"),
        name="hgrn_inproj_act",
    )(h, mod3, mod3, w)
    k, logf = pl.pallas_call(
        functools.partial(_hgrn_gate_kernel, layer=layer),
        grid=(m // tm, 2 * tiles),
        in_specs=common + [gate_w, pl.BlockSpec((None, DEPTH, tn), lambda i, j: (j // tiles, 0, j % tiles))],
        out_specs=[pl.BlockSpec((tm, tn), lambda i, j: (i, j))] * 2,
        out_shape=[jax.ShapeDtypeStruct((m, 2 * d), BF16), jax.ShapeDtypeStruct((m, 2 * d), F32)],
        scratch_shapes=[pltpu.VMEM((tm, d), BF16)],
        compiler_params=_params("arbitrary", "arbitrary"),
        name="hgrn_inproj_gate",
    )(h, mod3, mod3, w, lb_params)
    return act, k, logf


def _gla_matrices(reverse):
    c = GLA_CHUNK
    r_i = lax.broadcasted_iota(jnp.int32, (c, c), 0)
    c_i = lax.broadcasted_iota(jnp.int32, (c, c), 1)
    same = jnp.bitwise_xor(r_i, c_i)
    earlier = (c_i >= r_i) if reverse else (c_i <= r_i)
    in_block = same < GLA_BLOCK
    one = lambda m: jnp.where(m, 1.0, 0.0).astype(BF16)
    return dict(tri=one(earlier), tri_blk=one(jnp.logical_and(earlier, in_block)), ones_blk=one(in_block),
                diag_mask=jnp.logical_and(earlier, in_block), same=same)


def _gla_direct_block_terms(q, k, v, cum, bad, *, reverse):
    c = GLA_CHUNK
    row = lax.broadcasted_iota(jnp.int32, (c, HEAD_DIM), 0) % GLA_BLOCK
    qb = jnp.where(bad, q, 0.0)
    acc0 = jnp.sum(qb * k, axis=1, keepdims=True) * v

    def offset(dlt, acc):
        shift = (c - dlt) if reverse else dlt
        valid = (row <= GLA_BLOCK - 1 - dlt) if reverse else (row >= dlt)
        k_d = pltpu.roll(k, shift, axis=0)
        v_d = pltpu.roll(v, shift, axis=0)
        cum_d = pltpu.roll(cum, shift, axis=0)
        decay = jnp.exp2(jnp.where(valid, cum - cum_d, -jnp.inf))
        return acc + jnp.sum(qb * k_d * decay, axis=1, keepdims=True) * v_d

    return lax.fori_loop(1, GLA_BLOCK, offset, acc0)


def _gla_prefix_sums(lf, mats):
    lf = lf * LOG2_E
    lf_hi = lf.astype(BF16)
    lf_lo = (lf - lf_hi.astype(F32)).astype(BF16)
    two_term = lambda m: (jnp.dot(m, lf_hi, preferred_element_type=F32)
                          + jnp.dot(m, lf_lo, preferred_element_type=F32))
    return two_term(mats["tri"]), two_term(mats["tri_blk"])


def _gla_chunk(q, k, lf, v_bf, cum_ref, w_blk, mats, *, reverse, robust, want_o):
    c = GLA_CHUNK
    dn_t = (((1,), (1,)), ((), ()))
    cum = cum_ref[...]
    end_row = 0 if reverse else c - 1
    cum_end = cum_ref[end_row:end_row + 1, :]

    decayed = lambda x, log2_decay: (x * jnp.exp2(log2_decay)).astype(BF16)
    qe = decayed(q, cum)
    k_dec = decayed(k, cum_end - cum)
    u = lax.dot_general(v_bf, k_dec, (((0,), (0,)), ((), ())), preferred_element_type=F32)
    dec_end = jnp.exp2(cum_end)
    if not want_o:
        return None, None, qe, u, dec_end

    if robust:
        blk_tot = jnp.dot(mats["ones_blk"], (lf * LOG2_E).astype(BF16), preferred_element_type=F32)
        bad = blk_tot < -GLA_SAFE_RANGE
        a_d = jnp.where(bad, 0.0, q * jnp.exp2(w_blk)).astype(BF16)
        b_d = decayed(k, jnp.minimum(-w_blk, GLA_SAFE_RANGE + 20.0))
    else:
        a_d = decayed(q, w_blk)
        b_d = decayed(k, -w_blk)
    scores = jnp.where(mats["diag_mask"], lax.dot_general(a_d, b_d, dn_t, preferred_element_type=F32), 0.0)

    zeros = {m: jnp.zeros((m, HEAD_DIM), BF16) for m in GLA_LEVELS}
    for m in GLA_LEVELS:
        a_parts, b_parts = [], []
        for blk in range(c // (2 * m)):
            lo, mid, hi = blk * 2 * m, blk * 2 * m + m, (blk + 1) * 2 * m
            if reverse:
                ref_row = cum_ref[mid:mid + 1, :]
                qa = decayed(q[lo:mid], cum[lo:mid] - ref_row)
                kb = decayed(k[mid:hi], ref_row - cum[mid:hi])
                a_parts += [qa, zeros[m]]
                b_parts += [zeros[m], kb]
            else:
                ref_row = cum_ref[mid - 1:mid, :]
                qa = decayed(q[mid:hi], cum[mid:hi] - ref_row)
                kb = decayed(k[lo:mid], ref_row - cum[lo:mid])
                a_parts += [zeros[m], qa]
                b_parts += [kb, zeros[m]]
        a_m = jnp.concatenate(a_parts, axis=0)
        b_m = jnp.concatenate(b_parts, axis=0)
        s_m = lax.dot_general(a_m, b_m, dn_t, preferred_element_type=F32)
        if 2 * m < c:
            s_m = jnp.where(mats["same"] < 2 * m, s_m, 0.0)
        scores = scores + s_m
    if robust:
        direct = _gla_direct_block_terms(q, k, v_bf.astype(F32), cum, bad, reverse=reverse)
    else:
        direct = jnp.zeros((c, HEAD_DIM), F32)
    return direct, scores.astype(BF16), qe, u, dec_end


def _gla_kernel(*refs, emit_y, emit_state):
    q_ref, kf_ref, kb_ref, lff_ref, lfb_ref, v_ref, gate_ref, s0_ref, nw_ref = refs[:9]
    pos = 9
    y_ref = s_out_ref = None
    if emit_y:
        y_ref = refs[pos]
        pos += 1
    if emit_state:
        s_out_ref = refs[pos]
        pos += 1
    (of_ref, ob_ref, cumf_ref, cumb_ref, wf_ref, wb_ref, qef_ref, qeb_ref, scf_ref, scb_ref,
     uf_ref, ub_ref, df_ref, db_ref) = refs[pos:pos + 14]

    c = GLA_CHUNK
    n = q_ref.shape[0]
    n_chunks = n // c
    unroll = min(GLA_UNROLL, n_chunks)
    dn_t = (((1,), (1,)), ((), ()))
    chunk_rows = lambda ci: pl.ds(pl.multiple_of(ci * c, c), c)
    mats_f, mats_b = _gla_matrices(False), _gla_matrices(True)

    def prefix_pass():
        def body(ci, lowest):
            rows = chunk_rows(ci)
            cum_f, w_f = _gla_prefix_sums(lff_ref[rows, :], mats_f)
            cum_b, w_b = _gla_prefix_sums(lfb_ref[rows, :], mats_b)
            cumf_ref[rows, :], wf_ref[rows, :] = cum_f, w_f
            cumb_ref[rows, :], wb_ref[rows, :] = cum_b, w_b
            return jnp.minimum(lowest, jnp.minimum(w_f, w_b))

        return lax.fori_loop(0, n_chunks, body, jnp.zeros((c, HEAD_DIM), F32), unroll=unroll)

    worst = jnp.min(prefix_pass())

    def local_pass(robust):
        def body(ci, _):
            rows = chunk_rows(ci)
            q = q_ref[rows, :].astype(F32)
            v_bf = v_ref[rows, :]
            x_f, sc_f, qe_f, u_f, d_f = _gla_chunk(
                q, kf_ref[rows, :].astype(F32), lff_ref[rows, :], v_bf, cumf_ref.at[rows, :], wf_ref[rows, :],
                mats_f, reverse=False, robust=robust, want_o=emit_y)
            x_b, sc_b, qe_b, u_b, d_b = _gla_chunk(
                q, kb_ref[rows, :].astype(F32), lfb_ref[rows, :], v_bf, cumb_ref.at[rows, :], wb_ref[rows, :],
                mats_b, reverse=True, robust=robust, want_o=emit_y)
            uf_ref[ci] = u_f
            ub_ref[ci] = u_b
            df_ref[ci] = d_f
            db_ref[ci] = d_b
            if emit_y:
                of_ref[rows, :] = x_f
                ob_ref[rows, :] = x_b
                scf_ref[rows, :] = sc_f
                scb_ref[rows, :] = sc_b
                qef_ref[rows, :] = qe_f
                qeb_ref[rows, :] = qe_b
            return 0

        lax.fori_loop(0, n_chunks, body, 0, unroll=1 if robust else unroll)

    lax.cond(worst < -GLA_SAFE_RANGE, lambda: local_pass(True), lambda: local_pass(False))

    def state_pass(ci, carry):
        st_f, st_b = carry
        cb = n_chunks - 1 - ci
        if emit_y:
            rows_f, rows_b = chunk_rows(ci), chunk_rows(cb)
            of_ref[rows_f, :] += (
                jnp.dot(scf_ref[rows_f, :], v_ref[rows_f, :], preferred_element_type=F32)
                + lax.dot_general(qef_ref[rows_f, :], st_f.astype(BF16), dn_t, preferred_element_type=F32))
            ob_ref[rows_b, :] += (
                jnp.dot(scb_ref[rows_b, :], v_ref[rows_b, :], preferred_element_type=F32)
                + lax.dot_general(qeb_ref[rows_b, :], st_b.astype(BF16), dn_t, preferred_element_type=F32))
        return st_f * df_ref[ci] + uf_ref[ci], st_b * db_ref[cb] + ub_ref[cb]

    st_f, st_b = lax.fori_loop(0, n_chunks, state_pass, (s0_ref[0], s0_ref[1]), unroll=unroll)

    if emit_state:
        s_out_ref[0] = st_f
        s_out_ref[1] = st_b
    if emit_y:
        def readout(ci, _):
            rows = pl.ds(pl.multiple_of(ci * c, c), c)
            tot = of_ref[rows, :] + ob_ref[rows, :]
            ms = jnp.mean(tot * tot, axis=1, keepdims=True)
            y = tot * lax.rsqrt(ms + RMS_EPS) * nw_ref[...] * gate_ref[rows, :].astype(F32)
            y_ref[rows, :] = y.astype(BF16)
            return 0
        lax.fori_loop(0, n_chunks, readout, 0, unroll=unroll)


def _gla(act, k, logf, s0, norm_w, *, n_heads, emit_y, emit_state):
    b, n, _ = act.shape
    hd = HEAD_DIM
    col = lambda off: pl.BlockSpec((None, n, hd), lambda bi, h: (bi, 0, off * n_heads + h))
    st_spec = pl.BlockSpec((None, None, 2, hd, hd), lambda bi, h: (bi, h, 0, 0, 0))
    out_specs, out_shape = [], []
    if emit_y:
        out_specs.append(pl.BlockSpec((None, n, hd), lambda bi, h: (bi, 0, h)))
        out_shape.append(jax.ShapeDtypeStruct((b, n, n_heads * hd), BF16))
    if emit_state:
        out_specs.append(st_spec)
        out_shape.append(jax.ShapeDtypeStruct((b, n_heads, 2, hd, hd), F32))
    outs = pl.pallas_call(
        functools.partial(_gla_kernel, emit_y=emit_y, emit_state=emit_state),
        grid=(b, n_heads),
        in_specs=[col(0), col(0), col(1), col(0), col(1), col(1), col(2), st_spec,
                  pl.BlockSpec((1, hd), lambda bi, h: (0, 0))],
        out_specs=out_specs,
        out_shape=out_shape,
        scratch_shapes=(
            [pltpu.VMEM((n, hd), F32)] * 6
            + [pltpu.VMEM((n, hd), BF16)] * 4
            + [pltpu.VMEM((n // GLA_CHUNK, hd, hd), F32)] * 2
            + [pltpu.VMEM((n // GLA_CHUNK, 1, hd), F32)] * 2),
        compiler_params=_params("arbitrary", "arbitrary"),
        name="gla_scan",
    )(act, k, k, logf, logf, act, act, s0, norm_w.reshape(1, hd))
    return outs


def _rope_head_perm():
    half = ROPE_AXIS_DIM // 2
    first = [0, 2 * half, 4 * half, 6 * half]
    order = first + [s + half for s in first]
    return np.concatenate([np.arange(s, s + half) for s in order])


def _rope_tables(n):
    half = ROPE_AXIS_DIM // 2
    inv = (1.0 / (ROPE_BASE ** (np.arange(0, ROPE_AXIS_DIM, 2, dtype=np.float32) / ROPE_AXIS_DIM))).astype(np.float32)
    pos = np.arange(n)
    ang_r = (pos // GRID_W).astype(np.float32)[:, None] * inv[None, :]
    ang_c = (pos % GRID_W).astype(np.float32)[:, None] * inv[None, :]
    ang = np.concatenate([ang_r, ang_c] * 4, axis=1)
    sign = np.concatenate([-np.ones(4 * half, np.float32), np.ones(4 * half, np.float32)])
    return jnp.asarray(np.cos(ang), F32), jnp.asarray(np.sin(ang) * sign[None, :], F32)


def kernel(x, c, ctx, c_ctx, mod_w, mod_b, ln_mix_g, ln_mix_b, ln_ffn_g, ln_ffn_b, even_w_in, even_w_out, diff_lambda, diff_subln, hgrn_w_in, hgrn_w_out, hgrn_lower_bounds, hgrn_norm, ffn_w_up, ffn_conv_w, ffn_conv_b, ffn_w_down):
    b, n_lat, d = x.shape
    n_ctx = ctx.shape[1]
    assert mod_w.shape[0] == DEPTH and n_lat % GRID_W == 0
    d_ff = ffn_w_down.shape[1]
    d_ff_pad = -(-d_ff // (2 * MXU_DIM)) * (2 * MXU_DIM)
    diff_width = even_w_out.shape[2] * 3 // 4
    four_width = even_w_in.shape[2] - 3 * diff_width
    n_diff_heads = diff_width // HEAD_DIM
    n_hgrn_heads = d // HEAD_DIM

    cc = jnp.concatenate([c, c_ctx[None, :], jnp.zeros((2 * SUBLANES - b - 1, d), F32)], axis=0)
    mod = _modulation(cc, mod_w, mod_b)
    mod = mod.reshape(DEPTH, cc.shape[0], 1, 6 * d)

    h_lat = x.reshape(b * n_lat, d)
    h_ctx = ctx.reshape(b * n_ctx, d)

    tm_wide_lat, tm_lat = min(1024, n_lat), min(512, n_lat)
    ctx_tile = lambda cap: n_ctx * max(1, min(cap // n_ctx, b)) if cap >= n_ctx else cap
    tm_wide_ctx, tm_ctx = ctx_tile(1024), ctx_tile(512)
    assert (b * n_ctx) % tm_wide_ctx == 0 and (b * n_ctx) % tm_ctx == 0
    lat_row = lambda tm: (lambda i: (i * tm) // n_lat)
    ctx_row = lambda tm: (lambda i: b)
    down_k_tiles = 4 if d_ff_pad % (4 * LANES) == 0 else d_ff_pad // 512

    def ffn(h, mod3, row_fn, layer, seq_len, tm_up, tm_down):
        pad = d_ff_pad - d_ff
        w_up = ffn_w_up[layer]
        wa = jnp.pad(w_up[:, :d_ff].astype(BF16), ((0, 0), (0, pad)))
        wv = jnp.pad(w_up[:, d_ff:].astype(BF16), ((0, 0), (0, pad)))
        cw = jnp.pad(ffn_conv_w[layer], ((0, 0), (0, pad)))
        cb = jnp.pad(ffn_conv_b[layer], (0, pad)).reshape(1, d_ff_pad)
        wd = jnp.pad(ffn_w_down[layer].astype(BF16), ((0, pad), (0, 0)))
        g = _ffn_up(h, mod3, row_fn(tm_up), wa, wv, cw, cb, seq_len=seq_len, tm=tm_up, tn=512)
        return _proj_residual([g], [(wd, 0)], h, mod3, 5, row_fn(tm_down), ln_ffn_g[layer], ln_ffn_b[layer],
                              tm=tm_down, n_k=down_k_tiles)

    layer = 0
    lam_init = 0.8 - 0.6 * math.exp(-0.3 * layer)
    perm = _rope_head_perm()
    qk_cols = (np.arange(2 * n_diff_heads)[:, None] * HEAD_DIM + perm[None, :]).reshape(-1)
    cols = np.concatenate([qk_cols, np.arange(2 * diff_width, even_w_in.shape[2])])
    w_in = even_w_in[0][:, cols].astype(BF16)
    w_out = even_w_out[0].astype(BF16)
    cos_t, sin_t = _rope_tables(n_lat)
    mod3 = mod[layer]

    col_scale = jnp.asarray(np.where(np.arange(w_in.shape[1]) < diff_width, QK_SCALE, 1.0)[None, :], F32)
    p_lat = _even_inproj(h_lat, mod3, lat_row(tm_wide_lat), w_in, col_scale, cos_t, sin_t, qk_width=diff_width,
                         use_rope=True, tm=tm_wide_lat, tn=1024).reshape(b, n_lat, -1)
    p_ctx = _even_inproj(h_ctx, mod3, ctx_row(tm_wide_ctx), w_in, col_scale, cos_t[:SUBLANES], sin_t[:SUBLANES],
                         qk_width=diff_width, use_rope=False, tm=tm_wide_ctx, tn=1024).reshape(b, n_ctx, -1)

    attn_lat = _diff_attention(p_lat, [p_ctx, p_lat], diff_lambda[0], diff_subln[0], lam_init,
                               n_heads=n_diff_heads, tq=min(512, n_lat))
    attn_ctx = _diff_attention(p_ctx, [p_ctx], diff_lambda[0], diff_subln[0], lam_init,
                               n_heads=n_diff_heads, tq=n_ctx)
    four_block = 3 * diff_width // four_width
    four_lat = _fourier_mix(p_lat, four_block, four_width, tm=tm_lat)
    four_ctx = _fourier_mix(p_ctx, four_block, four_width, tm=n_ctx)

    w_list = [(w_out, 0), (w_out, diff_width)]
    h_lat = _proj_residual([attn_lat.reshape(b * n_lat, -1), four_lat.reshape(b * n_lat, -1)], w_list,
                           h_lat, mod3, 2, lat_row(tm_lat), ln_mix_g[layer], ln_mix_b[layer], tm=tm_lat)
    h_ctx = _proj_residual([attn_ctx.reshape(b * n_ctx, -1), four_ctx.reshape(b * n_ctx, -1)], w_list,
                           h_ctx, mod3, 2, ctx_row(tm_ctx), ln_mix_g[layer], ln_mix_b[layer], tm=tm_ctx)
    h_lat = ffn(h_lat, mod3, lat_row, layer, n_lat, tm_wide_lat, tm_wide_lat)
    h_ctx = ffn(h_ctx, mod3, ctx_row, layer, n_ctx, tm_wide_ctx, tm_wide_ctx)

    layer = 1
    mod3 = mod[layer]
    hw = hgrn_w_in[0]
    wrap = lambda t, n: t.reshape(b, n, -1)
    act_c, k_c, lf_c = _hgrn_inproj(h_ctx, mod3, ctx_row(tm_wide_ctx), hw, hgrn_lower_bounds,
                                    layer=layer, tm=tm_wide_ctx, tn=1024)
    act_l, k_l, lf_l = _hgrn_inproj(h_lat, mod3, lat_row(tm_wide_lat), hw, hgrn_lower_bounds,
                                    layer=layer, tm=tm_wide_lat, tn=1024)
    zero_state = jnp.zeros((b, n_hgrn_heads, 2, HEAD_DIM, HEAD_DIM), F32)
    (s_ctx,) = _gla(wrap(act_c, n_ctx), wrap(k_c, n_ctx), wrap(lf_c, n_ctx), zero_state, hgrn_norm[0],
                    n_heads=n_hgrn_heads, emit_y=False, emit_state=True)
    (y_lat,) = _gla(wrap(act_l, n_lat), wrap(k_l, n_lat), wrap(lf_l, n_lat), s_ctx, hgrn_norm[0],
                    n_heads=n_hgrn_heads, emit_y=True, emit_state=False)
    h_lat = _proj_residual([y_lat.reshape(b * n_lat, d)], [(hgrn_w_out[0].astype(BF16), 0)], h_lat, mod3, 2,
                           lat_row(tm_lat), ln_mix_g[layer], ln_mix_b[layer], tm=tm_lat)
    h_lat = ffn(h_lat, mod3, lat_row, layer, n_lat, tm_wide_lat, tm_wide_lat)
    return h_lat.reshape(b, n_lat, d)
```

```python
import functools
import math

import numpy as np
import jax
import jax.numpy as jnp
from jax import lax
from jax.experimental import pallas as pl
from jax.experimental.pallas import tpu as pltpu

F32 = jnp.float32
BF16 = jnp.bfloat16

LANES = 128
SUBLANES = 8
MXU_DIM = 256
VMEM_LIMIT_BYTES = 56 * 1024 * 1024

GRID_W = 64
DIFF_QK_DIM = 64
HEAD_DIM = 128
ROPE_AXIS_DIM = DIFF_QK_DIM // 2
ROPE_BASE = 10000.0
CONV_W = 3
LN_EPS = 1e-6
RMS_EPS = 1e-5
DEPTH = 2
ALPHA = (2.0 * DEPTH) ** 0.25
QK_SCALE = DIFF_QK_DIM ** -0.5 * math.log2(math.e)
ATTN_SAFE_LOG2 = 60.0

GLA_CHUNK = 128
GLA_LEVELS = (64, 32, 16)
GLA_BLOCK = 16
LOG2_E = math.log2(math.e)
GLA_SAFE_RANGE = 80.0
GLA_UNROLL = 16


def _params(*dims):
    return pltpu.CompilerParams(dimension_semantics=dims, vmem_limit_bytes=VMEM_LIMIT_BYTES)


def _layer_norm_rows(x):
    mu = jnp.mean(x, axis=-1, keepdims=True)
    xc = x - mu
    var = jnp.mean(xc * xc, axis=-1, keepdims=True)
    return xc * lax.rsqrt(var + LN_EPS)


def _silu(x):
    return x * jax.nn.sigmoid(x)


def _mod_kernel(c_ref, w_ref, b_ref, o_ref):
    x = _silu(c_ref[...]).astype(BF16)
    o_ref[...] = jnp.dot(x, w_ref[...].astype(BF16), preferred_element_type=F32) + b_ref[...]


def _modulation(cc, mod_w, mod_b, tn=1024):
    depth, d, n = mod_w.shape
    rows = cc.shape[0]
    return pl.pallas_call(
        _mod_kernel,
        grid=(depth, n // tn),
        in_specs=[
            pl.BlockSpec((rows, d), lambda l, j: (0, 0)),
            pl.BlockSpec((None, d, tn), lambda l, j: (l, 0, j)),
            pl.BlockSpec((None, 1, tn), lambda l, j: (l, 0, j)),
        ],
        out_specs=pl.BlockSpec((None, rows, tn), lambda l, j: (l, 0, j)),
        out_shape=jax.ShapeDtypeStruct((depth, rows, n), F32),
        compiler_params=_params("arbitrary", "arbitrary"),
        name="modulation",
    )(cc, mod_w, mod_b.reshape(depth, 1, n))


def _mod_spec(chunk, d, row_of_tile):
    return pl.BlockSpec((None, 1, d), lambda i, *_: (row_of_tile(i), 0, chunk))


def _even_inproj_kernel(h_ref, sh_ref, sc_ref, w_ref, cscale_ref, cos_ref, sin_ref, o_ref, u_ref,
                        *, n_rope_tiles, use_rope):
    j = pl.program_id(1)

    @pl.when(j == 0)
    def _():
        u_ref[...] = (_layer_norm_rows(h_ref[...]) * (1.0 + sc_ref[...]) + sh_ref[...]).astype(BF16)

    def tile(rope):
        for c0 in range(0, o_ref.shape[1], MXU_DIM):
            cols = slice(c0, c0 + MXU_DIM)
            y = jnp.dot(u_ref[...], w_ref[:, cols], preferred_element_type=F32)
            if rope:
                parts = []
                for hh in range(MXU_DIM // HEAD_DIM):
                    yh = y[:, hh * HEAD_DIM:(hh + 1) * HEAD_DIM]
                    parts.append(yh * cos_ref[...] + pltpu.roll(yh, HEAD_DIM // 2, axis=1) * sin_ref[...])
                y = jnp.concatenate(parts, axis=1)
            o_ref[:, cols] = (y * cscale_ref[:, cols]).astype(BF16)

    if use_rope:
        pl.when(j < n_rope_tiles)(lambda: tile(True))
        pl.when(j >= n_rope_tiles)(lambda: tile(False))
    else:
        tile(False)


def _even_inproj(h, mod3, row_of_tile, w, col_scale, cos_t, sin_t, *, qk_width, use_rope, tm, tn):
    m, d = h.shape
    n = w.shape[1]
    n_pos_tiles = cos_t.shape[0] // tm if use_rope else 1
    tab_rows = tm if use_rope else cos_t.shape[0]
    kern = functools.partial(_even_inproj_kernel, n_rope_tiles=2 * qk_width // tn, use_rope=use_rope)
    return pl.pallas_call(
        kern,
        grid=(m // tm, n // tn),
        in_specs=[
            pl.BlockSpec((tm, d), lambda i, j: (i, 0)),
            _mod_spec(0, d, row_of_tile),
            _mod_spec(1, d, row_of_tile),
            pl.BlockSpec((d, tn), lambda i, j: (0, j)),
            pl.BlockSpec((1, tn), lambda i, j: (0, j)),
            pl.BlockSpec((tab_rows, HEAD_DIM), lambda i, j: (i % n_pos_tiles, 0)),
            pl.BlockSpec((tab_rows, HEAD_DIM), lambda i, j: (i % n_pos_tiles, 0)),
        ],
        out_specs=pl.BlockSpec((tm, tn), lambda i, j: (i, j)),
        out_shape=jax.ShapeDtypeStruct((m, n), BF16),
        scratch_shapes=[pltpu.VMEM((tm, d), BF16)],
        compiler_params=_params("arbitrary", "arbitrary"),
        name="even_inproj",
    )(h, mod3, mod3, w, col_scale, cos_t, sin_t)


def _diff_attn_kernel(q_ref, qc_ref, kc_ref, kl_ref, vc_ref, vl_ref, lam_ref, subln_ref, o_ref, oc_ref,
                      *, lam_init, tq):
    lv = lam_ref[...]
    lam = (jnp.exp(jnp.sum(lv[0:1] * lv[1:2], axis=1, keepdims=True))
           - jnp.exp(jnp.sum(lv[2:3] * lv[3:4], axis=1, keepdims=True)) + lam_init)

    ones = jnp.ones((HEAD_DIM, HEAD_DIM), BF16)

    def max_sq_norm(ref):
        x = ref[...]
        row_sums = jnp.dot(x * x, ones, preferred_element_type=F32)
        return jnp.max(row_sums)

    knorm = jnp.maximum(max_sq_norm(kc_ref), max_sq_norm(kl_ref))
    qnorm = jnp.maximum(max_sq_norm(qc_ref), max_sq_norm(q_ref))
    small_scores = qnorm * knorm <= 0.98 * ATTN_SAFE_LOG2 ** 2

    dn = (((1,), (1,)), ((), ()))

    def attend(q, kv_refs, shift_by_max):
        lane = lax.broadcasted_iota(jnp.int32, q.shape, 1)
        first_map = (lane % DIFF_QK_DIM) < ROPE_AXIS_DIM
        zero = jnp.zeros_like(q)
        maps = []
        for qm in (jnp.where(first_map, q, zero), jnp.where(first_map, zero, q)):
            s = [lax.dot_general(qm, k_ref[...], dn, preferred_element_type=F32) for k_ref, _ in kv_refs]
            if shift_by_max:
                mx = functools.reduce(jnp.maximum, [jnp.max(x, axis=1, keepdims=True) for x in s])
                s = [x - mx for x in s]
            num = den = None
            for x, (_, v_ref) in zip(s, kv_refs):
                e = jnp.exp2(x)
                d = jnp.sum(e, axis=1, keepdims=True)
                part = jnp.dot(e.astype(BF16), v_ref[...], preferred_element_type=F32)
                den = d if den is None else den + d
                num = part if num is None else num + part
            maps.append((num, den))
        (n1, d1), (n2, d2) = maps
        acc = n1 * (1.0 / d1) - n2 * (lam / d2)
        ms = jnp.mean(acc * acc, axis=1, keepdims=True)
        y = acc * lax.rsqrt(ms + RMS_EPS) * subln_ref[...] * (1.0 - lam_init)
        return y.astype(BF16)

    def all_queries(shift_by_max):
        oc_ref[...] = attend(qc_ref[...], [(kc_ref, vc_ref)], shift_by_max)

        def body(t, carry):
            rows = pl.ds(pl.multiple_of(t * tq, tq), tq)
            o_ref[rows, :] = attend(q_ref[rows, :], [(kc_ref, vc_ref), (kl_ref, vl_ref)], shift_by_max)
            return carry
        n_tiles = q_ref.shape[0] // tq
        lax.fori_loop(0, n_tiles, body, 0, unroll=1 if shift_by_max else min(4, n_tiles))

    lax.cond(small_scores, lambda: all_queries(False), lambda: all_queries(True))


def _diff_attention(p_lat, p_ctx, lam_vecs, subln, lam_init, *, n_heads, tq):
    b, nq, _ = p_lat.shape
    n_ctx = p_ctx.shape[1]
    head = lambda rows, sec: pl.BlockSpec((None, rows, HEAD_DIM), lambda bi, h: (bi, 0, sec * n_heads + h))
    out = lambda rows: pl.BlockSpec((None, rows, HEAD_DIM), lambda bi, h: (bi, 0, h))
    return pl.pallas_call(
        functools.partial(_diff_attn_kernel, lam_init=lam_init, tq=tq),
        grid=(b, n_heads),
        in_specs=[head(nq, 0), head(n_ctx, 0), head(n_ctx, 1), head(nq, 1), head(n_ctx, 2), head(nq, 2),
                  pl.BlockSpec(lam_vecs.shape, lambda bi, h: (0, 0)),
                  pl.BlockSpec((1, HEAD_DIM), lambda bi, h: (0, 0))],
        out_specs=[out(nq), out(n_ctx)],
        out_shape=[jax.ShapeDtypeStruct((b, nq, n_heads * HEAD_DIM), BF16),
                   jax.ShapeDtypeStruct((b, n_ctx, n_heads * HEAD_DIM), BF16)],
        compiler_params=_params("arbitrary", "arbitrary"),
        name="diff_attention",
    )(p_lat, p_ctx, p_ctx, p_lat, p_ctx, p_lat, lam_vecs, subln.reshape(1, HEAD_DIM))


def _fourier_kernel(x_ref, dn_ref, cs_ref, o_ref, z_ref):
    n = x_ref.shape[0]

    @pl.when(pl.program_id(1) == 0)
    def _():
        for g in range(x_ref.shape[1] // HEAD_DIM):
            cols = slice(g * HEAD_DIM, (g + 1) * HEAD_DIM)
            zc = jnp.dot(x_ref[:, cols], cs_ref[...], preferred_element_type=F32)
            z_ref[0:n, cols] = zc[:, :HEAD_DIM].astype(BF16)
            z_ref[n:2 * n, cols] = zc[:, HEAD_DIM:].astype(BF16)

    o_ref[...] = jnp.dot(dn_ref[...], z_ref[...], preferred_element_type=F32).astype(BF16)


def _dft_tables(n):
    j = np.arange(n, dtype=np.int64)
    ang = 2.0 * np.pi * ((j[:, None] * j[None, :]) % n).astype(np.float64) / n
    return np.cos(ang) / math.sqrt(n), np.sin(ang) / math.sqrt(n)


def _fourier_mix(src, col_block, width, *, tm):
    b, n, _ = src.shape
    cn, sn = _dft_tables(n)
    cc, sc = _dft_tables(HEAD_DIM)
    dn = jnp.asarray(np.concatenate([cn, -sn], axis=1), dtype=BF16)
    cs = jnp.asarray(np.concatenate([cc, sc], axis=1), dtype=BF16)
    return pl.pallas_call(
        _fourier_kernel,
        grid=(b, n // tm),
        in_specs=[
            pl.BlockSpec((None, n, width), lambda bi, t: (bi, 0, col_block)),
            pl.BlockSpec((tm, 2 * n), lambda bi, t: (t, 0)),
            pl.BlockSpec((HEAD_DIM, 2 * HEAD_DIM), lambda bi, t: (0, 0)),
        ],
        out_specs=pl.BlockSpec((None, tm, width), lambda bi, t: (bi, t, 0)),
        out_shape=jax.ShapeDtypeStruct((b, n, width), BF16),
        scratch_shapes=[pltpu.VMEM((2 * n, width), BF16)],
        compiler_params=_params("arbitrary", "arbitrary"),
        name="fourier_mix",
    )(src, dn, cs)


def _proj_residual_kernel(*refs, n_in, n_k):
    a_refs = refs[:n_in]
    w_refs = refs[n_in:2 * n_in]
    h_ref, gate_ref, g_ref, b_ref, o_ref = refs[2 * n_in:2 * n_in + 5]
    k = pl.program_id(1)
    tm, d = o_ref.shape
    col_block, row_block = 2 * MXU_DIM, MXU_DIM

    def accumulate(first):
        for c0 in range(0, d, col_block):
            cols = slice(c0, c0 + col_block)
            part = None
            for a_ref, w_ref in zip(a_refs, w_refs):
                p = jnp.dot(a_ref[...], w_ref[:, cols], preferred_element_type=F32)
                part = p if part is None else part + p
            if first:
                o_ref[:, cols] = part
            else:
                o_ref[:, cols] += part

    def finish():
        for r0 in range(0, tm, row_block):
            rows = slice(r0, r0 + row_block)
            x = ALPHA * h_ref[rows, :] + gate_ref[...] * o_ref[rows, :]
            o_ref[rows, :] = _layer_norm_rows(x) * g_ref[...] + b_ref[...]

    if n_k == 1:
        accumulate(True)
        finish()
    else:
        pl.when(k == 0)(lambda: accumulate(True))
        pl.when(k > 0)(lambda: accumulate(False))
        pl.when(k == n_k - 1)(finish)


def _proj_residual(a_list, w_list, h, mod3, gate_chunk, row_of_tile, ln_g, ln_b, *, tm, n_k=1):
    m, d = h.shape
    in_specs = []
    for a in a_list:
        in_specs.append(pl.BlockSpec((tm, a.shape[1] // n_k), lambda i, k: (i, k)))
    for a, (w, first_row) in zip(a_list, w_list):
        rows = a.shape[1] // n_k
        in_specs.append(pl.BlockSpec((rows, d), lambda i, k, blk=first_row // rows: (blk + k, 0)))
    in_specs += [
        pl.BlockSpec((tm, d), lambda i, k: (i, 0)),
        _mod_spec(gate_chunk, d, row_of_tile),
        pl.BlockSpec((1, d), lambda i, k: (0, 0)),
        pl.BlockSpec((1, d), lambda i, k: (0, 0)),
    ]
    return pl.pallas_call(
        functools.partial(_proj_residual_kernel, n_in=len(a_list), n_k=n_k),
        grid=(m // tm, n_k),
        in_specs=in_specs,
        out_specs=pl.BlockSpec((tm, d), lambda i, k: (i, 0)),
        out_shape=jax.ShapeDtypeStruct((m, d), F32),
        compiler_params=_params("arbitrary", "arbitrary"),
        name="proj_residual",
    )(*a_list, *[w for w, _ in w_list], h, mod3, ln_g.reshape(1, d), ln_b.reshape(1, d))


def _ffn_up_kernel(h_ref, hp_ref, hn_ref, sh_ref, sc_ref, wa_ref, wv_ref, cw_ref, cb_ref, o_ref,
                   u_ref, uh_ref, *, seq_len):
    i = pl.program_id(0)
    j = pl.program_id(1)
    tm = h_ref.shape[0]
    whole_sequences = tm >= seq_len
    tiles_per_seq = max(seq_len // tm, 1)

    @pl.when(j == 0)
    def _():
        scale = 1.0 + sc_ref[...]
        u_ref[...] = (_layer_norm_rows(h_ref[...]) * scale + sh_ref[...]).astype(BF16)
        if not whole_sequences:
            uh_ref[0:SUBLANES, :] = (_layer_norm_rows(hp_ref[...]) * scale + sh_ref[...]).astype(BF16)
            uh_ref[SUBLANES:, :] = (_layer_norm_rows(hn_ref[...]) * scale + sh_ref[...]).astype(BF16)

    t_in_seq = i % tiles_per_seq
    row = lax.broadcasted_iota(jnp.int32, (tm, MXU_DIM), 0)
    row_in_seq = row % seq_len
    for c0 in range(0, o_ref.shape[1], MXU_DIM):
        cols = slice(c0, c0 + MXU_DIM)
        a = jnp.dot(u_ref[...], wa_ref[:, cols], preferred_element_type=F32)
        v = jnp.dot(u_ref[...], wv_ref[:, cols], preferred_element_type=F32)
        if whole_sequences:
            a_prev = jnp.where(row_in_seq == 0, 0.0, pltpu.roll(a, 1, axis=0))
            a_next = jnp.where(row_in_seq == seq_len - 1, 0.0, pltpu.roll(a, tm - 1, axis=0))
        else:
            halo = jnp.dot(uh_ref[...], wa_ref[:, cols], preferred_element_type=F32)
            prev_row = jnp.where(t_in_seq > 0, halo[SUBLANES - 1:SUBLANES], 0.0)
            next_row = jnp.where(t_in_seq < tiles_per_seq - 1, halo[SUBLANES:SUBLANES + 1], 0.0)
            a_prev = jnp.where(row == 0, prev_row, pltpu.roll(a, 1, axis=0))
            a_next = jnp.where(row == tm - 1, next_row, pltpu.roll(a, tm - 1, axis=0))
        cw = cw_ref[:, cols]
        conv = cb_ref[:, cols] + a_prev * cw[0:1] + a * cw[1:2] + a_next * cw[2:3]
        gelu = 0.5 * conv * (1.0 + lax.erf(conv * math.sqrt(0.5)))
        o_ref[:, cols] = (gelu * v).astype(BF16)


def _ffn_up(h, mod3, row_of_tile, wa, wv, conv_w, conv_b, *, seq_len, tm, tn):
    m, d = h.shape
    n = wa.shape[1]
    assert tm % seq_len == 0 or seq_len % tm == 0
    blocks_per_tile = tm // SUBLANES
    n_blocks = m // SUBLANES
    kern = functools.partial(_ffn_up_kernel, seq_len=seq_len)
    return pl.pallas_call(
        kern,
        grid=(m // tm, n // tn),
        in_specs=[
            pl.BlockSpec((tm, d), lambda i, j: (i, 0)),
            pl.BlockSpec((SUBLANES, d), lambda i, j: (jnp.maximum(i * blocks_per_tile - 1, 0), 0)),
            pl.BlockSpec((SUBLANES, d), lambda i, j: (jnp.minimum((i + 1) * blocks_per_tile, n_blocks - 1), 0)),
            _mod_spec(3, d, row_of_tile),
            _mod_spec(4, d, row_of_tile),
            pl.BlockSpec((d, tn), lambda i, j: (0, j)),
            pl.BlockSpec((d, tn), lambda i, j: (0, j)),
            pl.BlockSpec((CONV_W, tn), lambda i, j: (0, j)),
            pl.BlockSpec((1, tn), lambda i, j: (0, j)),
        ],
        out_specs=pl.BlockSpec((tm, tn), lambda i, j: (i, j)),
        out_shape=jax.ShapeDtypeStruct((m, n), BF16),
        scratch_shapes=[pltpu.VMEM((tm, d), BF16), pltpu.VMEM((2 * SUBLANES, d), BF16)],
        compiler_params=_params("arbitrary", "arbitrary"),
        name="ffn_up",
    )(h, h, h, mod3, mod3, wa, wv, conv_w, conv_b)


def _hgrn_act_kernel(h_ref, sh_ref, sc_ref, w_ref, o_ref, u_ref, *, lin_lo, lin_hi):
    j = pl.program_id(1)

    @pl.when(j == 0)
    def _():
        u_ref[...] = (_layer_norm_rows(h_ref[...]) * (1.0 + sc_ref[...]) + sh_ref[...]).astype(BF16)

    def tile(activation):
        for c0 in range(0, o_ref.shape[1], MXU_DIM):
            cols = slice(c0, c0 + MXU_DIM)
            acc = jnp.dot(u_ref[...], w_ref[:, cols], preferred_element_type=F32)
            o_ref[:, cols] = activation(acc).astype(BF16)

    linear = jnp.logical_and(j >= lin_lo, j < lin_hi)
    pl.when(linear)(lambda: tile(lambda x: x))
    pl.when(jnp.logical_not(linear))(lambda: tile(_silu))


def _hgrn_gate_kernel(h_ref, sh_ref, sc_ref, w_ref, lbp_ref, k_ref, lf_ref, u_ref, *, layer):
    j = pl.program_id(1)

    @pl.when(j == 0)
    def _():
        u_ref[...] = (_layer_norm_rows(h_ref[...]) * (1.0 + sc_ref[...]) + sh_ref[...]).astype(BF16)

    for c0 in range(0, k_ref.shape[1], MXU_DIM):
        cols = slice(c0, c0 + MXU_DIM)
        x = lbp_ref[:, cols]
        e = jnp.exp(x - jnp.max(x, axis=0, keepdims=True))
        lb = jnp.sum(e[1:layer + 1], axis=0, keepdims=True) / jnp.sum(e, axis=0, keepdims=True)

        f_pre = jnp.dot(u_ref[...], w_ref[:, cols], preferred_element_type=F32)
        gate = jax.nn.sigmoid(f_pre)
        k_ref[:, cols] = ((1.0 - lb) * (1.0 - gate)).astype(BF16)
        lf_ref[:, cols] = jnp.log(lb + (1.0 - lb) * gate)


def _hgrn_inproj(h, mod3, row_of_tile, w, lb_params, *, layer, tm, tn):
    m, d = h.shape
    tiles = d // tn
    common = [
        pl.BlockSpec((tm, d), lambda i, j: (i, 0)),
        _mod_spec(0, d, row_of_tile),
        _mod_spec(1, d, row_of_tile),
    ]
    act_w = pl.BlockSpec((d, tn), lambda i, j: (0, jnp.where(j < tiles, j, j + 2 * tiles)))
    gate_w = pl.BlockSpec((d, tn), lambda i, j: (0, j + tiles))
    act = pl.pallas_call(
        functools.partial(_hgrn_act_kernel, lin_lo=tiles, lin_hi=2 * tiles),
        grid=(m // tm, 3 * tiles),
        in_specs=common + [act_w],
        out_specs=pl.BlockSpec((tm, tn), lambda i, j: (i, j)),
        out_shape=jax.ShapeDtypeStruct((m, 3 * d), BF16),
        scratch_shapes=[pltpu.VMEM((tm, d), BF16)],
        compiler_params=_params("arbitrary", "arbitrary"),
        name="hgrn_inproj_act",
    )(h, mod3, mod3, w)
    k, logf = pl.pallas_call(
        functools.partial(_hgrn_gate_kernel, layer=layer),
        grid=(m // tm, 2 * tiles),
        in_specs=common + [gate_w, pl.BlockSpec((None, DEPTH, tn), lambda i, j: (j // tiles, 0, j % tiles))],
        out_specs=[pl.BlockSpec((tm, tn), lambda i, j: (i, j))] * 2,
        out_shape=[jax.ShapeDtypeStruct((m, 2 * d), BF16), jax.ShapeDtypeStruct((m, 2 * d), F32)],
        scratch_shapes=[pltpu.VMEM((tm, d), BF16)],
        compiler_params=_params("arbitrary", "arbitrary"),
        name="hgrn_inproj_gate",
    )(h, mod3, mod3, w, lb_params)
    return act, k, logf


def _gla_matrices(reverse):
    c = GLA_CHUNK
    r_i = lax.broadcasted_iota(jnp.int32, (c, c), 0)
    c_i = lax.broadcasted_iota(jnp.int32, (c, c), 1)
    same = jnp.bitwise_xor(r_i, c_i)
    earlier = (c_i >= r_i) if reverse else (c_i <= r_i)
    in_block = same < GLA_BLOCK
    one = lambda m: jnp.where(m, 1.0, 0.0).astype(BF16)
    return dict(tri=one(earlier), tri_blk=one(jnp.logical_and(earlier, in_block)), ones_blk=one(in_block),
                diag_mask=jnp.logical_and(earlier, in_block), same=same)


def _gla_direct_block_terms(q, k, v, cum, bad, *, reverse):
    c = GLA_CHUNK
    row = lax.broadcasted_iota(jnp.int32, (c, HEAD_DIM), 0) % GLA_BLOCK
    qb = jnp.where(bad, q, 0.0)
    acc0 = jnp.sum(qb * k, axis=1, keepdims=True) * v

    def offset(dlt, acc):
        shift = (c - dlt) if reverse else dlt
        valid = (row <= GLA_BLOCK - 1 - dlt) if reverse else (row >= dlt)
        k_d = pltpu.roll(k, shift, axis=0)
        v_d = pltpu.roll(v, shift, axis=0)
        cum_d = pltpu.roll(cum, shift, axis=0)
        decay = jnp.exp2(jnp.where(valid, cum - cum_d, -jnp.inf))
        return acc + jnp.sum(qb * k_d * decay, axis=1, keepdims=True) * v_d

    return lax.fori_loop(1, GLA_BLOCK, offset, acc0)


def _gla_prefix_sums(lf, mats):
    lf = lf * LOG2_E
    lf_hi = lf.astype(BF16)
    lf_lo = (lf - lf_hi.astype(F32)).astype(BF16)
    two_term = lambda m: (jnp.dot(m, lf_hi, preferred_element_type=F32)
                          + jnp.dot(m, lf_lo, preferred_element_type=F32))
    return two_term(mats["tri"]), two_term(mats["tri_blk"])


def _gla_chunk(q, k, lf, v_bf, cum_ref, w_blk, mats, *, reverse, robust, want_o):
    c = GLA_CHUNK
    dn_t = (((1,), (1,)), ((), ()))
    cum = cum_ref[...]
    end_row = 0 if reverse else c - 1
    cum_end = cum_ref[end_row:end_row + 1, :]

    decayed = lambda x, log2_decay: (x * jnp.exp2(log2_decay)).astype(BF16)
    k_dec = decayed(k, cum_end - cum)
    u = lax.dot_general(v_bf, k_dec, (((0,), (0,)), ((), ())), preferred_element_type=F32)
    dec_end = jnp.exp2(cum_end)
    if not want_o:
        return None, None, None, u, dec_end
    qe = decayed(q, cum)

    if robust:
        blk_tot = jnp.dot(mats["ones_blk"], (lf * LOG2_E).astype(BF16), preferred_element_type=F32)
        bad = blk_tot < -GLA_SAFE_RANGE
        a_d = jnp.where(bad, 0.0, q * jnp.exp2(w_blk)).astype(BF16)
        b_d = decayed(k, jnp.minimum(-w_blk, GLA_SAFE_RANGE + 20.0))
    else:
        a_d = decayed(q, w_blk)
        b_d = decayed(k, -w_blk)
    scores = jnp.where(mats["diag_mask"], lax.dot_general(a_d, b_d, dn_t, preferred_element_type=F32), 0.0)

    zeros = {m: jnp.zeros((m, HEAD_DIM), BF16) for m in GLA_LEVELS}
    for m in GLA_LEVELS:
        a_parts, b_parts = [], []
        for blk in range(c // (2 * m)):
            lo, mid, hi = blk * 2 * m, blk * 2 * m + m, (blk + 1) * 2 * m
            if reverse:
                ref_row = cum_ref[mid:mid + 1, :]
                qa = decayed(q[lo:mid], cum[lo:mid] - ref_row)
                kb = decayed(k[mid:hi], ref_row - cum[mid:hi])
                a_parts += [qa, zeros[m]]
                b_parts += [zeros[m], kb]
            else:
                ref_row = cum_ref[mid - 1:mid, :]
                qa = decayed(q[mid:hi], cum[mid:hi] - ref_row)
                kb = decayed(k[lo:mid], ref_row - cum[lo:mid])
                a_parts += [zeros[m], qa]
                b_parts += [kb, zeros[m]]
        a_m = jnp.concatenate(a_parts, axis=0)
        b_m = jnp.concatenate(b_parts, axis=0)
        s_m = lax.dot_general(a_m, b_m, dn_t, preferred_element_type=F32)
        if 2 * m < c:
            s_m = jnp.where(mats["same"] < 2 * m, s_m, 0.0)
        scores = scores + s_m
    if robust:
        direct = _gla_direct_block_terms(q, k, v_bf.astype(F32), cum, bad, reverse=reverse)
    else:
        direct = jnp.zeros((c, HEAD_DIM), F32)
    return direct, scores.astype(BF16), qe, u, dec_end


def _gla_kernel(kfc_ref, kbc_ref, lffc_ref, lfbc_ref, vc_ref, q_ref, kf_ref, kb_ref, lff_ref, lfb_ref, v_ref,
                gate_ref, nw_ref, y_ref, of_ref, ob_ref, cumf_ref, cumb_ref, wf_ref, wb_ref, qef_ref, qeb_ref,
                scf_ref, scb_ref, uf_ref, ub_ref, df_ref, db_ref):
    c = GLA_CHUNK
    dn_t = (((1,), (1,)), ((), ()))
    chunk_rows = lambda ci: pl.ds(pl.multiple_of(ci * c, c), c)
    mats_f, mats_b = _gla_matrices(False), _gla_matrices(True)

    def scan(q_ref, kf_ref, kb_ref, lff_ref, lfb_ref, v_ref, state, want_o):
        n_chunks = kf_ref.shape[0] // c
        unroll = min(GLA_UNROLL, n_chunks)

        def prefix_body(ci, lowest):
            rows = chunk_rows(ci)
            cum_f, w_f = _gla_prefix_sums(lff_ref[rows, :], mats_f)
            cum_b, w_b = _gla_prefix_sums(lfb_ref[rows, :], mats_b)
            cumf_ref[rows, :], wf_ref[rows, :] = cum_f, w_f
            cumb_ref[rows, :], wb_ref[rows, :] = cum_b, w_b
            return jnp.minimum(lowest, jnp.minimum(w_f, w_b))

        worst = jnp.min(lax.fori_loop(0, n_chunks, prefix_body, jnp.zeros((c, HEAD_DIM), F32), unroll=unroll))

        def local_pass(robust):
            def body(ci, _):
                rows = chunk_rows(ci)
                q = q_ref[rows, :].astype(F32) if want_o else None
                v_bf = v_ref[rows, :]
                x_f, sc_f, qe_f, u_f, d_f = _gla_chunk(
                    q, kf_ref[rows, :].astype(F32), lff_ref[rows, :], v_bf, cumf_ref.at[rows, :],
                    wf_ref[rows, :], mats_f, reverse=False, robust=robust, want_o=want_o)
                x_b, sc_b, qe_b, u_b, d_b = _gla_chunk(
                    q, kb_ref[rows, :].astype(F32), lfb_ref[rows, :], v_bf, cumb_ref.at[rows, :],
                    wb_ref[rows, :], mats_b, reverse=True, robust=robust, want_o=want_o)
                uf_ref[ci] = u_f
                ub_ref[ci] = u_b
                df_ref[ci] = d_f
                db_ref[ci] = d_b
                if want_o:
                    of_ref[rows, :] = x_f
                    ob_ref[rows, :] = x_b
                    scf_ref[rows, :] = sc_f
                    scb_ref[rows, :] = sc_b
                    qef_ref[rows, :] = qe_f
                    qeb_ref[rows, :] = qe_b
                return 0

            lax.fori_loop(0, n_chunks, body, 0, unroll=1 if robust else unroll)

        if want_o:
            lax.cond(worst < -GLA_SAFE_RANGE, lambda: local_pass(True), lambda: local_pass(False))
        else:
            local_pass(False)

        def state_pass(ci, carry):
            st_f, st_b = carry
            cb = n_chunks - 1 - ci
            if want_o:
                rows_f, rows_b = chunk_rows(ci), chunk_rows(cb)
                of_ref[rows_f, :] += (
                    jnp.dot(scf_ref[rows_f, :], v_ref[rows_f, :], preferred_element_type=F32)
                    + lax.dot_general(qef_ref[rows_f, :], st_f.astype(BF16), dn_t, preferred_element_type=F32))
                ob_ref[rows_b, :] += (
                    jnp.dot(scb_ref[rows_b, :], v_ref[rows_b, :], preferred_element_type=F32)
                    + lax.dot_general(qeb_ref[rows_b, :], st_b.astype(BF16), dn_t, preferred_element_type=F32))
            return st_f * df_ref[ci] + uf_ref[ci], st_b * db_ref[cb] + ub_ref[cb]

        return lax.fori_loop(0, n_chunks, state_pass, state, unroll=unroll)

    zero = jnp.zeros((HEAD_DIM, HEAD_DIM), F32)
    state = scan(None, kfc_ref, kbc_ref, lffc_ref, lfbc_ref, vc_ref, (zero, zero), False)
    scan(q_ref, kf_ref, kb_ref, lff_ref, lfb_ref, v_ref, state, True)

    n_chunks = q_ref.shape[0] // c

    def readout(ci, _):
        rows = chunk_rows(ci)
        tot = of_ref[rows, :] + ob_ref[rows, :]
        ms = jnp.mean(tot * tot, axis=1, keepdims=True)
        y = tot * lax.rsqrt(ms + RMS_EPS) * nw_ref[...] * gate_ref[rows, :].astype(F32)
        y_ref[rows, :] = y.astype(BF16)
        return 0
    lax.fori_loop(0, n_chunks, readout, 0, unroll=min(GLA_UNROLL, n_chunks))


def _gla(act, k, logf, act_ctx, k_ctx, logf_ctx, norm_w, *, n_heads):
    b, n, _ = act.shape
    n_ctx = act_ctx.shape[1]
    assert n_ctx <= n
    hd = HEAD_DIM
    col = lambda rows, off: pl.BlockSpec((None, rows, hd), lambda bi, h: (bi, 0, off * n_heads + h))
    return pl.pallas_call(
        _gla_kernel,
        grid=(b, n_heads),
        in_specs=[col(n_ctx, 0), col(n_ctx, 1), col(n_ctx, 0), col(n_ctx, 1), col(n_ctx, 1),
                  col(n, 0), col(n, 0), col(n, 1), col(n, 0), col(n, 1), col(n, 1), col(n, 2),
                  pl.BlockSpec((1, hd), lambda bi, h: (0, 0))],
        out_specs=pl.BlockSpec((None, n, hd), lambda bi, h: (bi, 0, h)),
        out_shape=jax.ShapeDtypeStruct((b, n, n_heads * hd), BF16),
        scratch_shapes=(
            [pltpu.VMEM((n, hd), F32)] * 6
            + [pltpu.VMEM((n, hd), BF16)] * 4
            + [pltpu.VMEM((n // GLA_CHUNK, hd, hd), F32)] * 2
            + [pltpu.VMEM((n // GLA_CHUNK, 1, hd), F32)] * 2),
        compiler_params=_params("arbitrary", "arbitrary"),
        name="gla_scan",
    )(k_ctx, k_ctx, logf_ctx, logf_ctx, act_ctx, act, k, k, logf, logf, act, act, norm_w.reshape(1, hd))


def _rope_head_perm():
    half = ROPE_AXIS_DIM // 2
    first = [0, 2 * half, 4 * half, 6 * half]
    order = first + [s + half for s in first]
    return np.concatenate([np.arange(s, s + half) for s in order])


def _rope_tables(n):
    half = ROPE_AXIS_DIM // 2
    inv = (1.0 / (ROPE_BASE ** (np.arange(0, ROPE_AXIS_DIM, 2, dtype=np.float32) / ROPE_AXIS_DIM))).astype(np.float32)
    pos = np.arange(n)
    ang_r = (pos // GRID_W).astype(np.float32)[:, None] * inv[None, :]
    ang_c = (pos % GRID_W).astype(np.float32)[:, None] * inv[None, :]
    ang = np.concatenate([ang_r, ang_c] * 4, axis=1)
    sign = np.concatenate([-np.ones(4 * half, np.float32), np.ones(4 * half, np.float32)])
    return jnp.asarray(np.cos(ang), F32), jnp.asarray(np.sin(ang) * sign[None, :], F32)


def kernel(x, c, ctx, c_ctx, mod_w, mod_b, ln_mix_g, ln_mix_b, ln_ffn_g, ln_ffn_b, even_w_in, even_w_out, diff_lambda, diff_subln, hgrn_w_in, hgrn_w_out, hgrn_lower_bounds, hgrn_norm, ffn_w_up, ffn_conv_w, ffn_conv_b, ffn_w_down):
    b, n_lat, d = x.shape
    n_ctx = ctx.shape[1]
    assert mod_w.shape[0] == DEPTH and n_lat % GRID_W == 0
    d_ff = ffn_w_down.shape[1]
    d_ff_pad = -(-d_ff // (2 * MXU_DIM)) * (2 * MXU_DIM)
    diff_width = even_w_out.shape[2] * 3 // 4
    four_width = even_w_in.shape[2] - 3 * diff_width
    n_diff_heads = diff_width // HEAD_DIM
    n_hgrn_heads = d // HEAD_DIM

    cc = jnp.concatenate([c, c_ctx[None, :], jnp.zeros((2 * SUBLANES - b - 1, d), F32)], axis=0)
    mod = _modulation(cc, mod_w, mod_b)
    mod = mod.reshape(DEPTH, cc.shape[0], 1, 6 * d)

    h_lat = x.reshape(b * n_lat, d)
    h_ctx = ctx.reshape(b * n_ctx, d)

    tm_wide_lat, tm_lat = min(1024, n_lat), min(512, n_lat)
    ctx_tile = lambda cap: n_ctx * max(1, min(cap // n_ctx, b)) if cap >= n_ctx else cap
    tm_wide_ctx, tm_ctx = ctx_tile(1024), ctx_tile(512)
    assert (b * n_ctx) % tm_wide_ctx == 0 and (b * n_ctx) % tm_ctx == 0
    lat_row = lambda tm: (lambda i: (i * tm) // n_lat)
    ctx_row = lambda tm: (lambda i: b)
    down_k_tiles = 4 if d_ff_pad % (4 * LANES) == 0 else d_ff_pad // 512

    def ffn(h, mod3, row_fn, layer, seq_len, tm_up, tm_down):
        pad = d_ff_pad - d_ff
        w_up = ffn_w_up[layer]
        wa = jnp.pad(w_up[:, :d_ff].astype(BF16), ((0, 0), (0, pad)))
        wv = jnp.pad(w_up[:, d_ff:].astype(BF16), ((0, 0), (0, pad)))
        cw = jnp.pad(ffn_conv_w[layer], ((0, 0), (0, pad)))
        cb = jnp.pad(ffn_conv_b[layer], (0, pad)).reshape(1, d_ff_pad)
        wd = jnp.pad(ffn_w_down[layer].astype(BF16), ((0, pad), (0, 0)))
        g = _ffn_up(h, mod3, row_fn(tm_up), wa, wv, cw, cb, seq_len=seq_len, tm=tm_up, tn=512)
        return _proj_residual([g], [(wd, 0)], h, mod3, 5, row_fn(tm_down), ln_ffn_g[layer], ln_ffn_b[layer],
                              tm=tm_down, n_k=down_k_tiles)

    layer = 0
    lam_init = 0.8 - 0.6 * math.exp(-0.3 * layer)
    perm = _rope_head_perm()
    qk_cols = (np.arange(2 * n_diff_heads)[:, None] * HEAD_DIM + perm[None, :]).reshape(-1)
    cols = np.concatenate([qk_cols, np.arange(2 * diff_width, even_w_in.shape[2])])
    w_in = even_w_in[0][:, cols].astype(BF16)
    w_out = even_w_out[0].astype(BF16)
    cos_t, sin_t = _rope_tables(n_lat)
    mod3 = mod[layer]

    col_scale = jnp.asarray(np.where(np.arange(w_in.shape[1]) < diff_width, QK_SCALE, 1.0)[None, :], F32)
    p_lat = _even_inproj(h_lat, mod3, lat_row(tm_wide_lat), w_in, col_scale, cos_t, sin_t, qk_width=diff_width,
                         use_rope=True, tm=tm_wide_lat, tn=1024).reshape(b, n_lat, -1)
    p_ctx = _even_inproj(h_ctx, mod3, ctx_row(tm_wide_ctx), w_in, col_scale, cos_t[:SUBLANES], sin_t[:SUBLANES],
                         qk_width=diff_width, use_rope=False, tm=tm_wide_ctx, tn=1024).reshape(b, n_ctx, -1)

    attn_lat, attn_ctx = _diff_attention(p_lat, p_ctx, diff_lambda[0], diff_subln[0], lam_init,
                                         n_heads=n_diff_heads, tq=min(512, n_lat))
    four_block = 3 * diff_width // four_width
    four_lat = _fourier_mix(p_lat, four_block, four_width, tm=tm_lat)
    four_ctx = _fourier_mix(p_ctx, four_block, four_width, tm=n_ctx)

    w_list = [(w_out, 0), (w_out, diff_width)]
    h_lat = _proj_residual([attn_lat.reshape(b * n_lat, -1), four_lat.reshape(b * n_lat, -1)], w_list,
                           h_lat, mod3, 2, lat_row(tm_lat), ln_mix_g[layer], ln_mix_b[layer], tm=tm_lat)
    h_ctx = _proj_residual([attn_ctx.reshape(b * n_ctx, -1), four_ctx.reshape(b * n_ctx, -1)], w_list,
                           h_ctx, mod3, 2, ctx_row(tm_ctx), ln_mix_g[layer], ln_mix_b[layer], tm=tm_ctx)
    h_lat = ffn(h_lat, mod3, lat_row, layer, n_lat, tm_wide_lat, tm_wide_lat)
    h_ctx = ffn(h_ctx, mod3, ctx_row, layer, n_ctx, tm_wide_ctx, tm_wide_ctx)

    layer = 1
    mod3 = mod[layer]
    hw = hgrn_w_in[0].astype(BF16)
    wrap = lambda t, n: t.reshape(b, n, -1)
    act_c, k_c, lf_c = _hgrn_inproj(h_ctx, mod3, ctx_row(tm_wide_ctx), hw, hgrn_lower_bounds,
                                    layer=layer, tm=tm_wide_ctx, tn=1024)
    act_l, k_l, lf_l = _hgrn_inproj(h_lat, mod3, lat_row(tm_wide_lat), hw, hgrn_lower_bounds,
                                    layer=layer, tm=tm_wide_lat, tn=1024)
    y_lat = _gla(wrap(act_l, n_lat), wrap(k_l, n_lat), wrap(lf_l, n_lat),
                 wrap(act_c, n_ctx), wrap(k_c, n_ctx), wrap(lf_c, n_ctx), hgrn_norm[0], n_heads=n_hgrn_heads)
    h_lat = _proj_residual([y_lat.reshape(b * n_lat, d)], [(hgrn_w_out[0].astype(BF16), 0)], h_lat, mod3, 2,
                           lat_row(tm_lat), ln_mix_g[layer], ln_mix_b[layer], tm=tm_lat)
    h_lat = ffn(h_lat, mod3, lat_row, layer, n_lat, tm_wide_lat, tm_wide_lat)
    return h_lat.reshape(b, n_lat, d)
```

```python
import functools
import math

import numpy as np
import jax
import jax.numpy as jnp
from jax import lax
from jax.experimental import pallas as pl
from jax.experimental.pallas import tpu as pltpu

F32 = jnp.float32
BF16 = jnp.bfloat16

LANES = 128
SUBLANES = 8
MXU_DIM = 256
VMEM_LIMIT_BYTES = 56 * 1024 * 1024

GRID_W = 64
DIFF_QK_DIM = 64
HEAD_DIM = 128
ROPE_AXIS_DIM = DIFF_QK_DIM // 2
ROPE_BASE = 10000.0
CONV_W = 3
LN_EPS = 1e-6
RMS_EPS = 1e-5
DEPTH = 2
ALPHA = (2.0 * DEPTH) ** 0.25
QK_SCALE = DIFF_QK_DIM ** -0.5 * math.log2(math.e)
ATTN_SAFE_LOG2 = 60.0

GLA_CHUNK = 128
GLA_LEVELS = (64, 32, 16)
GLA_BLOCK = 16
LOG2_E = math.log2(math.e)
GLA_SAFE_RANGE = 80.0
GLA_UNROLL = 16


def _params(*dims):
    return pltpu.CompilerParams(dimension_semantics=dims, vmem_limit_bytes=VMEM_LIMIT_BYTES)


def _layer_norm_rows(x):
    mu = jnp.mean(x, axis=-1, keepdims=True)
    xc = x - mu
    var = jnp.mean(xc * xc, axis=-1, keepdims=True)
    return xc * lax.rsqrt(var + LN_EPS)


def _silu(x):
    return x * jax.nn.sigmoid(x)


def _mod_kernel(c_ref, w_ref, b_ref, o_ref):
    x = _silu(c_ref[...]).astype(BF16)
    o_ref[...] = jnp.dot(x, w_ref[...].astype(BF16), preferred_element_type=F32) + b_ref[...]


def _modulation(cc, mod_w, mod_b, tn=1024):
    depth, d, n = mod_w.shape
    rows = cc.shape[0]
    return pl.pallas_call(
        _mod_kernel,
        grid=(depth, n // tn),
        in_specs=[
            pl.BlockSpec((rows, d), lambda l, j: (0, 0)),
            pl.BlockSpec((None, d, tn), lambda l, j: (l, 0, j)),
            pl.BlockSpec((None, 1, tn), lambda l, j: (l, 0, j)),
        ],
        out_specs=pl.BlockSpec((None, rows, tn), lambda l, j: (l, 0, j)),
        out_shape=jax.ShapeDtypeStruct((depth, rows, n), F32),
        compiler_params=_params("arbitrary", "arbitrary"),
        name="modulation",
    )(cc, mod_w, mod_b.reshape(depth, 1, n))


def _mod_spec(chunk, d, row_of_tile):
    return pl.BlockSpec((None, 1, d), lambda i, *_: (row_of_tile(i), 0, chunk))


def _even_inproj_kernel(h_ref, sh_ref, sc_ref, w_ref, cscale_ref, cos_ref, sin_ref, o_ref, u_ref,
                        *, n_rope_tiles, use_rope):
    j = pl.program_id(1)

    @pl.when(j == 0)
    def _():
        u_ref[...] = (_layer_norm_rows(h_ref[...]) * (1.0 + sc_ref[...]) + sh_ref[...]).astype(BF16)

    def tile(rope):
        half = ROPE_AXIS_DIM // 2
        if rope:
            lane = lax.broadcasted_iota(jnp.int32, (u_ref.shape[0], HEAD_DIM), 1)
            second_half = (lane % ROPE_AXIS_DIM) >= half
        for c0 in range(0, o_ref.shape[1], MXU_DIM):
            cols = slice(c0, c0 + MXU_DIM)
            y = jnp.dot(u_ref[...], w_ref[:, cols], preferred_element_type=F32)
            if rope:
                parts = []
                for hh in range(MXU_DIM // HEAD_DIM):
                    yh = y[:, hh * HEAD_DIM:(hh + 1) * HEAD_DIM]
                    partner = jnp.where(second_half, pltpu.roll(yh, half, axis=1),
                                        pltpu.roll(yh, HEAD_DIM - half, axis=1))
                    parts.append(yh * cos_ref[...] + partner * sin_ref[...])
                y = jnp.concatenate(parts, axis=1)
            o_ref[:, cols] = (y * cscale_ref[:, cols]).astype(BF16)

    if use_rope:
        pl.when(j < n_rope_tiles)(lambda: tile(True))
        pl.when(j >= n_rope_tiles)(lambda: tile(False))
    else:
        tile(False)


def _even_inproj(h, mod3, row_of_tile, w, col_scale, cos_t, sin_t, *, qk_width, use_rope, tm, tn):
    m, d = h.shape
    n = w.shape[1]
    n_pos_tiles = cos_t.shape[0] // tm if use_rope else 1
    tab_rows = tm if use_rope else cos_t.shape[0]
    kern = functools.partial(_even_inproj_kernel, n_rope_tiles=2 * qk_width // tn, use_rope=use_rope)
    return pl.pallas_call(
        kern,
        grid=(m // tm, n // tn),
        in_specs=[
            pl.BlockSpec((tm, d), lambda i, j: (i, 0)),
            _mod_spec(0, d, row_of_tile),
            _mod_spec(1, d, row_of_tile),
            pl.BlockSpec((d, tn), lambda i, j: (0, j)),
            pl.BlockSpec((1, tn), lambda i, j: (0, j)),
            pl.BlockSpec((tab_rows, HEAD_DIM), lambda i, j: (i % n_pos_tiles, 0)),
            pl.BlockSpec((tab_rows, HEAD_DIM), lambda i, j: (i % n_pos_tiles, 0)),
        ],
        out_specs=pl.BlockSpec((tm, tn), lambda i, j: (i, j)),
        out_shape=jax.ShapeDtypeStruct((m, n), BF16),
        scratch_shapes=[pltpu.VMEM((tm, d), BF16)],
        compiler_params=_params("arbitrary", "arbitrary"),
        name="even_inproj",
    )(h, mod3, mod3, w, col_scale, cos_t, sin_t)


def _diff_attn_kernel(q_ref, qc_ref, kc_ref, kl_ref, vc_ref, vl_ref, lam_ref, subln_ref, o_ref, oc_ref,
                      *, lam_init, tq):
    lv = lam_ref[...]
    lam = (jnp.exp(jnp.sum(lv[0:1] * lv[1:2], axis=1, keepdims=True))
           - jnp.exp(jnp.sum(lv[2:3] * lv[3:4], axis=1, keepdims=True)) + lam_init)

    ones = jnp.ones((HEAD_DIM, HEAD_DIM), BF16)

    def max_sq_norm(ref):
        x = ref[...]
        row_sums = jnp.dot(x * x, ones, preferred_element_type=F32)
        return jnp.max(row_sums)

    knorm = jnp.maximum(max_sq_norm(kc_ref), max_sq_norm(kl_ref))
    qnorm = jnp.maximum(max_sq_norm(qc_ref), max_sq_norm(q_ref))
    small_scores = qnorm * knorm <= 0.98 * ATTN_SAFE_LOG2 ** 2

    dn = (((1,), (1,)), ((), ()))

    def attend(q, kv_refs, shift_by_max):
        lane = lax.broadcasted_iota(jnp.int32, q.shape, 1)
        first_map = lane < DIFF_QK_DIM
        zero = jnp.zeros_like(q)
        maps = []
        for qm in (jnp.where(first_map, q, zero), jnp.where(first_map, zero, q)):
            s = [lax.dot_general(qm, k_ref[...], dn, preferred_element_type=F32) for k_ref, _ in kv_refs]
            if shift_by_max:
                mx = functools.reduce(jnp.maximum, [jnp.max(x, axis=1, keepdims=True) for x in s])
                s = [x - mx for x in s]
            num = den = None
            for x, (_, v_ref) in zip(s, kv_refs):
                e = jnp.exp2(x)
                d = jnp.sum(e, axis=1, keepdims=True)
                part = jnp.dot(e.astype(BF16), v_ref[...], preferred_element_type=F32)
                den = d if den is None else den + d
                num = part if num is None else num + part
            maps.append((num, den))
        (n1, d1), (n2, d2) = maps
        acc = n1 * (1.0 / d1) - n2 * (lam / d2)
        ms = jnp.mean(acc * acc, axis=1, keepdims=True)
        y = acc * lax.rsqrt(ms + RMS_EPS) * subln_ref[...] * (1.0 - lam_init)
        return y.astype(BF16)

    def all_queries(shift_by_max):
        oc_ref[...] = attend(qc_ref[...], [(kc_ref, vc_ref)], shift_by_max)

        def body(t, carry):
            rows = pl.ds(pl.multiple_of(t * tq, tq), tq)
            o_ref[rows, :] = attend(q_ref[rows, :], [(kc_ref, vc_ref), (kl_ref, vl_ref)], shift_by_max)
            return carry
        n_tiles = q_ref.shape[0] // tq
        lax.fori_loop(0, n_tiles, body, 0, unroll=1 if shift_by_max else min(4, n_tiles))

    lax.cond(small_scores, lambda: all_queries(False), lambda: all_queries(True))


def _diff_attention(p_lat, p_ctx, lam_vecs, subln, lam_init, *, n_heads, tq):
    b, nq, _ = p_lat.shape
    n_ctx = p_ctx.shape[1]
    head = lambda rows, sec: pl.BlockSpec((None, rows, HEAD_DIM), lambda bi, h: (bi, 0, sec * n_heads + h))
    out = lambda rows: pl.BlockSpec((None, rows, HEAD_DIM), lambda bi, h: (bi, 0, h))
    return pl.pallas_call(
        functools.partial(_diff_attn_kernel, lam_init=lam_init, tq=tq),
        grid=(b, n_heads),
        in_specs=[head(nq, 0), head(n_ctx, 0), head(n_ctx, 1), head(nq, 1), head(n_ctx, 2), head(nq, 2),
                  pl.BlockSpec(lam_vecs.shape, lambda bi, h: (0, 0)),
                  pl.BlockSpec((1, HEAD_DIM), lambda bi, h: (0, 0))],
        out_specs=[out(nq), out(n_ctx)],
        out_shape=[jax.ShapeDtypeStruct((b, nq, n_heads * HEAD_DIM), BF16),
                   jax.ShapeDtypeStruct((b, n_ctx, n_heads * HEAD_DIM), BF16)],
        compiler_params=_params("arbitrary", "arbitrary"),
        name="diff_attention",
    )(p_lat, p_ctx, p_ctx, p_lat, p_ctx, p_lat, lam_vecs, subln.reshape(1, HEAD_DIM))


def _fourier_kernel(x_ref, dn_ref, cs_ref, o_ref, z_ref):
    n = x_ref.shape[0]

    @pl.when(pl.program_id(1) == 0)
    def _():
        for g in range(x_ref.shape[1] // HEAD_DIM):
            cols = slice(g * HEAD_DIM, (g + 1) * HEAD_DIM)
            zc = jnp.dot(x_ref[:, cols], cs_ref[...], preferred_element_type=F32)
            z_ref[0:n, cols] = zc[:, :HEAD_DIM].astype(BF16)
            z_ref[n:2 * n, cols] = zc[:, HEAD_DIM:].astype(BF16)

    o_ref[...] = jnp.dot(dn_ref[...], z_ref[...], preferred_element_type=F32).astype(BF16)


def _dft_tables(n):
    j = np.arange(n, dtype=np.int64)
    ang = 2.0 * np.pi * ((j[:, None] * j[None, :]) % n).astype(np.float64) / n
    return np.cos(ang) / math.sqrt(n), np.sin(ang) / math.sqrt(n)


def _fourier_mix(src, col_block, width, *, tm):
    b, n, _ = src.shape
    cn, sn = _dft_tables(n)
    cc, sc = _dft_tables(HEAD_DIM)
    dn = jnp.asarray(np.concatenate([cn, -sn], axis=1), dtype=BF16)
    cs = jnp.asarray(np.concatenate([cc, sc], axis=1), dtype=BF16)
    return pl.pallas_call(
        _fourier_kernel,
        grid=(b, n // tm),
        in_specs=[
            pl.BlockSpec((None, n, width), lambda bi, t: (bi, 0, col_block)),
            pl.BlockSpec((tm, 2 * n), lambda bi, t: (t, 0)),
            pl.BlockSpec((HEAD_DIM, 2 * HEAD_DIM), lambda bi, t: (0, 0)),
        ],
        out_specs=pl.BlockSpec((None, tm, width), lambda bi, t: (bi, t, 0)),
        out_shape=jax.ShapeDtypeStruct((b, n, width), BF16),
        scratch_shapes=[pltpu.VMEM((2 * n, width), BF16)],
        compiler_params=_params("arbitrary", "arbitrary"),
        name="fourier_mix",
    )(src, dn, cs)


def _proj_residual_kernel(*refs, n_in, n_k):
    a_refs = refs[:n_in]
    w_refs = refs[n_in:2 * n_in]
    h_ref, gate_ref, g_ref, b_ref, o_ref = refs[2 * n_in:2 * n_in + 5]
    k = pl.program_id(1)
    tm, d = o_ref.shape
    col_block, row_block = 2 * MXU_DIM, MXU_DIM

    def accumulate(first):
        for c0 in range(0, d, col_block):
            cols = slice(c0, c0 + col_block)
            part = None
            for a_ref, w_ref in zip(a_refs, w_refs):
                p = jnp.dot(a_ref[...], w_ref[:, cols], preferred_element_type=F32)
                part = p if part is None else part + p
            if first:
                o_ref[:, cols] = part
            else:
                o_ref[:, cols] += part

    def finish():
        for r0 in range(0, tm, row_block):
            rows = slice(r0, r0 + row_block)
            x = ALPHA * h_ref[rows, :] + gate_ref[...] * o_ref[rows, :]
            o_ref[rows, :] = _layer_norm_rows(x) * g_ref[...] + b_ref[...]

    if n_k == 1:
        accumulate(True)
        finish()
    else:
        pl.when(k == 0)(lambda: accumulate(True))
        pl.when(k > 0)(lambda: accumulate(False))
        pl.when(k == n_k - 1)(finish)


def _proj_residual(a_list, w_list, h, mod3, gate_chunk, row_of_tile, ln_g, ln_b, *, tm, n_k=1):
    m, d = h.shape
    in_specs = []
    for a in a_list:
        in_specs.append(pl.BlockSpec((tm, a.shape[1] // n_k), lambda i, k: (i, k)))
    for a, (w, first_row) in zip(a_list, w_list):
        rows = a.shape[1] // n_k
        in_specs.append(pl.BlockSpec((rows, d), lambda i, k, blk=first_row // rows: (blk + k, 0)))
    in_specs += [
        pl.BlockSpec((tm, d), lambda i, k: (i, 0)),
        _mod_spec(gate_chunk, d, row_of_tile),
        pl.BlockSpec((1, d), lambda i, k: (0, 0)),
        pl.BlockSpec((1, d), lambda i, k: (0, 0)),
    ]
    return pl.pallas_call(
        functools.partial(_proj_residual_kernel, n_in=len(a_list), n_k=n_k),
        grid=(m // tm, n_k),
        in_specs=in_specs,
        out_specs=pl.BlockSpec((tm, d), lambda i, k: (i, 0)),
        out_shape=jax.ShapeDtypeStruct((m, d), F32),
        compiler_params=_params("arbitrary", "arbitrary"),
        name="proj_residual",
    )(*a_list, *[w for w, _ in w_list], h, mod3, ln_g.reshape(1, d), ln_b.reshape(1, d))


def _ffn_up_kernel(h_ref, hp_ref, hn_ref, sh_ref, sc_ref, wa_ref, wv_ref, cw_ref, cb_ref, o_ref,
                   u_ref, uh_ref, *, seq_len):
    i = pl.program_id(0)
    j = pl.program_id(1)
    tm = h_ref.shape[0]
    whole_sequences = tm >= seq_len
    tiles_per_seq = max(seq_len // tm, 1)

    @pl.when(j == 0)
    def _():
        scale = 1.0 + sc_ref[...]
        u_ref[...] = (_layer_norm_rows(h_ref[...]) * scale + sh_ref[...]).astype(BF16)
        if not whole_sequences:
            uh_ref[0:SUBLANES, :] = (_layer_norm_rows(hp_ref[...]) * scale + sh_ref[...]).astype(BF16)
            uh_ref[SUBLANES:, :] = (_layer_norm_rows(hn_ref[...]) * scale + sh_ref[...]).astype(BF16)

    t_in_seq = i % tiles_per_seq
    row = lax.broadcasted_iota(jnp.int32, (tm, MXU_DIM), 0)
    row_in_seq = row % seq_len
    for c0 in range(0, o_ref.shape[1], MXU_DIM):
        cols = slice(c0, c0 + MXU_DIM)
        a = jnp.dot(u_ref[...], wa_ref[:, cols], preferred_element_type=F32)
        v = jnp.dot(u_ref[...], wv_ref[:, cols], preferred_element_type=F32)
        if whole_sequences:
            a_prev = jnp.where(row_in_seq == 0, 0.0, pltpu.roll(a, 1, axis=0))
            a_next = jnp.where(row_in_seq == seq_len - 1, 0.0, pltpu.roll(a, tm - 1, axis=0))
        else:
            halo = jnp.dot(uh_ref[...], wa_ref[:, cols], preferred_element_type=F32)
            prev_row = jnp.where(t_in_seq > 0, halo[SUBLANES - 1:SUBLANES], 0.0)
            next_row = jnp.where(t_in_seq < tiles_per_seq - 1, halo[SUBLANES:SUBLANES + 1], 0.0)
            a_prev = jnp.where(row == 0, prev_row, pltpu.roll(a, 1, axis=0))
            a_next = jnp.where(row == tm - 1, next_row, pltpu.roll(a, tm - 1, axis=0))
        cw = cw_ref[:, cols]
        conv = cb_ref[:, cols] + a_prev * cw[0:1] + a * cw[1:2] + a_next * cw[2:3]
        gelu = 0.5 * conv * (1.0 + lax.erf(conv * math.sqrt(0.5)))
        o_ref[:, cols] = (gelu * v).astype(BF16)


def _ffn_up(h, mod3, row_of_tile, wa, wv, conv_w, conv_b, *, seq_len, tm, tn):
    m, d = h.shape
    n = wa.shape[1]
    assert tm % seq_len == 0 or seq_len % tm == 0
    blocks_per_tile = tm // SUBLANES
    n_blocks = m // SUBLANES
    kern = functools.partial(_ffn_up_kernel, seq_len=seq_len)
    return pl.pallas_call(
        kern,
        grid=(m // tm, n // tn),
        in_specs=[
            pl.BlockSpec((tm, d), lambda i, j: (i, 0)),
            pl.BlockSpec((SUBLANES, d), lambda i, j: (jnp.maximum(i * blocks_per_tile - 1, 0), 0)),
            pl.BlockSpec((SUBLANES, d), lambda i, j: (jnp.minimum((i + 1) * blocks_per_tile, n_blocks - 1), 0)),
            _mod_spec(3, d, row_of_tile),
            _mod_spec(4, d, row_of_tile),
            pl.BlockSpec((d, tn), lambda i, j: (0, j)),
            pl.BlockSpec((d, tn), lambda i, j: (0, j)),
            pl.BlockSpec((CONV_W, tn), lambda i, j: (0, j)),
            pl.BlockSpec((1, tn), lambda i, j: (0, j)),
        ],
        out_specs=pl.BlockSpec((tm, tn), lambda i, j: (i, j)),
        out_shape=jax.ShapeDtypeStruct((m, n), BF16),
        scratch_shapes=[pltpu.VMEM((tm, d), BF16), pltpu.VMEM((2 * SUBLANES, d), BF16)],
        compiler_params=_params("arbitrary", "arbitrary"),
        name="ffn_up",
    )(h, h, h, mod3, mod3, wa, wv, conv_w, conv_b)


def _hgrn_act_kernel(h_ref, sh_ref, sc_ref, w_ref, o_ref, u_ref, *, lin_lo, lin_hi):
    j = pl.program_id(1)

    @pl.when(j == 0)
    def _():
        u_ref[...] = (_layer_norm_rows(h_ref[...]) * (1.0 + sc_ref[...]) + sh_ref[...]).astype(BF16)

    def tile(activation):
        for c0 in range(0, o_ref.shape[1], MXU_DIM):
            cols = slice(c0, c0 + MXU_DIM)
            acc = jnp.dot(u_ref[...], w_ref[:, cols], preferred_element_type=F32)
            o_ref[:, cols] = activation(acc).astype(BF16)

    linear = jnp.logical_and(j >= lin_lo, j < lin_hi)
    pl.when(linear)(lambda: tile(lambda x: x))
    pl.when(jnp.logical_not(linear))(lambda: tile(_silu))


def _hgrn_gate_kernel(h_ref, sh_ref, sc_ref, w_ref, lbp_ref, k_ref, lf_ref, u_ref, *, layer):
    j = pl.program_id(1)

    @pl.when(j == 0)
    def _():
        u_ref[...] = (_layer_norm_rows(h_ref[...]) * (1.0 + sc_ref[...]) + sh_ref[...]).astype(BF16)

    for c0 in range(0, k_ref.shape[1], MXU_DIM):
        cols = slice(c0, c0 + MXU_DIM)
        x = lbp_ref[:, cols]
        e = jnp.exp(x - jnp.max(x, axis=0, keepdims=True))
        lb = jnp.sum(e[1:layer + 1], axis=0, keepdims=True) / jnp.sum(e, axis=0, keepdims=True)

        f_pre = jnp.dot(u_ref[...], w_ref[:, cols], preferred_element_type=F32)
        gate = jax.nn.sigmoid(f_pre)
        k_ref[:, cols] = ((1.0 - lb) * (1.0 - gate)).astype(BF16)
        lf_ref[:, cols] = jnp.log(lb + (1.0 - lb) * gate)


def _hgrn_inproj(h, mod3, row_of_tile, w, lb_params, *, layer, tm, tn):
    m, d = h.shape
    tiles = d // tn
    common = [
        pl.BlockSpec((tm, d), lambda i, j: (i, 0)),
        _mod_spec(0, d, row_of_tile),
        _mod_spec(1, d, row_of_tile),
    ]
    act_w = pl.BlockSpec((d, tn), lambda i, j: (0, jnp.where(j < tiles, j, j + 2 * tiles)))
    gate_w = pl.BlockSpec((d, tn), lambda i, j: (0, j + tiles))
    act = pl.pallas_call(
        functools.partial(_hgrn_act_kernel, lin_lo=tiles, lin_hi=2 * tiles),
        grid=(m // tm, 3 * tiles),
        in_specs=common + [act_w],
        out_specs=pl.BlockSpec((tm, tn), lambda i, j: (i, j)),
        out_shape=jax.ShapeDtypeStruct((m, 3 * d), BF16),
        scratch_shapes=[pltpu.VMEM((tm, d), BF16)],
        compiler_params=_params("arbitrary", "arbitrary"),
        name="hgrn_inproj_act",
    )(h, mod3, mod3, w)
    k, logf = pl.pallas_call(
        functools.partial(_hgrn_gate_kernel, layer=layer),
        grid=(m // tm, 2 * tiles),
        in_specs=common + [gate_w, pl.BlockSpec((None, DEPTH, tn), lambda i, j: (j // tiles, 0, j % tiles))],
        out_specs=[pl.BlockSpec((tm, tn), lambda i, j: (i, j))] * 2,
        out_shape=[jax.ShapeDtypeStruct((m, 2 * d), BF16), jax.ShapeDtypeStruct((m, 2 * d), F32)],
        scratch_shapes=[pltpu.VMEM((tm, d), BF16)],
        compiler_params=_params("arbitrary", "arbitrary"),
        name="hgrn_inproj_gate",
    )(h, mod3, mod3, w, lb_params)
    return act, k, logf


def _gla_matrices(reverse):
    c = GLA_CHUNK
    r_i = lax.broadcasted_iota(jnp.int32, (c, c), 0)
    c_i = lax.broadcasted_iota(jnp.int32, (c, c), 1)
    same = jnp.bitwise_xor(r_i, c_i)
    earlier = (c_i >= r_i) if reverse else (c_i <= r_i)
    in_block = same < GLA_BLOCK
    one = lambda m: jnp.where(m, 1.0, 0.0).astype(BF16)
    return dict(tri=one(earlier), tri_blk=one(jnp.logical_and(earlier, in_block)), ones_blk=one(in_block),
                diag_mask=jnp.logical_and(earlier, in_block), same=same)


def _gla_direct_block_terms(q, k, v, cum, bad, *, reverse):
    c = GLA_CHUNK
    row = lax.broadcasted_iota(jnp.int32, (c, HEAD_DIM), 0) % GLA_BLOCK
    qb = jnp.where(bad, q, 0.0)
    acc0 = jnp.sum(qb * k, axis=1, keepdims=True) * v

    def offset(dlt, acc):
        shift = (c - dlt) if reverse else dlt
        valid = (row <= GLA_BLOCK - 1 - dlt) if reverse else (row >= dlt)
        k_d = pltpu.roll(k, shift, axis=0)
        v_d = pltpu.roll(v, shift, axis=0)
        cum_d = pltpu.roll(cum, shift, axis=0)
        decay = jnp.exp2(jnp.where(valid, cum - cum_d, -jnp.inf))
        return acc + jnp.sum(qb * k_d * decay, axis=1, keepdims=True) * v_d

    return lax.fori_loop(1, GLA_BLOCK, offset, acc0)


def _gla_prefix_sums(lf, mats):
    lf = lf * LOG2_E
    lf_hi = lf.astype(BF16)
    lf_lo = (lf - lf_hi.astype(F32)).astype(BF16)
    two_term = lambda m: (jnp.dot(m, lf_hi, preferred_element_type=F32)
                          + jnp.dot(m, lf_lo, preferred_element_type=F32))
    return two_term(mats["tri"]), two_term(mats["tri_blk"])


def _gla_chunk(q, k, lf, v_bf, cum_ref, w_blk, mats, *, reverse, robust, want_o):
    c = GLA_CHUNK
    dn_t = (((1,), (1,)), ((), ()))
    cum = cum_ref[...]
    end_row = 0 if reverse else c - 1
    cum_end = cum_ref[end_row:end_row + 1, :]

    decayed = lambda x, log2_decay: (x * jnp.exp2(log2_decay)).astype(BF16)
    k_dec = decayed(k, cum_end - cum)
    u = lax.dot_general(v_bf, k_dec, (((0,), (0,)), ((), ())), preferred_element_type=F32)
    dec_end = jnp.exp2(cum_end)
    if not want_o:
        return None, None, None, u, dec_end
    qe = decayed(q, cum)

    if robust:
        blk_tot = jnp.dot(mats["ones_blk"], (lf * LOG2_E).astype(BF16), preferred_element_type=F32)
        bad = blk_tot < -GLA_SAFE_RANGE
        a_d = jnp.where(bad, 0.0, q * jnp.exp2(w_blk)).astype(BF16)
        b_d = decayed(k, jnp.minimum(-w_blk, GLA_SAFE_RANGE + 20.0))
    else:
        a_d = decayed(q, w_blk)
        b_d = decayed(k, -w_blk)
    scores = jnp.where(mats["diag_mask"], lax.dot_general(a_d, b_d, dn_t, preferred_element_type=F32), 0.0)

    zeros = {m: jnp.zeros((m, HEAD_DIM), BF16) for m in GLA_LEVELS}
    for m in GLA_LEVELS:
        a_parts, b_parts = [], []
        for blk in range(c // (2 * m)):
            lo, mid, hi = blk * 2 * m, blk * 2 * m + m, (blk + 1) * 2 * m
            if reverse:
                ref_row = cum_ref[mid:mid + 1, :]
                qa = decayed(q[lo:mid], cum[lo:mid] - ref_row)
                kb = decayed(k[mid:hi], ref_row - cum[mid:hi])
                a_parts += [qa, zeros[m]]
                b_parts += [zeros[m], kb]
            else:
                ref_row = cum_ref[mid - 1:mid, :]
                qa = decayed(q[mid:hi], cum[mid:hi] - ref_row)
                kb = decayed(k[lo:mid], ref_row - cum[lo:mid])
                a_parts += [zeros[m], qa]
                b_parts += [kb, zeros[m]]
        a_m = jnp.concatenate(a_parts, axis=0)
        b_m = jnp.concatenate(b_parts, axis=0)
        s_m = lax.dot_general(a_m, b_m, dn_t, preferred_element_type=F32)
        if 2 * m < c:
            s_m = jnp.where(mats["same"] < 2 * m, s_m, 0.0)
        scores = scores + s_m
    if robust:
        direct = _gla_direct_block_terms(q, k, v_bf.astype(F32), cum, bad, reverse=reverse)
    else:
        direct = jnp.zeros((c, HEAD_DIM), F32)
    return direct, scores.astype(BF16), qe, u, dec_end


def _gla_kernel(kfc_ref, kbc_ref, lffc_ref, lfbc_ref, vc_ref, q_ref, kf_ref, kb_ref, lff_ref, lfb_ref, v_ref,
                gate_ref, nw_ref, y_ref, of_ref, ob_ref, cumf_ref, cumb_ref, wf_ref, wb_ref, qef_ref, qeb_ref,
                scf_ref, scb_ref, uf_ref, ub_ref, df_ref, db_ref):
    c = GLA_CHUNK
    dn_t = (((1,), (1,)), ((), ()))
    chunk_rows = lambda ci: pl.ds(pl.multiple_of(ci * c, c), c)
    mats_f, mats_b = _gla_matrices(False), _gla_matrices(True)

    def scan(q_ref, kf_ref, kb_ref, lff_ref, lfb_ref, v_ref, state, want_o):
        n_chunks = kf_ref.shape[0] // c
        unroll = min(GLA_UNROLL, n_chunks)

        def prefix_body(ci, lowest):
            rows = chunk_rows(ci)
            cum_f, w_f = _gla_prefix_sums(lff_ref[rows, :], mats_f)
            cum_b, w_b = _gla_prefix_sums(lfb_ref[rows, :], mats_b)
            cumf_ref[rows, :], wf_ref[rows, :] = cum_f, w_f
            cumb_ref[rows, :], wb_ref[rows, :] = cum_b, w_b
            return jnp.minimum(lowest, jnp.minimum(w_f, w_b))

        worst = jnp.min(lax.fori_loop(0, n_chunks, prefix_body, jnp.zeros((c, HEAD_DIM), F32), unroll=unroll))

        def local_pass(robust):
            def body(ci, _):
                rows = chunk_rows(ci)
                q = q_ref[rows, :].astype(F32) if want_o else None
                v_bf = v_ref[rows, :]
                x_f, sc_f, qe_f, u_f, d_f = _gla_chunk(
                    q, kf_ref[rows, :].astype(F32), lff_ref[rows, :], v_bf, cumf_ref.at[rows, :],
                    wf_ref[rows, :], mats_f, reverse=False, robust=robust, want_o=want_o)
                x_b, sc_b, qe_b, u_b, d_b = _gla_chunk(
                    q, kb_ref[rows, :].astype(F32), lfb_ref[rows, :], v_bf, cumb_ref.at[rows, :],
                    wb_ref[rows, :], mats_b, reverse=True, robust=robust, want_o=want_o)
                uf_ref[ci] = u_f
                ub_ref[ci] = u_b
                df_ref[ci] = d_f
                db_ref[ci] = d_b
                if want_o:
                    of_ref[rows, :] = x_f
                    ob_ref[rows, :] = x_b
                    scf_ref[rows, :] = sc_f
                    scb_ref[rows, :] = sc_b
                    qef_ref[rows, :] = qe_f
                    qeb_ref[rows, :] = qe_b
                return 0

            lax.fori_loop(0, n_chunks, body, 0, unroll=1 if robust else unroll)

        if want_o:
            lax.cond(worst < -GLA_SAFE_RANGE, lambda: local_pass(True), lambda: local_pass(False))
        else:
            local_pass(False)

        def state_pass(ci, carry):
            st_f, st_b = carry
            cb = n_chunks - 1 - ci
            if want_o:
                rows_f, rows_b = chunk_rows(ci), chunk_rows(cb)
                of_ref[rows_f, :] += (
                    jnp.dot(scf_ref[rows_f, :], v_ref[rows_f, :], preferred_element_type=F32)
                    + lax.dot_general(qef_ref[rows_f, :], st_f.astype(BF16), dn_t, preferred_element_type=F32))
                ob_ref[rows_b, :] += (
                    jnp.dot(scb_ref[rows_b, :], v_ref[rows_b, :], preferred_element_type=F32)
                    + lax.dot_general(qeb_ref[rows_b, :], st_b.astype(BF16), dn_t, preferred_element_type=F32))
            return st_f * df_ref[ci] + uf_ref[ci], st_b * db_ref[cb] + ub_ref[cb]

        return lax.fori_loop(0, n_chunks, state_pass, state, unroll=unroll)

    zero = jnp.zeros((HEAD_DIM, HEAD_DIM), F32)
    state = scan(None, kfc_ref, kbc_ref, lffc_ref, lfbc_ref, vc_ref, (zero, zero), False)
    scan(q_ref, kf_ref, kb_ref, lff_ref, lfb_ref, v_ref, state, True)

    n_chunks = q_ref.shape[0] // c

    def readout(ci, _):
        rows = chunk_rows(ci)
        tot = of_ref[rows, :] + ob_ref[rows, :]
        ms = jnp.mean(tot * tot, axis=1, keepdims=True)
        y = tot * lax.rsqrt(ms + RMS_EPS) * nw_ref[...] * gate_ref[rows, :].astype(F32)
        y_ref[rows, :] = y.astype(BF16)
        return 0
    lax.fori_loop(0, n_chunks, readout, 0, unroll=min(GLA_UNROLL, n_chunks))


def _gla(act, k, logf, act_ctx, k_ctx, logf_ctx, norm_w, *, n_heads):
    b, n, _ = act.shape
    n_ctx = act_ctx.shape[1]
    assert n_ctx <= n
    hd = HEAD_DIM
    col = lambda rows, off: pl.BlockSpec((None, rows, hd), lambda bi, h: (bi, 0, off * n_heads + h))
    return pl.pallas_call(
        _gla_kernel,
        grid=(b, n_heads),
        in_specs=[col(n_ctx, 0), col(n_ctx, 1), col(n_ctx, 0), col(n_ctx, 1), col(n_ctx, 1),
                  col(n, 0), col(n, 0), col(n, 1), col(n, 0), col(n, 1), col(n, 1), col(n, 2),
                  pl.BlockSpec((1, hd), lambda bi, h: (0, 0))],
        out_specs=pl.BlockSpec((None, n, hd), lambda bi, h: (bi, 0, h)),
        out_shape=jax.ShapeDtypeStruct((b, n, n_heads * hd), BF16),
        scratch_shapes=(
            [pltpu.VMEM((n, hd), F32)] * 6
            + [pltpu.VMEM((n, hd), BF16)] * 4
            + [pltpu.VMEM((n // GLA_CHUNK, hd, hd), F32)] * 2
            + [pltpu.VMEM((n // GLA_CHUNK, 1, hd), F32)] * 2),
        compiler_params=_params("arbitrary", "arbitrary"),
        name="gla_scan",
    )(k_ctx, k_ctx, logf_ctx, logf_ctx, act_ctx, act, k, k, logf, logf, act, act, norm_w.reshape(1, hd))


def _rope_tables(n):
    half = ROPE_AXIS_DIM // 2
    inv = (1.0 / (ROPE_BASE ** (np.arange(0, ROPE_AXIS_DIM, 2, dtype=np.float32) / ROPE_AXIS_DIM))).astype(np.float32)
    pos = np.arange(n)
    ang_r = (pos // GRID_W).astype(np.float32)[:, None] * inv[None, :]
    ang_c = (pos % GRID_W).astype(np.float32)[:, None] * inv[None, :]
    ang = np.concatenate([ang_r, ang_r, ang_c, ang_c] * 2, axis=1)
    sign = np.tile(np.concatenate([-np.ones(half, np.float32), np.ones(half, np.float32)]), HEAD_DIM // ROPE_AXIS_DIM)
    return jnp.asarray(np.cos(ang), F32), jnp.asarray(np.sin(ang) * sign[None, :], F32)


def kernel(x, c, ctx, c_ctx, mod_w, mod_b, ln_mix_g, ln_mix_b, ln_ffn_g, ln_ffn_b, even_w_in, even_w_out, diff_lambda, diff_subln, hgrn_w_in, hgrn_w_out, hgrn_lower_bounds, hgrn_norm, ffn_w_up, ffn_conv_w, ffn_conv_b, ffn_w_down):
    b, n_lat, d = x.shape
    n_ctx = ctx.shape[1]
    assert mod_w.shape[0] == DEPTH and n_lat % GRID_W == 0
    d_ff = ffn_w_down.shape[1]
    d_ff_pad = -(-d_ff // (2 * MXU_DIM)) * (2 * MXU_DIM)
    diff_width = even_w_out.shape[2] * 3 // 4
    four_width = even_w_in.shape[2] - 3 * diff_width
    n_diff_heads = diff_width // HEAD_DIM
    n_hgrn_heads = d // HEAD_DIM

    cc = jnp.concatenate([c, c_ctx[None, :], jnp.zeros((2 * SUBLANES - b - 1, d), F32)], axis=0)
    mod = _modulation(cc, mod_w, mod_b)
    mod = mod.reshape(DEPTH, cc.shape[0], 1, 6 * d)

    h_lat = x.reshape(b * n_lat, d)
    h_ctx = ctx.reshape(b * n_ctx, d)

    tm_wide_lat, tm_lat = min(1024, n_lat), min(512, n_lat)
    ctx_tile = lambda cap: n_ctx * max(1, min(cap // n_ctx, b)) if cap >= n_ctx else cap
    tm_wide_ctx, tm_ctx = ctx_tile(1024), ctx_tile(512)
    assert (b * n_ctx) % tm_wide_ctx == 0 and (b * n_ctx) % tm_ctx == 0
    lat_row = lambda tm: (lambda i: (i * tm) // n_lat)
    ctx_row = lambda tm: (lambda i: b)
    down_k_tiles = 4 if d_ff_pad % (4 * LANES) == 0 else d_ff_pad // 512

    def ffn(h, mod3, row_fn, layer, seq_len, tm_up, tm_down):
        pad = d_ff_pad - d_ff
        w_up = ffn_w_up[layer]
        wa = jnp.pad(w_up[:, :d_ff].astype(BF16), ((0, 0), (0, pad)))
        wv = jnp.pad(w_up[:, d_ff:].astype(BF16), ((0, 0), (0, pad)))
        cw = jnp.pad(ffn_conv_w[layer], ((0, 0), (0, pad)))
        cb = jnp.pad(ffn_conv_b[layer], (0, pad)).reshape(1, d_ff_pad)
        wd = jnp.pad(ffn_w_down[layer].astype(BF16), ((0, pad), (0, 0)))
        g = _ffn_up(h, mod3, row_fn(tm_up), wa, wv, cw, cb, seq_len=seq_len, tm=tm_up, tn=512)
        return _proj_residual([g], [(wd, 0)], h, mod3, 5, row_fn(tm_down), ln_ffn_g[layer], ln_ffn_b[layer],
                              tm=tm_down, n_k=down_k_tiles)

    layer = 0
    lam_init = 0.8 - 0.6 * math.exp(-0.3 * layer)
    w_in = even_w_in[0].astype(BF16)
    w_out = even_w_out[0].astype(BF16)
    cos_t, sin_t = _rope_tables(n_lat)
    mod3 = mod[layer]

    col_scale = jnp.asarray(np.where(np.arange(w_in.shape[1]) < diff_width, QK_SCALE, 1.0)[None, :], F32)
    p_lat = _even_inproj(h_lat, mod3, lat_row(tm_wide_lat), w_in, col_scale, cos_t, sin_t, qk_width=diff_width,
                         use_rope=True, tm=tm_wide_lat, tn=1024).reshape(b, n_lat, -1)
    p_ctx = _even_inproj(h_ctx, mod3, ctx_row(tm_wide_ctx), w_in, col_scale, cos_t[:SUBLANES], sin_t[:SUBLANES],
                         qk_width=diff_width, use_rope=False, tm=tm_wide_ctx, tn=1024).reshape(b, n_ctx, -1)

    attn_lat, attn_ctx = _diff_attention(p_lat, p_ctx, diff_lambda[0], diff_subln[0], lam_init,
                                         n_heads=n_diff_heads, tq=min(512, n_lat))
    four_block = 3 * diff_width // four_width
    four_lat = _fourier_mix(p_lat, four_block, four_width, tm=tm_lat)
    four_ctx = _fourier_mix(p_ctx, four_block, four_width, tm=n_ctx)

    w_list = [(w_out, 0), (w_out, diff_width)]
    h_lat = _proj_residual([attn_lat.reshape(b * n_lat, -1), four_lat.reshape(b * n_lat, -1)], w_list,
                           h_lat, mod3, 2, lat_row(tm_lat), ln_mix_g[layer], ln_mix_b[layer], tm=tm_lat)
    h_ctx = _proj_residual([attn_ctx.reshape(b * n_ctx, -1), four_ctx.reshape(b * n_ctx, -1)], w_list,
                           h_ctx, mod3, 2, ctx_row(tm_ctx), ln_mix_g[layer], ln_mix_b[layer], tm=tm_ctx)
    h_lat = ffn(h_lat, mod3, lat_row, layer, n_lat, tm_wide_lat, tm_wide_lat)
    h_ctx = ffn(h_ctx, mod3, ctx_row, layer, n_ctx, tm_wide_ctx, tm_wide_ctx)

    layer = 1
    mod3 = mod[layer]
    hw = hgrn_w_in[0].astype(BF16)
    wrap = lambda t, n: t.reshape(b, n, -1)
    act_c, k_c, lf_c = _hgrn_inproj(h_ctx, mod3, ctx_row(tm_wide_ctx), hw, hgrn_lower_bounds,
                                    layer=layer, tm=tm_wide_ctx, tn=1024)
    act_l, k_l, lf_l = _hgrn_inproj(h_lat, mod3, lat_row(tm_wide_lat), hw, hgrn_lower_bounds,
                                    layer=layer, tm=tm_wide_lat, tn=1024)
    y_lat = _gla(wrap(act_l, n_lat), wrap(k_l, n_lat), wrap(lf_l, n_lat),
                 wrap(act_c, n_ctx), wrap(k_c, n_ctx), wrap(lf_c, n_ctx), hgrn_norm[0], n_heads=n_hgrn_heads)
    h_lat = _proj_residual([y_lat.reshape(b * n_lat, d)], [(hgrn_w_out[0].astype(BF16), 0)], h_lat, mod3, 2,
                           lat_row(tm_lat), ln_mix_g[layer], ln_mix_b[layer], tm=tm_lat)
    h_lat = ffn(h_lat, mod3, lat_row, layer, n_lat, tm_wide_lat, tm_wide_lat)
    return h_lat.reshape(b, n_lat, d)
```

```python
import functools
import math

import numpy as np
import jax
import jax.numpy as jnp
from jax import lax
from jax.experimental import pallas as pl
from jax.experimental.pallas import tpu as pltpu

F32 = jnp.float32
BF16 = jnp.bfloat16

LANES = 128
SUBLANES = 8
MXU_DIM = 256
VMEM_LIMIT_BYTES = 56 * 1024 * 1024

GRID_W = 64
DIFF_QK_DIM = 64
HEAD_DIM = 128
ROPE_AXIS_DIM = DIFF_QK_DIM // 2
ROPE_BASE = 10000.0
CONV_W = 3
LN_EPS = 1e-6
RMS_EPS = 1e-5
DEPTH = 2
ALPHA = (2.0 * DEPTH) ** 0.25
QK_SCALE = DIFF_QK_DIM ** -0.5 * math.log2(math.e)
ATTN_SAFE_LOG2 = 60.0

GLA_CHUNK = 128
GLA_LEVELS = (64, 32, 16)
GLA_BLOCK = 16
LOG2_E = math.log2(math.e)
GLA_SAFE_RANGE = 80.0
GLA_UNROLL = 16


def _params(*dims):
    return pltpu.CompilerParams(dimension_semantics=dims, vmem_limit_bytes=VMEM_LIMIT_BYTES)


def _layer_norm_rows(x):
    mu = jnp.mean(x, axis=-1, keepdims=True)
    xc = x - mu
    var = jnp.mean(xc * xc, axis=-1, keepdims=True)
    return xc * lax.rsqrt(var + LN_EPS)


def _silu(x):
    return x * jax.nn.sigmoid(x)


def _mod_kernel(c_ref, w_ref, b_ref, o_ref):
    x = _silu(c_ref[...]).astype(BF16)
    o_ref[...] = jnp.dot(x, w_ref[...].astype(BF16), preferred_element_type=F32) + b_ref[...]


def _modulation(cc, mod_w, mod_b, tn=1024):
    depth, d, n = mod_w.shape
    rows = cc.shape[0]
    return pl.pallas_call(
        _mod_kernel,
        grid=(depth, n // tn),
        in_specs=[
            pl.BlockSpec((rows, d), lambda l, j: (0, 0)),
            pl.BlockSpec((None, d, tn), lambda l, j: (l, 0, j)),
            pl.BlockSpec((None, 1, tn), lambda l, j: (l, 0, j)),
        ],
        out_specs=pl.BlockSpec((None, rows, tn), lambda l, j: (l, 0, j)),
        out_shape=jax.ShapeDtypeStruct((depth, rows, n), F32),
        compiler_params=_params("arbitrary", "arbitrary"),
        name="modulation",
    )(cc, mod_w, mod_b.reshape(depth, 1, n))


def _mod_spec(chunk, d, row_of_tile):
    return pl.BlockSpec((None, 1, d), lambda i, *_: (row_of_tile(i), 0, chunk))


def _even_inproj_kernel(h_ref, sh_ref, sc_ref, w_ref, cscale_ref, cos_ref, sin_ref, o_ref, u_ref,
                        *, n_rope_tiles, use_rope):
    j = pl.program_id(1)

    @pl.when(j == 0)
    def _():
        u_ref[...] = (_layer_norm_rows(h_ref[...]) * (1.0 + sc_ref[...]) + sh_ref[...]).astype(BF16)

    def tile(rope):
        half = ROPE_AXIS_DIM // 2
        if rope:
            lane = lax.broadcasted_iota(jnp.int32, (u_ref.shape[0], HEAD_DIM), 1)
            second_half = (lane % ROPE_AXIS_DIM) >= half
        for c0 in range(0, o_ref.shape[1], MXU_DIM):
            cols = slice(c0, c0 + MXU_DIM)
            y = jnp.dot(u_ref[...], w_ref[:, cols], preferred_element_type=F32)
            if rope:
                parts = []
                for hh in range(MXU_DIM // HEAD_DIM):
                    yh = y[:, hh * HEAD_DIM:(hh + 1) * HEAD_DIM]
                    partner = jnp.where(second_half, pltpu.roll(yh, half, axis=1),
                                        pltpu.roll(yh, HEAD_DIM - half, axis=1))
                    parts.append(yh * cos_ref[...] + partner * sin_ref[...])
                y = jnp.concatenate(parts, axis=1)
            o_ref[:, cols] = (y * cscale_ref[:, cols]).astype(BF16)

    if use_rope:
        pl.when(j < n_rope_tiles)(lambda: tile(True))
        pl.when(j >= n_rope_tiles)(lambda: tile(False))
    else:
        tile(False)


def _even_inproj(h, mod3, row_of_tile, w, col_scale, cos_t, sin_t, *, qk_width, use_rope, tm, tn):
    m, d = h.shape
    n = w.shape[1]
    n_pos_tiles = cos_t.shape[0] // tm if use_rope else 1
    tab_rows = tm if use_rope else cos_t.shape[0]
    kern = functools.partial(_even_inproj_kernel, n_rope_tiles=2 * qk_width // tn, use_rope=use_rope)
    return pl.pallas_call(
        kern,
        grid=(m // tm, n // tn),
        in_specs=[
            pl.BlockSpec((tm, d), lambda i, j: (i, 0)),
            _mod_spec(0, d, row_of_tile),
            _mod_spec(1, d, row_of_tile),
            pl.BlockSpec((d, tn), lambda i, j: (0, j)),
            pl.BlockSpec((1, tn), lambda i, j: (0, j)),
            pl.BlockSpec((tab_rows, HEAD_DIM), lambda i, j: (i % n_pos_tiles, 0)),
            pl.BlockSpec((tab_rows, HEAD_DIM), lambda i, j: (i % n_pos_tiles, 0)),
        ],
        out_specs=pl.BlockSpec((tm, tn), lambda i, j: (i, j)),
        out_shape=jax.ShapeDtypeStruct((m, n), BF16),
        scratch_shapes=[pltpu.VMEM((tm, d), BF16)],
        compiler_params=_params("arbitrary", "arbitrary"),
        name="even_inproj",
    )(h, mod3, mod3, w, col_scale, cos_t, sin_t)


def _diff_attn_kernel(q_ref, qc_ref, kc_ref, kl_ref, vc_ref, vl_ref, lam_ref, subln_ref, o_ref, oc_ref,
                      *, lam_init, tq):
    lv = lam_ref[...]
    lam = (jnp.exp(jnp.sum(lv[0:1] * lv[1:2], axis=1, keepdims=True))
           - jnp.exp(jnp.sum(lv[2:3] * lv[3:4], axis=1, keepdims=True)) + lam_init)

    ones = jnp.ones((HEAD_DIM, HEAD_DIM), BF16)

    def max_sq_norm(ref):
        x = ref[...]
        row_sums = jnp.dot(x * x, ones, preferred_element_type=F32)
        return jnp.max(row_sums)

    knorm = jnp.maximum(max_sq_norm(kc_ref), max_sq_norm(kl_ref))
    qnorm = jnp.maximum(max_sq_norm(qc_ref), max_sq_norm(q_ref))
    small_scores = qnorm * knorm <= 0.98 * ATTN_SAFE_LOG2 ** 2

    dn = (((1,), (1,)), ((), ()))

    def attend(q, kv_refs, shift_by_max):
        lane = lax.broadcasted_iota(jnp.int32, q.shape, 1)
        first_map = lane < DIFF_QK_DIM
        zero = jnp.zeros_like(q)
        maps = []
        for qm in (jnp.where(first_map, q, zero), jnp.where(first_map, zero, q)):
            s = [lax.dot_general(qm, k_ref[...], dn, preferred_element_type=F32) for k_ref, _ in kv_refs]
            if shift_by_max:
                mx = functools.reduce(jnp.maximum, [jnp.max(x, axis=1, keepdims=True) for x in s])
                s = [x - mx for x in s]
            num = den = None
            for x, (_, v_ref) in zip(s, kv_refs):
                e = jnp.exp2(x)
                d = jnp.sum(e, axis=1, keepdims=True)
                part = jnp.dot(e.astype(BF16), v_ref[...], preferred_element_type=F32)
                den = d if den is None else den + d
                num = part if num is None else num + part
            maps.append((num, den))
        (n1, d1), (n2, d2) = maps
        acc = n1 * (1.0 / d1) - n2 * (lam / d2)
        ms = jnp.mean(acc * acc, axis=1, keepdims=True)
        y = acc * lax.rsqrt(ms + RMS_EPS) * subln_ref[...] * (1.0 - lam_init)
        return y.astype(BF16)

    def all_queries(shift_by_max):
        oc_ref[...] = attend(qc_ref[...], [(kc_ref, vc_ref)], shift_by_max)

        def body(t, carry):
            rows = pl.ds(pl.multiple_of(t * tq, tq), tq)
            o_ref[rows, :] = attend(q_ref[rows, :], [(kc_ref, vc_ref), (kl_ref, vl_ref)], shift_by_max)
            return carry
        n_tiles = q_ref.shape[0] // tq
        lax.fori_loop(0, n_tiles, body, 0, unroll=1 if shift_by_max else min(4, n_tiles))

    lax.cond(small_scores, lambda: all_queries(False), lambda: all_queries(True))


def _diff_attention(p_lat, p_ctx, lam_vecs, subln, lam_init, *, n_heads, tq):
    b, nq, _ = p_lat.shape
    n_ctx = p_ctx.shape[1]
    head = lambda rows, sec: pl.BlockSpec((None, rows, HEAD_DIM), lambda bi, h: (bi, 0, sec * n_heads + h))
    out = lambda rows: pl.BlockSpec((None, rows, HEAD_DIM), lambda bi, h: (bi, 0, h))
    return pl.pallas_call(
        functools.partial(_diff_attn_kernel, lam_init=lam_init, tq=tq),
        grid=(b, n_heads),
        in_specs=[head(nq, 0), head(n_ctx, 0), head(n_ctx, 1), head(nq, 1), head(n_ctx, 2), head(nq, 2),
                  pl.BlockSpec(lam_vecs.shape, lambda bi, h: (0, 0)),
                  pl.BlockSpec((1, HEAD_DIM), lambda bi, h: (0, 0))],
        out_specs=[out(nq), out(n_ctx)],
        out_shape=[jax.ShapeDtypeStruct((b, nq, n_heads * HEAD_DIM), BF16),
                   jax.ShapeDtypeStruct((b, n_ctx, n_heads * HEAD_DIM), BF16)],
        compiler_params=_params("arbitrary", "arbitrary"),
        name="diff_attention",
    )(p_lat, p_ctx, p_ctx, p_lat, p_ctx, p_lat, lam_vecs, subln.reshape(1, HEAD_DIM))


def _fourier_kernel(x_ref, dn_ref, cs_ref, o_ref, z_ref):
    n = x_ref.shape[0]

    @pl.when(pl.program_id(1) == 0)
    def _():
        for g in range(x_ref.shape[1] // HEAD_DIM):
            cols = slice(g * HEAD_DIM, (g + 1) * HEAD_DIM)
            zc = jnp.dot(x_ref[:, cols], cs_ref[...], preferred_element_type=F32)
            z_ref[0:n, cols] = zc[:, :HEAD_DIM].astype(BF16)
            z_ref[n:2 * n, cols] = zc[:, HEAD_DIM:].astype(BF16)

    o_ref[...] = jnp.dot(dn_ref[...], z_ref[...], preferred_element_type=F32).astype(BF16)


def _dft_tables(n):
    j = np.arange(n, dtype=np.int64)
    ang = 2.0 * np.pi * ((j[:, None] * j[None, :]) % n).astype(np.float64) / n
    return np.cos(ang) / math.sqrt(n), np.sin(ang) / math.sqrt(n)


def _fourier_mix(src, col_block, width, *, tm):
    b, n, _ = src.shape
    cn, sn = _dft_tables(n)
    cc, sc = _dft_tables(HEAD_DIM)
    dn = jnp.asarray(np.concatenate([cn, -sn], axis=1), dtype=BF16)
    cs = jnp.asarray(np.concatenate([cc, sc], axis=1), dtype=BF16)
    return pl.pallas_call(
        _fourier_kernel,
        grid=(b, n // tm),
        in_specs=[
            pl.BlockSpec((None, n, width), lambda bi, t: (bi, 0, col_block)),
            pl.BlockSpec((tm, 2 * n), lambda bi, t: (t, 0)),
            pl.BlockSpec((HEAD_DIM, 2 * HEAD_DIM), lambda bi, t: (0, 0)),
        ],
        out_specs=pl.BlockSpec((None, tm, width), lambda bi, t: (bi, t, 0)),
        out_shape=jax.ShapeDtypeStruct((b, n, width), BF16),
        scratch_shapes=[pltpu.VMEM((2 * n, width), BF16)],
        compiler_params=_params("arbitrary", "arbitrary"),
        name="fourier_mix",
    )(src, dn, cs)


def _proj_residual_kernel(*refs, n_in, n_k):
    a_refs = refs[:n_in]
    w_refs = refs[n_in:2 * n_in]
    h_ref, gate_ref, g_ref, b_ref, o_ref = refs[2 * n_in:2 * n_in + 5]
    k = pl.program_id(1)
    tm, d = o_ref.shape
    col_block, row_block = 2 * MXU_DIM, MXU_DIM

    def accumulate(first, rows=slice(None)):
        for c0 in range(0, d, col_block):
            cols = slice(c0, c0 + col_block)
            part = None
            for a_ref, w_ref in zip(a_refs, w_refs):
                p = jnp.dot(a_ref[rows, :], w_ref[:, cols], preferred_element_type=F32)
                part = p if part is None else part + p
            if first:
                o_ref[rows, cols] = part
            else:
                o_ref[rows, cols] += part

    def finish(r_lo=0, r_hi=tm):
        for r0 in range(r_lo, r_hi, row_block):
            rows = slice(r0, r0 + row_block)
            x = ALPHA * h_ref[rows, :] + gate_ref[...] * o_ref[rows, :]
            o_ref[rows, :] = _layer_norm_rows(x) * g_ref[...] + b_ref[...]

    def last_step(first):
        halves = 2 if tm >= 4 * row_block else 1
        for r0 in range(0, tm, tm // halves):
            accumulate(first, slice(r0, r0 + tm // halves))
            finish(r0, r0 + tm // halves)

    if n_k == 1:
        last_step(True)
    else:
        pl.when(k == 0)(lambda: accumulate(True))
        pl.when(jnp.logical_and(k > 0, k < n_k - 1))(lambda: accumulate(False))
        pl.when(k == n_k - 1)(lambda: last_step(False))


def _proj_residual(a_list, w_list, h, mod3, gate_chunk, row_of_tile, ln_g, ln_b, *, tm, n_k=1):
    m, d = h.shape
    in_specs = []
    for a in a_list:
        in_specs.append(pl.BlockSpec((tm, a.shape[1] // n_k), lambda i, k: (i, k)))
    for a, (w, first_row) in zip(a_list, w_list):
        rows = a.shape[1] // n_k
        in_specs.append(pl.BlockSpec((rows, d), lambda i, k, blk=first_row // rows: (blk + k, 0)))
    in_specs += [
        pl.BlockSpec((tm, d), lambda i, k: (i, 0)),
        _mod_spec(gate_chunk, d, row_of_tile),
        pl.BlockSpec((1, d), lambda i, k: (0, 0)),
        pl.BlockSpec((1, d), lambda i, k: (0, 0)),
    ]
    return pl.pallas_call(
        functools.partial(_proj_residual_kernel, n_in=len(a_list), n_k=n_k),
        grid=(m // tm, n_k),
        in_specs=in_specs,
        out_specs=pl.BlockSpec((tm, d), lambda i, k: (i, 0)),
        out_shape=jax.ShapeDtypeStruct((m, d), F32),
        compiler_params=_params("arbitrary", "arbitrary"),
        name="proj_residual",
    )(*a_list, *[w for w, _ in w_list], h, mod3, ln_g.reshape(1, d), ln_b.reshape(1, d))


def _ffn_up_kernel(h_ref, hp_ref, hn_ref, sh_ref, sc_ref, wa_ref, wv_ref, cw_ref, cb_ref, o_ref,
                   u_ref, uh_ref, *, seq_len):
    i = pl.program_id(0)
    j = pl.program_id(1)
    tm = h_ref.shape[0]
    whole_sequences = tm >= seq_len
    tiles_per_seq = max(seq_len // tm, 1)

    @pl.when(j == 0)
    def _():
        scale = 1.0 + sc_ref[...]
        u_ref[...] = (_layer_norm_rows(h_ref[...]) * scale + sh_ref[...]).astype(BF16)
        if not whole_sequences:
            uh_ref[0:SUBLANES, :] = (_layer_norm_rows(hp_ref[...]) * scale + sh_ref[...]).astype(BF16)
            uh_ref[SUBLANES:, :] = (_layer_norm_rows(hn_ref[...]) * scale + sh_ref[...]).astype(BF16)

    t_in_seq = i % tiles_per_seq
    row = lax.broadcasted_iota(jnp.int32, (tm, MXU_DIM), 0)
    row_in_seq = row % seq_len
    col_blocks = [slice(c0, c0 + MXU_DIM) for c0 in range(0, o_ref.shape[1], MXU_DIM)]
    gates = []
    for cols in col_blocks:
        a = jnp.dot(u_ref[...], wa_ref[:, cols], preferred_element_type=F32)
        if whole_sequences:
            a_prev = jnp.where(row_in_seq == 0, 0.0, pltpu.roll(a, 1, axis=0))
            a_next = jnp.where(row_in_seq == seq_len - 1, 0.0, pltpu.roll(a, tm - 1, axis=0))
        else:
            halo = jnp.dot(uh_ref[...], wa_ref[:, cols], preferred_element_type=F32)
            prev_row = jnp.where(t_in_seq > 0, halo[SUBLANES - 1:SUBLANES], 0.0)
            next_row = jnp.where(t_in_seq < tiles_per_seq - 1, halo[SUBLANES:SUBLANES + 1], 0.0)
            a_prev = jnp.where(row == 0, prev_row, pltpu.roll(a, 1, axis=0))
            a_next = jnp.where(row == tm - 1, next_row, pltpu.roll(a, tm - 1, axis=0))
        cw = cw_ref[:, cols]
        conv = cb_ref[:, cols] + a_prev * cw[0:1] + a * cw[1:2] + a_next * cw[2:3]
        gates.append(0.5 * conv * (1.0 + lax.erf(conv * math.sqrt(0.5))))
    for cols, gelu in zip(col_blocks, gates):
        v = jnp.dot(u_ref[...], wv_ref[:, cols], preferred_element_type=F32)
        o_ref[:, cols] = (gelu * v).astype(BF16)


def _ffn_up(h, mod3, row_of_tile, wa, wv, conv_w, conv_b, *, seq_len, tm, tn):
    m, d = h.shape
    n = wa.shape[1]
    assert tm % seq_len == 0 or seq_len % tm == 0
    blocks_per_tile = tm // SUBLANES
    n_blocks = m // SUBLANES
    kern = functools.partial(_ffn_up_kernel, seq_len=seq_len)
    return pl.pallas_call(
        kern,
        grid=(m // tm, n // tn),
        in_specs=[
            pl.BlockSpec((tm, d), lambda i, j: (i, 0)),
            pl.BlockSpec((SUBLANES, d), lambda i, j: (jnp.maximum(i * blocks_per_tile - 1, 0), 0)),
            pl.BlockSpec((SUBLANES, d), lambda i, j: (jnp.minimum((i + 1) * blocks_per_tile, n_blocks - 1), 0)),
            _mod_spec(3, d, row_of_tile),
            _mod_spec(4, d, row_of_tile),
            pl.BlockSpec((d, tn), lambda i, j: (0, j)),
            pl.BlockSpec((d, tn), lambda i, j: (0, j)),
            pl.BlockSpec((CONV_W, tn), lambda i, j: (0, j)),
            pl.BlockSpec((1, tn), lambda i, j: (0, j)),
        ],
        out_specs=pl.BlockSpec((tm, tn), lambda i, j: (i, j)),
        out_shape=jax.ShapeDtypeStruct((m, n), BF16),
        scratch_shapes=[pltpu.VMEM((tm, d), BF16), pltpu.VMEM((2 * SUBLANES, d), BF16)],
        compiler_params=_params("arbitrary", "arbitrary"),
        name="ffn_up",
    )(h, h, h, mod3, mod3, wa, wv, conv_w, conv_b)


def _hgrn_act_kernel(h_ref, sh_ref, sc_ref, w_ref, o_ref, u_ref, *, lin_lo, lin_hi):
    j = pl.program_id(1)

    @pl.when(j == 0)
    def _():
        u_ref[...] = (_layer_norm_rows(h_ref[...]) * (1.0 + sc_ref[...]) + sh_ref[...]).astype(BF16)

    def tile(activation):
        for c0 in range(0, o_ref.shape[1], MXU_DIM):
            cols = slice(c0, c0 + MXU_DIM)
            acc = jnp.dot(u_ref[...], w_ref[:, cols], preferred_element_type=F32)
            o_ref[:, cols] = activation(acc).astype(BF16)

    linear = jnp.logical_and(j >= lin_lo, j < lin_hi)
    pl.when(linear)(lambda: tile(lambda x: x))
    pl.when(jnp.logical_not(linear))(lambda: tile(_silu))


def _hgrn_gate_kernel(h_ref, sh_ref, sc_ref, w_ref, lbp_ref, k_ref, lf_ref, u_ref, *, layer):
    j = pl.program_id(1)

    @pl.when(j == 0)
    def _():
        u_ref[...] = (_layer_norm_rows(h_ref[...]) * (1.0 + sc_ref[...]) + sh_ref[...]).astype(BF16)

    for c0 in range(0, k_ref.shape[1], MXU_DIM):
        cols = slice(c0, c0 + MXU_DIM)
        x = lbp_ref[:, cols]
        e = jnp.exp(x - jnp.max(x, axis=0, keepdims=True))
        lb = jnp.sum(e[1:layer + 1], axis=0, keepdims=True) / jnp.sum(e, axis=0, keepdims=True)

        f_pre = jnp.dot(u_ref[...], w_ref[:, cols], preferred_element_type=F32)
        gate = jax.nn.sigmoid(f_pre)
        k_ref[:, cols] = ((1.0 - lb) * (1.0 - gate)).astype(BF16)
        lf_ref[:, cols] = jnp.log(lb + (1.0 - lb) * gate)


def _hgrn_inproj(h, mod3, row_of_tile, w, lb_params, *, layer, tm, tn):
    m, d = h.shape
    tiles = d // tn
    common = [
        pl.BlockSpec((tm, d), lambda i, j: (i, 0)),
        _mod_spec(0, d, row_of_tile),
        _mod_spec(1, d, row_of_tile),
    ]
    act_w = pl.BlockSpec((d, tn), lambda i, j: (0, jnp.where(j < tiles, j, j + 2 * tiles)))
    gate_w = pl.BlockSpec((d, tn), lambda i, j: (0, j + tiles))
    act = pl.pallas_call(
        functools.partial(_hgrn_act_kernel, lin_lo=tiles, lin_hi=2 * tiles),
        grid=(m // tm, 3 * tiles),
        in_specs=common + [act_w],
        out_specs=pl.BlockSpec((tm, tn), lambda i, j: (i, j)),
        out_shape=jax.ShapeDtypeStruct((m, 3 * d), BF16),
        scratch_shapes=[pltpu.VMEM((tm, d), BF16)],
        compiler_params=_params("arbitrary", "arbitrary"),
        name="hgrn_inproj_act",
    )(h, mod3, mod3, w)
    k, logf = pl.pallas_call(
        functools.partial(_hgrn_gate_kernel, layer=layer),
        grid=(m // tm, 2 * tiles),
        in_specs=common + [gate_w, pl.BlockSpec((None, DEPTH, tn), lambda i, j: (j // tiles, 0, j % tiles))],
        out_specs=[pl.BlockSpec((tm, tn), lambda i, j: (i, j))] * 2,
        out_shape=[jax.ShapeDtypeStruct((m, 2 * d), BF16), jax.ShapeDtypeStruct((m, 2 * d), F32)],
        scratch_shapes=[pltpu.VMEM((tm, d), BF16)],
        compiler_params=_params("arbitrary", "arbitrary"),
        name="hgrn_inproj_gate",
    )(h, mod3, mod3, w, lb_params)
    return act, k, logf


def _gla_matrices(reverse):
    c = GLA_CHUNK
    r_i = lax.broadcasted_iota(jnp.int32, (c, c), 0)
    c_i = lax.broadcasted_iota(jnp.int32, (c, c), 1)
    same = jnp.bitwise_xor(r_i, c_i)
    earlier = (c_i >= r_i) if reverse else (c_i <= r_i)
    in_block = same < GLA_BLOCK
    one = lambda m: jnp.where(m, 1.0, 0.0).astype(BF16)
    return dict(tri=one(earlier), tri_blk=one(jnp.logical_and(earlier, in_block)), ones_blk=one(in_block),
                diag_mask=jnp.logical_and(earlier, in_block), same=same)


def _gla_direct_block_terms(q, k, v, cum, bad, *, reverse):
    c = GLA_CHUNK
    row = lax.broadcasted_iota(jnp.int32, (c, HEAD_DIM), 0) % GLA_BLOCK
    qb = jnp.where(bad, q, 0.0)
    acc0 = jnp.sum(qb * k, axis=1, keepdims=True) * v

    def offset(dlt, acc):
        shift = (c - dlt) if reverse else dlt
        valid = (row <= GLA_BLOCK - 1 - dlt) if reverse else (row >= dlt)
        k_d = pltpu.roll(k, shift, axis=0)
        v_d = pltpu.roll(v, shift, axis=0)
        cum_d = pltpu.roll(cum, shift, axis=0)
        decay = jnp.exp2(jnp.where(valid, cum - cum_d, -jnp.inf))
        return acc + jnp.sum(qb * k_d * decay, axis=1, keepdims=True) * v_d

    return lax.fori_loop(1, GLA_BLOCK, offset, acc0)


def _gla_prefix_sums(lf, mats):
    lf = lf * LOG2_E
    lf_hi = lf.astype(BF16)
    lf_lo = (lf - lf_hi.astype(F32)).astype(BF16)
    two_term = lambda m: (jnp.dot(m, lf_hi, preferred_element_type=F32)
                          + jnp.dot(m, lf_lo, preferred_element_type=F32))
    return two_term(mats["tri"]), two_term(mats["tri_blk"])


def _gla_chunk(q, k, lf, v_bf, cum_ref, w_blk, mats, *, reverse, robust, want_o):
    c = GLA_CHUNK
    dn_t = (((1,), (1,)), ((), ()))
    cum = cum_ref[...]
    end_row = 0 if reverse else c - 1
    cum_end = cum_ref[end_row:end_row + 1, :]

    decayed = lambda x, log2_decay: (x * jnp.exp2(log2_decay)).astype(BF16)
    k_dec = decayed(k, cum_end - cum)
    u = lax.dot_general(v_bf, k_dec, (((0,), (0,)), ((), ())), preferred_element_type=F32)
    dec_end = jnp.exp2(cum_end)
    if not want_o:
        return None, None, None, u, dec_end
    qe = decayed(q, cum)

    if robust:
        blk_tot = jnp.dot(mats["ones_blk"], (lf * LOG2_E).astype(BF16), preferred_element_type=F32)
        bad = blk_tot < -GLA_SAFE_RANGE
        a_d = jnp.where(bad, 0.0, q * jnp.exp2(w_blk)).astype(BF16)
        b_d = decayed(k, jnp.minimum(-w_blk, GLA_SAFE_RANGE + 20.0))
    else:
        a_d = decayed(q, w_blk)
        b_d = decayed(k, -w_blk)
    scores = jnp.where(mats["diag_mask"], lax.dot_general(a_d, b_d, dn_t, preferred_element_type=F32), 0.0)

    zeros = {m: jnp.zeros((m, HEAD_DIM), BF16) for m in GLA_LEVELS}
    for m in GLA_LEVELS:
        a_parts, b_parts = [], []
        for blk in range(c // (2 * m)):
            lo, mid, hi = blk * 2 * m, blk * 2 * m + m, (blk + 1) * 2 * m
            if reverse:
                ref_row = cum_ref[mid:mid + 1, :]
                qa = decayed(q[lo:mid], cum[lo:mid] - ref_row)
                kb = decayed(k[mid:hi], ref_row - cum[mid:hi])
                a_parts += [qa, zeros[m]]
                b_parts += [zeros[m], kb]
            else:
                ref_row = cum_ref[mid - 1:mid, :]
                qa = decayed(q[mid:hi], cum[mid:hi] - ref_row)
                kb = decayed(k[lo:mid], ref_row - cum[lo:mid])
                a_parts += [zeros[m], qa]
                b_parts += [kb, zeros[m]]
        a_m = jnp.concatenate(a_parts, axis=0)
        b_m = jnp.concatenate(b_parts, axis=0)
        s_m = lax.dot_general(a_m, b_m, dn_t, preferred_element_type=F32)
        if 2 * m < c:
            s_m = jnp.where(mats["same"] < 2 * m, s_m, 0.0)
        scores = scores + s_m
    if robust:
        direct = _gla_direct_block_terms(q, k, v_bf.astype(F32), cum, bad, reverse=reverse)
    else:
        direct = jnp.zeros((c, HEAD_DIM), F32)
    return direct, scores.astype(BF16), qe, u, dec_end


def _gla_kernel(kfc_ref, kbc_ref, lffc_ref, lfbc_ref, vc_ref, q_ref, kf_ref, kb_ref, lff_ref, lfb_ref, v_ref,
                gate_ref, nw_ref, y_ref, of_ref, ob_ref, cumf_ref, cumb_ref, wf_ref, wb_ref, qef_ref, qeb_ref,
                scf_ref, scb_ref, uf_ref, ub_ref, df_ref, db_ref):
    c = GLA_CHUNK
    dn_t = (((1,), (1,)), ((), ()))
    chunk_rows = lambda ci: pl.ds(pl.multiple_of(ci * c, c), c)
    mats_f, mats_b = _gla_matrices(False), _gla_matrices(True)

    def scan(q_ref, kf_ref, kb_ref, lff_ref, lfb_ref, v_ref, state, want_o):
        n_chunks = kf_ref.shape[0] // c
        unroll = min(GLA_UNROLL, n_chunks)

        def prefix_body(ci, lowest):
            rows = chunk_rows(ci)
            cum_f, w_f = _gla_prefix_sums(lff_ref[rows, :], mats_f)
            cum_b, w_b = _gla_prefix_sums(lfb_ref[rows, :], mats_b)
            cumf_ref[rows, :], wf_ref[rows, :] = cum_f, w_f
            cumb_ref[rows, :], wb_ref[rows, :] = cum_b, w_b
            return jnp.minimum(lowest, jnp.minimum(w_f, w_b))

        worst = jnp.min(lax.fori_loop(0, n_chunks, prefix_body, jnp.zeros((c, HEAD_DIM), F32), unroll=unroll))

        def local_pass(robust):
            def body(ci, _):
                rows = chunk_rows(ci)
                q = q_ref[rows, :].astype(F32) if want_o else None
                v_bf = v_ref[rows, :]
                x_f, sc_f, qe_f, u_f, d_f = _gla_chunk(
                    q, kf_ref[rows, :].astype(F32), lff_ref[rows, :], v_bf, cumf_ref.at[rows, :],
                    wf_ref[rows, :], mats_f, reverse=False, robust=robust, want_o=want_o)
                x_b, sc_b, qe_b, u_b, d_b = _gla_chunk(
                    q, kb_ref[rows, :].astype(F32), lfb_ref[rows, :], v_bf, cumb_ref.at[rows, :],
                    wb_ref[rows, :], mats_b, reverse=True, robust=robust, want_o=want_o)
                uf_ref[ci] = u_f
                ub_ref[ci] = u_b
                df_ref[ci] = d_f
                db_ref[ci] = d_b
                if want_o:
                    of_ref[rows, :] = x_f
                    ob_ref[rows, :] = x_b
                    scf_ref[rows, :] = sc_f
                    scb_ref[rows, :] = sc_b
                    qef_ref[rows, :] = qe_f
                    qeb_ref[rows, :] = qe_b
                return 0

            lax.fori_loop(0, n_chunks, body, 0, unroll=1 if robust else unroll)

        if want_o:
            lax.cond(worst < -GLA_SAFE_RANGE, lambda: local_pass(True), lambda: local_pass(False))
        else:
            local_pass(False)

        def state_pass(ci, carry):
            st_f, st_b = carry
            cb = n_chunks - 1 - ci
            if want_o:
                rows_f, rows_b = chunk_rows(ci), chunk_rows(cb)
                of_ref[rows_f, :] += (
                    jnp.dot(scf_ref[rows_f, :], v_ref[rows_f, :], preferred_element_type=F32)
                    + lax.dot_general(qef_ref[rows_f, :], st_f.astype(BF16), dn_t, preferred_element_type=F32))
                ob_ref[rows_b, :] += (
                    jnp.dot(scb_ref[rows_b, :], v_ref[rows_b, :], preferred_element_type=F32)
                    + lax.dot_general(qeb_ref[rows_b, :], st_b.astype(BF16), dn_t, preferred_element_type=F32))
            return st_f * df_ref[ci] + uf_ref[ci], st_b * db_ref[cb] + ub_ref[cb]

        return lax.fori_loop(0, n_chunks, state_pass, state, unroll=unroll)

    zero = jnp.zeros((HEAD_DIM, HEAD_DIM), F32)
    state = scan(None, kfc_ref, kbc_ref, lffc_ref, lfbc_ref, vc_ref, (zero, zero), False)
    scan(q_ref, kf_ref, kb_ref, lff_ref, lfb_ref, v_ref, state, True)

    n_chunks = q_ref.shape[0] // c

    def readout(ci, _):
        rows = chunk_rows(ci)
        tot = of_ref[rows, :] + ob_ref[rows, :]
        ms = jnp.mean(tot * tot, axis=1, keepdims=True)
        y = tot * lax.rsqrt(ms + RMS_EPS) * nw_ref[...] * gate_ref[rows, :].astype(F32)
        y_ref[rows, :] = y.astype(BF16)
        return 0
    lax.fori_loop(0, n_chunks, readout, 0, unroll=min(GLA_UNROLL, n_chunks))


def _gla(act, k, logf, act_ctx, k_ctx, logf_ctx, norm_w, *, n_heads):
    b, n, _ = act.shape
    n_ctx = act_ctx.shape[1]
    assert n_ctx <= n
    hd = HEAD_DIM
    col = lambda rows, off: pl.BlockSpec((None, rows, hd), lambda bi, h: (bi, 0, off * n_heads + h))
    return pl.pallas_call(
        _gla_kernel,
        grid=(b, n_heads),
        in_specs=[col(n_ctx, 0), col(n_ctx, 1), col(n_ctx, 0), col(n_ctx, 1), col(n_ctx, 1),
                  col(n, 0), col(n, 0), col(n, 1), col(n, 0), col(n, 1), col(n, 1), col(n, 2),
                  pl.BlockSpec((1, hd), lambda bi, h: (0, 0))],
        out_specs=pl.BlockSpec((None, n, hd), lambda bi, h: (bi, 0, h)),
        out_shape=jax.ShapeDtypeStruct((b, n, n_heads * hd), BF16),
        scratch_shapes=(
            [pltpu.VMEM((n, hd), F32)] * 6
            + [pltpu.VMEM((n, hd), BF16)] * 4
            + [pltpu.VMEM((n // GLA_CHUNK, hd, hd), F32)] * 2
            + [pltpu.VMEM((n // GLA_CHUNK, 1, hd), F32)] * 2),
        compiler_params=_params("arbitrary", "arbitrary"),
        name="gla_scan",
    )(k_ctx, k_ctx, logf_ctx, logf_ctx, act_ctx, act, k, k, logf, logf, act, act, norm_w.reshape(1, hd))


def _rope_tables(n):
    half = ROPE_AXIS_DIM // 2
    inv = (1.0 / (ROPE_BASE ** (np.arange(0, ROPE_AXIS_DIM, 2, dtype=np.float32) / ROPE_AXIS_DIM))).astype(np.float32)
    pos = np.arange(n)
    ang_r = (pos // GRID_W).astype(np.float32)[:, None] * inv[None, :]
    ang_c = (pos % GRID_W).astype(np.float32)[:, None] * inv[None, :]
    ang = np.concatenate([ang_r, ang_r, ang_c, ang_c] * 2, axis=1)
    sign = np.tile(np.concatenate([-np.ones(half, np.float32), np.ones(half, np.float32)]), HEAD_DIM // ROPE_AXIS_DIM)
    return jnp.asarray(np.cos(ang), F32), jnp.asarray(np.sin(ang) * sign[None, :], F32)


def kernel(x, c, ctx, c_ctx, mod_w, mod_b, ln_mix_g, ln_mix_b, ln_ffn_g, ln_ffn_b, even_w_in, even_w_out, diff_lambda, diff_subln, hgrn_w_in, hgrn_w_out, hgrn_lower_bounds, hgrn_norm, ffn_w_up, ffn_conv_w, ffn_conv_b, ffn_w_down):
    b, n_lat, d = x.shape
    n_ctx = ctx.shape[1]
    assert mod_w.shape[0] == DEPTH and n_lat % GRID_W == 0
    d_ff = ffn_w_down.shape[1]
    d_ff_pad = -(-d_ff // (2 * MXU_DIM)) * (2 * MXU_DIM)
    diff_width = even_w_out.shape[2] * 3 // 4
    four_width = even_w_in.shape[2] - 3 * diff_width
    n_diff_heads = diff_width // HEAD_DIM
    n_hgrn_heads = d // HEAD_DIM

    cc = jnp.concatenate([c, c_ctx[None, :], jnp.zeros((2 * SUBLANES - b - 1, d), F32)], axis=0)
    mod = _modulation(cc, mod_w, mod_b)
    mod = mod.reshape(DEPTH, cc.shape[0], 1, 6 * d)

    h_lat = x.reshape(b * n_lat, d)
    h_ctx = ctx.reshape(b * n_ctx, d)

    tm_wide_lat, tm_lat = min(1024, n_lat), min(512, n_lat)
    ctx_tile = lambda cap: n_ctx * max(1, min(cap // n_ctx, b)) if cap >= n_ctx else cap
    tm_wide_ctx, tm_ctx = ctx_tile(1024), ctx_tile(512)
    assert (b * n_ctx) % tm_wide_ctx == 0 and (b * n_ctx) % tm_ctx == 0
    lat_row = lambda tm: (lambda i: (i * tm) // n_lat)
    ctx_row = lambda tm: (lambda i: b)
    down_k_tiles = 4 if d_ff_pad % (4 * LANES) == 0 else d_ff_pad // 512

    def ffn(h, mod3, row_fn, layer, seq_len, tm_up, tm_down):
        pad = d_ff_pad - d_ff
        w_up = ffn_w_up[layer]
        wa = jnp.pad(w_up[:, :d_ff].astype(BF16), ((0, 0), (0, pad)))
        wv = jnp.pad(w_up[:, d_ff:].astype(BF16), ((0, 0), (0, pad)))
        cw = jnp.pad(ffn_conv_w[layer], ((0, 0), (0, pad)))
        cb = jnp.pad(ffn_conv_b[layer], (0, pad)).reshape(1, d_ff_pad)
        wd = jnp.pad(ffn_w_down[layer].astype(BF16), ((0, pad), (0, 0)))
        g = _ffn_up(h, mod3, row_fn(tm_up), wa, wv, cw, cb, seq_len=seq_len, tm=tm_up, tn=512)
        return _proj_residual([g], [(wd, 0)], h, mod3, 5, row_fn(tm_down), ln_ffn_g[layer], ln_ffn_b[layer],
                              tm=tm_down, n_k=down_k_tiles)

    layer = 0
    lam_init = 0.8 - 0.6 * math.exp(-0.3 * layer)
    w_in = even_w_in[0].astype(BF16)
    w_out = even_w_out[0].astype(BF16)
    cos_t, sin_t = _rope_tables(n_lat)
    mod3 = mod[layer]

    col_scale = jnp.asarray(np.where(np.arange(w_in.shape[1]) < diff_width, QK_SCALE, 1.0)[None, :], F32)
    p_lat = _even_inproj(h_lat, mod3, lat_row(tm_wide_lat), w_in, col_scale, cos_t, sin_t, qk_width=diff_width,
                         use_rope=True, tm=tm_wide_lat, tn=1024).reshape(b, n_lat, -1)
    p_ctx = _even_inproj(h_ctx, mod3, ctx_row(tm_wide_ctx), w_in, col_scale, cos_t[:SUBLANES], sin_t[:SUBLANES],
                         qk_width=diff_width, use_rope=False, tm=tm_wide_ctx, tn=1024).reshape(b, n_ctx, -1)

    attn_lat, attn_ctx = _diff_attention(p_lat, p_ctx, diff_lambda[0], diff_subln[0], lam_init,
                                         n_heads=n_diff_heads, tq=min(512, n_lat))
    four_block = 3 * diff_width // four_width
    four_lat = _fourier_mix(p_lat, four_block, four_width, tm=tm_lat)
    four_ctx = _fourier_mix(p_ctx, four_block, four_width, tm=n_ctx)

    w_list = [(w_out, 0), (w_out, diff_width)]
    h_lat = _proj_residual([attn_lat.reshape(b * n_lat, -1), four_lat.reshape(b * n_lat, -1)], w_list,
                           h_lat, mod3, 2, lat_row(tm_lat), ln_mix_g[layer], ln_mix_b[layer], tm=tm_lat)
    h_ctx = _proj_residual([attn_ctx.reshape(b * n_ctx, -1), four_ctx.reshape(b * n_ctx, -1)], w_list,
                           h_ctx, mod3, 2, ctx_row(tm_ctx), ln_mix_g[layer], ln_mix_b[layer], tm=tm_ctx)
    h_lat = ffn(h_lat, mod3, lat_row, layer, n_lat, tm_wide_lat, tm_wide_lat)
    h_ctx = ffn(h_ctx, mod3, ctx_row, layer, n_ctx, tm_wide_ctx, tm_wide_ctx)

    layer = 1
    mod3 = mod[layer]
    hw = hgrn_w_in[0].astype(BF16)
    wrap = lambda t, n: t.reshape(b, n, -1)
    act_c, k_c, lf_c = _hgrn_inproj(h_ctx, mod3, ctx_row(tm_wide_ctx), hw, hgrn_lower_bounds,
                                    layer=layer, tm=tm_wide_ctx, tn=1024)
    act_l, k_l, lf_l = _hgrn_inproj(h_lat, mod3, lat_row(tm_wide_lat), hw, hgrn_lower_bounds,
                                    layer=layer, tm=tm_wide_lat, tn=1024)
    y_lat = _gla(wrap(act_l, n_lat), wrap(k_l, n_lat), wrap(lf_l, n_lat),
                 wrap(act_c, n_ctx), wrap(k_c, n_ctx), wrap(lf_c, n_ctx), hgrn_norm[0], n_heads=n_hgrn_heads)
    h_lat = _proj_residual([y_lat.reshape(b * n_lat, d)], [(hgrn_w_out[0].astype(BF16), 0)], h_lat, mod3, 2,
                           lat_row(tm_lat), ln_mix_g[layer], ln_mix_b[layer], tm=tm_lat)
    h_lat = ffn(h_lat, mod3, lat_row, layer, n_lat, tm_wide_lat, tm_wide_lat)
    return h_lat.reshape(b, n_lat, d)
```

```python
import functools
import math

import numpy as np
import jax
import jax.numpy as jnp
from jax import lax
from jax.experimental import pallas as pl
from jax.experimental.pallas import tpu as pltpu

F32 = jnp.float32
BF16 = jnp.bfloat16

LANES = 128
SUBLANES = 8
MXU_DIM = 256
VMEM_LIMIT_BYTES = 56 * 1024 * 1024

GRID_W = 64
DIFF_QK_DIM = 64
HEAD_DIM = 128
ROPE_AXIS_DIM = DIFF_QK_DIM // 2
ROPE_BASE = 10000.0
CONV_W = 3
LN_EPS = 1e-6
RMS_EPS = 1e-5
DEPTH = 2
ALPHA = (2.0 * DEPTH) ** 0.25
QK_SCALE = DIFF_QK_DIM ** -0.5 * math.log2(math.e)
ATTN_SAFE_LOG2 = 60.0

GLA_CHUNK = 128
GLA_LEVELS = (64, 32, 16)
GLA_BLOCK = 16
LOG2_E = math.log2(math.e)
GLA_SAFE_RANGE = 80.0
GLA_UNROLL = 16


def _params(*dims):
    return pltpu.CompilerParams(dimension_semantics=dims, vmem_limit_bytes=VMEM_LIMIT_BYTES)


def _layer_norm_rows(x):
    mu = jnp.mean(x, axis=-1, keepdims=True)
    xc = x - mu
    var = jnp.mean(xc * xc, axis=-1, keepdims=True)
    return xc * lax.rsqrt(var + LN_EPS)


def _silu(x):
    return x * jax.nn.sigmoid(x)


def _mod_kernel(c_ref, w_ref, b_ref, o_ref):
    x = _silu(c_ref[...]).astype(BF16)
    o_ref[...] = jnp.dot(x, w_ref[...].astype(BF16), preferred_element_type=F32) + b_ref[...]


def _modulation(cc, mod_w, mod_b, tn=1024):
    depth, d, n = mod_w.shape
    rows = cc.shape[0]
    return pl.pallas_call(
        _mod_kernel,
        grid=(depth, n // tn),
        in_specs=[
            pl.BlockSpec((rows, d), lambda l, j: (0, 0)),
            pl.BlockSpec((None, d, tn), lambda l, j: (l, 0, j)),
            pl.BlockSpec((None, 1, tn), lambda l, j: (l, 0, j)),
        ],
        out_specs=pl.BlockSpec((None, rows, tn), lambda l, j: (l, 0, j)),
        out_shape=jax.ShapeDtypeStruct((depth, rows, n), F32),
        compiler_params=_params("arbitrary", "arbitrary"),
        name="modulation",
    )(cc, mod_w, mod_b.reshape(depth, 1, n))


def _mod_spec(chunk, d, row_of_tile):
    return pl.BlockSpec((None, 1, d), lambda i, *_: (row_of_tile(i), 0, chunk))


def _even_inproj_kernel(h_ref, sh_ref, sc_ref, w_ref, cscale_ref, cos_ref, sin_ref, o_ref, u_ref,
                        *, n_rope_tiles, use_rope):
    j = pl.program_id(1)

    @pl.when(j == 0)
    def _():
        u_ref[...] = (_layer_norm_rows(h_ref[...]) * (1.0 + sc_ref[...]) + sh_ref[...]).astype(BF16)

    def tile(rope):
        half = ROPE_AXIS_DIM // 2
        if rope:
            lane = lax.broadcasted_iota(jnp.int32, (u_ref.shape[0], HEAD_DIM), 1)
            second_half = (lane % ROPE_AXIS_DIM) >= half
        for c0 in range(0, o_ref.shape[1], MXU_DIM):
            cols = slice(c0, c0 + MXU_DIM)
            y = jnp.dot(u_ref[...], w_ref[:, cols], preferred_element_type=F32)
            if rope:
                parts = []
                for hh in range(MXU_DIM // HEAD_DIM):
                    yh = y[:, hh * HEAD_DIM:(hh + 1) * HEAD_DIM]
                    partner = jnp.where(second_half, pltpu.roll(yh, half, axis=1),
                                        pltpu.roll(yh, HEAD_DIM - half, axis=1))
                    parts.append(yh * cos_ref[...] + partner * sin_ref[...])
                y = jnp.concatenate(parts, axis=1)
            o_ref[:, cols] = (y * cscale_ref[:, cols]).astype(BF16)

    if use_rope:
        pl.when(j < n_rope_tiles)(lambda: tile(True))
        pl.when(j >= n_rope_tiles)(lambda: tile(False))
    else:
        tile(False)


def _even_inproj(h, mod3, row_of_tile, w, col_scale, cos_t, sin_t, *, qk_width, use_rope, tm, tn):
    m, d = h.shape
    n = w.shape[1]
    n_pos_tiles = cos_t.shape[0] // tm if use_rope else 1
    tab_rows = tm if use_rope else cos_t.shape[0]
    kern = functools.partial(_even_inproj_kernel, n_rope_tiles=2 * qk_width // tn, use_rope=use_rope)
    return pl.pallas_call(
        kern,
        grid=(m // tm, n // tn),
        in_specs=[
            pl.BlockSpec((tm, d), lambda i, j: (i, 0)),
            _mod_spec(0, d, row_of_tile),
            _mod_spec(1, d, row_of_tile),
            pl.BlockSpec((d, tn), lambda i, j: (0, j)),
            pl.BlockSpec((1, tn), lambda i, j: (0, j)),
            pl.BlockSpec((tab_rows, HEAD_DIM), lambda i, j: (i % n_pos_tiles, 0)),
            pl.BlockSpec((tab_rows, HEAD_DIM), lambda i, j: (i % n_pos_tiles, 0)),
        ],
        out_specs=pl.BlockSpec((tm, tn), lambda i, j: (i, j)),
        out_shape=jax.ShapeDtypeStruct((m, n), BF16),
        scratch_shapes=[pltpu.VMEM((tm, d), BF16)],
        compiler_params=_params("arbitrary", "arbitrary"),
        name="even_inproj",
    )(h, mod3, mod3, w, col_scale, cos_t, sin_t)


def _diff_attn_kernel(q_ref, qc_ref, kc_ref, kl_ref, vc_ref, vl_ref, lam_ref, subln_ref, o_ref, oc_ref,
                      *, lam_init, tq):
    lv = lam_ref[...]
    lam = (jnp.exp(jnp.sum(lv[0:1] * lv[1:2], axis=1, keepdims=True))
           - jnp.exp(jnp.sum(lv[2:3] * lv[3:4], axis=1, keepdims=True)) + lam_init)

    ones = jnp.ones((HEAD_DIM, HEAD_DIM), BF16)

    def max_sq_norm(ref):
        x = ref[...]
        row_sums = jnp.dot(x * x, ones, preferred_element_type=F32)
        return jnp.max(row_sums)

    knorm = jnp.maximum(max_sq_norm(kc_ref), max_sq_norm(kl_ref))
    qnorm = jnp.maximum(max_sq_norm(qc_ref), max_sq_norm(q_ref))
    small_scores = qnorm * knorm <= 0.98 * ATTN_SAFE_LOG2 ** 2

    dn = (((1,), (1,)), ((), ()))

    def attend(q, kv_refs, shift_by_max):
        lane = lax.broadcasted_iota(jnp.int32, q.shape, 1)
        first_map = lane < DIFF_QK_DIM
        zero = jnp.zeros_like(q)
        maps = []
        for qm in (jnp.where(first_map, q, zero), jnp.where(first_map, zero, q)):
            s = [lax.dot_general(qm, k_ref[...], dn, preferred_element_type=F32) for k_ref, _ in kv_refs]
            if shift_by_max:
                mx = functools.reduce(jnp.maximum, [jnp.max(x, axis=1, keepdims=True) for x in s])
                s = [x - mx for x in s]
            num = den = None
            for x, (_, v_ref) in zip(s, kv_refs):
                e = jnp.exp2(x)
                d = jnp.sum(e, axis=1, keepdims=True)
                part = jnp.dot(e.astype(BF16), v_ref[...], preferred_element_type=F32)
                den = d if den is None else den + d
                num = part if num is None else num + part
            maps.append((num, den))
        (n1, d1), (n2, d2) = maps
        acc = n1 * (1.0 / d1) - n2 * (lam / d2)
        ms = jnp.mean(acc * acc, axis=1, keepdims=True)
        y = acc * lax.rsqrt(ms + RMS_EPS) * subln_ref[...] * (1.0 - lam_init)
        return y.astype(BF16)

    def all_queries(shift_by_max):
        oc_ref[...] = attend(qc_ref[...], [(kc_ref, vc_ref)], shift_by_max)

        def body(t, carry):
            rows = pl.ds(pl.multiple_of(t * tq, tq), tq)
            o_ref[rows, :] = attend(q_ref[rows, :], [(kc_ref, vc_ref), (kl_ref, vl_ref)], shift_by_max)
            return carry
        n_tiles = q_ref.shape[0] // tq
        lax.fori_loop(0, n_tiles, body, 0, unroll=1 if shift_by_max else min(4, n_tiles))

    lax.cond(small_scores, lambda: all_queries(False), lambda: all_queries(True))


def _diff_attention(p_lat, p_ctx, lam_vecs, subln, lam_init, *, n_heads, tq):
    b, nq, _ = p_lat.shape
    n_ctx = p_ctx.shape[1]
    head = lambda rows, sec: pl.BlockSpec((None, rows, HEAD_DIM), lambda bi, h: (bi, 0, sec * n_heads + h))
    out = lambda rows: pl.BlockSpec((None, rows, HEAD_DIM), lambda bi, h: (bi, 0, h))
    return pl.pallas_call(
        functools.partial(_diff_attn_kernel, lam_init=lam_init, tq=tq),
        grid=(b, n_heads),
        in_specs=[head(nq, 0), head(n_ctx, 0), head(n_ctx, 1), head(nq, 1), head(n_ctx, 2), head(nq, 2),
                  pl.BlockSpec(lam_vecs.shape, lambda bi, h: (0, 0)),
                  pl.BlockSpec((1, HEAD_DIM), lambda bi, h: (0, 0))],
        out_specs=[out(nq), out(n_ctx)],
        out_shape=[jax.ShapeDtypeStruct((b, nq, n_heads * HEAD_DIM), BF16),
                   jax.ShapeDtypeStruct((b, n_ctx, n_heads * HEAD_DIM), BF16)],
        compiler_params=_params("arbitrary", "arbitrary"),
        name="diff_attention",
    )(p_lat, p_ctx, p_ctx, p_lat, p_ctx, p_lat, lam_vecs, subln.reshape(1, HEAD_DIM))


def _fourier_kernel(x_ref, dn_ref, cs_ref, o_ref, z_ref):
    n = x_ref.shape[0]

    @pl.when(pl.program_id(1) == 0)
    def _():
        for g in range(x_ref.shape[1] // HEAD_DIM):
            cols = slice(g * HEAD_DIM, (g + 1) * HEAD_DIM)
            zc = jnp.dot(x_ref[:, cols], cs_ref[...], preferred_element_type=F32)
            z_ref[0:n, cols] = zc[:, :HEAD_DIM].astype(BF16)
            z_ref[n:2 * n, cols] = zc[:, HEAD_DIM:].astype(BF16)

    o_ref[...] = jnp.dot(dn_ref[...], z_ref[...], preferred_element_type=F32).astype(BF16)


def _dft_tables(n):
    j = np.arange(n, dtype=np.int64)
    ang = 2.0 * np.pi * ((j[:, None] * j[None, :]) % n).astype(np.float64) / n
    return np.cos(ang) / math.sqrt(n), np.sin(ang) / math.sqrt(n)


def _fourier_mix(src, col_block, width, *, tm):
    b, n, _ = src.shape
    cn, sn = _dft_tables(n)
    cc, sc = _dft_tables(HEAD_DIM)
    dn = jnp.asarray(np.concatenate([cn, -sn], axis=1), dtype=BF16)
    cs = jnp.asarray(np.concatenate([cc, sc], axis=1), dtype=BF16)
    return pl.pallas_call(
        _fourier_kernel,
        grid=(b, n // tm),
        in_specs=[
            pl.BlockSpec((None, n, width), lambda bi, t: (bi, 0, col_block)),
            pl.BlockSpec((tm, 2 * n), lambda bi, t: (t, 0)),
            pl.BlockSpec((HEAD_DIM, 2 * HEAD_DIM), lambda bi, t: (0, 0)),
        ],
        out_specs=pl.BlockSpec((None, tm, width), lambda bi, t: (bi, t, 0)),
        out_shape=jax.ShapeDtypeStruct((b, n, width), BF16),
        scratch_shapes=[pltpu.VMEM((2 * n, width), BF16)],
        compiler_params=_params("arbitrary", "arbitrary"),
        name="fourier_mix",
    )(src, dn, cs)


def _proj_residual_kernel(*refs, n_in, n_k):
    a_refs = refs[:n_in]
    w_refs = refs[n_in:2 * n_in]
    h_ref, gate_ref, g_ref, b_ref, o_ref = refs[2 * n_in:2 * n_in + 5]
    k = pl.program_id(1)
    tm, d = o_ref.shape
    col_block, row_block = 2 * MXU_DIM, MXU_DIM

    def accumulate(first, rows=slice(None)):
        for c0 in range(0, d, col_block):
            cols = slice(c0, c0 + col_block)
            part = None
            for a_ref, w_ref in zip(a_refs, w_refs):
                p = jnp.dot(a_ref[rows, :], w_ref[:, cols], preferred_element_type=F32)
                part = p if part is None else part + p
            if first:
                o_ref[rows, cols] = part
            else:
                o_ref[rows, cols] += part

    def finish(r_lo=0, r_hi=tm):
        for r0 in range(r_lo, r_hi, row_block):
            rows = slice(r0, r0 + row_block)
            x = ALPHA * h_ref[rows, :] + gate_ref[...] * o_ref[rows, :]
            o_ref[rows, :] = _layer_norm_rows(x) * g_ref[...] + b_ref[...]

    def last_step(first):
        halves = 2 if tm >= 2 * row_block else 1
        for r0 in range(0, tm, tm // halves):
            accumulate(first, slice(r0, r0 + tm // halves))
            finish(r0, r0 + tm // halves)

    if n_k == 1:
        last_step(True)
    else:
        pl.when(k == 0)(lambda: accumulate(True))
        pl.when(jnp.logical_and(k > 0, k < n_k - 1))(lambda: accumulate(False))
        pl.when(k == n_k - 1)(lambda: last_step(False))


def _proj_residual(a_list, w_list, h, mod3, gate_chunk, row_of_tile, ln_g, ln_b, *, tm, n_k=1):
    m, d = h.shape
    in_specs = []
    for a in a_list:
        in_specs.append(pl.BlockSpec((tm, a.shape[1] // n_k), lambda i, k: (i, k)))
    for a, (w, first_row) in zip(a_list, w_list):
        rows = a.shape[1] // n_k
        in_specs.append(pl.BlockSpec((rows, d), lambda i, k, blk=first_row // rows: (blk + k, 0)))
    in_specs += [
        pl.BlockSpec((tm, d), lambda i, k: (i, 0)),
        _mod_spec(gate_chunk, d, row_of_tile),
        pl.BlockSpec((1, d), lambda i, k: (0, 0)),
        pl.BlockSpec((1, d), lambda i, k: (0, 0)),
    ]
    return pl.pallas_call(
        functools.partial(_proj_residual_kernel, n_in=len(a_list), n_k=n_k),
        grid=(m // tm, n_k),
        in_specs=in_specs,
        out_specs=pl.BlockSpec((tm, d), lambda i, k: (i, 0)),
        out_shape=jax.ShapeDtypeStruct((m, d), F32),
        compiler_params=_params("arbitrary", "arbitrary"),
        name="proj_residual",
    )(*a_list, *[w for w, _ in w_list], h, mod3, ln_g.reshape(1, d), ln_b.reshape(1, d))


def _ffn_up_kernel(h_ref, hp_ref, hn_ref, sh_ref, sc_ref, wa_ref, wv_ref, cw_ref, cb_ref, o_ref,
                   u_ref, uh_ref, *, seq_len):
    i = pl.program_id(0)
    j = pl.program_id(1)
    tm = h_ref.shape[0]
    whole_sequences = tm >= seq_len
    tiles_per_seq = max(seq_len // tm, 1)

    @pl.when(j == 0)
    def _():
        scale = 1.0 + sc_ref[...]
        u_ref[...] = (_layer_norm_rows(h_ref[...]) * scale + sh_ref[...]).astype(BF16)
        if not whole_sequences:
            uh_ref[0:SUBLANES, :] = (_layer_norm_rows(hp_ref[...]) * scale + sh_ref[...]).astype(BF16)
            uh_ref[SUBLANES:, :] = (_layer_norm_rows(hn_ref[...]) * scale + sh_ref[...]).astype(BF16)

    t_in_seq = i % tiles_per_seq
    row = lax.broadcasted_iota(jnp.int32, (tm, MXU_DIM), 0)
    row_in_seq = row % seq_len
    col_blocks = [slice(c0, c0 + MXU_DIM) for c0 in range(0, o_ref.shape[1], MXU_DIM)]
    gates = []
    for cols in col_blocks:
        a = jnp.dot(u_ref[...], wa_ref[:, cols], preferred_element_type=F32)
        if whole_sequences:
            a_prev = jnp.where(row_in_seq == 0, 0.0, pltpu.roll(a, 1, axis=0))
            a_next = jnp.where(row_in_seq == seq_len - 1, 0.0, pltpu.roll(a, tm - 1, axis=0))
        else:
            halo = jnp.dot(uh_ref[...], wa_ref[:, cols], preferred_element_type=F32)
            prev_row = jnp.where(t_in_seq > 0, halo[SUBLANES - 1:SUBLANES], 0.0)
            next_row = jnp.where(t_in_seq < tiles_per_seq - 1, halo[SUBLANES:SUBLANES + 1], 0.0)
            a_prev = jnp.where(row == 0, prev_row, pltpu.roll(a, 1, axis=0))
            a_next = jnp.where(row == tm - 1, next_row, pltpu.roll(a, tm - 1, axis=0))
        cw = cw_ref[:, cols]
        conv = cb_ref[:, cols] + a_prev * cw[0:1] + a * cw[1:2] + a_next * cw[2:3]
        gates.append(0.5 * conv * (1.0 + lax.erf(conv * math.sqrt(0.5))))
    for cols, gelu in zip(col_blocks, gates):
        v = jnp.dot(u_ref[...], wv_ref[:, cols], preferred_element_type=F32)
        o_ref[:, cols] = (gelu * v).astype(BF16)


def _ffn_up(h, mod3, row_of_tile, wa, wv, conv_w, conv_b, *, seq_len, tm, tn):
    m, d = h.shape
    n = wa.shape[1]
    assert tm % seq_len == 0 or seq_len % tm == 0
    blocks_per_tile = tm // SUBLANES
    n_blocks = m // SUBLANES
    kern = functools.partial(_ffn_up_kernel, seq_len=seq_len)
    return pl.pallas_call(
        kern,
        grid=(m // tm, n // tn),
        in_specs=[
            pl.BlockSpec((tm, d), lambda i, j: (i, 0)),
            pl.BlockSpec((SUBLANES, d), lambda i, j: (jnp.maximum(i * blocks_per_tile - 1, 0), 0)),
            pl.BlockSpec((SUBLANES, d), lambda i, j: (jnp.minimum((i + 1) * blocks_per_tile, n_blocks - 1), 0)),
            _mod_spec(3, d, row_of_tile),
            _mod_spec(4, d, row_of_tile),
            pl.BlockSpec((d, tn), lambda i, j: (0, j)),
            pl.BlockSpec((d, tn), lambda i, j: (0, j)),
            pl.BlockSpec((CONV_W, tn), lambda i, j: (0, j)),
            pl.BlockSpec((1, tn), lambda i, j: (0, j)),
        ],
        out_specs=pl.BlockSpec((tm, tn), lambda i, j: (i, j)),
        out_shape=jax.ShapeDtypeStruct((m, n), BF16),
        scratch_shapes=[pltpu.VMEM((tm, d), BF16), pltpu.VMEM((2 * SUBLANES, d), BF16)],
        compiler_params=_params("arbitrary", "arbitrary"),
        name="ffn_up",
    )(h, h, h, mod3, mod3, wa, wv, conv_w, conv_b)


def _hgrn_act_kernel(h_ref, sh_ref, sc_ref, w_ref, o_ref, u_ref, *, lin_lo, lin_hi):
    j = pl.program_id(1)

    @pl.when(j == 0)
    def _():
        u_ref[...] = (_layer_norm_rows(h_ref[...]) * (1.0 + sc_ref[...]) + sh_ref[...]).astype(BF16)

    def tile(activation):
        for c0 in range(0, o_ref.shape[1], MXU_DIM):
            cols = slice(c0, c0 + MXU_DIM)
            acc = jnp.dot(u_ref[...], w_ref[:, cols], preferred_element_type=F32)
            o_ref[:, cols] = activation(acc).astype(BF16)

    linear = jnp.logical_and(j >= lin_lo, j < lin_hi)
    pl.when(linear)(lambda: tile(lambda x: x))
    pl.when(jnp.logical_not(linear))(lambda: tile(_silu))


def _hgrn_gate_kernel(h_ref, sh_ref, sc_ref, w_ref, lbp_ref, k_ref, lf_ref, u_ref, *, layer):
    j = pl.program_id(1)

    @pl.when(j == 0)
    def _():
        u_ref[...] = (_layer_norm_rows(h_ref[...]) * (1.0 + sc_ref[...]) + sh_ref[...]).astype(BF16)

    for c0 in range(0, k_ref.shape[1], MXU_DIM):
        cols = slice(c0, c0 + MXU_DIM)
        x = lbp_ref[:, cols]
        e = jnp.exp(x - jnp.max(x, axis=0, keepdims=True))
        lb = jnp.sum(e[1:layer + 1], axis=0, keepdims=True) / jnp.sum(e, axis=0, keepdims=True)

        f_pre = jnp.dot(u_ref[...], w_ref[:, cols], preferred_element_type=F32)
        gate = jax.nn.sigmoid(f_pre)
        k_ref[:, cols] = ((1.0 - lb) * (1.0 - gate)).astype(BF16)
        lf_ref[:, cols] = jnp.log(lb + (1.0 - lb) * gate)


def _hgrn_inproj(h, mod3, row_of_tile, w, lb_params, *, layer, tm, tn):
    m, d = h.shape
    tiles = d // tn
    common = [
        pl.BlockSpec((tm, d), lambda i, j: (i, 0)),
        _mod_spec(0, d, row_of_tile),
        _mod_spec(1, d, row_of_tile),
    ]
    act_w = pl.BlockSpec((d, tn), lambda i, j: (0, jnp.where(j < tiles, j, j + 2 * tiles)))
    gate_w = pl.BlockSpec((d, tn), lambda i, j: (0, j + tiles))
    act = pl.pallas_call(
        functools.partial(_hgrn_act_kernel, lin_lo=tiles, lin_hi=2 * tiles),
        grid=(m // tm, 3 * tiles),
        in_specs=common + [act_w],
        out_specs=pl.BlockSpec((tm, tn), lambda i, j: (i, j)),
        out_shape=jax.ShapeDtypeStruct((m, 3 * d), BF16),
        scratch_shapes=[pltpu.VMEM((tm, d), BF16)],
        compiler_params=_params("arbitrary", "arbitrary"),
        name="hgrn_inproj_act",
    )(h, mod3, mod3, w)
    k, logf = pl.pallas_call(
        functools.partial(_hgrn_gate_kernel, layer=layer),
        grid=(m // tm, 2 * tiles),
        in_specs=common + [gate_w, pl.BlockSpec((None, DEPTH, tn), lambda i, j: (j // tiles, 0, j % tiles))],
        out_specs=[pl.BlockSpec((tm, tn), lambda i, j: (i, j))] * 2,
        out_shape=[jax.ShapeDtypeStruct((m, 2 * d), BF16), jax.ShapeDtypeStruct((m, 2 * d), F32)],
        scratch_shapes=[pltpu.VMEM((tm, d), BF16)],
        compiler_params=_params("arbitrary", "arbitrary"),
        name="hgrn_inproj_gate",
    )(h, mod3, mod3, w, lb_params)
    return act, k, logf


def _gla_matrices(reverse):
    c = GLA_CHUNK
    r_i = lax.broadcasted_iota(jnp.int32, (c, c), 0)
    c_i = lax.broadcasted_iota(jnp.int32, (c, c), 1)
    same = jnp.bitwise_xor(r_i, c_i)
    earlier = (c_i >= r_i) if reverse else (c_i <= r_i)
    in_block = same < GLA_BLOCK
    one = lambda m: jnp.where(m, 1.0, 0.0).astype(BF16)
    return dict(tri=one(earlier), tri_blk=one(jnp.logical_and(earlier, in_block)), ones_blk=one(in_block),
                diag_mask=jnp.logical_and(earlier, in_block), same=same)


def _gla_direct_block_terms(q, k, v, cum, bad, *, reverse):
    c = GLA_CHUNK
    row = lax.broadcasted_iota(jnp.int32, (c, HEAD_DIM), 0) % GLA_BLOCK
    qb = jnp.where(bad, q, 0.0)
    acc0 = jnp.sum(qb * k, axis=1, keepdims=True) * v

    def offset(dlt, acc):
        shift = (c - dlt) if reverse else dlt
        valid = (row <= GLA_BLOCK - 1 - dlt) if reverse else (row >= dlt)
        k_d = pltpu.roll(k, shift, axis=0)
        v_d = pltpu.roll(v, shift, axis=0)
        cum_d = pltpu.roll(cum, shift, axis=0)
        decay = jnp.exp2(jnp.where(valid, cum - cum_d, -jnp.inf))
        return acc + jnp.sum(qb * k_d * decay, axis=1, keepdims=True) * v_d

    return lax.fori_loop(1, GLA_BLOCK, offset, acc0)


def _gla_prefix_sums(lf, mats):
    lf = lf * LOG2_E
    lf_hi = lf.astype(BF16)
    lf_lo = (lf - lf_hi.astype(F32)).astype(BF16)
    two_term = lambda m: (jnp.dot(m, lf_hi, preferred_element_type=F32)
                          + jnp.dot(m, lf_lo, preferred_element_type=F32))
    return two_term(mats["tri"]), two_term(mats["tri_blk"])


def _gla_chunk(q, k, lf, v_bf, cum_ref, w_blk, mats, *, reverse, robust, want_o):
    c = GLA_CHUNK
    dn_t = (((1,), (1,)), ((), ()))
    cum = cum_ref[...]
    end_row = 0 if reverse else c - 1
    cum_end = cum_ref[end_row:end_row + 1, :]

    decayed = lambda x, log2_decay: (x * jnp.exp2(log2_decay)).astype(BF16)
    k_dec = decayed(k, cum_end - cum)
    u = lax.dot_general(v_bf, k_dec, (((0,), (0,)), ((), ())), preferred_element_type=F32)
    dec_end = jnp.exp2(cum_end)
    if not want_o:
        return None, None, None, u, dec_end
    qe = decayed(q, cum)

    if robust:
        blk_tot = jnp.dot(mats["ones_blk"], (lf * LOG2_E).astype(BF16), preferred_element_type=F32)
        bad = blk_tot < -GLA_SAFE_RANGE
        a_d = jnp.where(bad, 0.0, q * jnp.exp2(w_blk)).astype(BF16)
        b_d = decayed(k, jnp.minimum(-w_blk, GLA_SAFE_RANGE + 20.0))
    else:
        a_d = decayed(q, w_blk)
        b_d = decayed(k, -w_blk)
    scores = jnp.where(mats["diag_mask"], lax.dot_general(a_d, b_d, dn_t, preferred_element_type=F32), 0.0)

    zeros = {m: jnp.zeros((m, HEAD_DIM), BF16) for m in GLA_LEVELS}
    for m in GLA_LEVELS:
        a_parts, b_parts = [], []
        for blk in range(c // (2 * m)):
            lo, mid, hi = blk * 2 * m, blk * 2 * m + m, (blk + 1) * 2 * m
            if reverse:
                ref_row = cum_ref[mid:mid + 1, :]
                qa = decayed(q[lo:mid], cum[lo:mid] - ref_row)
                kb = decayed(k[mid:hi], ref_row - cum[mid:hi])
                a_parts += [qa, zeros[m]]
                b_parts += [zeros[m], kb]
            else:
                ref_row = cum_ref[mid - 1:mid, :]
                qa = decayed(q[mid:hi], cum[mid:hi] - ref_row)
                kb = decayed(k[lo:mid], ref_row - cum[lo:mid])
                a_parts += [zeros[m], qa]
                b_parts += [kb, zeros[m]]
        a_m = jnp.concatenate(a_parts, axis=0)
        b_m = jnp.concatenate(b_parts, axis=0)
        s_m = lax.dot_general(a_m, b_m, dn_t, preferred_element_type=F32)
        if 2 * m < c:
            s_m = jnp.where(mats["same"] < 2 * m, s_m, 0.0)
        scores = scores + s_m
    if robust:
        direct = _gla_direct_block_terms(q, k, v_bf.astype(F32), cum, bad, reverse=reverse)
    else:
        direct = jnp.zeros((c, HEAD_DIM), F32)
    return direct, scores.astype(BF16), qe, u, dec_end


def _gla_kernel(kfc_ref, kbc_ref, lffc_ref, lfbc_ref, vc_ref, q_ref, kf_ref, kb_ref, lff_ref, lfb_ref, v_ref,
                gate_ref, nw_ref, y_ref, of_ref, ob_ref, cumf_ref, cumb_ref, wf_ref, wb_ref, qef_ref, qeb_ref,
                scf_ref, scb_ref, uf_ref, ub_ref, df_ref, db_ref):
    c = GLA_CHUNK
    dn_t = (((1,), (1,)), ((), ()))
    chunk_rows = lambda ci: pl.ds(pl.multiple_of(ci * c, c), c)
    mats_f, mats_b = _gla_matrices(False), _gla_matrices(True)

    def scan(q_ref, kf_ref, kb_ref, lff_ref, lfb_ref, v_ref, state, want_o):
        n_chunks = kf_ref.shape[0] // c
        unroll = min(GLA_UNROLL, n_chunks)

        def prefix_body(ci, lowest):
            rows = chunk_rows(ci)
            cum_f, w_f = _gla_prefix_sums(lff_ref[rows, :], mats_f)
            cum_b, w_b = _gla_prefix_sums(lfb_ref[rows, :], mats_b)
            cumf_ref[rows, :], wf_ref[rows, :] = cum_f, w_f
            cumb_ref[rows, :], wb_ref[rows, :] = cum_b, w_b
            return jnp.minimum(lowest, jnp.minimum(w_f, w_b))

        worst = jnp.min(lax.fori_loop(0, n_chunks, prefix_body, jnp.zeros((c, HEAD_DIM), F32), unroll=unroll))

        def local_pass(robust):
            def body(ci, _):
                rows = chunk_rows(ci)
                q = q_ref[rows, :].astype(F32) if want_o else None
                v_bf = v_ref[rows, :]
                x_f, sc_f, qe_f, u_f, d_f = _gla_chunk(
                    q, kf_ref[rows, :].astype(F32), lff_ref[rows, :], v_bf, cumf_ref.at[rows, :],
                    wf_ref[rows, :], mats_f, reverse=False, robust=robust, want_o=want_o)
                x_b, sc_b, qe_b, u_b, d_b = _gla_chunk(
                    q, kb_ref[rows, :].astype(F32), lfb_ref[rows, :], v_bf, cumb_ref.at[rows, :],
                    wb_ref[rows, :], mats_b, reverse=True, robust=robust, want_o=want_o)
                uf_ref[ci] = u_f
                ub_ref[ci] = u_b
                df_ref[ci] = d_f
                db_ref[ci] = d_b
                if want_o:
                    of_ref[rows, :] = x_f
                    ob_ref[rows, :] = x_b
                    scf_ref[rows, :] = sc_f
                    scb_ref[rows, :] = sc_b
                    qef_ref[rows, :] = qe_f
                    qeb_ref[rows, :] = qe_b
                return 0

            lax.fori_loop(0, n_chunks, body, 0, unroll=1 if robust else unroll)

        if want_o:
            lax.cond(worst < -GLA_SAFE_RANGE, lambda: local_pass(True), lambda: local_pass(False))
        else:
            local_pass(False)

        def state_pass(ci, carry):
            st_f, st_b = carry
            cb = n_chunks - 1 - ci
            if want_o:
                rows_f, rows_b = chunk_rows(ci), chunk_rows(cb)
                of_ref[rows_f, :] += (
                    jnp.dot(scf_ref[rows_f, :], v_ref[rows_f, :], preferred_element_type=F32)
                    + lax.dot_general(qef_ref[rows_f, :], st_f.astype(BF16), dn_t, preferred_element_type=F32))
                ob_ref[rows_b, :] += (
                    jnp.dot(scb_ref[rows_b, :], v_ref[rows_b, :], preferred_element_type=F32)
                    + lax.dot_general(qeb_ref[rows_b, :], st_b.astype(BF16), dn_t, preferred_element_type=F32))
            return st_f * df_ref[ci] + uf_ref[ci], st_b * db_ref[cb] + ub_ref[cb]

        return lax.fori_loop(0, n_chunks, state_pass, state, unroll=unroll)

    zero = jnp.zeros((HEAD_DIM, HEAD_DIM), F32)
    state = scan(None, kfc_ref, kbc_ref, lffc_ref, lfbc_ref, vc_ref, (zero, zero), False)
    scan(q_ref, kf_ref, kb_ref, lff_ref, lfb_ref, v_ref, state, True)

    n_chunks = q_ref.shape[0] // c

    def readout(ci, _):
        rows = chunk_rows(ci)
        tot = of_ref[rows, :] + ob_ref[rows, :]
        ms = jnp.mean(tot * tot, axis=1, keepdims=True)
        y = tot * lax.rsqrt(ms + RMS_EPS) * nw_ref[...] * gate_ref[rows, :].astype(F32)
        y_ref[rows, :] = y.astype(BF16)
        return 0
    lax.fori_loop(0, n_chunks, readout, 0, unroll=min(GLA_UNROLL, n_chunks))


def _gla(act, k, logf, act_ctx, k_ctx, logf_ctx, norm_w, *, n_heads):
    b, n, _ = act.shape
    n_ctx = act_ctx.shape[1]
    assert n_ctx <= n
    hd = HEAD_DIM
    col = lambda rows, off: pl.BlockSpec((None, rows, hd), lambda bi, h: (bi, 0, off * n_heads + h))
    return pl.pallas_call(
        _gla_kernel,
        grid=(b, n_heads),
        in_specs=[col(n_ctx, 0), col(n_ctx, 1), col(n_ctx, 0), col(n_ctx, 1), col(n_ctx, 1),
                  col(n, 0), col(n, 0), col(n, 1), col(n, 0), col(n, 1), col(n, 1), col(n, 2),
                  pl.BlockSpec((1, hd), lambda bi, h: (0, 0))],
        out_specs=pl.BlockSpec((None, n, hd), lambda bi, h: (bi, 0, h)),
        out_shape=jax.ShapeDtypeStruct((b, n, n_heads * hd), BF16),
        scratch_shapes=(
            [pltpu.VMEM((n, hd), F32)] * 6
            + [pltpu.VMEM((n, hd), BF16)] * 4
            + [pltpu.VMEM((n // GLA_CHUNK, hd, hd), F32)] * 2
            + [pltpu.VMEM((n // GLA_CHUNK, 1, hd), F32)] * 2),
        compiler_params=_params("arbitrary", "arbitrary"),
        name="gla_scan",
    )(k_ctx, k_ctx, logf_ctx, logf_ctx, act_ctx, act, k, k, logf, logf, act, act, norm_w.reshape(1, hd))


def _rope_tables(n):
    half = ROPE_AXIS_DIM // 2
    inv = (1.0 / (ROPE_BASE ** (np.arange(0, ROPE_AXIS_DIM, 2, dtype=np.float32) / ROPE_AXIS_DIM))).astype(np.float32)
    pos = np.arange(n)
    ang_r = (pos // GRID_W).astype(np.float32)[:, None] * inv[None, :]
    ang_c = (pos % GRID_W).astype(np.float32)[:, None] * inv[None, :]
    ang = np.concatenate([ang_r, ang_r, ang_c, ang_c] * 2, axis=1)
    sign = np.tile(np.concatenate([-np.ones(half, np.float32), np.ones(half, np.float32)]), HEAD_DIM // ROPE_AXIS_DIM)
    return jnp.asarray(np.cos(ang), F32), jnp.asarray(np.sin(ang) * sign[None, :], F32)


def kernel(x, c, ctx, c_ctx, mod_w, mod_b, ln_mix_g, ln_mix_b, ln_ffn_g, ln_ffn_b, even_w_in, even_w_out, diff_lambda, diff_subln, hgrn_w_in, hgrn_w_out, hgrn_lower_bounds, hgrn_norm, ffn_w_up, ffn_conv_w, ffn_conv_b, ffn_w_down):
    b, n_lat, d = x.shape
    n_ctx = ctx.shape[1]
    assert mod_w.shape[0] == DEPTH and n_lat % GRID_W == 0
    d_ff = ffn_w_down.shape[1]
    d_ff_pad = -(-d_ff // (2 * MXU_DIM)) * (2 * MXU_DIM)
    diff_width = even_w_out.shape[2] * 3 // 4
    four_width = even_w_in.shape[2] - 3 * diff_width
    n_diff_heads = diff_width // HEAD_DIM
    n_hgrn_heads = d // HEAD_DIM

    cc = jnp.concatenate([c, c_ctx[None, :], jnp.zeros((2 * SUBLANES - b - 1, d), F32)], axis=0)
    mod = _modulation(cc, mod_w, mod_b)
    mod = mod.reshape(DEPTH, cc.shape[0], 1, 6 * d)

    h_lat = x.reshape(b * n_lat, d)
    h_ctx = ctx.reshape(b * n_ctx, d)

    tm_wide_lat, tm_lat = min(1024, n_lat), min(512, n_lat)
    ctx_tile = lambda cap: n_ctx * max(1, min(cap // n_ctx, b)) if cap >= n_ctx else cap
    tm_wide_ctx, tm_ctx = ctx_tile(1024), ctx_tile(512)
    assert (b * n_ctx) % tm_wide_ctx == 0 and (b * n_ctx) % tm_ctx == 0
    lat_row = lambda tm: (lambda i: (i * tm) // n_lat)
    ctx_row = lambda tm: (lambda i: b)
    down_k_tiles = 4 if d_ff_pad % (4 * LANES) == 0 else d_ff_pad // 512

    def ffn(h, mod3, row_fn, layer, seq_len, tm_up, tm_down):
        pad = d_ff_pad - d_ff
        w_up = ffn_w_up[layer]
        wa = jnp.pad(w_up[:, :d_ff].astype(BF16), ((0, 0), (0, pad)))
        wv = jnp.pad(w_up[:, d_ff:].astype(BF16), ((0, 0), (0, pad)))
        cw = jnp.pad(ffn_conv_w[layer], ((0, 0), (0, pad)))
        cb = jnp.pad(ffn_conv_b[layer], (0, pad)).reshape(1, d_ff_pad)
        wd = jnp.pad(ffn_w_down[layer].astype(BF16), ((0, pad), (0, 0)))
        g = _ffn_up(h, mod3, row_fn(tm_up), wa, wv, cw, cb, seq_len=seq_len, tm=tm_up, tn=512)
        return _proj_residual([g], [(wd, 0)], h, mod3, 5, row_fn(tm_down), ln_ffn_g[layer], ln_ffn_b[layer],
                              tm=tm_down, n_k=down_k_tiles)

    layer = 0
    lam_init = 0.8 - 0.6 * math.exp(-0.3 * layer)
    w_in = even_w_in[0].astype(BF16)
    w_out = even_w_out[0].astype(BF16)
    cos_t, sin_t = _rope_tables(n_lat)
    mod3 = mod[layer]

    col_scale = jnp.asarray(np.where(np.arange(w_in.shape[1]) < diff_width, QK_SCALE, 1.0)[None, :], F32)
    p_lat = _even_inproj(h_lat, mod3, lat_row(tm_wide_lat), w_in, col_scale, cos_t, sin_t, qk_width=diff_width,
                         use_rope=True, tm=tm_wide_lat, tn=1024).reshape(b, n_lat, -1)
    p_ctx = _even_inproj(h_ctx, mod3, ctx_row(tm_wide_ctx), w_in, col_scale, cos_t[:SUBLANES], sin_t[:SUBLANES],
                         qk_width=diff_width, use_rope=False, tm=tm_wide_ctx, tn=1024).reshape(b, n_ctx, -1)

    attn_lat, attn_ctx = _diff_attention(p_lat, p_ctx, diff_lambda[0], diff_subln[0], lam_init,
                                         n_heads=n_diff_heads, tq=min(512, n_lat))
    four_block = 3 * diff_width // four_width
    four_lat = _fourier_mix(p_lat, four_block, four_width, tm=tm_lat)
    four_ctx = _fourier_mix(p_ctx, four_block, four_width, tm=n_ctx)

    w_list = [(w_out, 0), (w_out, diff_width)]
    h_lat = _proj_residual([attn_lat.reshape(b * n_lat, -1), four_lat.reshape(b * n_lat, -1)], w_list,
                           h_lat, mod3, 2, lat_row(tm_lat), ln_mix_g[layer], ln_mix_b[layer], tm=tm_lat)
    h_ctx = _proj_residual([attn_ctx.reshape(b * n_ctx, -1), four_ctx.reshape(b * n_ctx, -1)], w_list,
                           h_ctx, mod3, 2, ctx_row(tm_ctx), ln_mix_g[layer], ln_mix_b[layer], tm=tm_ctx)
    h_lat = ffn(h_lat, mod3, lat_row, layer, n_lat, tm_wide_lat, tm_wide_lat)
    h_ctx = ffn(h_ctx, mod3, ctx_row, layer, n_ctx, tm_wide_ctx, tm_wide_ctx)

    layer = 1
    mod3 = mod[layer]
    hw = hgrn_w_in[0].astype(BF16)
    wrap = lambda t, n: t.reshape(b, n, -1)
    act_c, k_c, lf_c = _hgrn_inproj(h_ctx, mod3, ctx_row(tm_wide_ctx), hw, hgrn_lower_bounds,
                                    layer=layer, tm=tm_wide_ctx, tn=1024)
    act_l, k_l, lf_l = _hgrn_inproj(h_lat, mod3, lat_row(tm_wide_lat), hw, hgrn_lower_bounds,
                                    layer=layer, tm=tm_wide_lat, tn=1024)
    y_lat = _gla(wrap(act_l, n_lat), wrap(k_l, n_lat), wrap(lf_l, n_lat),
                 wrap(act_c, n_ctx), wrap(k_c, n_ctx), wrap(lf_c, n_ctx), hgrn_norm[0], n_heads=n_hgrn_heads)
    h_lat = _proj_residual([y_lat.reshape(b * n_lat, d)], [(hgrn_w_out[0].astype(BF16), 0)], h_lat, mod3, 2,
                           lat_row(tm_lat), ln_mix_g[layer], ln_mix_b[layer], tm=tm_lat)
    h_lat = ffn(h_lat, mod3, lat_row, layer, n_lat, tm_wide_lat, tm_wide_lat)
    return h_lat.reshape(b, n_lat, d)
```

```python
import functools
import math

import numpy as np
import jax
import jax.numpy as jnp
from jax import lax
from jax.experimental import pallas as pl
from jax.experimental.pallas import tpu as pltpu

F32 = jnp.float32
BF16 = jnp.bfloat16

LANES = 128
SUBLANES = 8
MXU_DIM = 256
VMEM_LIMIT_BYTES = 56 * 1024 * 1024

GRID_W = 64
DIFF_QK_DIM = 64
HEAD_DIM = 128
ROPE_AXIS_DIM = DIFF_QK_DIM // 2
ROPE_BASE = 10000.0
CONV_W = 3
LN_EPS = 1e-6
RMS_EPS = 1e-5
DEPTH = 2
ALPHA = (2.0 * DEPTH) ** 0.25
QK_SCALE = DIFF_QK_DIM ** -0.5 * math.log2(math.e)
ATTN_SAFE_LOG2 = 60.0

GLA_CHUNK = 128
GLA_LEVELS = (64, 32, 16)
GLA_BLOCK = 16
LOG2_E = math.log2(math.e)
GLA_SAFE_RANGE = 80.0
GLA_UNROLL = 16


def _params(*dims):
    return pltpu.CompilerParams(dimension_semantics=dims, vmem_limit_bytes=VMEM_LIMIT_BYTES)


def _layer_norm_rows(x):
    mu = jnp.mean(x, axis=-1, keepdims=True)
    xc = x - mu
    var = jnp.mean(xc * xc, axis=-1, keepdims=True)
    return xc * lax.rsqrt(var + LN_EPS)


def _silu(x):
    return x * jax.nn.sigmoid(x)


def _mod_kernel(c_ref, w_ref, b_ref, o_ref):
    x = _silu(c_ref[...]).astype(BF16)
    o_ref[...] = jnp.dot(x, w_ref[...].astype(BF16), preferred_element_type=F32) + b_ref[...]


def _modulation(cc, mod_w, mod_b, tn=1024):
    depth, d, n = mod_w.shape
    rows = cc.shape[0]
    return pl.pallas_call(
        _mod_kernel,
        grid=(depth, n // tn),
        in_specs=[
            pl.BlockSpec((rows, d), lambda l, j: (0, 0)),
            pl.BlockSpec((None, d, tn), lambda l, j: (l, 0, j)),
            pl.BlockSpec((None, 1, tn), lambda l, j: (l, 0, j)),
        ],
        out_specs=pl.BlockSpec((None, rows, tn), lambda l, j: (l, 0, j)),
        out_shape=jax.ShapeDtypeStruct((depth, rows, n), F32),
        compiler_params=_params("arbitrary", "arbitrary"),
        name="modulation",
    )(cc, mod_w, mod_b.reshape(depth, 1, n))


def _mod_spec(chunk, d, row_of_tile):
    return pl.BlockSpec((None, 1, d), lambda i, *_: (row_of_tile(i), 0, chunk))


def _even_inproj_kernel(h_ref, sh_ref, sc_ref, w_ref, cscale_ref, cos_ref, sin_ref, o_ref, u_ref,
                        *, n_rope_tiles, use_rope):
    j = pl.program_id(1)

    @pl.when(j == 0)
    def _():
        u_ref[...] = (_layer_norm_rows(h_ref[...]) * (1.0 + sc_ref[...]) + sh_ref[...]).astype(BF16)

    def tile(rope):
        half = ROPE_AXIS_DIM // 2
        if rope:
            lane = lax.broadcasted_iota(jnp.int32, (u_ref.shape[0], HEAD_DIM), 1)
            second_half = (lane % ROPE_AXIS_DIM) >= half
        for c0 in range(0, o_ref.shape[1], MXU_DIM):
            cols = slice(c0, c0 + MXU_DIM)
            y = jnp.dot(u_ref[...], w_ref[:, cols], preferred_element_type=F32)
            if rope:
                parts = []
                for hh in range(MXU_DIM // HEAD_DIM):
                    yh = y[:, hh * HEAD_DIM:(hh + 1) * HEAD_DIM]
                    partner = jnp.where(second_half, pltpu.roll(yh, half, axis=1),
                                        pltpu.roll(yh, HEAD_DIM - half, axis=1))
                    parts.append(yh * cos_ref[...] + partner * sin_ref[...])
                y = jnp.concatenate(parts, axis=1)
            o_ref[:, cols] = (y * cscale_ref[:, cols]).astype(BF16)

    if use_rope:
        pl.when(j < n_rope_tiles)(lambda: tile(True))
        pl.when(j >= n_rope_tiles)(lambda: tile(False))
    else:
        tile(False)


def _even_inproj(h, mod3, row_of_tile, w, col_scale, cos_t, sin_t, *, qk_width, use_rope, tm, tn):
    m, d = h.shape
    n = w.shape[1]
    n_pos_tiles = cos_t.shape[0] // tm if use_rope else 1
    tab_rows = tm if use_rope else cos_t.shape[0]
    kern = functools.partial(_even_inproj_kernel, n_rope_tiles=2 * qk_width // tn, use_rope=use_rope)
    return pl.pallas_call(
        kern,
        grid=(m // tm, n // tn),
        in_specs=[
            pl.BlockSpec((tm, d), lambda i, j: (i, 0)),
            _mod_spec(0, d, row_of_tile),
            _mod_spec(1, d, row_of_tile),
            pl.BlockSpec((d, tn), lambda i, j: (0, j)),
            pl.BlockSpec((1, tn), lambda i, j: (0, j)),
            pl.BlockSpec((tab_rows, HEAD_DIM), lambda i, j: (i % n_pos_tiles, 0)),
            pl.BlockSpec((tab_rows, HEAD_DIM), lambda i, j: (i % n_pos_tiles, 0)),
        ],
        out_specs=pl.BlockSpec((tm, tn), lambda i, j: (i, j)),
        out_shape=jax.ShapeDtypeStruct((m, n), BF16),
        scratch_shapes=[pltpu.VMEM((tm, d), BF16)],
        compiler_params=_params("arbitrary", "arbitrary"),
        name="even_inproj",
    )(h, mod3, mod3, w, col_scale, cos_t, sin_t)


def _diff_attn_kernel(q_ref, qc_ref, kc_ref, kl_ref, vc_ref, vl_ref, lam_ref, subln_ref, o_ref, oc_ref,
                      *, lam_init, tq):
    lv = lam_ref[...]
    lam = (jnp.exp(jnp.sum(lv[0:1] * lv[1:2], axis=1, keepdims=True))
           - jnp.exp(jnp.sum(lv[2:3] * lv[3:4], axis=1, keepdims=True)) + lam_init)

    ones = jnp.ones((HEAD_DIM, HEAD_DIM), BF16)

    def max_sq_norm(ref):
        x = ref[...]
        row_sums = jnp.dot(x * x, ones, preferred_element_type=F32)
        return jnp.max(row_sums)

    knorm = jnp.maximum(max_sq_norm(kc_ref), max_sq_norm(kl_ref))
    qnorm = jnp.maximum(max_sq_norm(qc_ref), max_sq_norm(q_ref))
    small_scores = qnorm * knorm <= 0.98 * ATTN_SAFE_LOG2 ** 2

    dn = (((1,), (1,)), ((), ()))

    def attend(q, kv_refs, shift_by_max):
        lane = lax.broadcasted_iota(jnp.int32, q.shape, 1)
        first_map = lane < DIFF_QK_DIM
        zero = jnp.zeros_like(q)
        maps = []
        for qm in (jnp.where(first_map, q, zero), jnp.where(first_map, zero, q)):
            s = [lax.dot_general(qm, k_ref[...], dn, preferred_element_type=F32) for k_ref, _ in kv_refs]
            if shift_by_max:
                mx = functools.reduce(jnp.maximum, [jnp.max(x, axis=1, keepdims=True) for x in s])
                s = [x - mx for x in s]
            num = den = None
            for x, (_, v_ref) in zip(s, kv_refs):
                e = jnp.exp2(x)
                d = jnp.sum(e, axis=1, keepdims=True)
                part = jnp.dot(e.astype(BF16), v_ref[...], preferred_element_type=F32)
                den = d if den is None else den + d
                num = part if num is None else num + part
            maps.append((num, den))
        (n1, d1), (n2, d2) = maps
        acc = n1 * (1.0 / d1) - n2 * (lam / d2)
        ms = jnp.mean(acc * acc, axis=1, keepdims=True)
        y = acc * lax.rsqrt(ms + RMS_EPS) * subln_ref[...] * (1.0 - lam_init)
        return y.astype(BF16)

    def all_queries(shift_by_max):
        oc_ref[...] = attend(qc_ref[...], [(kc_ref, vc_ref)], shift_by_max)

        def body(t, carry):
            rows = pl.ds(pl.multiple_of(t * tq, tq), tq)
            o_ref[rows, :] = attend(q_ref[rows, :], [(kc_ref, vc_ref), (kl_ref, vl_ref)], shift_by_max)
            return carry
        n_tiles = q_ref.shape[0] // tq
        lax.fori_loop(0, n_tiles, body, 0, unroll=1 if shift_by_max else min(4, n_tiles))

    lax.cond(small_scores, lambda: all_queries(False), lambda: all_queries(True))


def _diff_attention(p_lat, p_ctx, lam_vecs, subln, lam_init, *, n_heads, tq):
    b, nq, _ = p_lat.shape
    n_ctx = p_ctx.shape[1]
    head = lambda rows, sec: pl.BlockSpec((None, rows, HEAD_DIM), lambda bi, h: (bi, 0, sec * n_heads + h))
    out = lambda rows: pl.BlockSpec((None, rows, HEAD_DIM), lambda bi, h: (bi, 0, h))
    return pl.pallas_call(
        functools.partial(_diff_attn_kernel, lam_init=lam_init, tq=tq),
        grid=(b, n_heads),
        in_specs=[head(nq, 0), head(n_ctx, 0), head(n_ctx, 1), head(nq, 1), head(n_ctx, 2), head(nq, 2),
                  pl.BlockSpec(lam_vecs.shape, lambda bi, h: (0, 0)),
                  pl.BlockSpec((1, HEAD_DIM), lambda bi, h: (0, 0))],
        out_specs=[out(nq), out(n_ctx)],
        out_shape=[jax.ShapeDtypeStruct((b, nq, n_heads * HEAD_DIM), BF16),
                   jax.ShapeDtypeStruct((b, n_ctx, n_heads * HEAD_DIM), BF16)],
        compiler_params=_params("arbitrary", "arbitrary"),
        name="diff_attention",
    )(p_lat, p_ctx, p_ctx, p_lat, p_ctx, p_lat, lam_vecs, subln.reshape(1, HEAD_DIM))


def _fourier_kernel(x_ref, dn_ref, cs_ref, o_ref, z_ref):
    n = x_ref.shape[0]

    @pl.when(pl.program_id(1) == 0)
    def _():
        for g in range(x_ref.shape[1] // HEAD_DIM):
            cols = slice(g * HEAD_DIM, (g + 1) * HEAD_DIM)
            zc = jnp.dot(x_ref[:, cols], cs_ref[...], preferred_element_type=F32)
            z_ref[0:n, cols] = zc[:, :HEAD_DIM].astype(BF16)
            z_ref[n:2 * n, cols] = zc[:, HEAD_DIM:].astype(BF16)

    o_ref[...] = jnp.dot(dn_ref[...], z_ref[...], preferred_element_type=F32).astype(BF16)


def _dft_tables(n):
    j = np.arange(n, dtype=np.int64)
    ang = 2.0 * np.pi * ((j[:, None] * j[None, :]) % n).astype(np.float64) / n
    return np.cos(ang) / math.sqrt(n), np.sin(ang) / math.sqrt(n)


def _fourier_mix(src, col_block, width, *, tm):
    b, n, _ = src.shape
    cn, sn = _dft_tables(n)
    cc, sc = _dft_tables(HEAD_DIM)
    dn = jnp.asarray(np.concatenate([cn, -sn], axis=1), dtype=BF16)
    cs = jnp.asarray(np.concatenate([cc, sc], axis=1), dtype=BF16)
    return pl.pallas_call(
        _fourier_kernel,
        grid=(b, n // tm),
        in_specs=[
            pl.BlockSpec((None, n, width), lambda bi, t: (bi, 0, col_block)),
            pl.BlockSpec((tm, 2 * n), lambda bi, t: (t, 0)),
            pl.BlockSpec((HEAD_DIM, 2 * HEAD_DIM), lambda bi, t: (0, 0)),
        ],
        out_specs=pl.BlockSpec((None, tm, width), lambda bi, t: (bi, t, 0)),
        out_shape=jax.ShapeDtypeStruct((b, n, width), BF16),
        scratch_shapes=[pltpu.VMEM((2 * n, width), BF16)],
        compiler_params=_params("arbitrary", "arbitrary"),
        name="fourier_mix",
    )(src, dn, cs)


def _proj_residual_kernel(*refs, n_in, n_k):
    a_refs = refs[:n_in]
    w_refs = refs[n_in:2 * n_in]
    h_ref, gate_ref, g_ref, b_ref, o_ref = refs[2 * n_in:2 * n_in + 5]
    k = pl.program_id(1)
    tm, d = o_ref.shape
    col_block, row_block = 2 * MXU_DIM, MXU_DIM

    def accumulate(first, rows=slice(None)):
        for c0 in range(0, d, col_block):
            cols = slice(c0, c0 + col_block)
            part = None
            for a_ref, w_ref in zip(a_refs, w_refs):
                p = jnp.dot(a_ref[rows, :], w_ref[:, cols], preferred_element_type=F32)
                part = p if part is None else part + p
            if first:
                o_ref[rows, cols] = part
            else:
                o_ref[rows, cols] += part

    def finish(r_lo=0, r_hi=tm):
        for r0 in range(r_lo, r_hi, row_block):
            rows = slice(r0, r0 + row_block)
            x = ALPHA * h_ref[rows, :] + gate_ref[...] * o_ref[rows, :]
            o_ref[rows, :] = _layer_norm_rows(x) * g_ref[...] + b_ref[...]

    def last_step(first):
        halves = 2 if tm >= 2 * row_block else 1
        for r0 in range(0, tm, tm // halves):
            accumulate(first, slice(r0, r0 + tm // halves))
            finish(r0, r0 + tm // halves)

    if n_k == 1:
        last_step(True)
    else:
        pl.when(k == 0)(lambda: accumulate(True))
        pl.when(jnp.logical_and(k > 0, k < n_k - 1))(lambda: accumulate(False))
        pl.when(k == n_k - 1)(lambda: last_step(False))


def _proj_residual(a_list, w_list, h, mod3, gate_chunk, row_of_tile, ln_g, ln_b, *, tm, n_k=1):
    m, d = h.shape
    in_specs = []
    for a in a_list:
        in_specs.append(pl.BlockSpec((tm, a.shape[1] // n_k), lambda i, k: (i, k)))
    for a, (w, first_row) in zip(a_list, w_list):
        rows = a.shape[1] // n_k
        in_specs.append(pl.BlockSpec((rows, d), lambda i, k, blk=first_row // rows: (blk + k, 0)))
    in_specs += [
        pl.BlockSpec((tm, d), lambda i, k: (i, 0)),
        _mod_spec(gate_chunk, d, row_of_tile),
        pl.BlockSpec((1, d), lambda i, k: (0, 0)),
        pl.BlockSpec((1, d), lambda i, k: (0, 0)),
    ]
    return pl.pallas_call(
        functools.partial(_proj_residual_kernel, n_in=len(a_list), n_k=n_k),
        grid=(m // tm, n_k),
        in_specs=in_specs,
        out_specs=pl.BlockSpec((tm, d), lambda i, k: (i, 0)),
        out_shape=jax.ShapeDtypeStruct((m, d), F32),
        compiler_params=_params("arbitrary", "arbitrary"),
        name="proj_residual",
    )(*a_list, *[w for w, _ in w_list], h, mod3, ln_g.reshape(1, d), ln_b.reshape(1, d))


def _ffn_up_kernel(h_ref, hp_ref, hn_ref, sh_ref, sc_ref, wa_ref, wv_ref, cw_ref, cb_ref, o_ref,
                   u_ref, uh_ref, *, seq_len):
    i = pl.program_id(0)
    j = pl.program_id(1)
    tm = h_ref.shape[0]
    whole_sequences = tm >= seq_len
    tiles_per_seq = max(seq_len // tm, 1)

    @pl.when(j == 0)
    def _():
        scale = 1.0 + sc_ref[...]
        u_ref[...] = (_layer_norm_rows(h_ref[...]) * scale + sh_ref[...]).astype(BF16)
        if not whole_sequences:
            uh_ref[0:SUBLANES, :] = (_layer_norm_rows(hp_ref[...]) * scale + sh_ref[...]).astype(BF16)
            uh_ref[SUBLANES:, :] = (_layer_norm_rows(hn_ref[...]) * scale + sh_ref[...]).astype(BF16)

    t_in_seq = i % tiles_per_seq
    row = lax.broadcasted_iota(jnp.int32, (tm, MXU_DIM), 0)
    row_in_seq = row % seq_len
    col_blocks = [slice(c0, c0 + MXU_DIM) for c0 in range(0, o_ref.shape[1], MXU_DIM)]
    gates = []
    for cols in col_blocks:
        a = jnp.dot(u_ref[...], wa_ref[:, cols], preferred_element_type=F32)
        if whole_sequences:
            a_prev = jnp.where(row_in_seq == 0, 0.0, pltpu.roll(a, 1, axis=0))
            a_next = jnp.where(row_in_seq == seq_len - 1, 0.0, pltpu.roll(a, tm - 1, axis=0))
        else:
            halo = jnp.dot(uh_ref[...], wa_ref[:, cols], preferred_element_type=F32)
            prev_row = jnp.where(t_in_seq > 0, halo[SUBLANES - 1:SUBLANES], 0.0)
            next_row = jnp.where(t_in_seq < tiles_per_seq - 1, halo[SUBLANES:SUBLANES + 1], 0.0)
            a_prev = jnp.where(row == 0, prev_row, pltpu.roll(a, 1, axis=0))
            a_next = jnp.where(row == tm - 1, next_row, pltpu.roll(a, tm - 1, axis=0))
        cw = cw_ref[:, cols]
        conv = cb_ref[:, cols] + a_prev * cw[0:1] + a * cw[1:2] + a_next * cw[2:3]
        gates.append(0.5 * conv * (1.0 + lax.erf(conv * math.sqrt(0.5))))
    for cols, gelu in zip(col_blocks, gates):
        v = jnp.dot(u_ref[...], wv_ref[:, cols], preferred_element_type=F32)
        o_ref[:, cols] = (gelu * v).astype(BF16)


def _ffn_up(h, mod3, row_of_tile, wa, wv, conv_w, conv_b, *, seq_len, tm, tn):
    m, d = h.shape
    n = wa.shape[1]
    assert tm % seq_len == 0 or seq_len % tm == 0
    blocks_per_tile = tm // SUBLANES
    n_blocks = m // SUBLANES
    kern = functools.partial(_ffn_up_kernel, seq_len=seq_len)
    return pl.pallas_call(
        kern,
        grid=(m // tm, n // tn),
        in_specs=[
            pl.BlockSpec((tm, d), lambda i, j: (i, 0)),
            pl.BlockSpec((SUBLANES, d), lambda i, j: (jnp.maximum(i * blocks_per_tile - 1, 0), 0)),
            pl.BlockSpec((SUBLANES, d), lambda i, j: (jnp.minimum((i + 1) * blocks_per_tile, n_blocks - 1), 0)),
            _mod_spec(3, d, row_of_tile),
            _mod_spec(4, d, row_of_tile),
            pl.BlockSpec((d, tn), lambda i, j: (0, j)),
            pl.BlockSpec((d, tn), lambda i, j: (0, j)),
            pl.BlockSpec((CONV_W, tn), lambda i, j: (0, j)),
            pl.BlockSpec((1, tn), lambda i, j: (0, j)),
        ],
        out_specs=pl.BlockSpec((tm, tn), lambda i, j: (i, j)),
        out_shape=jax.ShapeDtypeStruct((m, n), BF16),
        scratch_shapes=[pltpu.VMEM((tm, d), BF16), pltpu.VMEM((2 * SUBLANES, d), BF16)],
        compiler_params=_params("arbitrary", "arbitrary"),
        name="ffn_up",
    )(h, h, h, mod3, mod3, wa, wv, conv_w, conv_b)


def _hgrn_act_kernel(h_ref, sh_ref, sc_ref, w_ref, o_ref, u_ref, *, lin_lo, lin_hi):
    j = pl.program_id(1)

    @pl.when(j == 0)
    def _():
        u_ref[...] = (_layer_norm_rows(h_ref[...]) * (1.0 + sc_ref[...]) + sh_ref[...]).astype(BF16)

    def tile(activation):
        for c0 in range(0, o_ref.shape[1], MXU_DIM):
            cols = slice(c0, c0 + MXU_DIM)
            acc = jnp.dot(u_ref[...], w_ref[:, cols], preferred_element_type=F32)
            o_ref[:, cols] = activation(acc).astype(BF16)

    linear = jnp.logical_and(j >= lin_lo, j < lin_hi)
    pl.when(linear)(lambda: tile(lambda x: x))
    pl.when(jnp.logical_not(linear))(lambda: tile(_silu))


def _hgrn_gate_kernel(h_ref, sh_ref, sc_ref, w_ref, lbp_ref, k_ref, lf_ref, u_ref, *, layer):
    j = pl.program_id(1)

    @pl.when(j == 0)
    def _():
        u_ref[...] = (_layer_norm_rows(h_ref[...]) * (1.0 + sc_ref[...]) + sh_ref[...]).astype(BF16)

    for c0 in range(0, k_ref.shape[1], MXU_DIM):
        cols = slice(c0, c0 + MXU_DIM)
        x = lbp_ref[:, cols]
        e = jnp.exp(x - jnp.max(x, axis=0, keepdims=True))
        lb = jnp.sum(e[1:layer + 1], axis=0, keepdims=True) / jnp.sum(e, axis=0, keepdims=True)

        f_pre = jnp.dot(u_ref[...], w_ref[:, cols], preferred_element_type=F32)
        gate = jax.nn.sigmoid(f_pre)
        k_ref[:, cols] = ((1.0 - lb) * (1.0 - gate)).astype(BF16)
        lf_ref[:, cols] = jnp.log(lb + (1.0 - lb) * gate)


def _hgrn_inproj(h, mod3, row_of_tile, w, lb_params, *, layer, tm, tn):
    m, d = h.shape
    tiles = d // tn
    common = [
        pl.BlockSpec((tm, d), lambda i, j: (i, 0)),
        _mod_spec(0, d, row_of_tile),
        _mod_spec(1, d, row_of_tile),
    ]
    act_w = pl.BlockSpec((d, tn), lambda i, j: (0, jnp.where(j < tiles, j, j + 2 * tiles)))
    gate_w = pl.BlockSpec((d, tn), lambda i, j: (0, j + tiles))
    act = pl.pallas_call(
        functools.partial(_hgrn_act_kernel, lin_lo=tiles, lin_hi=2 * tiles),
        grid=(m // tm, 3 * tiles),
        in_specs=common + [act_w],
        out_specs=pl.BlockSpec((tm, tn), lambda i, j: (i, j)),
        out_shape=jax.ShapeDtypeStruct((m, 3 * d), BF16),
        scratch_shapes=[pltpu.VMEM((tm, d), BF16)],
        compiler_params=_params("arbitrary", "arbitrary"),
        name="hgrn_inproj_act",
    )(h, mod3, mod3, w)
    k, logf = pl.pallas_call(
        functools.partial(_hgrn_gate_kernel, layer=layer),
        grid=(m // tm, 2 * tiles),
        in_specs=common + [gate_w, pl.BlockSpec((None, DEPTH, tn), lambda i, j: (j // tiles, 0, j % tiles))],
        out_specs=[pl.BlockSpec((tm, tn), lambda i, j: (i, j))] * 2,
        out_shape=[jax.ShapeDtypeStruct((m, 2 * d), BF16), jax.ShapeDtypeStruct((m, 2 * d), F32)],
        scratch_shapes=[pltpu.VMEM((tm, d), BF16)],
        compiler_params=_params("arbitrary", "arbitrary"),
        name="hgrn_inproj_gate",
    )(h, mod3, mod3, w, lb_params)
    return act, k, logf


def _gla_matrices():
    c = GLA_CHUNK
    r_i, c_i = np.meshgrid(np.arange(c), np.arange(c), indexing="ij")
    same = np.bitwise_xor(r_i, c_i)
    in_block = same < GLA_BLOCK
    out = []
    for earlier in (c_i <= r_i, c_i >= r_i):
        out += [jnp.asarray(earlier, BF16), jnp.asarray(earlier & in_block, BF16),
                jnp.asarray(earlier & in_block, jnp.int32)]
    return out + [jnp.asarray(in_block, BF16), jnp.asarray(same, jnp.int32)]


def _gla_direct_block_terms(q, k, v, cum, bad, *, reverse):
    c = GLA_CHUNK
    row = lax.broadcasted_iota(jnp.int32, (c, HEAD_DIM), 0) % GLA_BLOCK
    qb = jnp.where(bad, q, 0.0)
    acc0 = jnp.sum(qb * k, axis=1, keepdims=True) * v

    def offset(dlt, acc):
        shift = (c - dlt) if reverse else dlt
        valid = (row <= GLA_BLOCK - 1 - dlt) if reverse else (row >= dlt)
        k_d = pltpu.roll(k, shift, axis=0)
        v_d = pltpu.roll(v, shift, axis=0)
        cum_d = pltpu.roll(cum, shift, axis=0)
        decay = jnp.exp2(jnp.where(valid, cum - cum_d, -jnp.inf))
        return acc + jnp.sum(qb * k_d * decay, axis=1, keepdims=True) * v_d

    return lax.fori_loop(1, GLA_BLOCK, offset, acc0)


def _gla_prefix_sums(lf, mats):
    lf = lf * LOG2_E
    lf_hi = lf.astype(BF16)
    lf_lo = (lf - lf_hi.astype(F32)).astype(BF16)
    two_term = lambda m: (jnp.dot(m, lf_hi, preferred_element_type=F32)
                          + jnp.dot(m, lf_lo, preferred_element_type=F32))
    return two_term(mats["tri"]), two_term(mats["tri_blk"])


def _gla_chunk(q, k, lf, v_bf, cum_ref, w_blk, mats, *, reverse, robust, want_o):
    c = GLA_CHUNK
    dn_t = (((1,), (1,)), ((), ()))
    cum = cum_ref[...]
    end_row = 0 if reverse else c - 1
    cum_end = cum_ref[end_row:end_row + 1, :]

    decayed = lambda x, log2_decay: (x * jnp.exp2(log2_decay)).astype(BF16)
    k_dec = decayed(k, cum_end - cum)
    u = lax.dot_general(v_bf, k_dec, (((0,), (0,)), ((), ())), preferred_element_type=F32)
    dec_end = jnp.exp2(cum_end)
    if not want_o:
        return None, None, None, u, dec_end
    qe = decayed(q, cum)

    if robust:
        blk_tot = jnp.dot(mats["ones_blk"], (lf * LOG2_E).astype(BF16), preferred_element_type=F32)
        bad = blk_tot < -GLA_SAFE_RANGE
        a_d = jnp.where(bad, 0.0, q * jnp.exp2(w_blk)).astype(BF16)
        b_d = decayed(k, jnp.minimum(-w_blk, GLA_SAFE_RANGE + 20.0))
    else:
        a_d = decayed(q, w_blk)
        b_d = decayed(k, -w_blk)
    scores = jnp.where(mats["diag_mask"], lax.dot_general(a_d, b_d, dn_t, preferred_element_type=F32), 0.0)

    zeros = {m: jnp.zeros((m, HEAD_DIM), BF16) for m in GLA_LEVELS}
    for m in GLA_LEVELS:
        a_parts, b_parts = [], []
        for blk in range(c // (2 * m)):
            lo, mid, hi = blk * 2 * m, blk * 2 * m + m, (blk + 1) * 2 * m
            if reverse:
                ref_row = cum_ref[mid:mid + 1, :]
                qa = decayed(q[lo:mid], cum[lo:mid] - ref_row)
                kb = decayed(k[mid:hi], ref_row - cum[mid:hi])
                a_parts += [qa, zeros[m]]
                b_parts += [zeros[m], kb]
            else:
                ref_row = cum_ref[mid - 1:mid, :]
                qa = decayed(q[mid:hi], cum[mid:hi] - ref_row)
                kb = decayed(k[lo:mid], ref_row - cum[lo:mid])
                a_parts += [zeros[m], qa]
                b_parts += [kb, zeros[m]]
        a_m = jnp.concatenate(a_parts, axis=0)
        b_m = jnp.concatenate(b_parts, axis=0)
        s_m = lax.dot_general(a_m, b_m, dn_t, preferred_element_type=F32)
        if 2 * m < c:
            s_m = jnp.where(mats["same"] < 2 * m, s_m, 0.0)
        scores = scores + s_m
    if robust:
        direct = _gla_direct_block_terms(q, k, v_bf.astype(F32), cum, bad, reverse=reverse)
    else:
        direct = jnp.zeros((c, HEAD_DIM), F32)
    return direct, scores.astype(BF16), qe, u, dec_end


def _gla_kernel(kfc_ref, kbc_ref, lffc_ref, lfbc_ref, vc_ref, q_ref, kf_ref, kb_ref, lff_ref, lfb_ref, v_ref,
                gate_ref, nw_ref, trif_ref, tbf_ref, dgf_ref, trib_ref, tbb_ref, dgb_ref, ones_ref, same_ref,
                y_ref, of_ref, ob_ref, cumf_ref, cumb_ref, wf_ref, wb_ref, qef_ref, qeb_ref,
                scf_ref, scb_ref, uf_ref, ub_ref, df_ref, db_ref):
    c = GLA_CHUNK
    dn_t = (((1,), (1,)), ((), ()))
    chunk_rows = lambda ci: pl.ds(pl.multiple_of(ci * c, c), c)
    mats_f = dict(tri=trif_ref[...], tri_blk=tbf_ref[...], ones_blk=ones_ref[...],
                  diag_mask=dgf_ref[...] > 0, same=same_ref[...])
    mats_b = dict(tri=trib_ref[...], tri_blk=tbb_ref[...], ones_blk=ones_ref[...],
                  diag_mask=dgb_ref[...] > 0, same=same_ref[...])

    def scan(q_ref, kf_ref, kb_ref, lff_ref, lfb_ref, v_ref, state, want_o):
        n_chunks = kf_ref.shape[0] // c
        unroll = min(GLA_UNROLL, n_chunks)

        def prefix_body(ci, lowest):
            rows = chunk_rows(ci)
            cum_f, w_f = _gla_prefix_sums(lff_ref[rows, :], mats_f)
            cum_b, w_b = _gla_prefix_sums(lfb_ref[rows, :], mats_b)
            cumf_ref[rows, :], wf_ref[rows, :] = cum_f, w_f
            cumb_ref[rows, :], wb_ref[rows, :] = cum_b, w_b
            return jnp.minimum(lowest, jnp.minimum(w_f, w_b))

        worst = jnp.min(lax.fori_loop(0, n_chunks, prefix_body, jnp.zeros((c, HEAD_DIM), F32), unroll=unroll))

        def local_pass(robust):
            def body(ci, _):
                rows = chunk_rows(ci)
                q = q_ref[rows, :].astype(F32) if want_o else None
                v_bf = v_ref[rows, :]
                x_f, sc_f, qe_f, u_f, d_f = _gla_chunk(
                    q, kf_ref[rows, :].astype(F32), lff_ref[rows, :], v_bf, cumf_ref.at[rows, :],
                    wf_ref[rows, :], mats_f, reverse=False, robust=robust, want_o=want_o)
                x_b, sc_b, qe_b, u_b, d_b = _gla_chunk(
                    q, kb_ref[rows, :].astype(F32), lfb_ref[rows, :], v_bf, cumb_ref.at[rows, :],
                    wb_ref[rows, :], mats_b, reverse=True, robust=robust, want_o=want_o)
                uf_ref[ci] = u_f
                ub_ref[ci] = u_b
                df_ref[ci] = d_f
                db_ref[ci] = d_b
                if want_o:
                    of_ref[rows, :] = x_f
                    ob_ref[rows, :] = x_b
                    scf_ref[rows, :] = sc_f
                    scb_ref[rows, :] = sc_b
                    qef_ref[rows, :] = qe_f
                    qeb_ref[rows, :] = qe_b
                return 0

            lax.fori_loop(0, n_chunks, body, 0, unroll=1 if robust else unroll)

        if want_o:
            lax.cond(worst < -GLA_SAFE_RANGE, lambda: local_pass(True), lambda: local_pass(False))
        else:
            local_pass(False)

        def state_pass(ci, carry):
            st_f, st_b = carry
            cb = n_chunks - 1 - ci
            if want_o:
                rows_f, rows_b = chunk_rows(ci), chunk_rows(cb)
                of_ref[rows_f, :] += (
                    jnp.dot(scf_ref[rows_f, :], v_ref[rows_f, :], preferred_element_type=F32)
                    + lax.dot_general(qef_ref[rows_f, :], st_f.astype(BF16), dn_t, preferred_element_type=F32))
                ob_ref[rows_b, :] += (
                    jnp.dot(scb_ref[rows_b, :], v_ref[rows_b, :], preferred_element_type=F32)
                    + lax.dot_general(qeb_ref[rows_b, :], st_b.astype(BF16), dn_t, preferred_element_type=F32))
            return st_f * df_ref[ci] + uf_ref[ci], st_b * db_ref[cb] + ub_ref[cb]

        return lax.fori_loop(0, n_chunks, state_pass, state, unroll=unroll)

    zero = jnp.zeros((HEAD_DIM, HEAD_DIM), F32)
    state = scan(None, kfc_ref, kbc_ref, lffc_ref, lfbc_ref, vc_ref, (zero, zero), False)
    scan(q_ref, kf_ref, kb_ref, lff_ref, lfb_ref, v_ref, state, True)

    n_chunks = q_ref.shape[0] // c

    def readout(ci, _):
        rows = chunk_rows(ci)
        tot = of_ref[rows, :] + ob_ref[rows, :]
        ms = jnp.mean(tot * tot, axis=1, keepdims=True)
        y = tot * lax.rsqrt(ms + RMS_EPS) * nw_ref[...] * gate_ref[rows, :].astype(F32)
        y_ref[rows, :] = y.astype(BF16)
        return 0
    lax.fori_loop(0, n_chunks, readout, 0, unroll=min(GLA_UNROLL, n_chunks))


def _gla(act, k, logf, act_ctx, k_ctx, logf_ctx, norm_w, *, n_heads):
    b, n, _ = act.shape
    n_ctx = act_ctx.shape[1]
    assert n_ctx <= n
    hd = HEAD_DIM
    col = lambda rows, off: pl.BlockSpec((None, rows, hd), lambda bi, h: (bi, 0, off * n_heads + h))
    return pl.pallas_call(
        _gla_kernel,
        grid=(b, n_heads),
        in_specs=[col(n_ctx, 0), col(n_ctx, 1), col(n_ctx, 0), col(n_ctx, 1), col(n_ctx, 1),
                  col(n, 0), col(n, 0), col(n, 1), col(n, 0), col(n, 1), col(n, 1), col(n, 2),
                  pl.BlockSpec((1, hd), lambda bi, h: (0, 0))]
                 + [pl.BlockSpec((GLA_CHUNK, GLA_CHUNK), lambda bi, h: (0, 0))] * 8,
        out_specs=pl.BlockSpec((None, n, hd), lambda bi, h: (bi, 0, h)),
        out_shape=jax.ShapeDtypeStruct((b, n, n_heads * hd), BF16),
        scratch_shapes=(
            [pltpu.VMEM((n, hd), F32)] * 6
            + [pltpu.VMEM((n, hd), BF16)] * 4
            + [pltpu.VMEM((n // GLA_CHUNK, hd, hd), F32)] * 2
            + [pltpu.VMEM((n // GLA_CHUNK, 1, hd), F32)] * 2),
        compiler_params=_params("arbitrary", "arbitrary"),
        name="gla_scan",
    )(k_ctx, k_ctx, logf_ctx, logf_ctx, act_ctx, act, k, k, logf, logf, act, act, norm_w.reshape(1, hd),
      *_gla_matrices())


def _rope_tables(n):
    half = ROPE_AXIS_DIM // 2
    inv = (1.0 / (ROPE_BASE ** (np.arange(0, ROPE_AXIS_DIM, 2, dtype=np.float32) / ROPE_AXIS_DIM))).astype(np.float32)
    pos = np.arange(n)
    ang_r = (pos // GRID_W).astype(np.float32)[:, None] * inv[None, :]
    ang_c = (pos % GRID_W).astype(np.float32)[:, None] * inv[None, :]
    ang = np.concatenate([ang_r, ang_r, ang_c, ang_c] * 2, axis=1)
    sign = np.tile(np.concatenate([-np.ones(half, np.float32), np.ones(half, np.float32)]), HEAD_DIM // ROPE_AXIS_DIM)
    return jnp.asarray(np.cos(ang), F32), jnp.asarray(np.sin(ang) * sign[None, :], F32)


def kernel(x, c, ctx, c_ctx, mod_w, mod_b, ln_mix_g, ln_mix_b, ln_ffn_g, ln_ffn_b, even_w_in, even_w_out, diff_lambda, diff_subln, hgrn_w_in, hgrn_w_out, hgrn_lower_bounds, hgrn_norm, ffn_w_up, ffn_conv_w, ffn_conv_b, ffn_w_down):
    b, n_lat, d = x.shape
    n_ctx = ctx.shape[1]
    assert mod_w.shape[0] == DEPTH and n_lat % GRID_W == 0
    d_ff = ffn_w_down.shape[1]
    d_ff_pad = -(-d_ff // (2 * MXU_DIM)) * (2 * MXU_DIM)
    diff_width = even_w_out.shape[2] * 3 // 4
    four_width = even_w_in.shape[2] - 3 * diff_width
    n_diff_heads = diff_width // HEAD_DIM
    n_hgrn_heads = d // HEAD_DIM

    cc = jnp.concatenate([c, c_ctx[None, :], jnp.zeros((2 * SUBLANES - b - 1, d), F32)], axis=0)
    mod = _modulation(cc, mod_w, mod_b)
    mod = mod.reshape(DEPTH, cc.shape[0], 1, 6 * d)

    h_lat = x.reshape(b * n_lat, d)
    h_ctx = ctx.reshape(b * n_ctx, d)

    tm_wide_lat, tm_lat = min(1024, n_lat), min(512, n_lat)
    ctx_tile = lambda cap: n_ctx * max(1, min(cap // n_ctx, b)) if cap >= n_ctx else cap
    tm_wide_ctx, tm_ctx = ctx_tile(1024), ctx_tile(512)
    assert (b * n_ctx) % tm_wide_ctx == 0 and (b * n_ctx) % tm_ctx == 0
    lat_row = lambda tm: (lambda i: (i * tm) // n_lat)
    ctx_row = lambda tm: (lambda i: b)
    down_k_tiles = 4 if d_ff_pad % (4 * LANES) == 0 else d_ff_pad // 512

    def ffn(h, mod3, row_fn, layer, seq_len, tm_up, tm_down):
        pad = d_ff_pad - d_ff
        w_up = ffn_w_up[layer]
        wa = jnp.pad(w_up[:, :d_ff].astype(BF16), ((0, 0), (0, pad)))
        wv = jnp.pad(w_up[:, d_ff:].astype(BF16), ((0, 0), (0, pad)))
        cw = jnp.pad(ffn_conv_w[layer], ((0, 0), (0, pad)))
        cb = jnp.pad(ffn_conv_b[layer], (0, pad)).reshape(1, d_ff_pad)
        wd = jnp.pad(ffn_w_down[layer].astype(BF16), ((0, pad), (0, 0)))
        g = _ffn_up(h, mod3, row_fn(tm_up), wa, wv, cw, cb, seq_len=seq_len, tm=tm_up, tn=512)
        return _proj_residual([g], [(wd, 0)], h, mod3, 5, row_fn(tm_down), ln_ffn_g[layer], ln_ffn_b[layer],
                              tm=tm_down, n_k=down_k_tiles)

    layer = 0
    lam_init = 0.8 - 0.6 * math.exp(-0.3 * layer)
    w_in = even_w_in[0].astype(BF16)
    w_out = even_w_out[0].astype(BF16)
    cos_t, sin_t = _rope_tables(n_lat)
    mod3 = mod[layer]

    col_scale = jnp.asarray(np.where(np.arange(w_in.shape[1]) < diff_width, QK_SCALE, 1.0)[None, :], F32)
    p_lat = _even_inproj(h_lat, mod3, lat_row(tm_wide_lat), w_in, col_scale, cos_t, sin_t, qk_width=diff_width,
                         use_rope=True, tm=tm_wide_lat, tn=1024).reshape(b, n_lat, -1)
    p_ctx = _even_inproj(h_ctx, mod3, ctx_row(tm_wide_ctx), w_in, col_scale, cos_t[:SUBLANES], sin_t[:SUBLANES],
                         qk_width=diff_width, use_rope=False, tm=tm_wide_ctx, tn=1024).reshape(b, n_ctx, -1)

    attn_lat, attn_ctx = _diff_attention(p_lat, p_ctx, diff_lambda[0], diff_subln[0], lam_init,
                                         n_heads=n_diff_heads, tq=min(512, n_lat))
    four_block = 3 * diff_width // four_width
    four_lat = _fourier_mix(p_lat, four_block, four_width, tm=tm_lat)
    four_ctx = _fourier_mix(p_ctx, four_block, four_width, tm=n_ctx)

    w_list = [(w_out, 0), (w_out, diff_width)]
    h_lat = _proj_residual([attn_lat.reshape(b * n_lat, -1), four_lat.reshape(b * n_lat, -1)], w_list,
                           h_lat, mod3, 2, lat_row(tm_lat), ln_mix_g[layer], ln_mix_b[layer], tm=tm_lat)
    h_ctx = _proj_residual([attn_ctx.reshape(b * n_ctx, -1), four_ctx.reshape(b * n_ctx, -1)], w_list,
                           h_ctx, mod3, 2, ctx_row(tm_ctx), ln_mix_g[layer], ln_mix_b[layer], tm=tm_ctx)
    h_lat = ffn(h_lat, mod3, lat_row, layer, n_lat, tm_wide_lat, tm_wide_lat)
    h_ctx = ffn(h_ctx, mod3, ctx_row, layer, n_ctx, tm_wide_ctx, tm_wide_ctx)

    layer = 1
    mod3 = mod[layer]
    hw = hgrn_w_in[0].astype(BF16)
    wrap = lambda t, n: t.reshape(b, n, -1)
    act_c, k_c, lf_c = _hgrn_inproj(h_ctx, mod3, ctx_row(tm_wide_ctx), hw, hgrn_lower_bounds,
                                    layer=layer, tm=tm_wide_ctx, tn=1024)
    act_l, k_l, lf_l = _hgrn_inproj(h_lat, mod3, lat_row(tm_wide_lat), hw, hgrn_lower_bounds,
                                    layer=layer, tm=tm_wide_lat, tn=1024)
    y_lat = _gla(wrap(act_l, n_lat), wrap(k_l, n_lat), wrap(lf_l, n_lat),
                 wrap(act_c, n_ctx), wrap(k_c, n_ctx), wrap(lf_c, n_ctx), hgrn_norm[0], n_heads=n_hgrn_heads)
    h_lat = _proj_residual([y_lat.reshape(b * n_lat, d)], [(hgrn_w_out[0].astype(BF16), 0)], h_lat, mod3, 2,
                           lat_row(tm_lat), ln_mix_g[layer], ln_mix_b[layer], tm=tm_lat)
    h_lat = ffn(h_lat, mod3, lat_row, layer, n_lat, tm_wide_lat, tm_wide_lat)
    return h_lat.reshape(b, n_lat, d)
```
